```python
import math
import jax, jax.numpy as jnp
from jax import lax
import numpy as np

D_MODEL = 1024
BATCH = 8
SEQ = 2048
DEPTH = 4
DEC_BATCH = 128
DEC_SEQ = 4
PAST_LEN = 16384
PAGE_SIZE = 128

D_MIX = 2 * D_MODEL
D_SSD = D_MIX // 2
SSD_HEAD_DIM = 64
SSD_HEADS = D_SSD // SSD_HEAD_DIM
SSD_GROUPS = 2
SSD_STATE = 128
SSD_CONV_DIM = D_SSD + 2 * SSD_GROUPS * SSD_STATE
CONV_WIDTH = 4
D_RG = D_MIX // 4
RG_BLOCKS = 8
RG_BLOCK_W = D_RG // RG_BLOCKS
RG_C = 8.0
D_RET = D_MIX // 4
RET_HEADS = 4
RET_HEAD_DIM = D_RET // RET_HEADS
ROPE_BASE = 10000.0
CHUNK = 128
IN_DIM = D_SSD + SSD_CONV_DIM + SSD_HEADS + 2 * D_RG + 4 * D_RET
N_EXPERTS = 32
N_EXPERT_GROUPS = 4
EXPERTS_PER_GROUP = N_EXPERTS // N_EXPERT_GROUPS
TOP_K = 2
D_EXPERT = D_MODEL // 2
EXPERT_BLOCK = 128
ALPHA = (2 * DEPTH) ** 0.25
BETA = (8 * DEPTH) ** -0.25
EPS = 1e-5

kernel_name = "hymba_ssd_rglru_retention_moe_step"


def _layernorm(x, g, b):
    xf = x.astype(jnp.float32)
    mu = jnp.mean(xf, -1, keepdims=True)
    var = jnp.mean(jnp.square(xf - mu), -1, keepdims=True)
    return (xf - mu) * lax.rsqrt(var + EPS) * g + b


def _group_rmsnorm(y, w, groups):
    b, L, d = y.shape
    yf = y.astype(jnp.float32).reshape(b, L, groups, d // groups)
    yf = yf * lax.rsqrt(jnp.mean(yf * yf, -1, keepdims=True) + EPS)
    return yf.reshape(b, L, d) * w


def _causal_conv(x, buf, w, bias):
    L = x.shape[1]
    xp = jnp.concatenate([buf.astype(x.dtype), x], axis=1)
    y = bias + sum(xp[:, j:j + L] * w[j] for j in range(CONV_WIDTH))
    return y, xp[:, -(CONV_WIDTH - 1):]


def _rope(x, pos):
    half = x.shape[-1] // 2
    inv = ROPE_BASE ** (-jnp.arange(half, dtype=jnp.float32) / half)
    ang = pos[:, None] * inv[None, :]
    cos = jnp.cos(ang)[None, :, None, :]
    sin = jnp.sin(ang)[None, :, None, :]
    x1 = x[..., :half].astype(jnp.float32)
    x2 = x[..., half:].astype(jnp.float32)
    return jnp.concatenate([x1 * cos - x2 * sin, x2 * cos + x1 * sin], -1)


def _chunked_decay_recurrence(q, k, v, log_a, h0):
    f32 = jnp.float32
    b, L, nh, dk = q.shape
    dv = v.shape[-1]
    blk = CHUNK if L % CHUNK == 0 else L
    nc = L // blk
    qc = q.astype(f32).reshape(b, nc, blk, nh, dk)
    kc = k.astype(f32).reshape(b, nc, blk, nh, dk)
    vc = v.astype(f32).reshape(b, nc, blk, nh, dv)
    acum = jnp.cumsum(log_a.astype(f32).reshape(b, nc, blk, nh), axis=2)
    causal = jnp.tril(jnp.ones((blk, blk), bool))[None, None, :, :, None]
    decay = jnp.exp(jnp.where(causal, acum[:, :, :, None, :] - acum[:, :, None, :, :], -jnp.inf))
    scores = jnp.einsum("bcthd,bcshd->bctsh", qc, kc) * decay
    y = jnp.einsum("bctsh,bcshv->bcthv", scores, vc)
    to_end = jnp.exp(acum[:, :, -1:, :] - acum)
    chunk_state = jnp.einsum("bcshd,bcshv->bchdv", kc * to_end[..., None], vc)
    chunk_decay = jnp.exp(acum[:, :, -1, :])

    def step(S, inp):
        cs, cd = inp
        return cd[:, :, None, None] * S + cs, S

    h_final, h_prev = lax.scan(step, h0.astype(f32),
                               (jnp.moveaxis(chunk_state, 1, 0), jnp.moveaxis(chunk_decay, 1, 0)))
    h_prev = jnp.moveaxis(h_prev, 0, 1)
    y = y + jnp.einsum("bcthd,bchdv->bcthv", qc * jnp.exp(acum)[..., None], h_prev)
    return y.reshape(b, L, nh, dv), h_final


def _ssd_mixer(z, xbc, dt_raw, h0, buf, p):
    b, L, _ = z.shape
    xbc, new_buf = _causal_conv(xbc, buf, p["ssd_conv_w"], p["ssd_conv_b"])
    xbc = jax.nn.silu(xbc)
    xs, Bm, Cm = jnp.split(xbc, [D_SSD, D_SSD + SSD_GROUPS * SSD_STATE], axis=-1)
    xs = xs.reshape(b, L, SSD_HEADS, SSD_HEAD_DIM)
    rep = SSD_HEADS // SSD_GROUPS
    Bh = jnp.repeat(Bm.reshape(b, L, SSD_GROUPS, SSD_STATE), rep, axis=2)
    Ch = jnp.repeat(Cm.reshape(b, L, SSD_GROUPS, SSD_STATE), rep, axis=2)
    dt = jax.nn.softplus(dt_raw.astype(jnp.float32) + p["ssd_dt_bias"])
    A = -jnp.exp(p["ssd_a_log"].astype(jnp.float32))
    y, h_new = _chunked_decay_recurrence(Ch, Bh, xs * dt[..., None], dt * A, h0)
    y = y + p["ssd_d"][:, None] * xs
    y = y.reshape(b, L, D_SSD) * jax.nn.silu(z)
    return _group_rmsnorm(y, p["ssd_norm_w"], SSD_GROUPS), new_buf, h_new


def _lru_combine(left, right):
    a_l, u_l = left
    a_r, u_r = right
    return a_l * a_r, a_r * u_l + u_r


def _rglru_mixer(xr, gate, h0, buf, p):
    b, L, _ = xr.shape
    xc, new_buf = _causal_conv(xr, buf, p["rg_conv_w"], p["rg_conv_b"])
    xb = xc.reshape(b, L, RG_BLOCKS, RG_BLOCK_W)
    r = jax.nn.sigmoid(jnp.einsum("blnj,njk->blnk", xb, p["rg_wa"]).reshape(b, L, D_RG) + p["rg_ba"])
    i = jax.nn.sigmoid(jnp.einsum("blnj,njk->blnk", xb, p["rg_wx"]).reshape(b, L, D_RG) + p["rg_bx"])
    log_a = (-RG_C * jax.nn.softplus(-p["rg_lambda"].astype(jnp.float32))) * r.astype(jnp.float32)
    a = jnp.exp(log_a)
    u = jnp.sqrt(-jnp.expm1(2.0 * log_a)) * (i * xc).astype(jnp.float32)
    u = u.at[:, 0].add(a[:, 0] * h0.astype(jnp.float32))
    _, h = lax.associative_scan(_lru_combine, (a, u), axis=1)
    return h * jax.nn.gelu(gate), new_buf, h[:, -1]


def _retention_mixer(q, k, v, g, pos, h0, p):
    b, L, _ = q.shape
    q = _rope(q.reshape(b, L, RET_HEADS, RET_HEAD_DIM), pos)
    k = _rope(k.reshape(b, L, RET_HEADS, RET_HEAD_DIM), pos) * (RET_HEAD_DIM ** -0.5)
    v = v.reshape(b, L, RET_HEADS, RET_HEAD_DIM)
    log_gamma = jnp.log1p(-jnp.exp2(-5.0 - jnp.arange(RET_HEADS, dtype=jnp.float32)))
    log_a = jnp.broadcast_to(log_gamma, (b, L, RET_HEADS))
    y, h_new = _chunked_decay_recurrence(q, k, v, log_a, h0)
    mu = jnp.mean(y, -1, keepdims=True)
    var = jnp.mean(jnp.square(y - mu), -1, keepdims=True)
    y = ((y - mu) * lax.rsqrt(var + EPS)).reshape(b, L, D_RET) * p["ret_gn_w"]
    return jax.nn.silu(g) * y, h_new


def _moe(h, router_w, router_bias, p):
    b, L, D = h.shape
    T = b * L
    ht = h.reshape(T, D)
    scores = jax.nn.sigmoid((ht @ router_w).astype(jnp.float32))
    choice = (scores + router_bias.astype(jnp.float32)).reshape(T, N_EXPERT_GROUPS, EXPERTS_PER_GROUP)
    group_score = lax.top_k(choice, TOP_K)[0].sum(-1)
    _, g_idx = lax.top_k(group_score, 1)
    in_group = jnp.take_along_axis(choice, g_idx[:, :, None], axis=1)[:, 0]
    _, local = lax.top_k(in_group, TOP_K)
    expert_idx = g_idx * EXPERTS_PER_GROUP + local
    wts = jnp.take_along_axis(scores, expert_idx, axis=1)
    wts = wts / jnp.sum(wts, -1, keepdims=True)
    A = T * TOP_K
    flat_e = expert_idx.reshape(A)
    order = jnp.argsort(flat_e)
    sorted_e = flat_e[order]
    tok = order // TOP_K
    counts = jnp.zeros((N_EXPERTS,), jnp.int32).at[flat_e].add(1)
    padded = (counts + EXPERT_BLOCK - 1) // EXPERT_BLOCK * EXPERT_BLOCK
    pad_end = jnp.cumsum(padded)
    pad_start = pad_end - padded
    start = jnp.cumsum(counts) - counts
    dest = (pad_start - start)[sorted_e] + jnp.arange(A, dtype=jnp.int32)
    n_blocks = (A + N_EXPERTS * (EXPERT_BLOCK - 1) + EXPERT_BLOCK - 1) // EXPERT_BLOCK
    P = n_blocks * EXPERT_BLOCK
    slot_tok = jnp.full((P,), T, jnp.int32).at[dest].set(tok)
    xs = jnp.concatenate([ht, jnp.zeros((1, D), ht.dtype)], 0)[slot_tok].reshape(n_blocks, EXPERT_BLOCK, D)
    block_pos = jnp.arange(n_blocks, dtype=jnp.int32) * EXPERT_BLOCK
    block_expert = jnp.minimum(jnp.sum(pad_end[None, :] <= block_pos[:, None], axis=1), N_EXPERTS - 1)

    def expert_block(args):
        xb, e = args
        return (jax.nn.silu(xb @ p["e_gate"][e]) * (xb @ p["e_up"][e])) @ p["e_down"][e]

    ys = lax.map(expert_block, (xs, block_expert)).reshape(P, D)
    contrib = (ys[dest] * wts.reshape(A)[order][:, None]).astype(ht.dtype)
    out = jnp.zeros((T, D), ht.dtype).at[tok].add(contrib)
    return out.reshape(b, L, D)


def _layer(x, c, pos, st, p, router_w, router_bias):
    ssd_h0, ssd_buf, rg_h0, rg_buf, ret_h0 = st
    b, L, _ = x.shape
    mod = (jax.nn.silu(c) @ p["w_ada"] + p["b_ada"]).reshape(b, 6, D_MODEL)[:, :, None, :]
    sh1, sc1, g1, sh2, sc2, g2 = [mod[:, j] for j in range(6)]
    h = x * (1.0 + sc1) + sh1
    proj = h @ p["w_in"]
    sizes = [D_SSD, SSD_CONV_DIM, SSD_HEADS, D_RG, D_RG, D_RET, D_RET, D_RET, D_RET]
    z, xbc, dt_raw, rg_x, rg_gate, q, k, v, g = jnp.split(proj, np.cumsum(sizes)[:-1].tolist(), axis=-1)
    y_ssd, ssd_buf_new, ssd_h = _ssd_mixer(z, xbc, dt_raw, ssd_h0, ssd_buf, p)
    y_rg, rg_buf_new, rg_h = _rglru_mixer(rg_x, rg_gate, rg_h0, rg_buf, p)
    y_ret, ret_h = _retention_mixer(q, k, v, g, pos, ret_h0, p)
    mix = jnp.concatenate([y_ssd, y_rg, y_ret], -1).astype(x.dtype) @ p["w_out"]
    x = _layernorm(ALPHA * x + (1.0 + g1) * mix, p["ln1_g"], p["ln1_b"]).astype(x.dtype)
    h2 = x * (1.0 + sc2) + sh2
    moe = _moe(h2, router_w, router_bias, p)
    x = _layernorm(ALPHA * x + (1.0 + g2) * moe, p["ln2_g"], p["ln2_b"]).astype(x.dtype)
    return x, (ssd_h, ssd_buf_new, rg_h, rg_buf_new, ret_h)


def _trunk(x, c, pos0, states, params, router_w, router_bias):
    pos = pos0 + jnp.arange(x.shape[1], dtype=jnp.float32)
    new = []
    for l in range(DEPTH):
        x, st = _layer(x, c, pos, states[l], params[l], router_w, router_bias)
        new.append(st)
    ssd = jnp.stack([s[0] for s in new])
    ssd_conv = jnp.stack([s[1] for s in new])
    rg = jnp.stack([s[2] for s in new])
    rg_conv = jnp.stack([s[3] for s in new])
    ret = jnp.stack([s[4] for s in new])
    return x, ssd, ssd_conv, rg, rg_conv, ret


def setup_inputs(seed: int = 0) -> dict:
    key = jax.random.key(seed)
    ks = iter(jax.random.split(key, 48))
    f32 = jnp.float32

    def nrm(shape, s):
        return jax.random.normal(next(ks), shape, f32) * s

    a0 = jax.random.uniform(next(ks), (DEPTH, D_RG), f32, 0.9, 0.999)
    s = a0 ** (1.0 / RG_C)
    rg_lambda = jnp.log(s) - jnp.log1p(-s)
    dt0 = jnp.exp(jax.random.uniform(next(ks), (DEPTH, SSD_HEADS), f32, math.log(1e-3), math.log(1e-1)))
    ssd_dt_bias = dt0 + jnp.log(-jnp.expm1(-dt0))
    ssd_a_log = jnp.log(jax.random.uniform(next(ks), (DEPTH, SSD_HEADS), f32, 1.0, 16.0))
    return {
        "x_prompt": nrm((BATCH, SEQ, D_MODEL), 1.0),
        "x_sample": nrm((DEC_BATCH, DEC_SEQ, D_MODEL), 1.0),
        "c_prompt": nrm((BATCH, D_MODEL), 1.0),
        "c_sample": nrm((DEC_BATCH, D_MODEL), 1.0),
        "state_ssd": nrm((DEPTH, DEC_BATCH, SSD_HEADS, SSD_STATE, SSD_HEAD_DIM), 0.3),
        "state_ssd_conv": nrm((DEPTH, DEC_BATCH, CONV_WIDTH - 1, SSD_CONV_DIM), 1.0),
        "state_rglru": nrm((DEPTH, DEC_BATCH, D_RG), 0.5),
        "state_rglru_conv": nrm((DEPTH, DEC_BATCH, CONV_WIDTH - 1, D_RG), 1.0),
        "state_ret": nrm((DEPTH, DEC_BATCH, RET_HEADS, RET_HEAD_DIM, RET_HEAD_DIM), 0.5),
        "w_ada": nrm((DEPTH, D_MODEL, 6 * D_MODEL), 0.1 * D_MODEL ** -0.5),
        "b_ada": nrm((DEPTH, 6 * D_MODEL), 0.02),
        "w_in": nrm((DEPTH, D_MODEL, IN_DIM), D_MODEL ** -0.5),
        "ssd_conv_w": nrm((DEPTH, CONV_WIDTH, SSD_CONV_DIM), CONV_WIDTH ** -0.5),
        "ssd_conv_b": nrm((DEPTH, SSD_CONV_DIM), 0.02),
        "ssd_dt_bias": ssd_dt_bias,
        "ssd_a_log": ssd_a_log,
        "ssd_d": 1.0 + nrm((DEPTH, SSD_HEADS), 0.1),
        "ssd_norm_w": 1.0 + nrm((DEPTH, D_SSD), 0.02),
        "rg_conv_w": nrm((DEPTH, CONV_WIDTH, D_RG), CONV_WIDTH ** -0.5),
        "rg_conv_b": nrm((DEPTH, D_RG), 0.02),
        "rg_wa": nrm((DEPTH, RG_BLOCKS, RG_BLOCK_W, RG_BLOCK_W), RG_BLOCK_W ** -0.5),
        "rg_ba": nrm((DEPTH, D_RG), 0.02),
        "rg_wx": nrm((DEPTH, RG_BLOCKS, RG_BLOCK_W, RG_BLOCK_W), RG_BLOCK_W ** -0.5),
        "rg_bx": nrm((DEPTH, D_RG), 0.02),
        "rg_lambda": rg_lambda,
        "ret_gn_w": 1.0 + nrm((DEPTH, D_RET), 0.02),
        "w_out": nrm((DEPTH, D_MIX, D_MODEL), BETA * D_MIX ** -0.5),
        "ln1_g": 1.0 + nrm((DEPTH, D_MODEL), 0.02),
        "ln1_b": nrm((DEPTH, D_MODEL), 0.02),
        "router_w": nrm((D_MODEL, N_EXPERTS), D_MODEL ** -0.5),
        "router_bias": nrm((N_EXPERTS,), 0.01),
        "exp_w_gate": nrm((DEPTH, N_EXPERTS, D_MODEL, D_EXPERT), D_MODEL ** -0.5),
        "exp_w_up": nrm((DEPTH, N_EXPERTS, D_MODEL, D_EXPERT), D_MODEL ** -0.5),
        "exp_w_down": nrm((DEPTH, N_EXPERTS, D_EXPERT, D_MODEL), BETA * D_EXPERT ** -0.5),
        "ln2_g": 1.0 + nrm((DEPTH, D_MODEL), 0.02),
        "ln2_b": nrm((DEPTH, D_MODEL), 0.02),
    }


def reference(x_prompt, x_sample, c_prompt, c_sample, state_ssd, state_ssd_conv, state_rglru,
              state_rglru_conv, state_ret, w_ada, b_ada, w_in, ssd_conv_w, ssd_conv_b, ssd_dt_bias,
              ssd_a_log, ssd_d, ssd_norm_w, rg_conv_w, rg_conv_b, rg_wa, rg_ba, rg_wx, rg_bx,
              rg_lambda, ret_gn_w, w_out, ln1_g, ln1_b, router_w, router_bias, exp_w_gate, exp_w_up,
              exp_w_down, ln2_g, ln2_b):
    params = [dict(w_ada=w_ada[l], b_ada=b_ada[l], w_in=w_in[l], ssd_conv_w=ssd_conv_w[l],
                   ssd_conv_b=ssd_conv_b[l], ssd_dt_bias=ssd_dt_bias[l], ssd_a_log=ssd_a_log[l],
                   ssd_d=ssd_d[l], ssd_norm_w=ssd_norm_w[l], rg_conv_w=rg_conv_w[l],
                   rg_conv_b=rg_conv_b[l], rg_wa=rg_wa[l], rg_ba=rg_ba[l], rg_wx=rg_wx[l],
                   rg_bx=rg_bx[l], rg_lambda=rg_lambda[l], ret_gn_w=ret_gn_w[l], w_out=w_out[l],
                   ln1_g=ln1_g[l], ln1_b=ln1_b[l], e_gate=exp_w_gate[l], e_up=exp_w_up[l],
                   e_down=exp_w_down[l], ln2_g=ln2_g[l], ln2_b=ln2_b[l]) for l in range(DEPTH)]
    dt = x_prompt.dtype
    b = x_prompt.shape[0]
    zero_states = [(jnp.zeros((b, SSD_HEADS, SSD_STATE, SSD_HEAD_DIM), dt),
                    jnp.zeros((b, CONV_WIDTH - 1, SSD_CONV_DIM), dt),
                    jnp.zeros((b, D_RG), dt),
                    jnp.zeros((b, CONV_WIDTH - 1, D_RG), dt),
                    jnp.zeros((b, RET_HEADS, RET_HEAD_DIM, RET_HEAD_DIM), dt)) for _ in range(DEPTH)]
    past_states = [(state_ssd[l], state_ssd_conv[l], state_rglru[l], state_rglru_conv[l], state_ret[l])
                   for l in range(DEPTH)]
    y_prompt, ssd_p, ssd_conv_p, rg_p, rg_conv_p, ret_p = _trunk(
        x_prompt, c_prompt, 0.0, zero_states, params, router_w, router_bias)
    y_sample, ssd_s, ssd_conv_s, rg_s, rg_conv_s, ret_s = _trunk(
        x_sample, c_sample, float(PAST_LEN), past_states, params, router_w, router_bias)
    return (y_prompt, y_sample, ssd_p, ssd_conv_p, rg_p, rg_conv_p, ret_p,
            ssd_s, ssd_conv_s, rg_s, rg_conv_s, ret_s)
```

```python
import functools
import math

import jax
import jax.numpy as jnp
from jax import lax
from jax.experimental import pallas as pl
from jax.experimental.pallas import tpu as pltpu

F32 = jnp.float32
BF16 = jnp.bfloat16

D_MODEL = 1024
DEPTH = 4
D_SSD = 1024
SSD_HEAD_DIM = 64
SSD_HEADS = 16
SSD_GROUPS = 2
SSD_STATE = 128
SSD_CONV_DIM = D_SSD + 2 * SSD_GROUPS * SSD_STATE
CONV_WIDTH = 4
D_RG = 512
RG_BLOCKS = 8
RG_C = 8.0
D_RET = 512
RET_HEADS = 4
RET_HEAD_DIM = 128
ROPE_BASE = 10000.0
N_EXPERTS = 32
EXPERTS_PER_GROUP = 8
N_EXPERT_GROUPS = 4
D_EXPERT = 512
ALPHA = (2 * DEPTH) ** 0.25
EPS = 1e-5
PAST_LEN = 16384.0

LANES = 128
SUBLANES = 8
CHUNK = 128
TAIL = CHUNK
PROJ_TM = 384
MOE_BM = 256
VMEM_LIMIT = 56 * 1024 * 1024


def _cparams(sem):
    return pltpu.CompilerParams(dimension_semantics=sem, vmem_limit_bytes=VMEM_LIMIT)


def _pick_tm(L, cap):
    if L <= LANES:
        return L
    tm = cap - cap % LANES
    while L % tm:
        tm -= LANES
    return tm


def _sigmoid(x):
    return 1.0 / (1.0 + jnp.exp(-x))


def _silu(x):
    return x * _sigmoid(x)


def _softplus(x):
    return jnp.maximum(x, 0.0) + jnp.log1p(jnp.exp(-jnp.abs(x)))


def _split3(v):
    hi = v.astype(BF16)
    r = v - hi.astype(F32)
    mid = r.astype(BF16)
    lo = (r - mid.astype(F32)).astype(BF16)
    return hi, mid, lo


def _dot(a, b):
    return jnp.dot(a, b, preferred_element_type=F32)


def _split2(v):
    hi = v.astype(BF16)
    return hi, (v - hi.astype(F32)).astype(BF16)


def _mm_w(a, w_hi, w_lo):
    if w_lo is None:
        return _dot(a.astype(BF16), w_hi)
    ah, al = _split2(a)
    return _dot(ah, w_hi) + (_dot(al, w_hi) + _dot(ah, w_lo))


def _mm_a(a, b, precise, nt=False):
    dn = (((1,), (1,)), ((), ())) if nt else (((1,), (0,)), ((), ()))

    def d(x, y):
        return lax.dot_general(x, y, dn, preferred_element_type=F32)

    if not precise:
        return d(a.astype(BF16), b.astype(BF16))
    ah, al = _split2(a)
    bh, bl = _split2(b)
    return d(ah, bh) + (d(al, bh) + d(ah, bl))


def _hi_lo(w):
    hi32 = lax.reduce_precision(w, exponent_bits=8, mantissa_bits=7)
    return hi32.astype(BF16), (w - hi32).astype(BF16)


def _dot_exact_rhs(v, m):
    hi, mid, lo = _split3(v)
    return _dot(hi, m) + _dot(mid, m) + _dot(lo, m)


def _dot_exact_lhs(m, v):
    hi, mid, lo = _split3(v)
    return _dot(m, hi) + _dot(m, mid) + _dot(m, lo)


def _ada_kernel(c_ref, w_ref, b_ref, o_ref):
    o_ref[0] = _mm_a(_silu(c_ref[...]), w_ref[0], True) + b_ref[0]


def _ada(c_all, w_ada, b_ada):
    n = c_all.shape[0]
    tn = 1024
    return pl.pallas_call(
        _ada_kernel,
        grid=(DEPTH, 6 * D_MODEL // tn),
        in_specs=[
            pl.BlockSpec((n, D_MODEL), lambda l, j: (0, 0)),
            pl.BlockSpec((1, D_MODEL, tn), lambda l, j: (l, 0, j)),
            pl.BlockSpec((1, 1, tn), lambda l, j: (l, 0, j)),
        ],
        out_specs=pl.BlockSpec((1, n, tn), lambda l, j: (l, 0, j)),
        out_shape=jax.ShapeDtypeStruct((DEPTH, n, 6 * D_MODEL), F32),
        compiler_params=_cparams(("arbitrary", "arbitrary")),
        name="ada",
    )(c_all, w_ada, b_ada.reshape(DEPTH, 1, 6 * D_MODEL))


_IN_SEGS = (("z", 0, 1024), ("xbc", 1024, 2560), ("rg", 2560, 3584), ("ret", 3584, 5632), ("dt", 5632, 5760))
IN_PAD = 5760


def _inproj_kernel(x_ref, sc_ref, sh_ref, w_ref, *rest, precise):
    wl_ref = rest[0] if precise else None
    outs = rest[1:] if precise else rest
    h = x_ref[0] * (1.0 + sc_ref[0]) + sh_ref[0]
    if precise:
        hh, hl = _split2(h)
    else:
        hh = h.astype(BF16)
    for o_ref, (_, lo, hi) in zip(outs, _IN_SEGS):
        acc = _dot(hh, w_ref[:, lo:hi])
        if precise:
            acc = acc + (_dot(hl, w_ref[:, lo:hi]) + _dot(hh, wl_ref[:, lo:hi]))
        o_ref[0] = acc


def _inproj(x, sc, sh, w, per_token_mod, precise):
    B, L, _ = x.shape
    tm = _pick_tm(L, LANES if precise else PROJ_TM)
    if per_token_mod:
        mod_spec = pl.BlockSpec((1, tm, D_MODEL), lambda b, i: (b, i, 0))
    else:
        mod_spec = pl.BlockSpec((1, 1, D_MODEL), lambda b, i: (b, 0, 0))
    widths = [hi - lo for _, lo, hi in _IN_SEGS]
    w_spec = pl.BlockSpec((D_MODEL, IN_PAD), lambda b, i: (0, 0), pipeline_mode=pl.Buffered(1))
    ws = w if precise else w[:1]
    return pl.pallas_call(
        functools.partial(_inproj_kernel, precise=precise),
        grid=(B, L // tm),
        in_specs=[
            pl.BlockSpec((1, tm, D_MODEL), lambda b, i: (b, i, 0)),
            mod_spec,
            mod_spec,
        ] + [w_spec] * len(ws),
        out_specs=[pl.BlockSpec((1, tm, wd), lambda b, i: (b, i, 0)) for wd in widths],
        out_shape=[jax.ShapeDtypeStruct((B, L, wd), F32) for wd in widths],
        compiler_params=_cparams(("arbitrary", "arbitrary")),
        name="inproj",
    )(x, sc, sh, *ws)


def _conv_step(src_rows, buf_ref, cw_ref, cb_ref, nbuf_ref, xp_s, *, c, nc, Lv, Lp, has_state):
    @pl.when(c == 0)
    def _():
        xp_s[...] = jnp.zeros_like(xp_s)
        if has_state:
            xp_s[SUBLANES - (CONV_WIDTH - 1):SUBLANES, :] = buf_ref[0]

    xp_s[SUBLANES:SUBLANES + Lv, :] = src_rows
    base = SUBLANES - (CONV_WIDTH - 1)
    xc = cb_ref[...]
    for j in range(CONV_WIDTH):
        xc = xc + cw_ref[j:j + 1, :] * xp_s[base + j:base + j + Lp, :]

    @pl.when(c == nc - 1)
    def _():
        nbuf_ref[0] = xp_s[SUBLANES + Lv - (CONV_WIDTH - 1):SUBLANES + Lv, :]

    if nc > 1:
        xp_s[0:SUBLANES, :] = xp_s[Lp:Lp + SUBLANES, :]
    return xc


def _ssd_kernel(z_ref, xbc_ref, dt_ref, h0_ref, buf_ref, cw_ref, cb_ref, dtb_ref, alog_ref, dch_ref,
                nw_ref, e_ref, tri_ref, y_ref, nbuf_ref, hn_ref, xp_s, dtp_s, s_s,
                *, Lv, Lp, nc, has_state, precise):
    c = pl.program_id(1)

    @pl.when(c == 0)
    def _():
        if has_state:
            s_s[...] = h0_ref[0]
        else:
            s_s[...] = jnp.zeros_like(s_s)
        if Lv < Lp:
            dtp_s[...] = jnp.zeros_like(dtp_s)

    xc = _conv_step(xbc_ref[0], buf_ref, cw_ref, cb_ref, nbuf_ref, xp_s,
                    c=c, nc=nc, Lv=Lv, Lp=Lp, has_state=has_state)
    xbc = _silu(xc)
    xs = xbc[:, 0:D_SSD]
    Bm = xbc[:, D_SSD:D_SSD + 256]
    Cm = xbc[:, D_SSD + 256:D_SSD + 512]

    if Lv < Lp:
        dtp_s[0:Lv, :] = dt_ref[0]
        dtr = dtp_s[...]
    else:
        dtr = dt_ref[0]
    dt = _softplus(dtr + dtb_ref[...])
    if Lv < Lp:
        rowv = lax.broadcasted_iota(jnp.int32, (Lp, LANES), 0)
        dt = jnp.where(rowv < Lv, dt, 0.0)
    a = dt * (-jnp.exp(alog_ref[...]))
    acum = _dot_exact_lhs(tri_ref[...], a)
    dt_e = _dot_exact_rhs(dt, e_ref[...])
    ac_e = _dot_exact_rhs(acum, e_ref[...])
    acl_e = ac_e[Lp - 1:Lp, :]
    acum_t = acum.T

    vdt = xs * dt_e
    lane = lax.broadcasted_iota(jnp.int32, (Lp, LANES), 1)
    causal = lax.broadcasted_iota(jnp.int32, (Lp, Lp), 0) >= lax.broadcasted_iota(jnp.int32, (Lp, Lp), 1)
    mid_t = F32 if precise else BF16

    y_blocks = []
    for g in range(SSD_GROUPS):
        Cg = Cm[:, g * SSD_STATE:(g + 1) * SSD_STATE]
        Bg = Bm[:, g * SSD_STATE:(g + 1) * SSD_STATE]
        G = _mm_a(Cg, Bg, precise, nt=True)
        for jj in range(4):
            j = 4 * g + jj
            scs = []
            for h in (2 * j, 2 * j + 1):
                col = jnp.sum(jnp.where(lane == h, acum, 0.0), axis=1, keepdims=True)
                rw = acum_t[h:h + 1, :]
                dec = jnp.exp(jnp.where(causal, col - rw, -jnp.inf))
                scs.append((G * dec).astype(mid_t))
            sc = jnp.concatenate(scs, axis=1)
            vb = vdt[:, j * LANES:(j + 1) * LANES]
            v2 = jnp.concatenate([jnp.where(lane < SSD_HEAD_DIM, vb, 0.0),
                                  jnp.where(lane >= SSD_HEAD_DIM, vb, 0.0)], axis=0).astype(mid_t)
            y_blocks.append(_mm_a(sc, v2, precise))
    y = jnp.concatenate(y_blocks, axis=1)

    eac = jnp.exp(ac_e)
    vw = vdt * jnp.exp(acl_e - ac_e)
    cd = jnp.exp(acl_e)
    half = D_SSD // SSD_GROUPS
    ys_parts = []
    for g in range(SSD_GROUPS):
        Sg = s_s[g]
        ys_parts.append(_mm_a(Cm[:, g * SSD_STATE:(g + 1) * SSD_STATE], Sg, precise))
        BgT = Bm[:, g * SSD_STATE:(g + 1) * SSD_STATE].T
        s_s[g] = cd[:, g * half:(g + 1) * half] * Sg + _mm_a(BgT, vw[:, g * half:(g + 1) * half], precise)
    y = y + eac * jnp.concatenate(ys_parts, axis=1) + dch_ref[...] * xs

    yv = y[0:Lv, :] if Lv < Lp else y
    yz = yv * _silu(z_ref[0])
    outs = []
    for g in range(SSD_GROUPS):
        part = yz[:, g * half:(g + 1) * half]
        ms = jnp.mean(part * part, axis=1, keepdims=True)
        outs.append(part * lax.rsqrt(ms + EPS))
    y_ref[0] = jnp.concatenate(outs, axis=1) * nw_ref[...]

    @pl.when(c == nc - 1)
    def _():
        hn_ref[0] = s_s[...]


def _ssd(z, xbc, dt, h0, buf, p, consts, has_state, precise):
    B, L, _ = z.shape
    Lv = min(CHUNK, L)
    Lp = CHUNK
    nc = L // Lv
    kern = functools.partial(_ssd_kernel, Lv=Lv, Lp=Lp, nc=nc, has_state=has_state, precise=precise)
    row = lambda b, c: (b, c, 0)
    per_b3 = lambda b, c: (b, 0, 0)
    per_b4 = lambda b, c: (b, 0, 0, 0)
    fixed = lambda b, c: (0, 0)
    half = D_SSD // SSD_GROUPS
    return pl.pallas_call(
        kern,
        grid=(B, nc),
        in_specs=[
            pl.BlockSpec((1, Lv, D_SSD), row),
            pl.BlockSpec((1, Lv, SSD_CONV_DIM), row),
            pl.BlockSpec((1, Lv, LANES), row),
            pl.BlockSpec((1, SSD_GROUPS, SSD_STATE, half), per_b4),
            pl.BlockSpec((1, CONV_WIDTH - 1, SSD_CONV_DIM), per_b3),
            pl.BlockSpec((CONV_WIDTH, SSD_CONV_DIM), fixed),
            pl.BlockSpec((1, SSD_CONV_DIM), fixed),
            pl.BlockSpec((1, LANES), fixed),
            pl.BlockSpec((1, LANES), fixed),
            pl.BlockSpec((1, D_SSD), fixed),
            pl.BlockSpec((1, D_SSD), fixed),
            pl.BlockSpec((LANES, D_SSD), fixed),
            pl.BlockSpec((Lp, Lp), fixed),
        ],
        out_specs=[
            pl.BlockSpec((1, Lv, D_SSD), row),
            pl.BlockSpec((1, CONV_WIDTH - 1, SSD_CONV_DIM), per_b3),
            pl.BlockSpec((1, SSD_GROUPS, SSD_STATE, half), per_b4),
        ],
        out_shape=[
            jax.ShapeDtypeStruct((B, L, D_SSD), F32),
            jax.ShapeDtypeStruct((B, CONV_WIDTH - 1, SSD_CONV_DIM), F32),
            jax.ShapeDtypeStruct((B, SSD_GROUPS, SSD_STATE, half), F32),
        ],
        scratch_shapes=[
            pltpu.VMEM((SUBLANES + Lp, SSD_CONV_DIM), F32),
            pltpu.VMEM((Lp, LANES), F32),
            pltpu.VMEM((SSD_GROUPS, SSD_STATE, half), F32),
        ],
        compiler_params=_cparams(("arbitrary", "arbitrary")),
        name="ssd",
    )(z, xbc, dt, h0, buf, p["ssd_conv_w"], p["ssd_conv_b"], p["ssd_dt_bias"], p["ssd_a_log"],
      p["ssd_d_ch"], p["ssd_norm_w"], consts["head_expand"], consts["tri"])


def _rg_kernel(rg_ref, h0_ref, buf_ref, cw_ref, cb_ref, wa_ref, ba_ref, wx_ref, bx_ref, lam_ref,
               y_ref, nbuf_ref, hn_ref, xp_s, hc_s, *, Lv, Lp, nc, has_state, precise):
    c = pl.program_id(1)

    @pl.when(c == 0)
    def _():
        if has_state:
            hc_s[...] = h0_ref[0]
        else:
            hc_s[...] = jnp.zeros_like(hc_s)

    xc = _conv_step(rg_ref[0, :, 0:D_RG], buf_ref, cw_ref, cb_ref, nbuf_ref, xp_s,
                    c=c, nc=nc, Lv=Lv, Lp=Lp, has_state=has_state)
    r = _sigmoid(_mm_w(xc, wa_ref[0], wa_ref[1] if precise else None) + ba_ref[...])
    i = _sigmoid(_mm_w(xc, wx_ref[0], wx_ref[1] if precise else None) + bx_ref[...])
    la = (-RG_C * _softplus(-lam_ref[...])) * r
    u = jnp.sqrt(-jnp.tanh(la) * (jnp.exp(2.0 * la) + 1.0)) * (i * xc)
    row = lax.broadcasted_iota(jnp.int32, (Lp, D_RG), 0)
    if Lv < Lp:
        valid = row < Lv
        la = jnp.where(valid, la, 0.0)
        u = jnp.where(valid, u, 0.0)
    a = jnp.exp(la)
    s = 1
    while s < Lp:
        m = row >= s
        u_sh = jnp.where(m, pltpu.roll(u, s, 0), 0.0)
        a_sh = jnp.where(m, pltpu.roll(a, s, 0), 1.0)
        u = u + a * u_sh
        a = a * a_sh
        s *= 2
    h = u + a * hc_s[...]
    hc_s[...] = h[Lp - 1:Lp, :]
    hv = h[0:Lv, :] if Lv < Lp else h
    y_ref[0] = hv * jax.nn.gelu(rg_ref[0, :, D_RG:2 * D_RG])

    @pl.when(c == nc - 1)
    def _():
        hn_ref[0] = hc_s[...]


def _rg(rg, h0, buf, p, has_state, precise):
    B, L, _ = rg.shape
    Lv = min(CHUNK, L)
    Lp = CHUNK
    nc = L // Lv
    kern = functools.partial(_rg_kernel, Lv=Lv, Lp=Lp, nc=nc, has_state=has_state, precise=precise)
    row = lambda b, c: (b, c, 0)
    per_b3 = lambda b, c: (b, 0, 0)
    fixed = lambda b, c: (0, 0)
    fixed3 = lambda b, c: (0, 0, 0)
    return pl.pallas_call(
        kern,
        grid=(B, nc),
        in_specs=[
            pl.BlockSpec((1, Lv, 2 * D_RG), row),
            pl.BlockSpec((1, 1, D_RG), per_b3),
            pl.BlockSpec((1, CONV_WIDTH - 1, D_RG), per_b3),
            pl.BlockSpec((CONV_WIDTH, D_RG), fixed),
            pl.BlockSpec((1, D_RG), fixed),
            pl.BlockSpec((2, D_RG, D_RG), fixed3),
            pl.BlockSpec((1, D_RG), fixed),
            pl.BlockSpec((2, D_RG, D_RG), fixed3),
            pl.BlockSpec((1, D_RG), fixed),
            pl.BlockSpec((1, D_RG), fixed),
        ],
        out_specs=[
            pl.BlockSpec((1, Lv, D_RG), row),
            pl.BlockSpec((1, CONV_WIDTH - 1, D_RG), per_b3),
            pl.BlockSpec((1, 1, D_RG), per_b3),
        ],
        out_shape=[
            jax.ShapeDtypeStruct((B, L, D_RG), F32),
            jax.ShapeDtypeStruct((B, CONV_WIDTH - 1, D_RG), F32),
            jax.ShapeDtypeStruct((B, 1, D_RG), F32),
        ],
        scratch_shapes=[
            pltpu.VMEM((SUBLANES + Lp, D_RG), F32),
            pltpu.VMEM((1, D_RG), F32),
        ],
        compiler_params=_cparams(("arbitrary", "arbitrary")),
        name="rglru",
    )(rg, h0, buf, p["rg_conv_w"], p["rg_conv_b"], p["rg_wa_bd"], p["rg_ba"], p["rg_wx_bd"],
      p["rg_bx"], p["rg_lambda"])


def _ret_kernel(ret_ref, h0_ref, cos_ref, sin_ref, dm_ref, ea_ref, te_ref, cd_ref, gn_ref,
                y_ref, hn_ref, pad_s, s_s, *, Lv, Lp, nc, has_state, precise):
    c = pl.program_id(1)

    @pl.when(c == 0)
    def _():
        if has_state:
            s_s[...] = h0_ref[0]
        else:
            s_s[...] = jnp.zeros_like(s_s)
        if Lv < Lp:
            pad_s[...] = jnp.zeros_like(pad_s)

    if Lv < Lp:
        pad_s[0:Lv, :] = ret_ref[0, :, 0:3 * D_RET]
        qkv = pad_s[...]
    else:
        qkv = ret_ref[0, :, 0:3 * D_RET]
    cosf = cos_ref[...]
    sinf = sin_ref[...]
    outs = []
    for h in range(RET_HEADS):
        sl = slice(h * RET_HEAD_DIM, (h + 1) * RET_HEAD_DIM)
        q = qkv[:, sl]
        k = qkv[:, D_RET + h * RET_HEAD_DIM:D_RET + (h + 1) * RET_HEAD_DIM]
        v = qkv[:, 2 * D_RET + h * RET_HEAD_DIM:2 * D_RET + (h + 1) * RET_HEAD_DIM]
        q = q * cosf + pltpu.roll(q, RET_HEAD_DIM // 2, 1) * sinf
        k = (k * cosf + pltpu.roll(k, RET_HEAD_DIM // 2, 1) * sinf) * (RET_HEAD_DIM ** -0.5)
        G = _mm_a(q, k, precise, nt=True)
        S = s_s[h]
        y = _mm_a(G * dm_ref[h], v, precise) + _mm_a(q * ea_ref[h], S, precise)
        s_s[h] = cd_ref[h] * S + _mm_a((k * te_ref[h]).T, v, precise)
        yv = y[0:Lv, :] if Lv < Lp else y
        mu = jnp.mean(yv, axis=1, keepdims=True)
        d = yv - mu
        var = jnp.mean(d * d, axis=1, keepdims=True)
        outs.append(d * lax.rsqrt(var + EPS))
    yn = jnp.concatenate(outs, axis=1) * gn_ref[...]
    y_ref[0] = _silu(ret_ref[0, :, 3 * D_RET:4 * D_RET]) * yn

    @pl.when(c == nc - 1)
    def _():
        hn_ref[0] = s_s[...]


def _ret(ret, h0, p, rc, has_state, precise):
    B, L, _ = ret.shape
    Lv = min(CHUNK, L)
    Lp = CHUNK
    nc = L // Lv
    kern = functools.partial(_ret_kernel, Lv=Lv, Lp=Lp, nc=nc, has_state=has_state, precise=precise)
    row = lambda b, c: (b, c, 0)
    per_b4 = lambda b, c: (b, 0, 0, 0)
    fixed2 = lambda b, c: (0, 0)
    fixed3 = lambda b, c: (0, 0, 0)
    hd = RET_HEAD_DIM
    return pl.pallas_call(
        kern,
        grid=(B, nc),
        in_specs=[
            pl.BlockSpec((1, Lv, 4 * D_RET), row),
            pl.BlockSpec((1, RET_HEADS, hd, hd), per_b4),
            pl.BlockSpec((Lp, hd), lambda b, c: (c, 0)),
            pl.BlockSpec((Lp, hd), lambda b, c: (c, 0)),
            pl.BlockSpec((RET_HEADS, Lp, Lp), fixed3),
            pl.BlockSpec((RET_HEADS, Lp, hd), fixed3),
            pl.BlockSpec((RET_HEADS, Lp, hd), fixed3),
            pl.BlockSpec((RET_HEADS, hd, hd), fixed3),
            pl.BlockSpec((1, D_RET), fixed2),
        ],
        out_specs=[
            pl.BlockSpec((1, Lv, D_RET), row),
            pl.BlockSpec((1, RET_HEADS, hd, hd), per_b4),
        ],
        out_shape=[
            jax.ShapeDtypeStruct((B, L, D_RET), F32),
            jax.ShapeDtypeStruct((B, RET_HEADS, hd, hd), F32),
        ],
        scratch_shapes=[
            pltpu.VMEM((Lp, 3 * D_RET), F32),
            pltpu.VMEM((RET_HEADS, hd, hd), F32),
        ],
        compiler_params=_cparams(("arbitrary", "arbitrary")),
        name="retention",
    )(ret, h0, rc["cos"], rc["sin"], rc["dmat"], rc["eacum"], rc["toend"], rc["cdec"], p["ret_gn_w"])


def _ret_consts(L, pos0):
    Lv = min(CHUNK, L)
    Lp = CHUNK
    nc = L // Lv
    half = RET_HEAD_DIM // 2
    inv = ROPE_BASE ** (-jnp.arange(half, dtype=F32) / half)
    pos = pos0 + jnp.arange(L, dtype=F32)
    ang = pos[:, None] * inv[None, :]
    cos = jnp.cos(ang)
    sin = jnp.sin(ang)
    cosf = jnp.concatenate([cos, cos], axis=1)
    sinf = jnp.concatenate([-sin, sin], axis=1)
    if Lv < Lp:
        cosf = jnp.pad(cosf, ((0, Lp - Lv), (0, 0)))
        sinf = jnp.pad(sinf, ((0, Lp - Lv), (0, 0)))
    assert cosf.shape[0] == nc * Lp
    log_gamma = jnp.log1p(-jnp.exp2(-5.0 - jnp.arange(RET_HEADS, dtype=F32)))
    steps = jnp.minimum(jnp.arange(Lp) + 1, Lv).astype(F32)
    acum = log_gamma[:, None] * steps[None, :]
    causal = jnp.tril(jnp.ones((Lp, Lp), bool))
    dmat = jnp.exp(jnp.where(causal[None], acum[:, :, None] - acum[:, None, :], -jnp.inf))
    ones = jnp.ones((RET_HEADS, Lp, RET_HEAD_DIM), F32)
    eacum = jnp.exp(acum)[:, :, None] * ones
    toend = jnp.exp(acum[:, -1:] - acum)[:, :, None] * ones
    rowvalid = (jnp.arange(Lp) < Lv).astype(F32)[None, :, None]
    toend = toend * rowvalid
    cdec = jnp.exp(acum[:, -1])[:, None, None] * jnp.ones((RET_HEADS, RET_HEAD_DIM, RET_HEAD_DIM), F32)
    return dict(cos=cosf, sin=sinf, dmat=dmat, eacum=eacum, toend=toend, cdec=cdec)


def _layernorm(v, g, b):
    mu = jnp.mean(v, axis=1, keepdims=True)
    d = v - mu
    var = jnp.mean(d * d, axis=1, keepdims=True)
    return d * lax.rsqrt(var + EPS) * g + b


def _route(logits, bias):
    tm = logits.shape[0]
    lane = lax.broadcasted_iota(jnp.int32, (tm, LANES), 1)
    lanef = lane.astype(F32)
    scores = _sigmoid(logits)
    choice = scores + bias
    neg = -jnp.inf
    best = jnp.full((tm, 1), neg, F32)
    e1 = jnp.zeros((tm, 1), F32)
    e2 = jnp.zeros((tm, 1), F32)
    for g in range(N_EXPERT_GROUPS):
        ing = (lane >= g * EXPERTS_PER_GROUP) & (lane < (g + 1) * EXPERTS_PER_GROUP)
        cg = jnp.where(ing, choice, neg)
        m1 = jnp.max(cg, axis=1, keepdims=True)
        i1 = jnp.min(jnp.where(cg == m1, lanef, float(LANES)), axis=1, keepdims=True)
        cg2 = jnp.where(lanef == i1, neg, cg)
        m2 = jnp.max(cg2, axis=1, keepdims=True)
        i2 = jnp.min(jnp.where(cg2 == m2, lanef, float(LANES)), axis=1, keepdims=True)
        gs = m1 + m2
        better = gs > best
        best = jnp.where(better, gs, best)
        e1 = jnp.where(better, i1, e1)
        e2 = jnp.where(better, i2, e2)
    w1 = jnp.sum(jnp.where(lanef == e1, scores, 0.0), axis=1, keepdims=True)
    w2 = jnp.sum(jnp.where(lanef == e2, scores, 0.0), axis=1, keepdims=True)
    den = w1 + w2
    out = jnp.where(lane == 0, e1, jnp.where(lane == 1, e2, jnp.where(lane == 2, w1 / den, jnp.where(lane == 3, w2 / den, 0.0))))
    return out


def _outproj_kernel(ys_ref, yr_ref, yt_ref, x_ref, g1_ref, sc2_ref, sh2_ref, w_ref, lng_ref, lnb_ref,
                    rwh_ref, rwl_ref, rb_ref, x1_ref, h2_ref, route_ref, *, precise):
    def part(y_ref, lo, hi):
        return _mm_w(y_ref[0], w_ref[0, lo:hi, :], w_ref[1, lo:hi, :] if precise else None)

    mix = part(ys_ref, 0, D_SSD) + part(yr_ref, D_SSD, D_SSD + D_RG) + part(yt_ref, D_SSD + D_RG, 2 * D_MODEL)
    x1 = _layernorm(ALPHA * x_ref[0] + (1.0 + g1_ref[0]) * mix, lng_ref[...], lnb_ref[...])
    x1_ref[0] = x1
    h2 = x1 * (1.0 + sc2_ref[0]) + sh2_ref[0]
    hi, lo = _split2(h2)
    h2_ref[0] = h2 if precise else hi
    logits = _dot(hi, rwh_ref[...]) + (_dot(lo, rwh_ref[...]) + _dot(hi, rwl_ref[...]))
    route_ref[0] = _route(logits, rb_ref[...])


def _outproj(y_ssd, y_rg, y_ret, x, g1, sc2, sh2, p, consts, per_token_mod, precise):
    B, L, _ = x.shape
    tm = _pick_tm(L, LANES if precise else PROJ_TM)
    row = lambda b, i: (b, i, 0)
    fixed = lambda b, i: (0, 0)
    if per_token_mod:
        mod_spec = pl.BlockSpec((1, tm, D_MODEL), row)
    else:
        mod_spec = pl.BlockSpec((1, 1, D_MODEL), lambda b, i: (b, 0, 0))
    return pl.pallas_call(
        functools.partial(_outproj_kernel, precise=precise),
        grid=(B, L // tm),
        in_specs=[
            pl.BlockSpec((1, tm, D_SSD), row),
            pl.BlockSpec((1, tm, D_RG), row),
            pl.BlockSpec((1, tm, D_RET), row),
            pl.BlockSpec((1, tm, D_MODEL), row),
            mod_spec, mod_spec, mod_spec,
            pl.BlockSpec((2, 2 * D_MODEL, D_MODEL), lambda b, i: (0, 0, 0)),
            pl.BlockSpec((1, D_MODEL), fixed),
            pl.BlockSpec((1, D_MODEL), fixed),
            pl.BlockSpec((D_MODEL, LANES), fixed),
            pl.BlockSpec((D_MODEL, LANES), fixed),
            pl.BlockSpec((1, LANES), fixed),
        ],
        out_specs=[
            pl.BlockSpec((1, tm, D_MODEL), row),
            pl.BlockSpec((1, tm, D_MODEL), row),
            pl.BlockSpec((1, tm, LANES), row),
        ],
        out_shape=[
            jax.ShapeDtypeStruct((B, L, D_MODEL), F32),
            jax.ShapeDtypeStruct((B, L, D_MODEL), F32 if precise else BF16),
            jax.ShapeDtypeStruct((B, L, LANES), F32),
        ],
        compiler_params=_cparams(("arbitrary", "arbitrary")),
        name="outproj",
    )(y_ssd, y_rg, y_ret, x, g1, sc2, sh2, p["w_out"], p["ln1_g"], p["ln1_b"],
      consts["rw_hi"], consts["rw_lo"], consts["rbias"])


def _moe_kernel(be_ref, nu_ref, x_ref, *rest, precise):
    o_ref = rest[-1]
    wg, wu, wd = rest[0], rest[1], rest[2]
    wgl, wul, wdl = (rest[3], rest[4], rest[5]) if precise else (None, None, None)
    i = pl.program_id(0)

    @pl.when(i < nu_ref[0])
    def _():
        x = x_ref[...]
        g = _mm_w(x, wg[0], wgl[0] if precise else None)
        u = _mm_w(x, wu[0], wul[0] if precise else None)
        o_ref[...] = _mm_w(_silu(g) * u, wd[0], wdl[0] if precise else None)

    @pl.when(i >= nu_ref[0])
    def _():
        o_ref[...] = jnp.zeros_like(o_ref)


def _moe(xs, block_expert, n_used, weights, precise):
    P = xs.shape[0]
    nblk = P // MOE_BM
    in_w = pl.BlockSpec((1, D_MODEL, D_EXPERT), lambda i, be, nu: (be[i], 0, 0))
    out_w = pl.BlockSpec((1, D_EXPERT, D_MODEL), lambda i, be, nu: (be[i], 0, 0))
    grid_spec = pltpu.PrefetchScalarGridSpec(
        num_scalar_prefetch=2,
        grid=(nblk,),
        in_specs=[pl.BlockSpec((MOE_BM, D_MODEL), lambda i, be, nu: (i, 0))]
        + [in_w, in_w, out_w] * (2 if precise else 1),
        out_specs=pl.BlockSpec((MOE_BM, D_MODEL), lambda i, be, nu: (i, 0)),
    )
    return pl.pallas_call(
        functools.partial(_moe_kernel, precise=precise),
        grid_spec=grid_spec,
        out_shape=jax.ShapeDtypeStruct((P, D_MODEL), F32),
        compiler_params=_cparams(("arbitrary",)),
        name="moe",
    )(block_expert, n_used, xs, *weights)


def _combine_kernel(x1_ref, ya_ref, yb_ref, route_ref, g2_ref, lng_ref, lnb_ref, o_ref):
    tm = x1_ref.shape[1]
    lane = lax.broadcasted_iota(jnp.int32, (tm, LANES), 1)
    rt = route_ref[0]
    w1 = jnp.sum(jnp.where(lane == 2, rt, 0.0), axis=1, keepdims=True)
    w2 = jnp.sum(jnp.where(lane == 3, rt, 0.0), axis=1, keepdims=True)
    moe = ya_ref[0] * w1 + yb_ref[0] * w2
    o_ref[0] = _layernorm(ALPHA * x1_ref[0] + (1.0 + g2_ref[0]) * moe, lng_ref[...], lnb_ref[...])


def _combine(x1, ya, yb, route, g2, p, per_token_mod):
    B, L, _ = x1.shape
    tm = _pick_tm(L, PROJ_TM)
    row = lambda b, i: (b, i, 0)
    fixed = lambda b, i: (0, 0)
    if per_token_mod:
        mod_spec = pl.BlockSpec((1, tm, D_MODEL), row)
    else:
        mod_spec = pl.BlockSpec((1, 1, D_MODEL), lambda b, i: (b, 0, 0))
    return pl.pallas_call(
        _combine_kernel,
        grid=(B, L // tm),
        in_specs=[
            pl.BlockSpec((1, tm, D_MODEL), row),
            pl.BlockSpec((1, tm, D_MODEL), row),
            pl.BlockSpec((1, tm, D_MODEL), row),
            pl.BlockSpec((1, tm, LANES), row),
            mod_spec,
            pl.BlockSpec((1, D_MODEL), fixed),
            pl.BlockSpec((1, D_MODEL), fixed),
        ],
        out_specs=pl.BlockSpec((1, tm, D_MODEL), row),
        out_shape=jax.ShapeDtypeStruct((B, L, D_MODEL), F32),
        compiler_params=_cparams(("arbitrary", "arbitrary")),
        name="combine",
    )(x1, ya, yb, route, g2, p["ln2_g"], p["ln2_b"])


def _dispatch_plan(eidx):
    T = eidx.shape[0]
    A = 2 * T
    flat_e = eidx.reshape(A)
    onehot = (flat_e[:, None] == jnp.arange(N_EXPERTS, dtype=jnp.int32)[None, :]).astype(jnp.int32)
    cum = jnp.cumsum(onehot, axis=0)
    rank = jnp.take_along_axis(cum, flat_e[:, None], axis=1)[:, 0] - 1
    counts = cum[-1]
    padded = (counts + MOE_BM - 1) // MOE_BM * MOE_BM
    pad_end = jnp.cumsum(padded)
    pad_start = pad_end - padded
    dest = pad_start[flat_e] + rank
    nblk = (A + N_EXPERTS * (MOE_BM - 1) + MOE_BM - 1) // MOE_BM
    block_pos = jnp.arange(nblk, dtype=jnp.int32) * MOE_BM
    block_expert = jnp.minimum(jnp.sum(pad_end[None, :] <= block_pos[:, None], axis=1), N_EXPERTS - 1).astype(jnp.int32)
    n_used = (pad_end[-1] // MOE_BM).astype(jnp.int32).reshape(1)
    return dest, block_expert, n_used, nblk * MOE_BM


def _mods(mod_l, n_prompt, dec_seq):
    mp = mod_l[:n_prompt].reshape(n_prompt, 6, D_MODEL)
    ms = mod_l[n_prompt:].reshape(-1, 6, D_MODEL)
    prompt = [mp[:, j][:, None, :] for j in range(6)]
    sample = [jnp.repeat(ms[:, j], dec_seq, axis=0)[None] for j in range(6)]
    return prompt, sample


def _ssd_state_in(s):
    B = s.shape[0]
    s = s.reshape(B, SSD_GROUPS, SSD_HEADS // SSD_GROUPS, SSD_STATE, SSD_HEAD_DIM)
    return s.transpose(0, 1, 3, 2, 4).reshape(B, SSD_GROUPS, SSD_STATE, D_SSD // SSD_GROUPS)


def _ssd_state_out(s):
    B = s.shape[0]
    s = s.reshape(B, SSD_GROUPS, SSD_STATE, SSD_HEADS // SSD_GROUPS, SSD_HEAD_DIM)
    return s.transpose(0, 1, 3, 2, 4).reshape(B, SSD_HEADS, SSD_STATE, SSD_HEAD_DIM)


def kernel(x_prompt, x_sample, c_prompt, c_sample, state_ssd, state_ssd_conv, state_rglru, state_rglru_conv, state_ret, w_ada, b_ada, w_in, ssd_conv_w, ssd_conv_b, ssd_dt_bias, ssd_a_log, ssd_d, ssd_norm_w, rg_conv_w, rg_conv_b, rg_wa, rg_ba, rg_wx, rg_bx, rg_lambda, ret_gn_w, w_out, ln1_g, ln1_b, router_w, router_bias, exp_w_gate, exp_w_up, exp_w_down, ln2_g, ln2_b):
    BP, LP, _ = x_prompt.shape
    BS, LS, _ = x_sample.shape
    TP, TS = BP * LP, BS * LS

    def pad_lanes(v):
        return jnp.pad(v, ((0, 0), (0, LANES - v.shape[-1])))

    eye = jnp.eye(RG_BLOCKS, dtype=F32)

    def block_diag(w):
        return jnp.stack(_hi_lo(jnp.einsum("njk,nm->njmk", w, eye).reshape(D_RG, D_RG)))

    params = []
    for l in range(DEPTH):
        wi = w_in[l]
        wi = jnp.concatenate([wi[:, :2560], wi[:, 2576:5648], wi[:, 2560:2576],
                              jnp.zeros((D_MODEL, LANES - SSD_HEADS), F32)], axis=1)
        params.append(dict(
            w_in=_hi_lo(wi),
            ssd_conv_w=ssd_conv_w[l], ssd_conv_b=ssd_conv_b[l][None],
            ssd_dt_bias=pad_lanes(ssd_dt_bias[l][None]), ssd_a_log=pad_lanes(ssd_a_log[l][None]),
            ssd_d_ch=jnp.repeat(ssd_d[l], SSD_HEAD_DIM)[None], ssd_norm_w=ssd_norm_w[l][None],
            rg_conv_w=rg_conv_w[l], rg_conv_b=rg_conv_b[l][None],
            rg_wa_bd=block_diag(rg_wa[l]), rg_ba=rg_ba[l][None],
            rg_wx_bd=block_diag(rg_wx[l]), rg_bx=rg_bx[l][None],
            rg_lambda=rg_lambda[l][None], ret_gn_w=ret_gn_w[l][None],
            w_out=jnp.stack(_hi_lo(w_out[l])), ln1_g=ln1_g[l][None], ln1_b=ln1_b[l][None],
            experts=[h for w in (exp_w_gate[l], exp_w_up[l], exp_w_down[l]) for h in _hi_lo(w)],
            ln2_g=ln2_g[l][None], ln2_b=ln2_b[l][None],
        ))

    rw_hi, rw_lo = _hi_lo(pad_lanes(router_w))
    rbias = pad_lanes(router_bias[None])
    head_of_ch = jnp.arange(D_SSD) // SSD_HEAD_DIM
    consts = dict(
        rw_hi=rw_hi, rw_lo=rw_lo, rbias=rbias.astype(F32),
        head_expand=(jnp.arange(LANES)[:, None] == head_of_ch[None, :]).astype(BF16),
        tri=jnp.tril(jnp.ones((CHUNK, CHUNK), F32)).astype(BF16),
    )
    mod = _ada(jnp.concatenate([c_prompt, c_sample], axis=0), w_ada, b_ada)

    LM = LP - TAIL
    paths = [
        dict(x=x_prompt[:, :LM], B=BP, L=LM, per_token=False, has_state=False, precise=False, rc=_ret_consts(LM, 0.0)),
        dict(x=x_prompt[:, LM:], B=BP, L=TAIL, per_token=False, has_state=True, precise=True, rc=_ret_consts(TAIL, float(LM))),
        dict(x=x_sample.reshape(1, TS, D_MODEL), B=BS, L=LS, per_token=True, has_state=True, precise=False,
             rc=_ret_consts(LS, PAST_LEN)),
    ]
    zero_states = (jnp.zeros((BP, SSD_GROUPS, SSD_STATE, D_SSD // SSD_GROUPS), F32),
                   jnp.zeros((BP, CONV_WIDTH - 1, SSD_CONV_DIM), F32),
                   jnp.zeros((BP, 1, D_RG), F32),
                   jnp.zeros((BP, CONV_WIDTH - 1, D_RG), F32),
                   jnp.zeros((BP, RET_HEADS, RET_HEAD_DIM, RET_HEAD_DIM), F32))

    def mixer_stack(path, p, mods, states):
        B, L, per_token, precise = path["B"], path["L"], path["per_token"], path["precise"]
        w_in_l = p["w_in"] if precise else p["w_in"][:1]
        z, xbc, rg, ret, dt = _inproj(path["x"], mods[1], mods[0], w_in_l, per_token, precise)
        if per_token:
            z, xbc, rg, ret, dt = [a.reshape(B, L, a.shape[-1]) for a in (z, xbc, rg, ret, dt)]
        s_ssd, s_cbuf, s_rg, s_rbuf, s_ret = states
        hs = path["has_state"]
        y_ssd, cbuf_n, ssd_n = _ssd(z, xbc, dt, s_ssd, s_cbuf, p, consts, hs, precise)
        y_rg, rbuf_n, rg_n = _rg(rg, s_rg, s_rbuf, p, hs, precise)
        y_ret, ret_n = _ret(ret, s_ret, p, path["rc"], hs, precise)
        if per_token:
            y_ssd, y_rg, y_ret = [a.reshape(1, B * L, a.shape[-1]) for a in (y_ssd, y_rg, y_ret)]
        return (y_ssd, y_rg, y_ret), (ssd_n, cbuf_n, rg_n, rbuf_n, ret_n)

    def experts(h2_list, route_list, weights, precise):
        sizes = [h.shape[0] * h.shape[1] for h in h2_list]
        h2_all = jnp.concatenate([h.reshape(-1, D_MODEL) for h in h2_list], axis=0)
        route_all = jnp.concatenate([r.reshape(-1, LANES) for r in route_list], axis=0)
        T_all = h2_all.shape[0]
        dest, block_expert, n_used, P = _dispatch_plan(route_all[:, 0:2].astype(jnp.int32))
        tok = jnp.arange(2 * T_all, dtype=jnp.int32) // 2
        slot_tok = jnp.full((P,), T_all, jnp.int32).at[dest].set(tok)
        xs_sorted = jnp.concatenate([h2_all, jnp.zeros((1, D_MODEL), h2_all.dtype)], axis=0)[slot_tok]
        ys_sorted = _moe(xs_sorted, block_expert, n_used, weights, precise)
        yab = ys_sorted[dest].reshape(T_all, 2, D_MODEL)
        outs, off = [], 0
        for h, n in zip(h2_list, sizes):
            outs.append((yab[off:off + n, 0].reshape(h.shape), yab[off:off + n, 1].reshape(h.shape)))
            off += n
        return outs

    new_tail, new_s = [], []
    for l in range(DEPTH):
        p = params[l]
        mods_p, mods_s = _mods(mod[l], BP, LS)
        mods = [mods_p, mods_p, mods_s]
        st_s = (_ssd_state_in(state_ssd[l]), state_ssd_conv[l], state_rglru[l][:, None, :],
                state_rglru_conv[l], state_ret[l])

        ys_m, st_m = mixer_stack(paths[0], p, mods[0], zero_states)
        ys_t, st_t = mixer_stack(paths[1], p, mods[1], st_m)
        ys_c, st_c = mixer_stack(paths[2], p, mods[2], st_s)
        new_tail.append(st_t)
        new_s.append(st_c)

        post = []
        for path, ys, m in zip(paths, (ys_m, ys_t, ys_c), mods):
            post.append(_outproj(*ys, path["x"], m[2], m[4], m[3], p, consts, path["per_token"], path["precise"]))

        hi_w = p["experts"][0::2]
        (ya_m, yb_m), (ya_c, yb_c) = experts([post[0][1], post[2][1]], [post[0][2], post[2][2]], hi_w, False)
        ((ya_t, yb_t),) = experts([post[1][1]], [post[1][2]], hi_w + p["experts"][1::2], True)

        for path, (x1, _, route), ya, yb, m in zip(paths, post, (ya_m, ya_t, ya_c), (yb_m, yb_t, yb_c), mods):
            path["x"] = _combine(x1, ya, yb, route, m[5], p, path["per_token"])

    def stack(lst, k):
        return jnp.stack([s[k] for s in lst])

    y_prompt = jnp.concatenate([paths[0]["x"], paths[1]["x"]], axis=1)
    return (y_prompt, paths[2]["x"].reshape(BS, LS, D_MODEL),
            jnp.stack([_ssd_state_out(s[0]) for s in new_tail]), stack(new_tail, 1), stack(new_tail, 2)[:, :, 0],
            stack(new_tail, 3), stack(new_tail, 4),
            jnp.stack([_ssd_state_out(s[0]) for s in new_s]), stack(new_s, 1), stack(new_s, 2)[:, :, 0],
            stack(new_s, 3), stack(new_s, 4))
```

```python
import functools
import math

import jax
import jax.numpy as jnp
from jax import lax
from jax.experimental import pallas as pl
from jax.experimental.pallas import tpu as pltpu

F32 = jnp.float32
BF16 = jnp.bfloat16

D_MODEL = 1024
DEPTH = 4
D_SSD = 1024
SSD_HEAD_DIM = 64
SSD_HEADS = 16
SSD_GROUPS = 2
SSD_STATE = 128
SSD_CONV_DIM = D_SSD + 2 * SSD_GROUPS * SSD_STATE
CONV_WIDTH = 4
D_RG = 512
RG_BLOCKS = 8
RG_C = 8.0
D_RET = 512
RET_HEADS = 4
RET_HEAD_DIM = 128
ROPE_BASE = 10000.0
N_EXPERTS = 32
EXPERTS_PER_GROUP = 8
N_EXPERT_GROUPS = 4
D_EXPERT = 512
ALPHA = (2 * DEPTH) ** 0.25
EPS = 1e-5
PAST_LEN = 16384.0

LANES = 128
SUBLANES = 8
CHUNK = 128
TAIL = CHUNK
PROJ_TM = 384
MOE_BM = 256
VMEM_LIMIT = 56 * 1024 * 1024


def _cparams(sem):
    return pltpu.CompilerParams(dimension_semantics=sem, vmem_limit_bytes=VMEM_LIMIT)


def _pick_tm(L, cap):
    if L <= LANES:
        return L
    tm = cap - cap % LANES
    while L % tm:
        tm -= LANES
    return tm


def _sigmoid(x):
    return 1.0 / (1.0 + jnp.exp(-x))


def _silu(x):
    return x * _sigmoid(x)


def _softplus(x):
    return jnp.maximum(x, 0.0) + jnp.log1p(jnp.exp(-jnp.abs(x)))


def _split3(v):
    hi = v.astype(BF16)
    r = v - hi.astype(F32)
    mid = r.astype(BF16)
    lo = (r - mid.astype(F32)).astype(BF16)
    return hi, mid, lo


def _dot(a, b):
    return jnp.dot(a, b, preferred_element_type=F32)


def _split2(v):
    hi = v.astype(BF16)
    return hi, (v - hi.astype(F32)).astype(BF16)


def _mm_w(a, w_hi, w_lo):
    if w_lo is None:
        return _dot(a.astype(BF16), w_hi)
    ah, al = _split2(a)
    return _dot(ah, w_hi) + (_dot(al, w_hi) + _dot(ah, w_lo))


def _mm_a(a, b, precise, nt=False):
    dn = (((1,), (1,)), ((), ())) if nt else (((1,), (0,)), ((), ()))

    def d(x, y):
        return lax.dot_general(x, y, dn, preferred_element_type=F32)

    if not precise:
        return d(a.astype(BF16), b.astype(BF16))
    ah, al = _split2(a)
    bh, bl = _split2(b)
    return d(ah, bh) + (d(al, bh) + d(ah, bl))


def _hi_lo(w):
    hi = lax.optimization_barrier(w.astype(BF16))
    return hi, (w - hi.astype(F32)).astype(BF16)


def _dot_exact_rhs(v, m):
    hi, mid, lo = _split3(v)
    return _dot(hi, m) + _dot(mid, m) + _dot(lo, m)


def _dot_exact_lhs(m, v):
    hi, mid, lo = _split3(v)
    return _dot(m, hi) + _dot(m, mid) + _dot(m, lo)


def _ada_kernel(c_ref, w_ref, b_ref, o_ref):
    o_ref[0] = _mm_a(_silu(c_ref[...]), w_ref[0], True) + b_ref[0]


def _ada(c_all, w_ada, b_ada):
    n = c_all.shape[0]
    tn = 1024
    return pl.pallas_call(
        _ada_kernel,
        grid=(DEPTH, 6 * D_MODEL // tn),
        in_specs=[
            pl.BlockSpec((n, D_MODEL), lambda l, j: (0, 0)),
            pl.BlockSpec((1, D_MODEL, tn), lambda l, j: (l, 0, j)),
            pl.BlockSpec((1, 1, tn), lambda l, j: (l, 0, j)),
        ],
        out_specs=pl.BlockSpec((1, n, tn), lambda l, j: (l, 0, j)),
        out_shape=jax.ShapeDtypeStruct((DEPTH, n, 6 * D_MODEL), F32),
        compiler_params=_cparams(("arbitrary", "arbitrary")),
        name="ada",
    )(c_all, w_ada, b_ada.reshape(DEPTH, 1, 6 * D_MODEL))


_IN_SEGS = (("z", 0, 1024), ("xbc", 1024, 2560), ("rg", 2560, 3584), ("ret", 3584, 5632), ("dt", 5632, 5760))
IN_PAD = 5760


def _inproj_kernel(x_ref, sc_ref, sh_ref, w_ref, *rest, precise):
    wl_ref = rest[0] if precise else None
    outs = rest[1:] if precise else rest
    h = x_ref[0] * (1.0 + sc_ref[0]) + sh_ref[0]
    if precise:
        hh, hl = _split2(h)
    else:
        hh = h.astype(BF16)
    for o_ref, (_, lo, hi) in zip(outs, _IN_SEGS):
        acc = _dot(hh, w_ref[:, lo:hi])
        if precise:
            acc = acc + (_dot(hl, w_ref[:, lo:hi]) + _dot(hh, wl_ref[:, lo:hi]))
        o_ref[0] = acc


def _inproj(x, sc, sh, w, per_token_mod, precise):
    B, L, _ = x.shape
    tm = _pick_tm(L, LANES if precise else PROJ_TM)
    if per_token_mod:
        mod_spec = pl.BlockSpec((1, tm, D_MODEL), lambda b, i: (b, i, 0))
    else:
        mod_spec = pl.BlockSpec((1, 1, D_MODEL), lambda b, i: (b, 0, 0))
    widths = [hi - lo for _, lo, hi in _IN_SEGS]
    w_spec = pl.BlockSpec((D_MODEL, IN_PAD), lambda b, i: (0, 0), pipeline_mode=pl.Buffered(1))
    ws = w if precise else w[:1]
    return pl.pallas_call(
        functools.partial(_inproj_kernel, precise=precise),
        grid=(B, L // tm),
        in_specs=[
            pl.BlockSpec((1, tm, D_MODEL), lambda b, i: (b, i, 0)),
            mod_spec,
            mod_spec,
        ] + [w_spec] * len(ws),
        out_specs=[pl.BlockSpec((1, tm, wd), lambda b, i: (b, i, 0)) for wd in widths],
        out_shape=[jax.ShapeDtypeStruct((B, L, wd), F32) for wd in widths],
        compiler_params=_cparams(("arbitrary", "arbitrary")),
        name="inproj",
    )(x, sc, sh, *ws)


def _conv_step(src_rows, buf_ref, cw_ref, cb_ref, nbuf_ref, xp_s, *, c, nc, Lv, Lp, has_state):
    @pl.when(c == 0)
    def _():
        xp_s[...] = jnp.zeros_like(xp_s)
        if has_state:
            xp_s[SUBLANES - (CONV_WIDTH - 1):SUBLANES, :] = buf_ref[0]

    xp_s[SUBLANES:SUBLANES + Lv, :] = src_rows
    base = SUBLANES - (CONV_WIDTH - 1)
    xc = cb_ref[...]
    for j in range(CONV_WIDTH):
        xc = xc + cw_ref[j:j + 1, :] * xp_s[base + j:base + j + Lp, :]

    @pl.when(c == nc - 1)
    def _():
        nbuf_ref[0] = xp_s[SUBLANES + Lv - (CONV_WIDTH - 1):SUBLANES + Lv, :]

    if nc > 1:
        xp_s[0:SUBLANES, :] = xp_s[Lp:Lp + SUBLANES, :]
    return xc


def _ssd_kernel(z_ref, xbc_ref, dt_ref, h0_ref, buf_ref, cw_ref, cb_ref, dtb_ref, alog_ref, dch_ref,
                nw_ref, e_ref, tri_ref, y_ref, nbuf_ref, hn_ref, xp_s, dtp_s, s_s,
                *, Lv, Lp, nc, has_state, precise):
    c = pl.program_id(1)

    @pl.when(c == 0)
    def _():
        if has_state:
            s_s[...] = h0_ref[0]
        else:
            s_s[...] = jnp.zeros_like(s_s)
        if Lv < Lp:
            dtp_s[...] = jnp.zeros_like(dtp_s)

    xc = _conv_step(xbc_ref[0], buf_ref, cw_ref, cb_ref, nbuf_ref, xp_s,
                    c=c, nc=nc, Lv=Lv, Lp=Lp, has_state=has_state)
    xbc = _silu(xc)
    xs = xbc[:, 0:D_SSD]
    Bm = xbc[:, D_SSD:D_SSD + 256]
    Cm = xbc[:, D_SSD + 256:D_SSD + 512]

    if Lv < Lp:
        dtp_s[0:Lv, :] = dt_ref[0]
        dtr = dtp_s[...]
    else:
        dtr = dt_ref[0]
    dt = _softplus(dtr + dtb_ref[...])
    if Lv < Lp:
        rowv = lax.broadcasted_iota(jnp.int32, (Lp, LANES), 0)
        dt = jnp.where(rowv < Lv, dt, 0.0)
    a = dt * (-jnp.exp(alog_ref[...]))
    acum = _dot_exact_lhs(tri_ref[...], a)
    dt_e = _dot_exact_rhs(dt, e_ref[...])
    ac_e = _dot_exact_rhs(acum, e_ref[...])
    acl_e = ac_e[Lp - 1:Lp, :]
    acum_t = acum.T

    vdt = xs * dt_e
    lane = lax.broadcasted_iota(jnp.int32, (Lp, LANES), 1)
    causal = lax.broadcasted_iota(jnp.int32, (Lp, Lp), 0) >= lax.broadcasted_iota(jnp.int32, (Lp, Lp), 1)
    mid_t = F32 if precise else BF16

    y_blocks = []
    for g in range(SSD_GROUPS):
        Cg = Cm[:, g * SSD_STATE:(g + 1) * SSD_STATE]
        Bg = Bm[:, g * SSD_STATE:(g + 1) * SSD_STATE]
        G = _mm_a(Cg, Bg, precise, nt=True)
        for jj in range(4):
            j = 4 * g + jj
            scs = []
            for h in (2 * j, 2 * j + 1):
                col = jnp.sum(jnp.where(lane == h, acum, 0.0), axis=1, keepdims=True)
                rw = acum_t[h:h + 1, :]
                dec = jnp.exp(jnp.where(causal, col - rw, -jnp.inf))
                scs.append((G * dec).astype(mid_t))
            sc = jnp.concatenate(scs, axis=1)
            vb = vdt[:, j * LANES:(j + 1) * LANES]
            v2 = jnp.concatenate([jnp.where(lane < SSD_HEAD_DIM, vb, 0.0),
                                  jnp.where(lane >= SSD_HEAD_DIM, vb, 0.0)], axis=0).astype(mid_t)
            y_blocks.append(_mm_a(sc, v2, precise))
    y = jnp.concatenate(y_blocks, axis=1)

    eac = jnp.exp(ac_e)
    vw = vdt * jnp.exp(acl_e - ac_e)
    cd = jnp.exp(acl_e)
    half = D_SSD // SSD_GROUPS
    ys_parts = []
    for g in range(SSD_GROUPS):
        Sg = s_s[g]
        ys_parts.append(_mm_a(Cm[:, g * SSD_STATE:(g + 1) * SSD_STATE], Sg, precise))
        BgT = Bm[:, g * SSD_STATE:(g + 1) * SSD_STATE].T
        s_s[g] = cd[:, g * half:(g + 1) * half] * Sg + _mm_a(BgT, vw[:, g * half:(g + 1) * half], precise)
    y = y + eac * jnp.concatenate(ys_parts, axis=1) + dch_ref[...] * xs

    yv = y[0:Lv, :] if Lv < Lp else y
    yz = yv * _silu(z_ref[0])
    outs = []
    for g in range(SSD_GROUPS):
        part = yz[:, g * half:(g + 1) * half]
        ms = jnp.mean(part * part, axis=1, keepdims=True)
        outs.append(part * lax.rsqrt(ms + EPS))
    y_ref[0] = jnp.concatenate(outs, axis=1) * nw_ref[...]

    @pl.when(c == nc - 1)
    def _():
        hn_ref[0] = s_s[...]


def _ssd(z, xbc, dt, h0, buf, p, consts, has_state, precise):
    B, L, _ = z.shape
    Lv = min(CHUNK, L)
    Lp = CHUNK
    nc = L // Lv
    kern = functools.partial(_ssd_kernel, Lv=Lv, Lp=Lp, nc=nc, has_state=has_state, precise=precise)
    row = lambda b, c: (b, c, 0)
    per_b3 = lambda b, c: (b, 0, 0)
    per_b4 = lambda b, c: (b, 0, 0, 0)
    fixed = lambda b, c: (0, 0)
    half = D_SSD // SSD_GROUPS
    return pl.pallas_call(
        kern,
        grid=(B, nc),
        in_specs=[
            pl.BlockSpec((1, Lv, D_SSD), row),
            pl.BlockSpec((1, Lv, SSD_CONV_DIM), row),
            pl.BlockSpec((1, Lv, LANES), row),
            pl.BlockSpec((1, SSD_GROUPS, SSD_STATE, half), per_b4),
            pl.BlockSpec((1, CONV_WIDTH - 1, SSD_CONV_DIM), per_b3),
            pl.BlockSpec((CONV_WIDTH, SSD_CONV_DIM), fixed),
            pl.BlockSpec((1, SSD_CONV_DIM), fixed),
            pl.BlockSpec((1, LANES), fixed),
            pl.BlockSpec((1, LANES), fixed),
            pl.BlockSpec((1, D_SSD), fixed),
            pl.BlockSpec((1, D_SSD), fixed),
            pl.BlockSpec((LANES, D_SSD), fixed),
            pl.BlockSpec((Lp, Lp), fixed),
        ],
        out_specs=[
            pl.BlockSpec((1, Lv, D_SSD), row),
            pl.BlockSpec((1, CONV_WIDTH - 1, SSD_CONV_DIM), per_b3),
            pl.BlockSpec((1, SSD_GROUPS, SSD_STATE, half), per_b4),
        ],
        out_shape=[
            jax.ShapeDtypeStruct((B, L, D_SSD), F32),
            jax.ShapeDtypeStruct((B, CONV_WIDTH - 1, SSD_CONV_DIM), F32),
            jax.ShapeDtypeStruct((B, SSD_GROUPS, SSD_STATE, half), F32),
        ],
        scratch_shapes=[
            pltpu.VMEM((SUBLANES + Lp, SSD_CONV_DIM), F32),
            pltpu.VMEM((Lp, LANES), F32),
            pltpu.VMEM((SSD_GROUPS, SSD_STATE, half), F32),
        ],
        compiler_params=_cparams(("arbitrary", "arbitrary")),
        name="ssd",
    )(z, xbc, dt, h0, buf, p["ssd_conv_w"], p["ssd_conv_b"], p["ssd_dt_bias"], p["ssd_a_log"],
      p["ssd_d_ch"], p["ssd_norm_w"], consts["head_expand"], consts["tri"])


def _rg_kernel(rg_ref, h0_ref, buf_ref, cw_ref, cb_ref, wa_ref, ba_ref, wx_ref, bx_ref, lam_ref,
               y_ref, nbuf_ref, hn_ref, xp_s, hc_s, *, Lv, Lp, nc, has_state, precise):
    c = pl.program_id(1)

    @pl.when(c == 0)
    def _():
        if has_state:
            hc_s[...] = h0_ref[0]
        else:
            hc_s[...] = jnp.zeros_like(hc_s)

    xc = _conv_step(rg_ref[0, :, 0:D_RG], buf_ref, cw_ref, cb_ref, nbuf_ref, xp_s,
                    c=c, nc=nc, Lv=Lv, Lp=Lp, has_state=has_state)
    r = _sigmoid(_mm_w(xc, wa_ref[0], wa_ref[1] if precise else None) + ba_ref[...])
    i = _sigmoid(_mm_w(xc, wx_ref[0], wx_ref[1] if precise else None) + bx_ref[...])
    la = (-RG_C * _softplus(-lam_ref[...])) * r
    u = jnp.sqrt(-jnp.tanh(la) * (jnp.exp(2.0 * la) + 1.0)) * (i * xc)
    row = lax.broadcasted_iota(jnp.int32, (Lp, D_RG), 0)
    if Lv < Lp:
        valid = row < Lv
        la = jnp.where(valid, la, 0.0)
        u = jnp.where(valid, u, 0.0)
    a = jnp.exp(la)
    s = 1
    while s < Lp:
        m = row >= s
        u_sh = jnp.where(m, pltpu.roll(u, s, 0), 0.0)
        a_sh = jnp.where(m, pltpu.roll(a, s, 0), 1.0)
        u = u + a * u_sh
        a = a * a_sh
        s *= 2
    h = u + a * hc_s[...]
    hc_s[...] = h[Lp - 1:Lp, :]
    hv = h[0:Lv, :] if Lv < Lp else h
    y_ref[0] = hv * jax.nn.gelu(rg_ref[0, :, D_RG:2 * D_RG])

    @pl.when(c == nc - 1)
    def _():
        hn_ref[0] = hc_s[...]


def _rg(rg, h0, buf, p, has_state, precise):
    B, L, _ = rg.shape
    Lv = min(CHUNK, L)
    Lp = CHUNK
    nc = L // Lv
    kern = functools.partial(_rg_kernel, Lv=Lv, Lp=Lp, nc=nc, has_state=has_state, precise=precise)
    row = lambda b, c: (b, c, 0)
    per_b3 = lambda b, c: (b, 0, 0)
    fixed = lambda b, c: (0, 0)
    fixed3 = lambda b, c: (0, 0, 0)
    return pl.pallas_call(
        kern,
        grid=(B, nc),
        in_specs=[
            pl.BlockSpec((1, Lv, 2 * D_RG), row),
            pl.BlockSpec((1, 1, D_RG), per_b3),
            pl.BlockSpec((1, CONV_WIDTH - 1, D_RG), per_b3),
            pl.BlockSpec((CONV_WIDTH, D_RG), fixed),
            pl.BlockSpec((1, D_RG), fixed),
            pl.BlockSpec((2, D_RG, D_RG), fixed3),
            pl.BlockSpec((1, D_RG), fixed),
            pl.BlockSpec((2, D_RG, D_RG), fixed3),
            pl.BlockSpec((1, D_RG), fixed),
            pl.BlockSpec((1, D_RG), fixed),
        ],
        out_specs=[
            pl.BlockSpec((1, Lv, D_RG), row),
            pl.BlockSpec((1, CONV_WIDTH - 1, D_RG), per_b3),
            pl.BlockSpec((1, 1, D_RG), per_b3),
        ],
        out_shape=[
            jax.ShapeDtypeStruct((B, L, D_RG), F32),
            jax.ShapeDtypeStruct((B, CONV_WIDTH - 1, D_RG), F32),
            jax.ShapeDtypeStruct((B, 1, D_RG), F32),
        ],
        scratch_shapes=[
            pltpu.VMEM((SUBLANES + Lp, D_RG), F32),
            pltpu.VMEM((1, D_RG), F32),
        ],
        compiler_params=_cparams(("arbitrary", "arbitrary")),
        name="rglru",
    )(rg, h0, buf, p["rg_conv_w"], p["rg_conv_b"], p["rg_wa_bd"], p["rg_ba"], p["rg_wx_bd"],
      p["rg_bx"], p["rg_lambda"])


def _ret_kernel(ret_ref, h0_ref, cos_ref, sin_ref, dm_ref, ea_ref, te_ref, cd_ref, gn_ref,
                y_ref, hn_ref, pad_s, s_s, *, Lv, Lp, nc, has_state, precise):
    c = pl.program_id(1)

    @pl.when(c == 0)
    def _():
        if has_state:
            s_s[...] = h0_ref[0]
        else:
            s_s[...] = jnp.zeros_like(s_s)
        if Lv < Lp:
            pad_s[...] = jnp.zeros_like(pad_s)

    if Lv < Lp:
        pad_s[0:Lv, :] = ret_ref[0, :, 0:3 * D_RET]
        qkv = pad_s[...]
    else:
        qkv = ret_ref[0, :, 0:3 * D_RET]
    cosf = cos_ref[...]
    sinf = sin_ref[...]
    outs = []
    for h in range(RET_HEADS):
        sl = slice(h * RET_HEAD_DIM, (h + 1) * RET_HEAD_DIM)
        q = qkv[:, sl]
        k = qkv[:, D_RET + h * RET_HEAD_DIM:D_RET + (h + 1) * RET_HEAD_DIM]
        v = qkv[:, 2 * D_RET + h * RET_HEAD_DIM:2 * D_RET + (h + 1) * RET_HEAD_DIM]
        q = q * cosf + pltpu.roll(q, RET_HEAD_DIM // 2, 1) * sinf
        k = (k * cosf + pltpu.roll(k, RET_HEAD_DIM // 2, 1) * sinf) * (RET_HEAD_DIM ** -0.5)
        G = _mm_a(q, k, precise, nt=True)
        S = s_s[h]
        y = _mm_a(G * dm_ref[h], v, precise) + _mm_a(q * ea_ref[h], S, precise)
        s_s[h] = cd_ref[h] * S + _mm_a((k * te_ref[h]).T, v, precise)
        yv = y[0:Lv, :] if Lv < Lp else y
        mu = jnp.mean(yv, axis=1, keepdims=True)
        d = yv - mu
        var = jnp.mean(d * d, axis=1, keepdims=True)
        outs.append(d * lax.rsqrt(var + EPS))
    yn = jnp.concatenate(outs, axis=1) * gn_ref[...]
    y_ref[0] = _silu(ret_ref[0, :, 3 * D_RET:4 * D_RET]) * yn

    @pl.when(c == nc - 1)
    def _():
        hn_ref[0] = s_s[...]


def _ret(ret, h0, p, rc, has_state, precise):
    B, L, _ = ret.shape
    Lv = min(CHUNK, L)
    Lp = CHUNK
    nc = L // Lv
    kern = functools.partial(_ret_kernel, Lv=Lv, Lp=Lp, nc=nc, has_state=has_state, precise=precise)
    row = lambda b, c: (b, c, 0)
    per_b4 = lambda b, c: (b, 0, 0, 0)
    fixed2 = lambda b, c: (0, 0)
    fixed3 = lambda b, c: (0, 0, 0)
    hd = RET_HEAD_DIM
    return pl.pallas_call(
        kern,
        grid=(B, nc),
        in_specs=[
            pl.BlockSpec((1, Lv, 4 * D_RET), row),
            pl.BlockSpec((1, RET_HEADS, hd, hd), per_b4),
            pl.BlockSpec((Lp, hd), lambda b, c: (c, 0)),
            pl.BlockSpec((Lp, hd), lambda b, c: (c, 0)),
            pl.BlockSpec((RET_HEADS, Lp, Lp), fixed3),
            pl.BlockSpec((RET_HEADS, Lp, hd), fixed3),
            pl.BlockSpec((RET_HEADS, Lp, hd), fixed3),
            pl.BlockSpec((RET_HEADS, hd, hd), fixed3),
            pl.BlockSpec((1, D_RET), fixed2),
        ],
        out_specs=[
            pl.BlockSpec((1, Lv, D_RET), row),
            pl.BlockSpec((1, RET_HEADS, hd, hd), per_b4),
        ],
        out_shape=[
            jax.ShapeDtypeStruct((B, L, D_RET), F32),
            jax.ShapeDtypeStruct((B, RET_HEADS, hd, hd), F32),
        ],
        scratch_shapes=[
            pltpu.VMEM((Lp, 3 * D_RET), F32),
            pltpu.VMEM((RET_HEADS, hd, hd), F32),
        ],
        compiler_params=_cparams(("arbitrary", "arbitrary")),
        name="retention",
    )(ret, h0, rc["cos"], rc["sin"], rc["dmat"], rc["eacum"], rc["toend"], rc["cdec"], p["ret_gn_w"])


def _ret_consts(L, pos0):
    Lv = min(CHUNK, L)
    Lp = CHUNK
    nc = L // Lv
    half = RET_HEAD_DIM // 2
    inv = ROPE_BASE ** (-jnp.arange(half, dtype=F32) / half)
    pos = pos0 + jnp.arange(L, dtype=F32)
    ang = pos[:, None] * inv[None, :]
    cos = jnp.cos(ang)
    sin = jnp.sin(ang)
    cosf = jnp.concatenate([cos, cos], axis=1)
    sinf = jnp.concatenate([-sin, sin], axis=1)
    if Lv < Lp:
        cosf = jnp.pad(cosf, ((0, Lp - Lv), (0, 0)))
        sinf = jnp.pad(sinf, ((0, Lp - Lv), (0, 0)))
    assert cosf.shape[0] == nc * Lp
    log_gamma = jnp.log1p(-jnp.exp2(-5.0 - jnp.arange(RET_HEADS, dtype=F32)))
    steps = jnp.minimum(jnp.arange(Lp) + 1, Lv).astype(F32)
    acum = log_gamma[:, None] * steps[None, :]
    causal = jnp.tril(jnp.ones((Lp, Lp), bool))
    dmat = jnp.exp(jnp.where(causal[None], acum[:, :, None] - acum[:, None, :], -jnp.inf))
    ones = jnp.ones((RET_HEADS, Lp, RET_HEAD_DIM), F32)
    eacum = jnp.exp(acum)[:, :, None] * ones
    toend = jnp.exp(acum[:, -1:] - acum)[:, :, None] * ones
    rowvalid = (jnp.arange(Lp) < Lv).astype(F32)[None, :, None]
    toend = toend * rowvalid
    cdec = jnp.exp(acum[:, -1])[:, None, None] * jnp.ones((RET_HEADS, RET_HEAD_DIM, RET_HEAD_DIM), F32)
    return dict(cos=cosf, sin=sinf, dmat=dmat, eacum=eacum, toend=toend, cdec=cdec)


def _layernorm(v, g, b):
    mu = jnp.mean(v, axis=1, keepdims=True)
    d = v - mu
    var = jnp.mean(d * d, axis=1, keepdims=True)
    return d * lax.rsqrt(var + EPS) * g + b


def _route(logits, bias):
    tm = logits.shape[0]
    lane = lax.broadcasted_iota(jnp.int32, (tm, LANES), 1)
    lanef = lane.astype(F32)
    scores = _sigmoid(logits)
    choice = scores + bias
    neg = -jnp.inf
    best = jnp.full((tm, 1), neg, F32)
    e1 = jnp.zeros((tm, 1), F32)
    e2 = jnp.zeros((tm, 1), F32)
    for g in range(N_EXPERT_GROUPS):
        ing = (lane >= g * EXPERTS_PER_GROUP) & (lane < (g + 1) * EXPERTS_PER_GROUP)
        cg = jnp.where(ing, choice, neg)
        m1 = jnp.max(cg, axis=1, keepdims=True)
        i1 = jnp.min(jnp.where(cg == m1, lanef, float(LANES)), axis=1, keepdims=True)
        cg2 = jnp.where(lanef == i1, neg, cg)
        m2 = jnp.max(cg2, axis=1, keepdims=True)
        i2 = jnp.min(jnp.where(cg2 == m2, lanef, float(LANES)), axis=1, keepdims=True)
        gs = m1 + m2
        better = gs > best
        best = jnp.where(better, gs, best)
        e1 = jnp.where(better, i1, e1)
        e2 = jnp.where(better, i2, e2)
    w1 = jnp.sum(jnp.where(lanef == e1, scores, 0.0), axis=1, keepdims=True)
    w2 = jnp.sum(jnp.where(lanef == e2, scores, 0.0), axis=1, keepdims=True)
    den = w1 + w2
    out = jnp.where(lane == 0, e1, jnp.where(lane == 1, e2, jnp.where(lane == 2, w1 / den, jnp.where(lane == 3, w2 / den, 0.0))))
    return out


def _outproj_kernel(ys_ref, yr_ref, yt_ref, x_ref, g1_ref, sc2_ref, sh2_ref, w_ref, lng_ref, lnb_ref,
                    rwh_ref, rwl_ref, rb_ref, x1_ref, h2_ref, route_ref, *, precise):
    def part(y_ref, lo, hi):
        return _mm_w(y_ref[0], w_ref[0, lo:hi, :], w_ref[1, lo:hi, :] if precise else None)

    mix = part(ys_ref, 0, D_SSD) + part(yr_ref, D_SSD, D_SSD + D_RG) + part(yt_ref, D_SSD + D_RG, 2 * D_MODEL)
    x1 = _layernorm(ALPHA * x_ref[0] + (1.0 + g1_ref[0]) * mix, lng_ref[...], lnb_ref[...])
    x1_ref[0] = x1
    h2 = x1 * (1.0 + sc2_ref[0]) + sh2_ref[0]
    hi, lo = _split2(h2)
    h2_ref[0] = h2
    logits = _dot(hi, rwh_ref[...]) + (_dot(lo, rwh_ref[...]) + _dot(hi, rwl_ref[...]))
    route_ref[0] = _route(logits, rb_ref[...])


def _outproj(y_ssd, y_rg, y_ret, x, g1, sc2, sh2, p, consts, per_token_mod, precise):
    B, L, _ = x.shape
    tm = _pick_tm(L, LANES if precise else PROJ_TM)
    row = lambda b, i: (b, i, 0)
    fixed = lambda b, i: (0, 0)
    if per_token_mod:
        mod_spec = pl.BlockSpec((1, tm, D_MODEL), row)
    else:
        mod_spec = pl.BlockSpec((1, 1, D_MODEL), lambda b, i: (b, 0, 0))
    return pl.pallas_call(
        functools.partial(_outproj_kernel, precise=precise),
        grid=(B, L // tm),
        in_specs=[
            pl.BlockSpec((1, tm, D_SSD), row),
            pl.BlockSpec((1, tm, D_RG), row),
            pl.BlockSpec((1, tm, D_RET), row),
            pl.BlockSpec((1, tm, D_MODEL), row),
            mod_spec, mod_spec, mod_spec,
            pl.BlockSpec((2, 2 * D_MODEL, D_MODEL), lambda b, i: (0, 0, 0)),
            pl.BlockSpec((1, D_MODEL), fixed),
            pl.BlockSpec((1, D_MODEL), fixed),
            pl.BlockSpec((D_MODEL, LANES), fixed),
            pl.BlockSpec((D_MODEL, LANES), fixed),
            pl.BlockSpec((1, LANES), fixed),
        ],
        out_specs=[
            pl.BlockSpec((1, tm, D_MODEL), row),
            pl.BlockSpec((1, tm, D_MODEL), row),
            pl.BlockSpec((1, tm, LANES), row),
        ],
        out_shape=[
            jax.ShapeDtypeStruct((B, L, D_MODEL), F32),
            jax.ShapeDtypeStruct((B, L, D_MODEL), F32),
            jax.ShapeDtypeStruct((B, L, LANES), F32),
        ],
        compiler_params=_cparams(("arbitrary", "arbitrary")),
        name="outproj",
    )(y_ssd, y_rg, y_ret, x, g1, sc2, sh2, p["w_out"], p["ln1_g"], p["ln1_b"],
      consts["rw_hi"], consts["rw_lo"], consts["rbias"])


def _moe_kernel(be_ref, nu_ref, x_ref, *rest, precise):
    o_ref = rest[-1]
    wg, wu, wd = rest[0], rest[1], rest[2]
    wgl, wul, wdl = (rest[3], rest[4], rest[5]) if precise else (None, None, None)
    i = pl.program_id(0)

    @pl.when(i < nu_ref[0])
    def _():
        x = x_ref[...]
        g = _mm_w(x, wg[0], wgl[0] if precise else None)
        u = _mm_w(x, wu[0], wul[0] if precise else None)
        o_ref[...] = _mm_w(_silu(g) * u, wd[0], wdl[0] if precise else None)

    @pl.when(i >= nu_ref[0])
    def _():
        o_ref[...] = jnp.zeros_like(o_ref)


def _moe(xs, block_expert, n_used, weights, precise):
    P = xs.shape[0]
    nblk = P // MOE_BM
    in_w = pl.BlockSpec((1, D_MODEL, D_EXPERT), lambda i, be, nu: (be[i], 0, 0))
    out_w = pl.BlockSpec((1, D_EXPERT, D_MODEL), lambda i, be, nu: (be[i], 0, 0))
    grid_spec = pltpu.PrefetchScalarGridSpec(
        num_scalar_prefetch=2,
        grid=(nblk,),
        in_specs=[pl.BlockSpec((MOE_BM, D_MODEL), lambda i, be, nu: (i, 0))]
        + [in_w, in_w, out_w] * (2 if precise else 1),
        out_specs=pl.BlockSpec((MOE_BM, D_MODEL), lambda i, be, nu: (i, 0)),
    )
    return pl.pallas_call(
        functools.partial(_moe_kernel, precise=precise),
        grid_spec=grid_spec,
        out_shape=jax.ShapeDtypeStruct((P, D_MODEL), F32),
        compiler_params=_cparams(("arbitrary",)),
        name="moe",
    )(block_expert, n_used, xs, *weights)


def _combine_kernel(x1_ref, ya_ref, yb_ref, route_ref, g2_ref, lng_ref, lnb_ref, o_ref):
    tm = x1_ref.shape[1]
    lane = lax.broadcasted_iota(jnp.int32, (tm, LANES), 1)
    rt = route_ref[0]
    w1 = jnp.sum(jnp.where(lane == 2, rt, 0.0), axis=1, keepdims=True)
    w2 = jnp.sum(jnp.where(lane == 3, rt, 0.0), axis=1, keepdims=True)
    moe = ya_ref[0] * w1 + yb_ref[0] * w2
    o_ref[0] = _layernorm(ALPHA * x1_ref[0] + (1.0 + g2_ref[0]) * moe, lng_ref[...], lnb_ref[...])


def _combine(x1, ya, yb, route, g2, p, per_token_mod):
    B, L, _ = x1.shape
    tm = _pick_tm(L, PROJ_TM)
    row = lambda b, i: (b, i, 0)
    fixed = lambda b, i: (0, 0)
    if per_token_mod:
        mod_spec = pl.BlockSpec((1, tm, D_MODEL), row)
    else:
        mod_spec = pl.BlockSpec((1, 1, D_MODEL), lambda b, i: (b, 0, 0))
    return pl.pallas_call(
        _combine_kernel,
        grid=(B, L // tm),
        in_specs=[
            pl.BlockSpec((1, tm, D_MODEL), row),
            pl.BlockSpec((1, tm, D_MODEL), row),
            pl.BlockSpec((1, tm, D_MODEL), row),
            pl.BlockSpec((1, tm, LANES), row),
            mod_spec,
            pl.BlockSpec((1, D_MODEL), fixed),
            pl.BlockSpec((1, D_MODEL), fixed),
        ],
        out_specs=pl.BlockSpec((1, tm, D_MODEL), row),
        out_shape=jax.ShapeDtypeStruct((B, L, D_MODEL), F32),
        compiler_params=_cparams(("arbitrary", "arbitrary")),
        name="combine",
    )(x1, ya, yb, route, g2, p["ln2_g"], p["ln2_b"])


def _dispatch_plan(eidx):
    T = eidx.shape[0]
    A = 2 * T
    flat_e = eidx.reshape(A)
    onehot = (flat_e[:, None] == jnp.arange(N_EXPERTS, dtype=jnp.int32)[None, :]).astype(jnp.int32)
    cum = jnp.cumsum(onehot, axis=0)
    rank = jnp.take_along_axis(cum, flat_e[:, None], axis=1)[:, 0] - 1
    counts = cum[-1]
    padded = (counts + MOE_BM - 1) // MOE_BM * MOE_BM
    pad_end = jnp.cumsum(padded)
    pad_start = pad_end - padded
    dest = pad_start[flat_e] + rank
    nblk = (A + N_EXPERTS * (MOE_BM - 1) + MOE_BM - 1) // MOE_BM
    block_pos = jnp.arange(nblk, dtype=jnp.int32) * MOE_BM
    block_expert = jnp.minimum(jnp.sum(pad_end[None, :] <= block_pos[:, None], axis=1), N_EXPERTS - 1).astype(jnp.int32)
    n_used = (pad_end[-1] // MOE_BM).astype(jnp.int32).reshape(1)
    return dest, block_expert, n_used, nblk * MOE_BM


def _mods(mod_l, n_prompt, dec_seq):
    mp = mod_l[:n_prompt].reshape(n_prompt, 6, D_MODEL)
    ms = mod_l[n_prompt:].reshape(-1, 6, D_MODEL)
    prompt = [mp[:, j][:, None, :] for j in range(6)]
    sample = [jnp.repeat(ms[:, j], dec_seq, axis=0)[None] for j in range(6)]
    return prompt, sample


def _ssd_state_in(s):
    B = s.shape[0]
    s = s.reshape(B, SSD_GROUPS, SSD_HEADS // SSD_GROUPS, SSD_STATE, SSD_HEAD_DIM)
    return s.transpose(0, 1, 3, 2, 4).reshape(B, SSD_GROUPS, SSD_STATE, D_SSD // SSD_GROUPS)


def _ssd_state_out(s):
    B = s.shape[0]
    s = s.reshape(B, SSD_GROUPS, SSD_STATE, SSD_HEADS // SSD_GROUPS, SSD_HEAD_DIM)
    return s.transpose(0, 1, 3, 2, 4).reshape(B, SSD_HEADS, SSD_STATE, SSD_HEAD_DIM)


def kernel(x_prompt, x_sample, c_prompt, c_sample, state_ssd, state_ssd_conv, state_rglru, state_rglru_conv, state_ret, w_ada, b_ada, w_in, ssd_conv_w, ssd_conv_b, ssd_dt_bias, ssd_a_log, ssd_d, ssd_norm_w, rg_conv_w, rg_conv_b, rg_wa, rg_ba, rg_wx, rg_bx, rg_lambda, ret_gn_w, w_out, ln1_g, ln1_b, router_w, router_bias, exp_w_gate, exp_w_up, exp_w_down, ln2_g, ln2_b):
    BP, LP, _ = x_prompt.shape
    BS, LS, _ = x_sample.shape
    TP, TS = BP * LP, BS * LS

    def pad_lanes(v):
        return jnp.pad(v, ((0, 0), (0, LANES - v.shape[-1])))

    eye = jnp.eye(RG_BLOCKS, dtype=F32)

    def block_diag(w):
        return jnp.stack(_hi_lo(jnp.einsum("njk,nm->njmk", w, eye).reshape(D_RG, D_RG)))

    params = []
    for l in range(DEPTH):
        wi = w_in[l]
        wi = jnp.concatenate([wi[:, :2560], wi[:, 2576:5648], wi[:, 2560:2576],
                              jnp.zeros((D_MODEL, LANES - SSD_HEADS), F32)], axis=1)
        params.append(dict(
            w_in=_hi_lo(wi),
            ssd_conv_w=ssd_conv_w[l], ssd_conv_b=ssd_conv_b[l][None],
            ssd_dt_bias=pad_lanes(ssd_dt_bias[l][None]), ssd_a_log=pad_lanes(ssd_a_log[l][None]),
            ssd_d_ch=jnp.repeat(ssd_d[l], SSD_HEAD_DIM)[None], ssd_norm_w=ssd_norm_w[l][None],
            rg_conv_w=rg_conv_w[l], rg_conv_b=rg_conv_b[l][None],
            rg_wa_bd=block_diag(rg_wa[l]), rg_ba=rg_ba[l][None],
            rg_wx_bd=block_diag(rg_wx[l]), rg_bx=rg_bx[l][None],
            rg_lambda=rg_lambda[l][None], ret_gn_w=ret_gn_w[l][None],
            w_out=jnp.stack(_hi_lo(w_out[l])), ln1_g=ln1_g[l][None], ln1_b=ln1_b[l][None],
            experts=[h for w in (exp_w_gate[l], exp_w_up[l], exp_w_down[l]) for h in _hi_lo(w)],
            ln2_g=ln2_g[l][None], ln2_b=ln2_b[l][None],
        ))

    rw_hi, rw_lo = _hi_lo(pad_lanes(router_w))
    rbias = pad_lanes(router_bias[None])
    head_of_ch = jnp.arange(D_SSD) // SSD_HEAD_DIM
    consts = dict(
        rw_hi=rw_hi, rw_lo=rw_lo, rbias=rbias.astype(F32),
        head_expand=(jnp.arange(LANES)[:, None] == head_of_ch[None, :]).astype(BF16),
        tri=jnp.tril(jnp.ones((CHUNK, CHUNK), F32)).astype(BF16),
    )
    mod = _ada(jnp.concatenate([c_prompt, c_sample], axis=0), w_ada, b_ada)

    LM = LP - TAIL
    paths = [
        dict(x=x_prompt[:, :LM], B=BP, L=LM, per_token=False, has_state=False, precise=False, rc=_ret_consts(LM, 0.0)),
        dict(x=x_prompt[:, LM:], B=BP, L=TAIL, per_token=False, has_state=True, precise=True, rc=_ret_consts(TAIL, float(LM))),
        dict(x=x_sample.reshape(1, TS, D_MODEL), B=BS, L=LS, per_token=True, has_state=True, precise=False,
             rc=_ret_consts(LS, PAST_LEN)),
    ]
    zero_states = (jnp.zeros((BP, SSD_GROUPS, SSD_STATE, D_SSD // SSD_GROUPS), F32),
                   jnp.zeros((BP, CONV_WIDTH - 1, SSD_CONV_DIM), F32),
                   jnp.zeros((BP, 1, D_RG), F32),
                   jnp.zeros((BP, CONV_WIDTH - 1, D_RG), F32),
                   jnp.zeros((BP, RET_HEADS, RET_HEAD_DIM, RET_HEAD_DIM), F32))

    def mixer_stack(path, p, mods, states):
        B, L, per_token, precise = path["B"], path["L"], path["per_token"], path["precise"]
        w_in_l = p["w_in"] if precise else p["w_in"][:1]
        z, xbc, rg, ret, dt = _inproj(path["x"], mods[1], mods[0], w_in_l, per_token, precise)
        if per_token:
            z, xbc, rg, ret, dt = [a.reshape(B, L, a.shape[-1]) for a in (z, xbc, rg, ret, dt)]
        s_ssd, s_cbuf, s_rg, s_rbuf, s_ret = states
        hs = path["has_state"]
        y_ssd, cbuf_n, ssd_n = _ssd(z, xbc, dt, s_ssd, s_cbuf, p, consts, hs, precise)
        y_rg, rbuf_n, rg_n = _rg(rg, s_rg, s_rbuf, p, hs, precise)
        y_ret, ret_n = _ret(ret, s_ret, p, path["rc"], hs, precise)
        if per_token:
            y_ssd, y_rg, y_ret = [a.reshape(1, B * L, a.shape[-1]) for a in (y_ssd, y_rg, y_ret)]
        return (y_ssd, y_rg, y_ret), (ssd_n, cbuf_n, rg_n, rbuf_n, ret_n)

    def experts(h2_list, route_list, weights, precise):
        sizes = [h.shape[0] * h.shape[1] for h in h2_list]
        h2_all = jnp.concatenate([h.reshape(-1, D_MODEL) for h in h2_list], axis=0)
        route_all = jnp.concatenate([r.reshape(-1, LANES) for r in route_list], axis=0)
        T_all = h2_all.shape[0]
        dest, block_expert, n_used, P = _dispatch_plan(route_all[:, 0:2].astype(jnp.int32))
        tok = jnp.arange(2 * T_all, dtype=jnp.int32) // 2
        slot_tok = jnp.full((P,), T_all, jnp.int32).at[dest].set(tok)
        xs_sorted = jnp.concatenate([h2_all, jnp.zeros((1, D_MODEL), h2_all.dtype)], axis=0)[slot_tok]
        ys_sorted = _moe(xs_sorted, block_expert, n_used, weights, precise)
        ya_all = ys_sorted[dest[0::2]]
        yb_all = ys_sorted[dest[1::2]]
        outs, off = [], 0
        for h, n in zip(h2_list, sizes):
            outs.append((ya_all[off:off + n].reshape(h.shape), yb_all[off:off + n].reshape(h.shape)))
            off += n
        return outs

    new_tail, new_s = [], []
    for l in range(DEPTH):
        p = params[l]
        mods_p, mods_s = _mods(mod[l], BP, LS)
        mods = [mods_p, mods_p, mods_s]
        st_s = (_ssd_state_in(state_ssd[l]), state_ssd_conv[l], state_rglru[l][:, None, :],
                state_rglru_conv[l], state_ret[l])

        ys_m, st_m = mixer_stack(paths[0], p, mods[0], zero_states)
        ys_t, st_t = mixer_stack(paths[1], p, mods[1], st_m)
        ys_c, st_c = mixer_stack(paths[2], p, mods[2], st_s)
        new_tail.append(st_t)
        new_s.append(st_c)

        post = []
        for path, ys, m in zip(paths, (ys_m, ys_t, ys_c), mods):
            post.append(_outproj(*ys, path["x"], m[2], m[4], m[3], p, consts, path["per_token"], path["precise"]))

        hi_w = p["experts"][0::2]
        (ya_m, yb_m), (ya_c, yb_c) = experts([post[0][1], post[2][1]], [post[0][2], post[2][2]], hi_w, False)
        ((ya_t, yb_t),) = experts([post[1][1]], [post[1][2]], hi_w + p["experts"][1::2], True)

        for path, (x1, _, route), ya, yb, m in zip(paths, post, (ya_m, ya_t, ya_c), (yb_m, yb_t, yb_c), mods):
            path["x"] = _combine(x1, ya, yb, route, m[5], p, path["per_token"])

    def stack(lst, k):
        return jnp.stack([s[k] for s in lst])

    y_prompt = jnp.concatenate([paths[0]["x"], paths[1]["x"]], axis=1)
    return (y_prompt, paths[2]["x"].reshape(BS, LS, D_MODEL),
            jnp.stack([_ssd_state_out(s[0]) for s in new_tail]), stack(new_tail, 1), stack(new_tail, 2)[:, :, 0],
            stack(new_tail, 3), stack(new_tail, 4),
            jnp.stack([_ssd_state_out(s[0]) for s in new_s]), stack(new_s, 1), stack(new_s, 2)[:, :, 0],
            stack(new_s, 3), stack(new_s, 4))
```

```python
import functools
import math

import jax
import jax.numpy as jnp
from jax import lax
from jax.experimental import pallas as pl
from jax.experimental.pallas import tpu as pltpu

F32 = jnp.float32
BF16 = jnp.bfloat16

D_MODEL = 1024
DEPTH = 4
D_SSD = 1024
SSD_HEAD_DIM = 64
SSD_HEADS = 16
SSD_GROUPS = 2
SSD_STATE = 128
SSD_CONV_DIM = D_SSD + 2 * SSD_GROUPS * SSD_STATE
CONV_WIDTH = 4
D_RG = 512
RG_BLOCKS = 8
RG_C = 8.0
D_RET = 512
RET_HEADS = 4
RET_HEAD_DIM = 128
ROPE_BASE = 10000.0
N_EXPERTS = 32
EXPERTS_PER_GROUP = 8
N_EXPERT_GROUPS = 4
D_EXPERT = 512
ALPHA = (2 * DEPTH) ** 0.25
EPS = 1e-5
PAST_LEN = 16384.0

LANES = 128
SUBLANES = 8
CHUNK = 128
TAIL = CHUNK
PROJ_TM = 384
MOE_BM = 256
VMEM_LIMIT = 56 * 1024 * 1024


def _cparams(sem):
    return pltpu.CompilerParams(dimension_semantics=sem, vmem_limit_bytes=VMEM_LIMIT)


def _pick_tm(L, cap):
    if L <= LANES:
        return L
    tm = cap - cap % LANES
    while L % tm:
        tm -= LANES
    return tm


def _sigmoid(x):
    return 1.0 / (1.0 + jnp.exp(-x))


def _silu(x):
    return x * _sigmoid(x)


def _softplus(x):
    return jnp.maximum(x, 0.0) + jnp.log1p(jnp.exp(-jnp.abs(x)))


def _split3(v):
    hi = v.astype(BF16)
    r = v - hi.astype(F32)
    mid = r.astype(BF16)
    lo = (r - mid.astype(F32)).astype(BF16)
    return hi, mid, lo


def _dot(a, b):
    return jnp.dot(a, b, preferred_element_type=F32)


def _split2(v):
    hi = v.astype(BF16)
    return hi, (v - hi.astype(F32)).astype(BF16)


def _mm_w(a, w_hi, w_lo):
    if w_lo is None:
        return _dot(a.astype(BF16), w_hi)
    ah, al = _split2(a)
    return _dot(ah, w_hi) + (_dot(al, w_hi) + _dot(ah, w_lo))


def _mm_a(a, b, precise, nt=False):
    dn = (((1,), (1,)), ((), ())) if nt else (((1,), (0,)), ((), ()))

    def d(x, y):
        return lax.dot_general(x, y, dn, preferred_element_type=F32)

    if not precise:
        return d(a.astype(BF16), b.astype(BF16))
    ah, al = _split2(a)
    bh, bl = _split2(b)
    return d(ah, bh) + (d(al, bh) + d(ah, bl))


def _hi_lo(w):
    hi32 = lax.reduce_precision(w, exponent_bits=8, mantissa_bits=7)
    return hi32.astype(BF16), (w - hi32).astype(BF16)


def _dot_exact_rhs(v, m):
    hi, mid, lo = _split3(v)
    return _dot(hi, m) + _dot(mid, m) + _dot(lo, m)


def _dot_exact_lhs(m, v):
    hi, mid, lo = _split3(v)
    return _dot(m, hi) + _dot(m, mid) + _dot(m, lo)


def _ada_kernel(c_ref, w_ref, b_ref, o_ref):
    o_ref[0] = _mm_a(_silu(c_ref[...]), w_ref[0], True) + b_ref[0]


def _ada(c_all, w_ada, b_ada):
    n = c_all.shape[0]
    tn = 1024
    return pl.pallas_call(
        _ada_kernel,
        grid=(DEPTH, 6 * D_MODEL // tn),
        in_specs=[
            pl.BlockSpec((n, D_MODEL), lambda l, j: (0, 0)),
            pl.BlockSpec((1, D_MODEL, tn), lambda l, j: (l, 0, j)),
            pl.BlockSpec((1, 1, tn), lambda l, j: (l, 0, j)),
        ],
        out_specs=pl.BlockSpec((1, n, tn), lambda l, j: (l, 0, j)),
        out_shape=jax.ShapeDtypeStruct((DEPTH, n, 6 * D_MODEL), F32),
        compiler_params=_cparams(("arbitrary", "arbitrary")),
        name="ada",
    )(c_all, w_ada, b_ada.reshape(DEPTH, 1, 6 * D_MODEL))


_IN_SEGS = (("z", 0, 1024), ("xbc", 1024, 2560), ("rg", 2560, 3584), ("ret", 3584, 5632), ("dt", 5632, 5760))
IN_PAD = 5760


def _inproj_kernel(x_ref, sc_ref, sh_ref, w_ref, *rest, precise):
    wl_ref = rest[0] if precise else None
    outs = rest[1:] if precise else rest
    h = x_ref[0] * (1.0 + sc_ref[0]) + sh_ref[0]
    if precise:
        hh, hl = _split2(h)
    else:
        hh = h.astype(BF16)
    for o_ref, (_, lo, hi) in zip(outs, _IN_SEGS):
        acc = _dot(hh, w_ref[:, lo:hi])
        if precise:
            acc = acc + (_dot(hl, w_ref[:, lo:hi]) + _dot(hh, wl_ref[:, lo:hi]))
        o_ref[0] = acc


def _inproj(x, sc, sh, w, per_token_mod, precise):
    B, L, _ = x.shape
    tm = _pick_tm(L, LANES if precise else PROJ_TM)
    if per_token_mod:
        mod_spec = pl.BlockSpec((1, tm, D_MODEL), lambda b, i: (b, i, 0))
    else:
        mod_spec = pl.BlockSpec((1, 1, D_MODEL), lambda b, i: (b, 0, 0))
    widths = [hi - lo for _, lo, hi in _IN_SEGS]
    w_spec = pl.BlockSpec((D_MODEL, IN_PAD), lambda b, i: (0, 0), pipeline_mode=pl.Buffered(1))
    ws = w if precise else w[:1]
    return pl.pallas_call(
        functools.partial(_inproj_kernel, precise=precise),
        grid=(B, L // tm),
        in_specs=[
            pl.BlockSpec((1, tm, D_MODEL), lambda b, i: (b, i, 0)),
            mod_spec,
            mod_spec,
        ] + [w_spec] * len(ws),
        out_specs=[pl.BlockSpec((1, tm, wd), lambda b, i: (b, i, 0)) for wd in widths],
        out_shape=[jax.ShapeDtypeStruct((B, L, wd), F32) for wd in widths],
        compiler_params=_cparams(("arbitrary", "arbitrary")),
        name="inproj",
    )(x, sc, sh, *ws)


def _conv_step(src_rows, buf_ref, cw_ref, cb_ref, nbuf_ref, xp_s, *, c, nc, Lv, Lp, has_state):
    @pl.when(c == 0)
    def _():
        xp_s[...] = jnp.zeros_like(xp_s)
        if has_state:
            xp_s[SUBLANES - (CONV_WIDTH - 1):SUBLANES, :] = buf_ref[0]

    xp_s[SUBLANES:SUBLANES + Lv, :] = src_rows
    base = SUBLANES - (CONV_WIDTH - 1)
    xc = cb_ref[...]
    for j in range(CONV_WIDTH):
        xc = xc + cw_ref[j:j + 1, :] * xp_s[base + j:base + j + Lp, :]

    @pl.when(c == nc - 1)
    def _():
        nbuf_ref[0] = xp_s[SUBLANES + Lv - (CONV_WIDTH - 1):SUBLANES + Lv, :]

    if nc > 1:
        xp_s[0:SUBLANES, :] = xp_s[Lp:Lp + SUBLANES, :]
    return xc


def _ssd_kernel(z_ref, xbc_ref, dt_ref, h0_ref, buf_ref, cw_ref, cb_ref, dtb_ref, alog_ref, dch_ref,
                nw_ref, e_ref, tri_ref, *rest, Lv, Lp, nc, has_state, precise, native_in, native_out):
    y_ref, nbuf_ref, hn_ref, xp_s, dtp_s, s_s = rest[-6:]
    c = pl.program_id(1)
    hpg = SSD_HEADS // SSD_GROUPS

    @pl.when(c == 0)
    def _():
        if not has_state:
            s_s[...] = jnp.zeros_like(s_s)
        elif native_in:
            for g in range(SSD_GROUPS):
                s_s[g] = jnp.concatenate([h0_ref[g * hpg + i] for i in range(hpg)], axis=1)
        else:
            s_s[...] = h0_ref[...]
        if Lv < Lp:
            dtp_s[...] = jnp.zeros_like(dtp_s)

    xc = _conv_step(xbc_ref[0], buf_ref, cw_ref, cb_ref, nbuf_ref, xp_s,
                    c=c, nc=nc, Lv=Lv, Lp=Lp, has_state=has_state)
    xbc = _silu(xc)
    xs = xbc[:, 0:D_SSD]
    Bm = xbc[:, D_SSD:D_SSD + 256]
    Cm = xbc[:, D_SSD + 256:D_SSD + 512]

    if Lv < Lp:
        dtp_s[0:Lv, :] = dt_ref[0]
        dtr = dtp_s[...]
    else:
        dtr = dt_ref[0]
    dt = _softplus(dtr + dtb_ref[...])
    if Lv < Lp:
        rowv = lax.broadcasted_iota(jnp.int32, (Lp, LANES), 0)
        dt = jnp.where(rowv < Lv, dt, 0.0)
    a = dt * (-jnp.exp(alog_ref[...]))
    acum = _dot_exact_lhs(tri_ref[...], a)
    dt_e = _dot_exact_rhs(dt, e_ref[...])
    ac_e = _dot_exact_rhs(acum, e_ref[...])
    acl_e = ac_e[Lp - 1:Lp, :]
    acum_t = acum.T

    vdt = xs * dt_e
    lane = lax.broadcasted_iota(jnp.int32, (Lp, LANES), 1)
    causal = lax.broadcasted_iota(jnp.int32, (Lp, Lp), 0) >= lax.broadcasted_iota(jnp.int32, (Lp, Lp), 1)
    mid_t = F32 if precise else BF16

    y_blocks = []
    for g in range(SSD_GROUPS):
        Cg = Cm[:, g * SSD_STATE:(g + 1) * SSD_STATE]
        Bg = Bm[:, g * SSD_STATE:(g + 1) * SSD_STATE]
        G = _mm_a(Cg, Bg, precise, nt=True)
        for jj in range(4):
            j = 4 * g + jj
            scs = []
            for h in (2 * j, 2 * j + 1):
                col = jnp.sum(jnp.where(lane == h, acum, 0.0), axis=1, keepdims=True)
                rw = acum_t[h:h + 1, :]
                dec = jnp.exp(jnp.where(causal, col - rw, -jnp.inf))
                scs.append((G * dec).astype(mid_t))
            sc = jnp.concatenate(scs, axis=1)
            vb = vdt[:, j * LANES:(j + 1) * LANES]
            v2 = jnp.concatenate([jnp.where(lane < SSD_HEAD_DIM, vb, 0.0),
                                  jnp.where(lane >= SSD_HEAD_DIM, vb, 0.0)], axis=0).astype(mid_t)
            y_blocks.append(_mm_a(sc, v2, precise))
    y = jnp.concatenate(y_blocks, axis=1)

    eac = jnp.exp(ac_e)
    vw = vdt * jnp.exp(acl_e - ac_e)
    cd = jnp.exp(acl_e)
    half = D_SSD // SSD_GROUPS
    ys_parts = []
    for g in range(SSD_GROUPS):
        Sg = s_s[g]
        ys_parts.append(_mm_a(Cm[:, g * SSD_STATE:(g + 1) * SSD_STATE], Sg, precise))
        BgT = Bm[:, g * SSD_STATE:(g + 1) * SSD_STATE].T
        s_s[g] = cd[:, g * half:(g + 1) * half] * Sg + _mm_a(BgT, vw[:, g * half:(g + 1) * half], precise)
    y = y + eac * jnp.concatenate(ys_parts, axis=1) + dch_ref[...] * xs

    yv = y[0:Lv, :] if Lv < Lp else y
    yz = yv * _silu(z_ref[0])
    outs = []
    for g in range(SSD_GROUPS):
        part = yz[:, g * half:(g + 1) * half]
        ms = jnp.mean(part * part, axis=1, keepdims=True)
        outs.append(part * lax.rsqrt(ms + EPS))
    y_ref[0] = jnp.concatenate(outs, axis=1) * nw_ref[...]

    @pl.when(c == nc - 1)
    def _():
        if native_out:
            for g in range(SSD_GROUPS):
                sg = s_s[g]
                for i in range(hpg):
                    hn_ref[g * hpg + i] = sg[:, i * SSD_HEAD_DIM:(i + 1) * SSD_HEAD_DIM]
        else:
            hn_ref[...] = s_s[...]


def _state_specs(B, shape, in_layer, out_stack):
    zeros = (0,) * len(shape)
    if in_layer is None:
        in_spec = pl.BlockSpec((None,) + shape, lambda b, c: (b,) + zeros)
    else:
        in_spec = pl.BlockSpec((None, None) + shape, lambda b, c: (in_layer, b) + zeros)
    if out_stack is None:
        out_spec = pl.BlockSpec((None,) + shape, lambda b, c: (b,) + zeros)
        out_shape = jax.ShapeDtypeStruct((B,) + shape, F32)
    else:
        l = out_stack[0]
        out_spec = pl.BlockSpec((None, None) + shape, lambda b, c: (l, b) + zeros)
        out_shape = jax.ShapeDtypeStruct((DEPTH, B) + shape, F32)
    return in_spec, out_spec, out_shape


def _ssd(z, xbc, dt, h0, buf, p, consts, has_state, precise, h0_layer=None, native_out=False, out_stack=None):
    B, L, _ = z.shape
    Lv = min(CHUNK, L)
    Lp = CHUNK
    nc = L // Lv
    native_in = h0_layer is not None
    kern = functools.partial(_ssd_kernel, Lv=Lv, Lp=Lp, nc=nc, has_state=has_state, precise=precise,
                             native_in=native_in, native_out=native_out)
    row = lambda b, c: (b, c, 0)
    per_b3 = lambda b, c: (b, 0, 0)
    fixed = lambda b, c: (0, 0)
    half = D_SSD // SSD_GROUPS
    native = (SSD_HEADS, SSD_STATE, SSD_HEAD_DIM)
    packed = (SSD_GROUPS, SSD_STATE, half)
    h0_spec, _, _ = _state_specs(B, native if native_in else packed, h0_layer, None)
    _, hn_spec, hn_shape = _state_specs(B, native if native_out else packed, None, out_stack)
    prev = out_stack[1] if out_stack is not None else None
    extra_in, extra_specs, aliases = [], [], {}
    if prev is not None:
        extra_in, extra_specs, aliases = [prev], [pl.BlockSpec(memory_space=pl.ANY)], {13: 2}
    return pl.pallas_call(
        kern,
        grid=(B, nc),
        in_specs=[
            pl.BlockSpec((1, Lv, D_SSD), row),
            pl.BlockSpec((1, Lv, SSD_CONV_DIM), row),
            pl.BlockSpec((1, Lv, LANES), row),
            h0_spec,
            pl.BlockSpec((1, CONV_WIDTH - 1, SSD_CONV_DIM), per_b3),
            pl.BlockSpec((CONV_WIDTH, SSD_CONV_DIM), fixed),
            pl.BlockSpec((1, SSD_CONV_DIM), fixed),
            pl.BlockSpec((1, LANES), fixed),
            pl.BlockSpec((1, LANES), fixed),
            pl.BlockSpec((1, D_SSD), fixed),
            pl.BlockSpec((1, D_SSD), fixed),
            pl.BlockSpec((LANES, D_SSD), fixed),
            pl.BlockSpec((Lp, Lp), fixed),
        ] + extra_specs,
        out_specs=[
            pl.BlockSpec((1, Lv, D_SSD), row),
            pl.BlockSpec((1, CONV_WIDTH - 1, SSD_CONV_DIM), per_b3),
            hn_spec,
        ],
        out_shape=[
            jax.ShapeDtypeStruct((B, L, D_SSD), F32),
            jax.ShapeDtypeStruct((B, CONV_WIDTH - 1, SSD_CONV_DIM), F32),
            hn_shape,
        ],
        scratch_shapes=[
            pltpu.VMEM((SUBLANES + Lp, SSD_CONV_DIM), F32),
            pltpu.VMEM((Lp, LANES), F32),
            pltpu.VMEM((SSD_GROUPS, SSD_STATE, half), F32),
        ],
        input_output_aliases=aliases,
        compiler_params=_cparams(("arbitrary", "arbitrary")),
        name="ssd",
    )(z, xbc, dt, h0, buf, p["ssd_conv_w"], p["ssd_conv_b"], p["ssd_dt_bias"], p["ssd_a_log"],
      p["ssd_d_ch"], p["ssd_norm_w"], consts["head_expand"], consts["tri"], *extra_in)


def _rg_kernel(rg_ref, h0_ref, buf_ref, cw_ref, cb_ref, wa_ref, ba_ref, wx_ref, bx_ref, lam_ref,
               y_ref, nbuf_ref, hn_ref, xp_s, hc_s, *, Lv, Lp, nc, has_state, precise):
    c = pl.program_id(1)

    @pl.when(c == 0)
    def _():
        if has_state:
            hc_s[...] = h0_ref[0]
        else:
            hc_s[...] = jnp.zeros_like(hc_s)

    xc = _conv_step(rg_ref[0, :, 0:D_RG], buf_ref, cw_ref, cb_ref, nbuf_ref, xp_s,
                    c=c, nc=nc, Lv=Lv, Lp=Lp, has_state=has_state)
    r = _sigmoid(_mm_w(xc, wa_ref[0], wa_ref[1] if precise else None) + ba_ref[...])
    i = _sigmoid(_mm_w(xc, wx_ref[0], wx_ref[1] if precise else None) + bx_ref[...])
    la = (-RG_C * _softplus(-lam_ref[...])) * r
    u = jnp.sqrt(-jnp.tanh(la) * (jnp.exp(2.0 * la) + 1.0)) * (i * xc)
    row = lax.broadcasted_iota(jnp.int32, (Lp, D_RG), 0)
    if Lv < Lp:
        valid = row < Lv
        la = jnp.where(valid, la, 0.0)
        u = jnp.where(valid, u, 0.0)
    a = jnp.exp(la)
    s = 1
    while s < Lp:
        m = row >= s
        u_sh = jnp.where(m, pltpu.roll(u, s, 0), 0.0)
        a_sh = jnp.where(m, pltpu.roll(a, s, 0), 1.0)
        u = u + a * u_sh
        a = a * a_sh
        s *= 2
    h = u + a * hc_s[...]
    hc_s[...] = h[Lp - 1:Lp, :]
    hv = h[0:Lv, :] if Lv < Lp else h
    y_ref[0] = hv * jax.nn.gelu(rg_ref[0, :, D_RG:2 * D_RG])

    @pl.when(c == nc - 1)
    def _():
        hn_ref[0] = hc_s[...]


def _rg(rg, h0, buf, p, has_state, precise):
    B, L, _ = rg.shape
    Lv = min(CHUNK, L)
    Lp = CHUNK
    nc = L // Lv
    kern = functools.partial(_rg_kernel, Lv=Lv, Lp=Lp, nc=nc, has_state=has_state, precise=precise)
    row = lambda b, c: (b, c, 0)
    per_b3 = lambda b, c: (b, 0, 0)
    fixed = lambda b, c: (0, 0)
    fixed3 = lambda b, c: (0, 0, 0)
    return pl.pallas_call(
        kern,
        grid=(B, nc),
        in_specs=[
            pl.BlockSpec((1, Lv, 2 * D_RG), row),
            pl.BlockSpec((1, 1, D_RG), per_b3),
            pl.BlockSpec((1, CONV_WIDTH - 1, D_RG), per_b3),
            pl.BlockSpec((CONV_WIDTH, D_RG), fixed),
            pl.BlockSpec((1, D_RG), fixed),
            pl.BlockSpec((2, D_RG, D_RG), fixed3),
            pl.BlockSpec((1, D_RG), fixed),
            pl.BlockSpec((2, D_RG, D_RG), fixed3),
            pl.BlockSpec((1, D_RG), fixed),
            pl.BlockSpec((1, D_RG), fixed),
        ],
        out_specs=[
            pl.BlockSpec((1, Lv, D_RG), row),
            pl.BlockSpec((1, CONV_WIDTH - 1, D_RG), per_b3),
            pl.BlockSpec((1, 1, D_RG), per_b3),
        ],
        out_shape=[
            jax.ShapeDtypeStruct((B, L, D_RG), F32),
            jax.ShapeDtypeStruct((B, CONV_WIDTH - 1, D_RG), F32),
            jax.ShapeDtypeStruct((B, 1, D_RG), F32),
        ],
        scratch_shapes=[
            pltpu.VMEM((SUBLANES + Lp, D_RG), F32),
            pltpu.VMEM((1, D_RG), F32),
        ],
        compiler_params=_cparams(("arbitrary", "arbitrary")),
        name="rglru",
    )(rg, h0, buf, p["rg_conv_w"], p["rg_conv_b"], p["rg_wa_bd"], p["rg_ba"], p["rg_wx_bd"],
      p["rg_bx"], p["rg_lambda"])


def _ret_kernel(ret_ref, h0_ref, cos_ref, sin_ref, dm_ref, ea_ref, te_ref, cd_ref, gn_ref,
                *rest, Lv, Lp, nc, has_state, precise):
    y_ref, hn_ref, pad_s, s_s = rest[-4:]
    c = pl.program_id(1)

    @pl.when(c == 0)
    def _():
        if has_state:
            s_s[...] = h0_ref[...]
        else:
            s_s[...] = jnp.zeros_like(s_s)
        if Lv < Lp:
            pad_s[...] = jnp.zeros_like(pad_s)

    if Lv < Lp:
        pad_s[0:Lv, :] = ret_ref[0, :, 0:3 * D_RET]
        qkv = pad_s[...]
    else:
        qkv = ret_ref[0, :, 0:3 * D_RET]
    cosf = cos_ref[...]
    sinf = sin_ref[...]
    outs = []
    for h in range(RET_HEADS):
        sl = slice(h * RET_HEAD_DIM, (h + 1) * RET_HEAD_DIM)
        q = qkv[:, sl]
        k = qkv[:, D_RET + h * RET_HEAD_DIM:D_RET + (h + 1) * RET_HEAD_DIM]
        v = qkv[:, 2 * D_RET + h * RET_HEAD_DIM:2 * D_RET + (h + 1) * RET_HEAD_DIM]
        q = q * cosf + pltpu.roll(q, RET_HEAD_DIM // 2, 1) * sinf
        k = (k * cosf + pltpu.roll(k, RET_HEAD_DIM // 2, 1) * sinf) * (RET_HEAD_DIM ** -0.5)
        G = _mm_a(q, k, precise, nt=True)
        S = s_s[h]
        y = _mm_a(G * dm_ref[h], v, precise) + _mm_a(q * ea_ref[h], S, precise)
        s_s[h] = cd_ref[h] * S + _mm_a((k * te_ref[h]).T, v, precise)
        yv = y[0:Lv, :] if Lv < Lp else y
        mu = jnp.mean(yv, axis=1, keepdims=True)
        d = yv - mu
        var = jnp.mean(d * d, axis=1, keepdims=True)
        outs.append(d * lax.rsqrt(var + EPS))
    yn = jnp.concatenate(outs, axis=1) * gn_ref[...]
    y_ref[0] = _silu(ret_ref[0, :, 3 * D_RET:4 * D_RET]) * yn

    @pl.when(c == nc - 1)
    def _():
        hn_ref[...] = s_s[...]


def _ret(ret, h0, p, rc, has_state, precise, h0_layer=None, out_stack=None):
    B, L, _ = ret.shape
    Lv = min(CHUNK, L)
    Lp = CHUNK
    nc = L // Lv
    kern = functools.partial(_ret_kernel, Lv=Lv, Lp=Lp, nc=nc, has_state=has_state, precise=precise)
    row = lambda b, c: (b, c, 0)
    fixed2 = lambda b, c: (0, 0)
    fixed3 = lambda b, c: (0, 0, 0)
    hd = RET_HEAD_DIM
    h0_spec, _, _ = _state_specs(B, (RET_HEADS, hd, hd), h0_layer, None)
    _, hn_spec, hn_shape = _state_specs(B, (RET_HEADS, hd, hd), None, out_stack)
    prev = out_stack[1] if out_stack is not None else None
    extra_in, extra_specs, aliases = [], [], {}
    if prev is not None:
        extra_in, extra_specs, aliases = [prev], [pl.BlockSpec(memory_space=pl.ANY)], {9: 1}
    return pl.pallas_call(
        kern,
        grid=(B, nc),
        in_specs=[
            pl.BlockSpec((1, Lv, 4 * D_RET), row),
            h0_spec,
            pl.BlockSpec((Lp, hd), lambda b, c: (c, 0)),
            pl.BlockSpec((Lp, hd), lambda b, c: (c, 0)),
            pl.BlockSpec((RET_HEADS, Lp, Lp), fixed3),
            pl.BlockSpec((RET_HEADS, Lp, hd), fixed3),
            pl.BlockSpec((RET_HEADS, Lp, hd), fixed3),
            pl.BlockSpec((RET_HEADS, hd, hd), fixed3),
            pl.BlockSpec((1, D_RET), fixed2),
        ] + extra_specs,
        out_specs=[
            pl.BlockSpec((1, Lv, D_RET), row),
            hn_spec,
        ],
        out_shape=[
            jax.ShapeDtypeStruct((B, L, D_RET), F32),
            hn_shape,
        ],
        scratch_shapes=[
            pltpu.VMEM((Lp, 3 * D_RET), F32),
            pltpu.VMEM((RET_HEADS, hd, hd), F32),
        ],
        input_output_aliases=aliases,
        compiler_params=_cparams(("arbitrary", "arbitrary")),
        name="retention",
    )(ret, h0, rc["cos"], rc["sin"], rc["dmat"], rc["eacum"], rc["toend"], rc["cdec"], p["ret_gn_w"], *extra_in)


def _ret_consts(L, pos0):
    Lv = min(CHUNK, L)
    Lp = CHUNK
    nc = L // Lv
    half = RET_HEAD_DIM // 2
    inv = ROPE_BASE ** (-jnp.arange(half, dtype=F32) / half)
    pos = pos0 + jnp.arange(L, dtype=F32)
    ang = pos[:, None] * inv[None, :]
    cos = jnp.cos(ang)
    sin = jnp.sin(ang)
    cosf = jnp.concatenate([cos, cos], axis=1)
    sinf = jnp.concatenate([-sin, sin], axis=1)
    if Lv < Lp:
        cosf = jnp.pad(cosf, ((0, Lp - Lv), (0, 0)))
        sinf = jnp.pad(sinf, ((0, Lp - Lv), (0, 0)))
    assert cosf.shape[0] == nc * Lp
    log_gamma = jnp.log1p(-jnp.exp2(-5.0 - jnp.arange(RET_HEADS, dtype=F32)))
    steps = jnp.minimum(jnp.arange(Lp) + 1, Lv).astype(F32)
    acum = log_gamma[:, None] * steps[None, :]
    causal = jnp.tril(jnp.ones((Lp, Lp), bool))
    dmat = jnp.exp(jnp.where(causal[None], acum[:, :, None] - acum[:, None, :], -jnp.inf))
    ones = jnp.ones((RET_HEADS, Lp, RET_HEAD_DIM), F32)
    eacum = jnp.exp(acum)[:, :, None] * ones
    toend = jnp.exp(acum[:, -1:] - acum)[:, :, None] * ones
    rowvalid = (jnp.arange(Lp) < Lv).astype(F32)[None, :, None]
    toend = toend * rowvalid
    cdec = jnp.exp(acum[:, -1])[:, None, None] * jnp.ones((RET_HEADS, RET_HEAD_DIM, RET_HEAD_DIM), F32)
    return dict(cos=cosf, sin=sinf, dmat=dmat, eacum=eacum, toend=toend, cdec=cdec)


def _layernorm(v, g, b):
    mu = jnp.mean(v, axis=1, keepdims=True)
    d = v - mu
    var = jnp.mean(d * d, axis=1, keepdims=True)
    return d * lax.rsqrt(var + EPS) * g + b


def _route(logits, bias):
    tm = logits.shape[0]
    lane = lax.broadcasted_iota(jnp.int32, (tm, LANES), 1)
    lanef = lane.astype(F32)
    scores = _sigmoid(logits)
    choice = scores + bias
    neg = -jnp.inf
    best = jnp.full((tm, 1), neg, F32)
    e1 = jnp.zeros((tm, 1), F32)
    e2 = jnp.zeros((tm, 1), F32)
    for g in range(N_EXPERT_GROUPS):
        ing = (lane >= g * EXPERTS_PER_GROUP) & (lane < (g + 1) * EXPERTS_PER_GROUP)
        cg = jnp.where(ing, choice, neg)
        m1 = jnp.max(cg, axis=1, keepdims=True)
        i1 = jnp.min(jnp.where(cg == m1, lanef, float(LANES)), axis=1, keepdims=True)
        cg2 = jnp.where(lanef == i1, neg, cg)
        m2 = jnp.max(cg2, axis=1, keepdims=True)
        i2 = jnp.min(jnp.where(cg2 == m2, lanef, float(LANES)), axis=1, keepdims=True)
        gs = m1 + m2
        better = gs > best
        best = jnp.where(better, gs, best)
        e1 = jnp.where(better, i1, e1)
        e2 = jnp.where(better, i2, e2)
    w1 = jnp.sum(jnp.where(lanef == e1, scores, 0.0), axis=1, keepdims=True)
    w2 = jnp.sum(jnp.where(lanef == e2, scores, 0.0), axis=1, keepdims=True)
    den = w1 + w2
    out = jnp.where(lane == 0, e1, jnp.where(lane == 1, e2, jnp.where(lane == 2, w1 / den, jnp.where(lane == 3, w2 / den, 0.0))))
    return out


def _outproj_kernel(ys_ref, yr_ref, yt_ref, x_ref, g1_ref, sc2_ref, sh2_ref, w_ref, lng_ref, lnb_ref,
                    rwh_ref, rwl_ref, rb_ref, x1_ref, h2_ref, route_ref, *, precise):
    def part(y_ref, lo, hi):
        return _mm_w(y_ref[0], w_ref[0, lo:hi, :], w_ref[1, lo:hi, :] if precise else None)

    mix = part(ys_ref, 0, D_SSD) + part(yr_ref, D_SSD, D_SSD + D_RG) + part(yt_ref, D_SSD + D_RG, 2 * D_MODEL)
    x1 = _layernorm(ALPHA * x_ref[0] + (1.0 + g1_ref[0]) * mix, lng_ref[...], lnb_ref[...])
    x1_ref[0] = x1
    h2 = x1 * (1.0 + sc2_ref[0]) + sh2_ref[0]
    hi, lo = _split2(h2)
    h2_ref[0] = h2
    logits = _dot(hi, rwh_ref[...]) + (_dot(lo, rwh_ref[...]) + _dot(hi, rwl_ref[...]))
    route_ref[0] = _route(logits, rb_ref[...])


def _outproj(y_ssd, y_rg, y_ret, x, g1, sc2, sh2, p, consts, per_token_mod, precise):
    B, L, _ = x.shape
    tm = _pick_tm(L, LANES if precise else PROJ_TM)
    row = lambda b, i: (b, i, 0)
    fixed = lambda b, i: (0, 0)
    if per_token_mod:
        mod_spec = pl.BlockSpec((1, tm, D_MODEL), row)
    else:
        mod_spec = pl.BlockSpec((1, 1, D_MODEL), lambda b, i: (b, 0, 0))
    return pl.pallas_call(
        functools.partial(_outproj_kernel, precise=precise),
        grid=(B, L // tm),
        in_specs=[
            pl.BlockSpec((1, tm, D_SSD), row),
            pl.BlockSpec((1, tm, D_RG), row),
            pl.BlockSpec((1, tm, D_RET), row),
            pl.BlockSpec((1, tm, D_MODEL), row),
            mod_spec, mod_spec, mod_spec,
            pl.BlockSpec((2, 2 * D_MODEL, D_MODEL), lambda b, i: (0, 0, 0)),
            pl.BlockSpec((1, D_MODEL), fixed),
            pl.BlockSpec((1, D_MODEL), fixed),
            pl.BlockSpec((D_MODEL, LANES), fixed),
            pl.BlockSpec((D_MODEL, LANES), fixed),
            pl.BlockSpec((1, LANES), fixed),
        ],
        out_specs=[
            pl.BlockSpec((1, tm, D_MODEL), row),
            pl.BlockSpec((1, tm, D_MODEL), row),
            pl.BlockSpec((1, tm, LANES), row),
        ],
        out_shape=[
            jax.ShapeDtypeStruct((B, L, D_MODEL), F32),
            jax.ShapeDtypeStruct((B, L, D_MODEL), F32),
            jax.ShapeDtypeStruct((B, L, LANES), F32),
        ],
        compiler_params=_cparams(("arbitrary", "arbitrary")),
        name="outproj",
    )(y_ssd, y_rg, y_ret, x, g1, sc2, sh2, p["w_out"], p["ln1_g"], p["ln1_b"],
      consts["rw_hi"], consts["rw_lo"], consts["rbias"])


def _moe_kernel(be_ref, nu_ref, x_ref, *rest, precise):
    o_ref = rest[-1]
    wg, wu, wd = rest[0], rest[1], rest[2]
    wgl, wul, wdl = (rest[3], rest[4], rest[5]) if precise else (None, None, None)
    i = pl.program_id(0)

    @pl.when(i < nu_ref[0])
    def _():
        x = x_ref[...]
        g = _mm_w(x, wg[0], wgl[0] if precise else None)
        u = _mm_w(x, wu[0], wul[0] if precise else None)
        o_ref[...] = _mm_w(_silu(g) * u, wd[0], wdl[0] if precise else None)

    @pl.when(i >= nu_ref[0])
    def _():
        o_ref[...] = jnp.zeros_like(o_ref)


def _moe(xs, block_expert, n_used, weights, precise):
    P = xs.shape[0]
    nblk = P // MOE_BM
    in_w = pl.BlockSpec((1, D_MODEL, D_EXPERT), lambda i, be, nu: (be[i], 0, 0))
    out_w = pl.BlockSpec((1, D_EXPERT, D_MODEL), lambda i, be, nu: (be[i], 0, 0))
    grid_spec = pltpu.PrefetchScalarGridSpec(
        num_scalar_prefetch=2,
        grid=(nblk,),
        in_specs=[pl.BlockSpec((MOE_BM, D_MODEL), lambda i, be, nu: (i, 0))]
        + [in_w, in_w, out_w] * (2 if precise else 1),
        out_specs=pl.BlockSpec((MOE_BM, D_MODEL), lambda i, be, nu: (i, 0)),
    )
    return pl.pallas_call(
        functools.partial(_moe_kernel, precise=precise),
        grid_spec=grid_spec,
        out_shape=jax.ShapeDtypeStruct((P, D_MODEL), F32),
        compiler_params=_cparams(("arbitrary",)),
        name="moe",
    )(block_expert, n_used, xs, *weights)


def _combine_kernel(x1_ref, ya_ref, yb_ref, route_ref, g2_ref, lng_ref, lnb_ref, o_ref):
    tm = x1_ref.shape[1]
    lane = lax.broadcasted_iota(jnp.int32, (tm, LANES), 1)
    rt = route_ref[0]
    w1 = jnp.sum(jnp.where(lane == 2, rt, 0.0), axis=1, keepdims=True)
    w2 = jnp.sum(jnp.where(lane == 3, rt, 0.0), axis=1, keepdims=True)
    moe = ya_ref[...] * w1 + yb_ref[...] * w2
    o_ref[0] = _layernorm(ALPHA * x1_ref[0] + (1.0 + g2_ref[0]) * moe, lng_ref[...], lnb_ref[...])


def _combine(x1, ya, yb, row_off, route, g2, p, per_token_mod):
    B, L, _ = x1.shape
    tm = _pick_tm(L, PROJ_TM)
    assert row_off % tm == 0
    row = lambda b, i: (b, i, 0)
    fixed = lambda b, i: (0, 0)
    flat = lambda b, i: (row_off // tm + b * (L // tm) + i, 0)
    if per_token_mod:
        mod_spec = pl.BlockSpec((1, tm, D_MODEL), row)
    else:
        mod_spec = pl.BlockSpec((1, 1, D_MODEL), lambda b, i: (b, 0, 0))
    return pl.pallas_call(
        _combine_kernel,
        grid=(B, L // tm),
        in_specs=[
            pl.BlockSpec((1, tm, D_MODEL), row),
            pl.BlockSpec((tm, D_MODEL), flat),
            pl.BlockSpec((tm, D_MODEL), flat),
            pl.BlockSpec((1, tm, LANES), row),
            mod_spec,
            pl.BlockSpec((1, D_MODEL), fixed),
            pl.BlockSpec((1, D_MODEL), fixed),
        ],
        out_specs=pl.BlockSpec((1, tm, D_MODEL), row),
        out_shape=jax.ShapeDtypeStruct((B, L, D_MODEL), F32),
        compiler_params=_cparams(("arbitrary", "arbitrary")),
        name="combine",
    )(x1, ya, yb, route, g2, p["ln2_g"], p["ln2_b"])


def _dispatch_plan(eidx):
    T = eidx.shape[0]
    A = 2 * T
    flat_e = eidx.reshape(A)
    onehot = (flat_e[:, None] == jnp.arange(N_EXPERTS, dtype=jnp.int32)[None, :]).astype(jnp.int32)
    cum = jnp.cumsum(onehot, axis=0)
    rank = jnp.take_along_axis(cum, flat_e[:, None], axis=1)[:, 0] - 1
    counts = cum[-1]
    padded = (counts + MOE_BM - 1) // MOE_BM * MOE_BM
    pad_end = jnp.cumsum(padded)
    pad_start = pad_end - padded
    dest = pad_start[flat_e] + rank
    nblk = (A + N_EXPERTS * (MOE_BM - 1) + MOE_BM - 1) // MOE_BM
    block_pos = jnp.arange(nblk, dtype=jnp.int32) * MOE_BM
    block_expert = jnp.minimum(jnp.sum(pad_end[None, :] <= block_pos[:, None], axis=1), N_EXPERTS - 1).astype(jnp.int32)
    n_used = (pad_end[-1] // MOE_BM).astype(jnp.int32).reshape(1)
    return dest, block_expert, n_used, nblk * MOE_BM


def _mods(mod_l, n_prompt, dec_seq):
    mp = mod_l[:n_prompt].reshape(n_prompt, 6, D_MODEL)
    ms = mod_l[n_prompt:].reshape(-1, 6, D_MODEL)
    prompt = [mp[:, j][:, None, :] for j in range(6)]
    sample = [jnp.repeat(ms[:, j], dec_seq, axis=0)[None] for j in range(6)]
    return prompt, sample


def kernel(x_prompt, x_sample, c_prompt, c_sample, state_ssd, state_ssd_conv, state_rglru, state_rglru_conv, state_ret, w_ada, b_ada, w_in, ssd_conv_w, ssd_conv_b, ssd_dt_bias, ssd_a_log, ssd_d, ssd_norm_w, rg_conv_w, rg_conv_b, rg_wa, rg_ba, rg_wx, rg_bx, rg_lambda, ret_gn_w, w_out, ln1_g, ln1_b, router_w, router_bias, exp_w_gate, exp_w_up, exp_w_down, ln2_g, ln2_b):
    BP, LP, _ = x_prompt.shape
    BS, LS, _ = x_sample.shape
    TP, TS = BP * LP, BS * LS

    def pad_lanes(v):
        return jnp.pad(v, ((0, 0), (0, LANES - v.shape[-1])))

    eye = jnp.eye(RG_BLOCKS, dtype=F32)

    def block_diag(w):
        return jnp.stack(_hi_lo(jnp.einsum("njk,nm->njmk", w, eye).reshape(D_RG, D_RG)))

    params = []
    for l in range(DEPTH):
        wi = w_in[l]
        wi = jnp.concatenate([wi[:, :2560], wi[:, 2576:5648], wi[:, 2560:2576],
                              jnp.zeros((D_MODEL, LANES - SSD_HEADS), F32)], axis=1)
        params.append(dict(
            w_in=_hi_lo(wi),
            ssd_conv_w=ssd_conv_w[l], ssd_conv_b=ssd_conv_b[l][None],
            ssd_dt_bias=pad_lanes(ssd_dt_bias[l][None]), ssd_a_log=pad_lanes(ssd_a_log[l][None]),
            ssd_d_ch=jnp.repeat(ssd_d[l], SSD_HEAD_DIM)[None], ssd_norm_w=ssd_norm_w[l][None],
            rg_conv_w=rg_conv_w[l], rg_conv_b=rg_conv_b[l][None],
            rg_wa_bd=block_diag(rg_wa[l]), rg_ba=rg_ba[l][None],
            rg_wx_bd=block_diag(rg_wx[l]), rg_bx=rg_bx[l][None],
            rg_lambda=rg_lambda[l][None], ret_gn_w=ret_gn_w[l][None],
            w_out=jnp.stack(_hi_lo(w_out[l])), ln1_g=ln1_g[l][None], ln1_b=ln1_b[l][None],
            experts=[h for w in (exp_w_gate[l], exp_w_up[l], exp_w_down[l]) for h in _hi_lo(w)],
            ln2_g=ln2_g[l][None], ln2_b=ln2_b[l][None],
        ))

    rw_hi, rw_lo = _hi_lo(pad_lanes(router_w))
    rbias = pad_lanes(router_bias[None])
    head_of_ch = jnp.arange(D_SSD) // SSD_HEAD_DIM
    consts = dict(
        rw_hi=rw_hi, rw_lo=rw_lo, rbias=rbias.astype(F32),
        head_expand=(jnp.arange(LANES)[:, None] == head_of_ch[None, :]).astype(BF16),
        tri=jnp.tril(jnp.ones((CHUNK, CHUNK), F32)).astype(BF16),
    )
    mod = _ada(jnp.concatenate([c_prompt, c_sample], axis=0), w_ada, b_ada)

    LM = LP - TAIL
    paths = [
        dict(x=x_prompt[:, :LM], B=BP, L=LM, per_token=False, has_state=False, precise=False, rc=_ret_consts(LM, 0.0)),
        dict(x=x_prompt[:, LM:], B=BP, L=TAIL, per_token=False, has_state=True, precise=True, rc=_ret_consts(TAIL, float(LM))),
        dict(x=x_sample.reshape(1, TS, D_MODEL), B=BS, L=LS, per_token=True, has_state=True, precise=False,
             rc=_ret_consts(LS, PAST_LEN)),
    ]
    zero_states = (jnp.zeros((BP, SSD_GROUPS, SSD_STATE, D_SSD // SSD_GROUPS), F32),
                   jnp.zeros((BP, CONV_WIDTH - 1, SSD_CONV_DIM), F32),
                   jnp.zeros((BP, 1, D_RG), F32),
                   jnp.zeros((BP, CONV_WIDTH - 1, D_RG), F32),
                   jnp.zeros((BP, RET_HEADS, RET_HEAD_DIM, RET_HEAD_DIM), F32))

    def mixer_stack(path, p, mods, states, ssd_kw, ret_kw):
        B, L, per_token, precise = path["B"], path["L"], path["per_token"], path["precise"]
        w_in_l = p["w_in"] if precise else p["w_in"][:1]
        z, xbc, rg, ret, dt = _inproj(path["x"], mods[1], mods[0], w_in_l, per_token, precise)
        if per_token:
            z, xbc, rg, ret, dt = [a.reshape(B, L, a.shape[-1]) for a in (z, xbc, rg, ret, dt)]
        s_ssd, s_cbuf, s_rg, s_rbuf, s_ret = states
        hs = path["has_state"]
        y_ssd, cbuf_n, ssd_n = _ssd(z, xbc, dt, s_ssd, s_cbuf, p, consts, hs, precise, **ssd_kw)
        y_rg, rbuf_n, rg_n = _rg(rg, s_rg, s_rbuf, p, hs, precise)
        y_ret, ret_n = _ret(ret, s_ret, p, path["rc"], hs, precise, **ret_kw)
        if per_token:
            y_ssd, y_rg, y_ret = [a.reshape(1, B * L, a.shape[-1]) for a in (y_ssd, y_rg, y_ret)]
        return (y_ssd, y_rg, y_ret), (ssd_n, cbuf_n, rg_n, rbuf_n, ret_n)

    def experts(h2_list, route_list, weights, precise):
        h2_all = jnp.concatenate([h.reshape(-1, D_MODEL) for h in h2_list], axis=0)
        route_all = jnp.concatenate([r.reshape(-1, LANES) for r in route_list], axis=0)
        T_all = h2_all.shape[0]
        dest, block_expert, n_used, P = _dispatch_plan(route_all[:, 0:2].astype(jnp.int32))
        tok = jnp.arange(2 * T_all, dtype=jnp.int32) // 2
        slot_tok = jnp.full((P,), T_all, jnp.int32).at[dest].set(tok)
        xs_sorted = jnp.concatenate([h2_all, jnp.zeros((1, D_MODEL), h2_all.dtype)], axis=0)[slot_tok]
        ys_sorted = _moe(xs_sorted, block_expert, n_used, weights, precise)
        return ys_sorted[dest[0::2]], ys_sorted[dest[1::2]]

    new_tail = []
    ssd_s = ret_s = None
    cbuf_s, rg_s, rbuf_s = [], [], []
    for l in range(DEPTH):
        p = params[l]
        mods_p, mods_s = _mods(mod[l], BP, LS)
        mods = [mods_p, mods_p, mods_s]
        st_s = (state_ssd, state_ssd_conv[l], state_rglru[l][:, None, :], state_rglru_conv[l], state_ret)

        ys_m, st_m = mixer_stack(paths[0], p, mods[0], zero_states, {}, {})
        ys_t, st_t = mixer_stack(paths[1], p, mods[1], st_m, dict(native_out=True), {})
        ys_c, st_c = mixer_stack(paths[2], p, mods[2], st_s,
                                 dict(h0_layer=l, native_out=True, out_stack=(l, ssd_s)),
                                 dict(h0_layer=l, out_stack=(l, ret_s)))
        new_tail.append(st_t)
        ssd_s, ret_s = st_c[0], st_c[4]
        cbuf_s.append(st_c[1])
        rg_s.append(st_c[2])
        rbuf_s.append(st_c[3])

        post = []
        for path, ys, m in zip(paths, (ys_m, ys_t, ys_c), mods):
            post.append(_outproj(*ys, path["x"], m[2], m[4], m[3], p, consts, path["per_token"], path["precise"]))

        hi_w = p["experts"][0::2]
        ya_b, yb_b = experts([post[0][1], post[2][1]], [post[0][2], post[2][2]], hi_w, False)
        ya_t, yb_t = experts([post[1][1]], [post[1][2]], hi_w + p["experts"][1::2], True)

        offs = (0, 0, BP * LM)
        for path, (x1, _, route), ya, yb, off, m in zip(paths, post, (ya_b, ya_t, ya_b), (yb_b, yb_t, yb_b), offs, mods):
            path["x"] = _combine(x1, ya, yb, off, route, m[5], p, path["per_token"])

    def stack(lst, k):
        return jnp.stack([s[k] for s in lst])

    y_prompt = jnp.concatenate([paths[0]["x"], paths[1]["x"]], axis=1)
    return (y_prompt, paths[2]["x"].reshape(BS, LS, D_MODEL),
            stack(new_tail, 0), stack(new_tail, 1), stack(new_tail, 2)[:, :, 0], stack(new_tail, 3), stack(new_tail, 4),
            ssd_s, jnp.stack(cbuf_s), jnp.stack(rg_s)[:, :, 0], jnp.stack(rbuf_s), ret_s)
```

```python
import functools
import math

import jax
import jax.numpy as jnp
from jax import lax
from jax.experimental import pallas as pl
from jax.experimental.pallas import tpu as pltpu

F32 = jnp.float32
BF16 = jnp.bfloat16

D_MODEL = 1024
DEPTH = 4
D_SSD = 1024
SSD_HEAD_DIM = 64
SSD_HEADS = 16
SSD_GROUPS = 2
SSD_STATE = 128
SSD_CONV_DIM = D_SSD + 2 * SSD_GROUPS * SSD_STATE
CONV_WIDTH = 4
D_RG = 512
RG_BLOCKS = 8
RG_C = 8.0
D_RET = 512
RET_HEADS = 4
RET_HEAD_DIM = 128
ROPE_BASE = 10000.0
N_EXPERTS = 32
EXPERTS_PER_GROUP = 8
N_EXPERT_GROUPS = 4
D_EXPERT = 512
ALPHA = (2 * DEPTH) ** 0.25
EPS = 1e-5
PAST_LEN = 16384.0

LANES = 128
SUBLANES = 8
CHUNK = 128
TAIL = CHUNK
PROJ_TM = 384
MOE_BM = 256
VMEM_LIMIT = 56 * 1024 * 1024


def _cparams(sem):
    return pltpu.CompilerParams(dimension_semantics=sem, vmem_limit_bytes=VMEM_LIMIT)


def _padded_rows(L):
    return CHUNK if L >= CHUNK else -(-L // SUBLANES) * SUBLANES


def _pick_tm(L, cap):
    if L <= LANES:
        return L
    tm = cap - cap % LANES
    while L % tm:
        tm -= LANES
    return tm


def _sigmoid(x):
    return 1.0 / (1.0 + jnp.exp(-x))


def _silu(x):
    return x * _sigmoid(x)


def _softplus(x):
    return jnp.maximum(x, 0.0) + jnp.log1p(jnp.exp(-jnp.abs(x)))


def _split3(v):
    hi = v.astype(BF16)
    r = v - hi.astype(F32)
    mid = r.astype(BF16)
    lo = (r - mid.astype(F32)).astype(BF16)
    return hi, mid, lo


def _dot(a, b):
    return jnp.dot(a, b, preferred_element_type=F32)


def _split2(v):
    hi = v.astype(BF16)
    return hi, (v - hi.astype(F32)).astype(BF16)


def _mm_w(a, w_hi, w_lo):
    if w_lo is None:
        return _dot(a.astype(BF16), w_hi)
    ah, al = _split2(a)
    return _dot(ah, w_hi) + (_dot(al, w_hi) + _dot(ah, w_lo))


def _mm_a(a, b, precise, nt=False):
    dn = (((1,), (1,)), ((), ())) if nt else (((1,), (0,)), ((), ()))

    def d(x, y):
        return lax.dot_general(x, y, dn, preferred_element_type=F32)

    if not precise:
        return d(a.astype(BF16), b.astype(BF16))
    ah, al = _split2(a)
    bh, bl = _split2(b)
    return d(ah, bh) + (d(al, bh) + d(ah, bl))


def _hi_lo(w):
    hi32 = lax.reduce_precision(w, exponent_bits=8, mantissa_bits=7)
    return hi32.astype(BF16), (w - hi32).astype(BF16)


def _dot_exact_rhs(v, m):
    hi, mid, lo = _split3(v)
    return _dot(hi, m) + _dot(mid, m) + _dot(lo, m)


def _dot_exact_lhs(m, v):
    hi, mid, lo = _split3(v)
    return _dot(m, hi) + _dot(m, mid) + _dot(m, lo)


def _ada_kernel(c_ref, w_ref, b_ref, o_ref):
    o_ref[0] = _mm_a(_silu(c_ref[...]), w_ref[0], True) + b_ref[0]


def _ada(c_all, w_ada, b_ada):
    n = c_all.shape[0]
    tn = 1024
    return pl.pallas_call(
        _ada_kernel,
        grid=(DEPTH, 6 * D_MODEL // tn),
        in_specs=[
            pl.BlockSpec((n, D_MODEL), lambda l, j: (0, 0)),
            pl.BlockSpec((1, D_MODEL, tn), lambda l, j: (l, 0, j)),
            pl.BlockSpec((1, 1, tn), lambda l, j: (l, 0, j)),
        ],
        out_specs=pl.BlockSpec((1, n, tn), lambda l, j: (l, 0, j)),
        out_shape=jax.ShapeDtypeStruct((DEPTH, n, 6 * D_MODEL), F32),
        compiler_params=_cparams(("arbitrary", "arbitrary")),
        name="ada",
    )(c_all, w_ada, b_ada.reshape(DEPTH, 1, 6 * D_MODEL))


_IN_SEGS = (("z", 0, 1024), ("xbc", 1024, 2560), ("rg", 2560, 3584), ("ret", 3584, 5632), ("dt", 5632, 5760))
IN_PAD = 5760


def _inproj_kernel(x_ref, sc_ref, sh_ref, w_ref, *rest, precise):
    wl_ref = rest[0] if precise else None
    outs = rest[1:] if precise else rest
    h = x_ref[0] * (1.0 + sc_ref[0]) + sh_ref[0]
    if precise:
        hh, hl = _split2(h)
    else:
        hh = h.astype(BF16)
    for o_ref, (_, lo, hi) in zip(outs, _IN_SEGS):
        acc = _dot(hh, w_ref[:, lo:hi])
        if precise:
            acc = acc + (_dot(hl, w_ref[:, lo:hi]) + _dot(hh, wl_ref[:, lo:hi]))
        o_ref[0] = acc


def _inproj(x, sc, sh, w, per_token_mod, precise):
    B, L, _ = x.shape
    tm = _pick_tm(L, LANES if precise else PROJ_TM)
    if per_token_mod:
        mod_spec = pl.BlockSpec((1, tm, D_MODEL), lambda b, i: (b, i, 0))
    else:
        mod_spec = pl.BlockSpec((1, 1, D_MODEL), lambda b, i: (b, 0, 0))
    widths = [hi - lo for _, lo, hi in _IN_SEGS]
    w_spec = pl.BlockSpec((D_MODEL, IN_PAD), lambda b, i: (0, 0), pipeline_mode=pl.Buffered(1))
    ws = w if precise else w[:1]
    return pl.pallas_call(
        functools.partial(_inproj_kernel, precise=precise),
        grid=(B, L // tm),
        in_specs=[
            pl.BlockSpec((1, tm, D_MODEL), lambda b, i: (b, i, 0)),
            mod_spec,
            mod_spec,
        ] + [w_spec] * len(ws),
        out_specs=[pl.BlockSpec((1, tm, wd), lambda b, i: (b, i, 0)) for wd in widths],
        out_shape=[jax.ShapeDtypeStruct((B, L, wd), F32) for wd in widths],
        compiler_params=_cparams(("arbitrary", "arbitrary")),
        name="inproj",
    )(x, sc, sh, *ws)


def _conv_step(src_rows, buf_ref, cw_ref, cb_ref, nbuf_ref, xp_s, *, c, nc, Lv, Lp, has_state):
    @pl.when(c == 0)
    def _():
        xp_s[...] = jnp.zeros_like(xp_s)
        if has_state:
            xp_s[SUBLANES - (CONV_WIDTH - 1):SUBLANES, :] = buf_ref[0]

    xp_s[SUBLANES:SUBLANES + Lv, :] = src_rows
    base = SUBLANES - (CONV_WIDTH - 1)
    xc = cb_ref[...]
    for j in range(CONV_WIDTH):
        xc = xc + cw_ref[j:j + 1, :] * xp_s[base + j:base + j + Lp, :]

    @pl.when(c == nc - 1)
    def _():
        nbuf_ref[0] = xp_s[SUBLANES + Lv - (CONV_WIDTH - 1):SUBLANES + Lv, :]

    if nc > 1:
        xp_s[0:SUBLANES, :] = xp_s[Lp:Lp + SUBLANES, :]
    return xc


def _ssd_kernel(z_ref, xbc_ref, dt_ref, h0_ref, buf_ref, cw_ref, cb_ref, dtb_ref, alog_ref, dch_ref,
                nw_ref, e_ref, tri_ref, *rest, Lv, Lp, nc, has_state, precise, native_in, native_out):
    y_ref, nbuf_ref, hn_ref, xp_s, dtp_s, s_s = rest[-6:]
    c = pl.program_id(1)
    hpg = SSD_HEADS // SSD_GROUPS

    @pl.when(c == 0)
    def _():
        if not has_state:
            s_s[...] = jnp.zeros_like(s_s)
        elif native_in:
            for g in range(SSD_GROUPS):
                s_s[g] = jnp.concatenate([h0_ref[g * hpg + i] for i in range(hpg)], axis=0).T
        else:
            s_s[...] = h0_ref[...]
        if Lv < Lp:
            dtp_s[...] = jnp.zeros_like(dtp_s)

    xc = _conv_step(xbc_ref[0], buf_ref, cw_ref, cb_ref, nbuf_ref, xp_s,
                    c=c, nc=nc, Lv=Lv, Lp=Lp, has_state=has_state)
    xbc = _silu(xc)
    xs = xbc[:, 0:D_SSD]
    Bm = xbc[:, D_SSD:D_SSD + 256]
    Cm = xbc[:, D_SSD + 256:D_SSD + 512]

    if Lv < Lp:
        dtp_s[0:Lv, :] = dt_ref[0]
        dtr = dtp_s[...]
    else:
        dtr = dt_ref[0]
    dt = _softplus(dtr + dtb_ref[...])
    if Lv < Lp:
        rowv = lax.broadcasted_iota(jnp.int32, (Lp, LANES), 0)
        dt = jnp.where(rowv < Lv, dt, 0.0)
    a = dt * (-jnp.exp(alog_ref[...]))
    acum = _dot_exact_lhs(tri_ref[...], a)
    dt_e = _dot_exact_rhs(dt, e_ref[...])
    ac_e = _dot_exact_rhs(acum, e_ref[...])
    acl_e = ac_e[Lp - 1:Lp, :]
    acum_t = acum.T

    vdt = xs * dt_e
    lane = lax.broadcasted_iota(jnp.int32, (Lp, LANES), 1)
    causal = lax.broadcasted_iota(jnp.int32, (Lp, Lp), 0) >= lax.broadcasted_iota(jnp.int32, (Lp, Lp), 1)
    mid_t = F32 if precise else BF16

    y_blocks = []
    for g in range(SSD_GROUPS):
        Cg = Cm[:, g * SSD_STATE:(g + 1) * SSD_STATE]
        Bg = Bm[:, g * SSD_STATE:(g + 1) * SSD_STATE]
        G = _mm_a(Cg, Bg, precise, nt=True)
        for jj in range(4):
            j = 4 * g + jj
            scs = []
            for h in (2 * j, 2 * j + 1):
                col = jnp.sum(jnp.where(lane == h, acum, 0.0), axis=1, keepdims=True)
                rw = acum_t[h:h + 1, :]
                dec = jnp.exp(jnp.where(causal, col - rw, -jnp.inf))
                scs.append((G * dec).astype(mid_t))
            sc = jnp.concatenate(scs, axis=1)
            vb = vdt[:, j * LANES:(j + 1) * LANES]
            v2 = jnp.concatenate([jnp.where(lane < SSD_HEAD_DIM, vb, 0.0),
                                  jnp.where(lane >= SSD_HEAD_DIM, vb, 0.0)], axis=0).astype(mid_t)
            y_blocks.append(_mm_a(sc, v2, precise))
    y = jnp.concatenate(y_blocks, axis=1)

    eac = jnp.exp(ac_e)
    vw = vdt * jnp.exp(acl_e - ac_e)
    cd = jnp.exp(acl_e)
    half = D_SSD // SSD_GROUPS
    ys_parts = []
    for g in range(SSD_GROUPS):
        Sg = s_s[g]
        ys_parts.append(_mm_a(Cm[:, g * SSD_STATE:(g + 1) * SSD_STATE], Sg, precise))
        BgT = Bm[:, g * SSD_STATE:(g + 1) * SSD_STATE].T
        s_s[g] = cd[:, g * half:(g + 1) * half] * Sg + _mm_a(BgT, vw[:, g * half:(g + 1) * half], precise)
    y = y + eac * jnp.concatenate(ys_parts, axis=1) + dch_ref[...] * xs

    yv = y[0:Lv, :] if Lv < Lp else y
    yz = yv * _silu(z_ref[0])
    outs = []
    for g in range(SSD_GROUPS):
        part = yz[:, g * half:(g + 1) * half]
        ms = jnp.mean(part * part, axis=1, keepdims=True)
        outs.append(part * lax.rsqrt(ms + EPS))
    y_ref[0] = jnp.concatenate(outs, axis=1) * nw_ref[...]

    @pl.when(c == nc - 1)
    def _():
        if native_out:
            for g in range(SSD_GROUPS):
                sgt = s_s[g].T
                for i in range(hpg):
                    hn_ref[g * hpg + i] = sgt[i * SSD_HEAD_DIM:(i + 1) * SSD_HEAD_DIM, :]
        else:
            hn_ref[...] = s_s[...]


def _state_specs(B, shape, in_layer, out_stack):
    zeros = (0,) * len(shape)
    if in_layer is None:
        in_spec = pl.BlockSpec((None,) + shape, lambda b, c: (b,) + zeros)
    else:
        in_spec = pl.BlockSpec((None, None) + shape, lambda b, c: (in_layer, b) + zeros)
    if out_stack is None:
        out_spec = pl.BlockSpec((None,) + shape, lambda b, c: (b,) + zeros)
        out_shape = jax.ShapeDtypeStruct((B,) + shape, F32)
    else:
        l = out_stack[0]
        out_spec = pl.BlockSpec((None, None) + shape, lambda b, c: (l, b) + zeros)
        out_shape = jax.ShapeDtypeStruct((DEPTH, B) + shape, F32)
    return in_spec, out_spec, out_shape


def _ssd(z, xbc, dt, h0, buf, p, consts, has_state, precise, h0_layer=None, native_out=False, out_stack=None):
    B, L, _ = z.shape
    Lv = min(CHUNK, L)
    Lp = _padded_rows(L)
    nc = L // Lv
    native_in = h0_layer is not None
    kern = functools.partial(_ssd_kernel, Lv=Lv, Lp=Lp, nc=nc, has_state=has_state, precise=precise,
                             native_in=native_in, native_out=native_out)
    row = lambda b, c: (b, c, 0)
    per_b3 = lambda b, c: (b, 0, 0)
    fixed = lambda b, c: (0, 0)
    half = D_SSD // SSD_GROUPS
    native = (SSD_HEADS, SSD_HEAD_DIM, SSD_STATE)
    packed = (SSD_GROUPS, SSD_STATE, half)
    h0_spec, _, _ = _state_specs(B, native if native_in else packed, h0_layer, None)
    _, hn_spec, hn_shape = _state_specs(B, native if native_out else packed, None, out_stack)
    prev = out_stack[1] if out_stack is not None else None
    extra_in, extra_specs, aliases = [], [], {}
    if prev is not None:
        extra_in, extra_specs, aliases = [prev], [pl.BlockSpec(memory_space=pl.ANY)], {13: 2}
    return pl.pallas_call(
        kern,
        grid=(B, nc),
        in_specs=[
            pl.BlockSpec((1, Lv, D_SSD), row),
            pl.BlockSpec((1, Lv, SSD_CONV_DIM), row),
            pl.BlockSpec((1, Lv, LANES), row),
            h0_spec,
            pl.BlockSpec((1, CONV_WIDTH - 1, SSD_CONV_DIM), per_b3),
            pl.BlockSpec((CONV_WIDTH, SSD_CONV_DIM), fixed),
            pl.BlockSpec((1, SSD_CONV_DIM), fixed),
            pl.BlockSpec((1, LANES), fixed),
            pl.BlockSpec((1, LANES), fixed),
            pl.BlockSpec((1, D_SSD), fixed),
            pl.BlockSpec((1, D_SSD), fixed),
            pl.BlockSpec((LANES, D_SSD), fixed),
            pl.BlockSpec((Lp, Lp), fixed),
        ] + extra_specs,
        out_specs=[
            pl.BlockSpec((1, Lv, D_SSD), row),
            pl.BlockSpec((1, CONV_WIDTH - 1, SSD_CONV_DIM), per_b3),
            hn_spec,
        ],
        out_shape=[
            jax.ShapeDtypeStruct((B, L, D_SSD), F32),
            jax.ShapeDtypeStruct((B, CONV_WIDTH - 1, SSD_CONV_DIM), F32),
            hn_shape,
        ],
        scratch_shapes=[
            pltpu.VMEM((SUBLANES + Lp, SSD_CONV_DIM), F32),
            pltpu.VMEM((Lp, LANES), F32),
            pltpu.VMEM((SSD_GROUPS, SSD_STATE, half), F32),
        ],
        input_output_aliases=aliases,
        compiler_params=_cparams(("arbitrary", "arbitrary")),
        name="ssd",
    )(z, xbc, dt, h0, buf, p["ssd_conv_w"], p["ssd_conv_b"], p["ssd_dt_bias"], p["ssd_a_log"],
      p["ssd_d_ch"], p["ssd_norm_w"], consts["head_expand"],
      jnp.tril(jnp.ones((Lp, Lp), F32)).astype(BF16), *extra_in)


def _rg_kernel(rg_ref, h0_ref, buf_ref, cw_ref, cb_ref, wa_ref, ba_ref, wx_ref, bx_ref, lam_ref,
               y_ref, nbuf_ref, hn_ref, xp_s, hc_s, *, Lv, Lp, nc, has_state, precise):
    c = pl.program_id(1)

    @pl.when(c == 0)
    def _():
        if has_state:
            hc_s[...] = h0_ref[0]
        else:
            hc_s[...] = jnp.zeros_like(hc_s)

    xc = _conv_step(rg_ref[0, :, 0:D_RG], buf_ref, cw_ref, cb_ref, nbuf_ref, xp_s,
                    c=c, nc=nc, Lv=Lv, Lp=Lp, has_state=has_state)
    r = _sigmoid(_mm_w(xc, wa_ref[0], wa_ref[1] if precise else None) + ba_ref[...])
    i = _sigmoid(_mm_w(xc, wx_ref[0], wx_ref[1] if precise else None) + bx_ref[...])
    la = (-RG_C * _softplus(-lam_ref[...])) * r
    u = jnp.sqrt(-jnp.tanh(la) * (jnp.exp(2.0 * la) + 1.0)) * (i * xc)
    row = lax.broadcasted_iota(jnp.int32, (Lp, D_RG), 0)
    if Lv < Lp:
        valid = row < Lv
        la = jnp.where(valid, la, 0.0)
        u = jnp.where(valid, u, 0.0)
    a = jnp.exp(la)
    s = 1
    while s < Lp:
        m = row >= s
        u_sh = jnp.where(m, pltpu.roll(u, s, 0), 0.0)
        a_sh = jnp.where(m, pltpu.roll(a, s, 0), 1.0)
        u = u + a * u_sh
        a = a * a_sh
        s *= 2
    h = u + a * hc_s[...]
    hc_s[...] = h[Lp - 1:Lp, :]
    hv = h[0:Lv, :] if Lv < Lp else h
    y_ref[0] = hv * jax.nn.gelu(rg_ref[0, :, D_RG:2 * D_RG])

    @pl.when(c == nc - 1)
    def _():
        hn_ref[0] = hc_s[...]


def _rg(rg, h0, buf, p, has_state, precise):
    B, L, _ = rg.shape
    Lv = min(CHUNK, L)
    Lp = _padded_rows(L)
    nc = L // Lv
    kern = functools.partial(_rg_kernel, Lv=Lv, Lp=Lp, nc=nc, has_state=has_state, precise=precise)
    row = lambda b, c: (b, c, 0)
    per_b3 = lambda b, c: (b, 0, 0)
    fixed = lambda b, c: (0, 0)
    fixed3 = lambda b, c: (0, 0, 0)
    return pl.pallas_call(
        kern,
        grid=(B, nc),
        in_specs=[
            pl.BlockSpec((1, Lv, 2 * D_RG), row),
            pl.BlockSpec((1, 1, D_RG), per_b3),
            pl.BlockSpec((1, CONV_WIDTH - 1, D_RG), per_b3),
            pl.BlockSpec((CONV_WIDTH, D_RG), fixed),
            pl.BlockSpec((1, D_RG), fixed),
            pl.BlockSpec((2, D_RG, D_RG), fixed3),
            pl.BlockSpec((1, D_RG), fixed),
            pl.BlockSpec((2, D_RG, D_RG), fixed3),
            pl.BlockSpec((1, D_RG), fixed),
            pl.BlockSpec((1, D_RG), fixed),
        ],
        out_specs=[
            pl.BlockSpec((1, Lv, D_RG), row),
            pl.BlockSpec((1, CONV_WIDTH - 1, D_RG), per_b3),
            pl.BlockSpec((1, 1, D_RG), per_b3),
        ],
        out_shape=[
            jax.ShapeDtypeStruct((B, L, D_RG), F32),
            jax.ShapeDtypeStruct((B, CONV_WIDTH - 1, D_RG), F32),
            jax.ShapeDtypeStruct((B, 1, D_RG), F32),
        ],
        scratch_shapes=[
            pltpu.VMEM((SUBLANES + Lp, D_RG), F32),
            pltpu.VMEM((1, D_RG), F32),
        ],
        compiler_params=_cparams(("arbitrary", "arbitrary")),
        name="rglru",
    )(rg, h0, buf, p["rg_conv_w"], p["rg_conv_b"], p["rg_wa_bd"], p["rg_ba"], p["rg_wx_bd"],
      p["rg_bx"], p["rg_lambda"])


def _ret_kernel(ret_ref, h0_ref, cos_ref, sin_ref, dm_ref, ea_ref, te_ref, cd_ref, gn_ref,
                *rest, Lv, Lp, nc, has_state, precise):
    y_ref, hn_ref, pad_s, s_s = rest[-4:]
    c = pl.program_id(1)

    @pl.when(c == 0)
    def _():
        if has_state:
            s_s[...] = h0_ref[...]
        else:
            s_s[...] = jnp.zeros_like(s_s)
        if Lv < Lp:
            pad_s[...] = jnp.zeros_like(pad_s)

    if Lv < Lp:
        pad_s[0:Lv, :] = ret_ref[0, :, 0:3 * D_RET]
        qkv = pad_s[...]
    else:
        qkv = ret_ref[0, :, 0:3 * D_RET]
    cosf = cos_ref[...]
    sinf = sin_ref[...]
    outs = []
    for h in range(RET_HEADS):
        sl = slice(h * RET_HEAD_DIM, (h + 1) * RET_HEAD_DIM)
        q = qkv[:, sl]
        k = qkv[:, D_RET + h * RET_HEAD_DIM:D_RET + (h + 1) * RET_HEAD_DIM]
        v = qkv[:, 2 * D_RET + h * RET_HEAD_DIM:2 * D_RET + (h + 1) * RET_HEAD_DIM]
        q = q * cosf + pltpu.roll(q, RET_HEAD_DIM // 2, 1) * sinf
        k = (k * cosf + pltpu.roll(k, RET_HEAD_DIM // 2, 1) * sinf) * (RET_HEAD_DIM ** -0.5)
        G = _mm_a(q, k, precise, nt=True)
        S = s_s[h]
        y = _mm_a(G * dm_ref[h], v, precise) + _mm_a(q * ea_ref[h], S, precise)
        s_s[h] = cd_ref[h] * S + _mm_a((k * te_ref[h]).T, v, precise)
        yv = y[0:Lv, :] if Lv < Lp else y
        mu = jnp.mean(yv, axis=1, keepdims=True)
        d = yv - mu
        var = jnp.mean(d * d, axis=1, keepdims=True)
        outs.append(d * lax.rsqrt(var + EPS))
    yn = jnp.concatenate(outs, axis=1) * gn_ref[...]
    y_ref[0] = _silu(ret_ref[0, :, 3 * D_RET:4 * D_RET]) * yn

    @pl.when(c == nc - 1)
    def _():
        hn_ref[...] = s_s[...]


def _ret(ret, h0, p, rc, has_state, precise, h0_layer=None, out_stack=None):
    B, L, _ = ret.shape
    Lv = min(CHUNK, L)
    Lp = _padded_rows(L)
    nc = L // Lv
    kern = functools.partial(_ret_kernel, Lv=Lv, Lp=Lp, nc=nc, has_state=has_state, precise=precise)
    row = lambda b, c: (b, c, 0)
    fixed2 = lambda b, c: (0, 0)
    fixed3 = lambda b, c: (0, 0, 0)
    hd = RET_HEAD_DIM
    h0_spec, _, _ = _state_specs(B, (RET_HEADS, hd, hd), h0_layer, None)
    _, hn_spec, hn_shape = _state_specs(B, (RET_HEADS, hd, hd), None, out_stack)
    prev = out_stack[1] if out_stack is not None else None
    extra_in, extra_specs, aliases = [], [], {}
    if prev is not None:
        extra_in, extra_specs, aliases = [prev], [pl.BlockSpec(memory_space=pl.ANY)], {9: 1}
    return pl.pallas_call(
        kern,
        grid=(B, nc),
        in_specs=[
            pl.BlockSpec((1, Lv, 4 * D_RET), row),
            h0_spec,
            pl.BlockSpec((Lp, hd), lambda b, c: (c, 0)),
            pl.BlockSpec((Lp, hd), lambda b, c: (c, 0)),
            pl.BlockSpec((RET_HEADS, Lp, Lp), fixed3),
            pl.BlockSpec((RET_HEADS, Lp, hd), fixed3),
            pl.BlockSpec((RET_HEADS, Lp, hd), fixed3),
            pl.BlockSpec((RET_HEADS, hd, hd), fixed3),
            pl.BlockSpec((1, D_RET), fixed2),
        ] + extra_specs,
        out_specs=[
            pl.BlockSpec((1, Lv, D_RET), row),
            hn_spec,
        ],
        out_shape=[
            jax.ShapeDtypeStruct((B, L, D_RET), F32),
            hn_shape,
        ],
        scratch_shapes=[
            pltpu.VMEM((Lp, 3 * D_RET), F32),
            pltpu.VMEM((RET_HEADS, hd, hd), F32),
        ],
        input_output_aliases=aliases,
        compiler_params=_cparams(("arbitrary", "arbitrary")),
        name="retention",
    )(ret, h0, rc["cos"], rc["sin"], rc["dmat"], rc["eacum"], rc["toend"], rc["cdec"], p["ret_gn_w"], *extra_in)


def _ret_consts(L, pos0):
    Lv = min(CHUNK, L)
    Lp = _padded_rows(L)
    nc = L // Lv
    half = RET_HEAD_DIM // 2
    inv = ROPE_BASE ** (-jnp.arange(half, dtype=F32) / half)
    pos = pos0 + jnp.arange(L, dtype=F32)
    ang = pos[:, None] * inv[None, :]
    cos = jnp.cos(ang)
    sin = jnp.sin(ang)
    cosf = jnp.concatenate([cos, cos], axis=1)
    sinf = jnp.concatenate([-sin, sin], axis=1)
    if Lv < Lp:
        cosf = jnp.pad(cosf, ((0, Lp - Lv), (0, 0)))
        sinf = jnp.pad(sinf, ((0, Lp - Lv), (0, 0)))
    assert cosf.shape[0] == nc * Lp
    log_gamma = jnp.log1p(-jnp.exp2(-5.0 - jnp.arange(RET_HEADS, dtype=F32)))
    steps = jnp.minimum(jnp.arange(Lp) + 1, Lv).astype(F32)
    acum = log_gamma[:, None] * steps[None, :]
    causal = jnp.tril(jnp.ones((Lp, Lp), bool))
    dmat = jnp.exp(jnp.where(causal[None], acum[:, :, None] - acum[:, None, :], -jnp.inf))
    ones = jnp.ones((RET_HEADS, Lp, RET_HEAD_DIM), F32)
    eacum = jnp.exp(acum)[:, :, None] * ones
    toend = jnp.exp(acum[:, -1:] - acum)[:, :, None] * ones
    rowvalid = (jnp.arange(Lp) < Lv).astype(F32)[None, :, None]
    toend = toend * rowvalid
    cdec = jnp.exp(acum[:, -1])[:, None, None] * jnp.ones((RET_HEADS, RET_HEAD_DIM, RET_HEAD_DIM), F32)
    return dict(cos=cosf, sin=sinf, dmat=dmat, eacum=eacum, toend=toend, cdec=cdec)


def _layernorm(v, g, b):
    mu = jnp.mean(v, axis=1, keepdims=True)
    d = v - mu
    var = jnp.mean(d * d, axis=1, keepdims=True)
    return d * lax.rsqrt(var + EPS) * g + b


def _route(logits, bias):
    tm = logits.shape[0]
    lane = lax.broadcasted_iota(jnp.int32, (tm, LANES), 1)
    lanef = lane.astype(F32)
    scores = _sigmoid(logits)
    choice = scores + bias
    neg = -jnp.inf
    best = jnp.full((tm, 1), neg, F32)
    e1 = jnp.zeros((tm, 1), F32)
    e2 = jnp.zeros((tm, 1), F32)
    for g in range(N_EXPERT_GROUPS):
        ing = (lane >= g * EXPERTS_PER_GROUP) & (lane < (g + 1) * EXPERTS_PER_GROUP)
        cg = jnp.where(ing, choice, neg)
        m1 = jnp.max(cg, axis=1, keepdims=True)
        i1 = jnp.min(jnp.where(cg == m1, lanef, float(LANES)), axis=1, keepdims=True)
        cg2 = jnp.where(lanef == i1, neg, cg)
        m2 = jnp.max(cg2, axis=1, keepdims=True)
        i2 = jnp.min(jnp.where(cg2 == m2, lanef, float(LANES)), axis=1, keepdims=True)
        gs = m1 + m2
        better = gs > best
        best = jnp.where(better, gs, best)
        e1 = jnp.where(better, i1, e1)
        e2 = jnp.where(better, i2, e2)
    w1 = jnp.sum(jnp.where(lanef == e1, scores, 0.0), axis=1, keepdims=True)
    w2 = jnp.sum(jnp.where(lanef == e2, scores, 0.0), axis=1, keepdims=True)
    den = w1 + w2
    out = jnp.where(lane == 0, e1, jnp.where(lane == 1, e2, jnp.where(lane == 2, w1 / den, jnp.where(lane == 3, w2 / den, 0.0))))
    return out


def _outproj_kernel(ys_ref, yr_ref, yt_ref, x_ref, g1_ref, sc2_ref, sh2_ref, w_ref, lng_ref, lnb_ref,
                    rwh_ref, rwl_ref, rb_ref, x1_ref, h2_ref, route_ref, *, precise):
    def part(y_ref, lo, hi):
        return _mm_w(y_ref[0], w_ref[0, lo:hi, :], w_ref[1, lo:hi, :] if precise else None)

    mix = part(ys_ref, 0, D_SSD) + part(yr_ref, D_SSD, D_SSD + D_RG) + part(yt_ref, D_SSD + D_RG, 2 * D_MODEL)
    x1 = _layernorm(ALPHA * x_ref[0] + (1.0 + g1_ref[0]) * mix, lng_ref[...], lnb_ref[...])
    x1_ref[0] = x1
    h2 = x1 * (1.0 + sc2_ref[0]) + sh2_ref[0]
    hi, lo = _split2(h2)
    h2_ref[0] = h2
    logits = _dot(hi, rwh_ref[...]) + (_dot(lo, rwh_ref[...]) + _dot(hi, rwl_ref[...]))
    route_ref[0] = _route(logits, rb_ref[...])


def _outproj(y_ssd, y_rg, y_ret, x, g1, sc2, sh2, p, consts, per_token_mod, precise):
    B, L, _ = x.shape
    tm = _pick_tm(L, LANES if precise else PROJ_TM)
    row = lambda b, i: (b, i, 0)
    fixed = lambda b, i: (0, 0)
    if per_token_mod:
        mod_spec = pl.BlockSpec((1, tm, D_MODEL), row)
    else:
        mod_spec = pl.BlockSpec((1, 1, D_MODEL), lambda b, i: (b, 0, 0))
    return pl.pallas_call(
        functools.partial(_outproj_kernel, precise=precise),
        grid=(B, L // tm),
        in_specs=[
            pl.BlockSpec((1, tm, D_SSD), row),
            pl.BlockSpec((1, tm, D_RG), row),
            pl.BlockSpec((1, tm, D_RET), row),
            pl.BlockSpec((1, tm, D_MODEL), row),
            mod_spec, mod_spec, mod_spec,
            pl.BlockSpec((2, 2 * D_MODEL, D_MODEL), lambda b, i: (0, 0, 0)),
            pl.BlockSpec((1, D_MODEL), fixed),
            pl.BlockSpec((1, D_MODEL), fixed),
            pl.BlockSpec((D_MODEL, LANES), fixed),
            pl.BlockSpec((D_MODEL, LANES), fixed),
            pl.BlockSpec((1, LANES), fixed),
        ],
        out_specs=[
            pl.BlockSpec((1, tm, D_MODEL), row),
            pl.BlockSpec((1, tm, D_MODEL), row),
            pl.BlockSpec((1, tm, LANES), row),
        ],
        out_shape=[
            jax.ShapeDtypeStruct((B, L, D_MODEL), F32),
            jax.ShapeDtypeStruct((B, L, D_MODEL), F32),
            jax.ShapeDtypeStruct((B, L, LANES), F32),
        ],
        compiler_params=_cparams(("arbitrary", "arbitrary")),
        name="outproj",
    )(y_ssd, y_rg, y_ret, x, g1, sc2, sh2, p["w_out"], p["ln1_g"], p["ln1_b"],
      consts["rw_hi"], consts["rw_lo"], consts["rbias"])


def _moe_kernel(be_ref, nu_ref, x_ref, wg_ref, wu_ref, wd_ref, o_ref, *scr, precise):
    i = pl.program_id(0)
    used = i < nu_ref[0]
    new_expert = jnp.logical_or(i == 0, be_ref[i] != be_ref[jnp.maximum(i - 1, 0)])

    @pl.when(jnp.logical_and(used, new_expert))
    def _():
        for k, w_ref in enumerate((wg_ref, wu_ref, wd_ref)):
            w = w_ref[...]
            if precise:
                hi, lo = _split2(w)
                scr[k][...] = hi
                scr[3 + k][...] = lo
            else:
                scr[k][...] = w.astype(BF16)

    @pl.when(used)
    def _():
        x = x_ref[...]
        lo = (scr[3][...], scr[4][...], scr[5][...]) if precise else (None, None, None)
        g = _mm_w(x, scr[0][...], lo[0])
        u = _mm_w(x, scr[1][...], lo[1])
        o_ref[...] = _mm_w(_silu(g) * u, scr[2][...], lo[2])

    @pl.when(jnp.logical_not(used))
    def _():
        o_ref[...] = jnp.zeros_like(o_ref)


def _moe(xs, block_expert, n_used, weights, layer, precise):
    P = xs.shape[0]
    nblk = P // MOE_BM
    in_w = pl.BlockSpec((None, None, D_MODEL, D_EXPERT), lambda i, be, nu: (layer, be[i], 0, 0))
    out_w = pl.BlockSpec((None, None, D_EXPERT, D_MODEL), lambda i, be, nu: (layer, be[i], 0, 0))
    grid_spec = pltpu.PrefetchScalarGridSpec(
        num_scalar_prefetch=2,
        grid=(nblk,),
        in_specs=[pl.BlockSpec((MOE_BM, D_MODEL), lambda i, be, nu: (i, 0)), in_w, in_w, out_w],
        out_specs=pl.BlockSpec((MOE_BM, D_MODEL), lambda i, be, nu: (i, 0)),
        scratch_shapes=[pltpu.VMEM((D_MODEL, D_EXPERT), BF16), pltpu.VMEM((D_MODEL, D_EXPERT), BF16),
                        pltpu.VMEM((D_EXPERT, D_MODEL), BF16)] * (2 if precise else 1),
    )
    return pl.pallas_call(
        functools.partial(_moe_kernel, precise=precise),
        grid_spec=grid_spec,
        out_shape=jax.ShapeDtypeStruct((P, D_MODEL), F32),
        compiler_params=_cparams(("arbitrary",)),
        name="moe",
    )(block_expert, n_used, xs, *weights)


def _combine_kernel(x1_ref, ya_ref, yb_ref, route_ref, g2_ref, lng_ref, lnb_ref, o_ref):
    tm = x1_ref.shape[1]
    lane = lax.broadcasted_iota(jnp.int32, (tm, LANES), 1)
    rt = route_ref[0]
    w1 = jnp.sum(jnp.where(lane == 2, rt, 0.0), axis=1, keepdims=True)
    w2 = jnp.sum(jnp.where(lane == 3, rt, 0.0), axis=1, keepdims=True)
    moe = ya_ref[...] * w1 + yb_ref[...] * w2
    o_ref[0] = _layernorm(ALPHA * x1_ref[0] + (1.0 + g2_ref[0]) * moe, lng_ref[...], lnb_ref[...])


def _combine(x1, ya, yb, row_off, route, g2, p, per_token_mod):
    B, L, _ = x1.shape
    tm = _pick_tm(L, PROJ_TM)
    assert row_off % tm == 0
    row = lambda b, i: (b, i, 0)
    fixed = lambda b, i: (0, 0)
    flat = lambda b, i: (row_off // tm + b * (L // tm) + i, 0)
    if per_token_mod:
        mod_spec = pl.BlockSpec((1, tm, D_MODEL), row)
    else:
        mod_spec = pl.BlockSpec((1, 1, D_MODEL), lambda b, i: (b, 0, 0))
    return pl.pallas_call(
        _combine_kernel,
        grid=(B, L // tm),
        in_specs=[
            pl.BlockSpec((1, tm, D_MODEL), row),
            pl.BlockSpec((tm, D_MODEL), flat),
            pl.BlockSpec((tm, D_MODEL), flat),
            pl.BlockSpec((1, tm, LANES), row),
            mod_spec,
            pl.BlockSpec((1, D_MODEL), fixed),
            pl.BlockSpec((1, D_MODEL), fixed),
        ],
        out_specs=pl.BlockSpec((1, tm, D_MODEL), row),
        out_shape=jax.ShapeDtypeStruct((B, L, D_MODEL), F32),
        compiler_params=_cparams(("arbitrary", "arbitrary")),
        name="combine",
    )(x1, ya, yb, route, g2, p["ln2_g"], p["ln2_b"])


def _dispatch_plan(eidx):
    T = eidx.shape[0]
    A = 2 * T
    flat_e = eidx.reshape(A)
    onehot = (flat_e[:, None] == jnp.arange(N_EXPERTS, dtype=jnp.int32)[None, :]).astype(jnp.int32)
    cum = jnp.cumsum(onehot, axis=0)
    rank = jnp.take_along_axis(cum, flat_e[:, None], axis=1)[:, 0] - 1
    counts = cum[-1]
    padded = (counts + MOE_BM - 1) // MOE_BM * MOE_BM
    pad_end = jnp.cumsum(padded)
    pad_start = pad_end - padded
    dest = pad_start[flat_e] + rank
    nblk = (A + N_EXPERTS * (MOE_BM - 1) + MOE_BM - 1) // MOE_BM
    block_pos = jnp.arange(nblk, dtype=jnp.int32) * MOE_BM
    block_expert = jnp.minimum(jnp.sum(pad_end[None, :] <= block_pos[:, None], axis=1), N_EXPERTS - 1).astype(jnp.int32)
    n_used = (pad_end[-1] // MOE_BM).astype(jnp.int32).reshape(1)
    return dest, block_expert, n_used, nblk * MOE_BM


def _mods(mod_l, n_prompt, dec_seq):
    mp = mod_l[:n_prompt].reshape(n_prompt, 6, D_MODEL)
    ms = mod_l[n_prompt:].reshape(-1, 6, D_MODEL)
    prompt = [mp[:, j][:, None, :] for j in range(6)]
    sample = [jnp.repeat(ms[:, j], dec_seq, axis=0)[None] for j in range(6)]
    return prompt, sample


def kernel(x_prompt, x_sample, c_prompt, c_sample, state_ssd, state_ssd_conv, state_rglru, state_rglru_conv, state_ret, w_ada, b_ada, w_in, ssd_conv_w, ssd_conv_b, ssd_dt_bias, ssd_a_log, ssd_d, ssd_norm_w, rg_conv_w, rg_conv_b, rg_wa, rg_ba, rg_wx, rg_bx, rg_lambda, ret_gn_w, w_out, ln1_g, ln1_b, router_w, router_bias, exp_w_gate, exp_w_up, exp_w_down, ln2_g, ln2_b):
    BP, LP, _ = x_prompt.shape
    BS, LS, _ = x_sample.shape
    TP, TS = BP * LP, BS * LS

    def pad_lanes(v):
        return jnp.pad(v, ((0, 0), (0, LANES - v.shape[-1])))

    eye = jnp.eye(RG_BLOCKS, dtype=F32)

    def block_diag(w):
        return jnp.stack(_hi_lo(jnp.einsum("njk,nm->njmk", w, eye).reshape(D_RG, D_RG)))

    params = []
    for l in range(DEPTH):
        wi = w_in[l]
        wi = jnp.concatenate([wi[:, :2560], wi[:, 2576:5648], wi[:, 2560:2576],
                              jnp.zeros((D_MODEL, LANES - SSD_HEADS), F32)], axis=1)
        params.append(dict(
            w_in=_hi_lo(wi),
            ssd_conv_w=ssd_conv_w[l], ssd_conv_b=ssd_conv_b[l][None],
            ssd_dt_bias=pad_lanes(ssd_dt_bias[l][None]), ssd_a_log=pad_lanes(ssd_a_log[l][None]),
            ssd_d_ch=jnp.repeat(ssd_d[l], SSD_HEAD_DIM)[None], ssd_norm_w=ssd_norm_w[l][None],
            rg_conv_w=rg_conv_w[l], rg_conv_b=rg_conv_b[l][None],
            rg_wa_bd=block_diag(rg_wa[l]), rg_ba=rg_ba[l][None],
            rg_wx_bd=block_diag(rg_wx[l]), rg_bx=rg_bx[l][None],
            rg_lambda=rg_lambda[l][None], ret_gn_w=ret_gn_w[l][None],
            w_out=jnp.stack(_hi_lo(w_out[l])), ln1_g=ln1_g[l][None], ln1_b=ln1_b[l][None],
            ln2_g=ln2_g[l][None], ln2_b=ln2_b[l][None],
        ))

    rw_hi, rw_lo = _hi_lo(pad_lanes(router_w))
    rbias = pad_lanes(router_bias[None])
    head_of_ch = jnp.arange(D_SSD) // SSD_HEAD_DIM
    consts = dict(
        rw_hi=rw_hi, rw_lo=rw_lo, rbias=rbias.astype(F32),
        head_expand=(jnp.arange(LANES)[:, None] == head_of_ch[None, :]).astype(BF16),
        tri=jnp.tril(jnp.ones((CHUNK, CHUNK), F32)).astype(BF16),
    )
    mod = _ada(jnp.concatenate([c_prompt, c_sample], axis=0), w_ada, b_ada)

    LM = LP - TAIL
    paths = [
        dict(x=x_prompt[:, :LM], B=BP, L=LM, per_token=False, has_state=False, precise=False, rc=_ret_consts(LM, 0.0)),
        dict(x=x_prompt[:, LM:], B=BP, L=TAIL, per_token=False, has_state=True, precise=True, rc=_ret_consts(TAIL, float(LM))),
        dict(x=x_sample.reshape(1, TS, D_MODEL), B=BS, L=LS, per_token=True, has_state=True, precise=False,
             rc=_ret_consts(LS, PAST_LEN)),
    ]
    zero_states = (jnp.zeros((BP, SSD_GROUPS, SSD_STATE, D_SSD // SSD_GROUPS), F32),
                   jnp.zeros((BP, CONV_WIDTH - 1, SSD_CONV_DIM), F32),
                   jnp.zeros((BP, 1, D_RG), F32),
                   jnp.zeros((BP, CONV_WIDTH - 1, D_RG), F32),
                   jnp.zeros((BP, RET_HEADS, RET_HEAD_DIM, RET_HEAD_DIM), F32))

    def mixer_stack(path, p, mods, states, ssd_kw, ret_kw):
        B, L, per_token, precise = path["B"], path["L"], path["per_token"], path["precise"]
        w_in_l = p["w_in"] if precise else p["w_in"][:1]
        z, xbc, rg, ret, dt = _inproj(path["x"], mods[1], mods[0], w_in_l, per_token, precise)
        if per_token:
            z, xbc, rg, ret, dt = [a.reshape(B, L, a.shape[-1]) for a in (z, xbc, rg, ret, dt)]
        s_ssd, s_cbuf, s_rg, s_rbuf, s_ret = states
        hs = path["has_state"]
        y_ssd, cbuf_n, ssd_n = _ssd(z, xbc, dt, s_ssd, s_cbuf, p, consts, hs, precise, **ssd_kw)
        y_rg, rbuf_n, rg_n = _rg(rg, s_rg, s_rbuf, p, hs, precise)
        y_ret, ret_n = _ret(ret, s_ret, p, path["rc"], hs, precise, **ret_kw)
        if per_token:
            y_ssd, y_rg, y_ret = [a.reshape(1, B * L, a.shape[-1]) for a in (y_ssd, y_rg, y_ret)]
        return (y_ssd, y_rg, y_ret), (ssd_n, cbuf_n, rg_n, rbuf_n, ret_n)

    expert_w = (exp_w_gate, exp_w_up, exp_w_down)

    def experts(h2_list, route_list, layer, precise):
        h2_all = jnp.concatenate([h.reshape(-1, D_MODEL) for h in h2_list], axis=0)
        route_all = jnp.concatenate([r.reshape(-1, LANES) for r in route_list], axis=0)
        T_all = h2_all.shape[0]
        dest, block_expert, n_used, P = _dispatch_plan(route_all[:, 0:2].astype(jnp.int32))
        tok = jnp.arange(2 * T_all, dtype=jnp.int32) // 2
        slot_tok = jnp.full((P,), T_all, jnp.int32).at[dest].set(tok)
        xs_sorted = jnp.concatenate([h2_all, jnp.zeros((1, D_MODEL), h2_all.dtype)], axis=0)[slot_tok]
        ys_sorted = _moe(xs_sorted, block_expert, n_used, expert_w, layer, precise)
        return ys_sorted[dest[0::2]], ys_sorted[dest[1::2]]

    new_tail = []
    state_ssd_t = jnp.swapaxes(state_ssd, -1, -2)
    ssd_s = ret_s = None
    cbuf_s, rg_s, rbuf_s = [], [], []
    for l in range(DEPTH):
        p = params[l]
        mods_p, mods_s = _mods(mod[l], BP, LS)
        mods = [mods_p, mods_p, mods_s]
        st_s = (state_ssd_t, state_ssd_conv[l], state_rglru[l][:, None, :], state_rglru_conv[l], state_ret)

        ys_m, st_m = mixer_stack(paths[0], p, mods[0], zero_states, {}, {})
        ys_t, st_t = mixer_stack(paths[1], p, mods[1], st_m, dict(native_out=True), {})
        ys_c, st_c = mixer_stack(paths[2], p, mods[2], st_s,
                                 dict(h0_layer=l, native_out=True, out_stack=(l, ssd_s)),
                                 dict(h0_layer=l, out_stack=(l, ret_s)))
        new_tail.append(st_t)
        ssd_s, ret_s = st_c[0], st_c[4]
        cbuf_s.append(st_c[1])
        rg_s.append(st_c[2])
        rbuf_s.append(st_c[3])

        post = []
        for path, ys, m in zip(paths, (ys_m, ys_t, ys_c), mods):
            post.append(_outproj(*ys, path["x"], m[2], m[4], m[3], p, consts, path["per_token"], path["precise"]))

        ya_b, yb_b = experts([post[0][1], post[2][1]], [post[0][2], post[2][2]], l, False)
        ya_t, yb_t = experts([post[1][1]], [post[1][2]], l, True)

        offs = (0, 0, BP * LM)
        for path, (x1, _, route), ya, yb, off, m in zip(paths, post, (ya_b, ya_t, ya_b), (yb_b, yb_t, yb_b), offs, mods):
            path["x"] = _combine(x1, ya, yb, off, route, m[5], p, path["per_token"])

    def stack(lst, k):
        return jnp.stack([s[k] for s in lst])

    y_prompt = jnp.concatenate([paths[0]["x"], paths[1]["x"]], axis=1)
    return (y_prompt, paths[2]["x"].reshape(BS, LS, D_MODEL),
            jnp.swapaxes(stack(new_tail, 0), -1, -2), stack(new_tail, 1), stack(new_tail, 2)[:, :, 0],
            stack(new_tail, 3), stack(new_tail, 4),
            jnp.swapaxes(ssd_s, -1, -2), jnp.stack(cbuf_s), jnp.stack(rg_s)[:, :, 0], jnp.stack(rbuf_s), ret_s)
```

```python
import functools
import math

import jax
import jax.numpy as jnp
from jax import lax
from jax.experimental import pallas as pl
from jax.experimental.pallas import tpu as pltpu

F32 = jnp.float32
BF16 = jnp.bfloat16

D_MODEL = 1024
DEPTH = 4
D_SSD = 1024
SSD_HEAD_DIM = 64
SSD_HEADS = 16
SSD_GROUPS = 2
SSD_STATE = 128
SSD_CONV_DIM = D_SSD + 2 * SSD_GROUPS * SSD_STATE
CONV_WIDTH = 4
D_RG = 512
RG_BLOCKS = 8
RG_C = 8.0
D_RET = 512
RET_HEADS = 4
RET_HEAD_DIM = 128
ROPE_BASE = 10000.0
N_EXPERTS = 32
EXPERTS_PER_GROUP = 8
N_EXPERT_GROUPS = 4
D_EXPERT = 512
ALPHA = (2 * DEPTH) ** 0.25
EPS = 1e-5
PAST_LEN = 16384.0

LANES = 128
SUBLANES = 8
CHUNK = 128
TAIL = CHUNK
PROJ_TM = 384
MOE_BM = 256
VMEM_LIMIT = 56 * 1024 * 1024


def _cparams(sem):
    return pltpu.CompilerParams(dimension_semantics=sem, vmem_limit_bytes=VMEM_LIMIT)


def _padded_rows(L):
    return CHUNK if L >= CHUNK else -(-L // SUBLANES) * SUBLANES


def _pick_tm(L, cap):
    if L <= LANES:
        return L
    tm = cap - cap % LANES
    while L % tm:
        tm -= LANES
    return tm


def _sigmoid(x):
    return 1.0 / (1.0 + jnp.exp(-x))


def _silu(x):
    return x * _sigmoid(x)


def _softplus(x):
    return jnp.maximum(x, 0.0) + jnp.log1p(jnp.exp(-jnp.abs(x)))


def _split3(v):
    hi = v.astype(BF16)
    r = v - hi.astype(F32)
    mid = r.astype(BF16)
    lo = (r - mid.astype(F32)).astype(BF16)
    return hi, mid, lo


def _dot(a, b):
    return jnp.dot(a, b, preferred_element_type=F32)


def _split2(v):
    hi = v.astype(BF16)
    return hi, (v - hi.astype(F32)).astype(BF16)


def _mm_w(a, w_hi, w_lo):
    if w_lo is None:
        return _dot(a.astype(BF16), w_hi)
    ah, al = _split2(a)
    return _dot(ah, w_hi) + (_dot(al, w_hi) + _dot(ah, w_lo))


def _mm_a(a, b, precise, nt=False):
    dn = (((1,), (1,)), ((), ())) if nt else (((1,), (0,)), ((), ()))

    def d(x, y):
        return lax.dot_general(x, y, dn, preferred_element_type=F32)

    if not precise:
        return d(a.astype(BF16), b.astype(BF16))
    ah, al = _split2(a)
    bh, bl = _split2(b)
    return d(ah, bh) + (d(al, bh) + d(ah, bl))


def _hi_lo(w):
    hi32 = lax.reduce_precision(w, exponent_bits=8, mantissa_bits=7)
    return hi32.astype(BF16), (w - hi32).astype(BF16)


def _dot_exact_rhs(v, m):
    hi, mid, lo = _split3(v)
    return _dot(hi, m) + _dot(mid, m) + _dot(lo, m)


def _dot_exact_lhs(m, v):
    hi, mid, lo = _split3(v)
    return _dot(m, hi) + _dot(m, mid) + _dot(m, lo)


def _ada_kernel(c_ref, w_ref, b_ref, o_ref):
    o_ref[0] = _mm_a(_silu(c_ref[...]), w_ref[0], True) + b_ref[0]


def _ada(c_all, w_ada, b_ada):
    n = c_all.shape[0]
    tn = 1024
    return pl.pallas_call(
        _ada_kernel,
        grid=(DEPTH, 6 * D_MODEL // tn),
        in_specs=[
            pl.BlockSpec((n, D_MODEL), lambda l, j: (0, 0)),
            pl.BlockSpec((1, D_MODEL, tn), lambda l, j: (l, 0, j)),
            pl.BlockSpec((1, 1, tn), lambda l, j: (l, 0, j)),
        ],
        out_specs=pl.BlockSpec((1, n, tn), lambda l, j: (l, 0, j)),
        out_shape=jax.ShapeDtypeStruct((DEPTH, n, 6 * D_MODEL), F32),
        compiler_params=_cparams(("arbitrary", "arbitrary")),
        name="ada",
    )(c_all, w_ada, b_ada.reshape(DEPTH, 1, 6 * D_MODEL))


_IN_SEGS = (("z", 0, 1024), ("xbc", 1024, 2560), ("rg", 2560, 3584), ("ret", 3584, 5632), ("dt", 5632, 5760))
IN_PAD = 5760


def _inproj_kernel(x_ref, sc_ref, sh_ref, w_ref, *rest, precise):
    wl_ref = rest[0] if precise else None
    outs = rest[1:] if precise else rest
    h = x_ref[0] * (1.0 + sc_ref[0]) + sh_ref[0]
    if precise:
        hh, hl = _split2(h)
    else:
        hh = h.astype(BF16)
    for o_ref, (_, lo, hi) in zip(outs, _IN_SEGS):
        acc = _dot(hh, w_ref[:, lo:hi])
        if precise:
            acc = acc + (_dot(hl, w_ref[:, lo:hi]) + _dot(hh, wl_ref[:, lo:hi]))
        o_ref[0] = acc


def _inproj(x, sc, sh, w, per_token_mod, precise):
    B, L, _ = x.shape
    tm = _pick_tm(L, LANES if precise else PROJ_TM)
    if per_token_mod:
        mod_spec = pl.BlockSpec((1, tm, D_MODEL), lambda b, i: (b, i, 0))
    else:
        mod_spec = pl.BlockSpec((1, 1, D_MODEL), lambda b, i: (b, 0, 0))
    widths = [hi - lo for _, lo, hi in _IN_SEGS]
    w_spec = pl.BlockSpec((D_MODEL, IN_PAD), lambda b, i: (0, 0), pipeline_mode=pl.Buffered(1))
    ws = w if precise else w[:1]
    return pl.pallas_call(
        functools.partial(_inproj_kernel, precise=precise),
        grid=(B, L // tm),
        in_specs=[
            pl.BlockSpec((1, tm, D_MODEL), lambda b, i: (b, i, 0)),
            mod_spec,
            mod_spec,
        ] + [w_spec] * len(ws),
        out_specs=[pl.BlockSpec((1, tm, wd), lambda b, i: (b, i, 0)) for wd in widths],
        out_shape=[jax.ShapeDtypeStruct((B, L, wd), F32) for wd in widths],
        compiler_params=_cparams(("arbitrary", "arbitrary")),
        name="inproj",
    )(x, sc, sh, *ws)


def _conv_step(src_rows, buf_ref, cw_ref, cb_ref, nbuf_ref, xp_s, *, c, nc, Lv, Lp, has_state):
    @pl.when(c == 0)
    def _():
        xp_s[...] = jnp.zeros_like(xp_s)
        if has_state:
            xp_s[SUBLANES - (CONV_WIDTH - 1):SUBLANES, :] = buf_ref[0]

    xp_s[SUBLANES:SUBLANES + Lv, :] = src_rows
    base = SUBLANES - (CONV_WIDTH - 1)
    xc = cb_ref[...]
    for j in range(CONV_WIDTH):
        xc = xc + cw_ref[j:j + 1, :] * xp_s[base + j:base + j + Lp, :]

    @pl.when(c == nc - 1)
    def _():
        nbuf_ref[0] = xp_s[SUBLANES + Lv - (CONV_WIDTH - 1):SUBLANES + Lv, :]

    if nc > 1:
        xp_s[0:SUBLANES, :] = xp_s[Lp:Lp + SUBLANES, :]
    return xc


def _ssd_kernel(z_ref, xbc_ref, dt_ref, h0_ref, buf_ref, cw_ref, cb_ref, dtb_ref, alog_ref, dch_ref,
                nw_ref, e_ref, tri_ref, *rest, Lv, Lp, nc, has_state, precise, native_in, native_out):
    y_ref, nbuf_ref, hn_ref, xp_s, dtp_s, s_s = rest[-6:]
    c = pl.program_id(1)
    hpg = SSD_HEADS // SSD_GROUPS

    @pl.when(c == 0)
    def _():
        if not has_state:
            s_s[...] = jnp.zeros_like(s_s)
        elif native_in:
            for g in range(SSD_GROUPS):
                s_s[g] = jnp.concatenate([h0_ref[g * hpg + i] for i in range(hpg)], axis=0).T
        else:
            s_s[...] = h0_ref[...]
        if Lv < Lp:
            dtp_s[...] = jnp.zeros_like(dtp_s)

    xc = _conv_step(xbc_ref[0], buf_ref, cw_ref, cb_ref, nbuf_ref, xp_s,
                    c=c, nc=nc, Lv=Lv, Lp=Lp, has_state=has_state)
    xbc = _silu(xc)
    xs = xbc[:, 0:D_SSD]
    Bm = xbc[:, D_SSD:D_SSD + 256]
    Cm = xbc[:, D_SSD + 256:D_SSD + 512]

    if Lv < Lp:
        dtp_s[0:Lv, :] = dt_ref[0]
        dtr = dtp_s[...]
    else:
        dtr = dt_ref[0]
    dt = _softplus(dtr + dtb_ref[...])
    if Lv < Lp:
        rowv = lax.broadcasted_iota(jnp.int32, (Lp, LANES), 0)
        dt = jnp.where(rowv < Lv, dt, 0.0)
    a = dt * (-jnp.exp(alog_ref[...]))
    acum = _dot_exact_lhs(tri_ref[...], a)
    dt_e = _dot_exact_rhs(dt, e_ref[...])
    ac_e = _dot_exact_rhs(acum, e_ref[...])
    acl_e = ac_e[Lp - 1:Lp, :]
    acum_t = acum.T

    vdt = xs * dt_e
    lane = lax.broadcasted_iota(jnp.int32, (Lp, LANES), 1)
    causal = lax.broadcasted_iota(jnp.int32, (Lp, Lp), 0) >= lax.broadcasted_iota(jnp.int32, (Lp, Lp), 1)
    mid_t = F32 if precise else BF16

    y_blocks = []
    for g in range(SSD_GROUPS):
        Cg = Cm[:, g * SSD_STATE:(g + 1) * SSD_STATE]
        Bg = Bm[:, g * SSD_STATE:(g + 1) * SSD_STATE]
        G = _mm_a(Cg, Bg, precise, nt=True)
        for jj in range(4):
            j = 4 * g + jj
            scs = []
            for h in (2 * j, 2 * j + 1):
                col = jnp.sum(jnp.where(lane == h, acum, 0.0), axis=1, keepdims=True)
                rw = acum_t[h:h + 1, :]
                dec = jnp.exp(jnp.where(causal, col - rw, -jnp.inf))
                scs.append((G * dec).astype(mid_t))
            sc = jnp.concatenate(scs, axis=1)
            vb = vdt[:, j * LANES:(j + 1) * LANES]
            v2 = jnp.concatenate([jnp.where(lane < SSD_HEAD_DIM, vb, 0.0),
                                  jnp.where(lane >= SSD_HEAD_DIM, vb, 0.0)], axis=0).astype(mid_t)
            y_blocks.append(_mm_a(sc, v2, precise))
    y = jnp.concatenate(y_blocks, axis=1)

    eac = jnp.exp(ac_e)
    vw = vdt * jnp.exp(acl_e - ac_e)
    cd = jnp.exp(acl_e)
    half = D_SSD // SSD_GROUPS
    ys_parts = []
    for g in range(SSD_GROUPS):
        Sg = s_s[g]
        ys_parts.append(_mm_a(Cm[:, g * SSD_STATE:(g + 1) * SSD_STATE], Sg, precise))
        BgT = Bm[:, g * SSD_STATE:(g + 1) * SSD_STATE].T
        s_s[g] = cd[:, g * half:(g + 1) * half] * Sg + _mm_a(BgT, vw[:, g * half:(g + 1) * half], precise)
    y = y + eac * jnp.concatenate(ys_parts, axis=1) + dch_ref[...] * xs

    yv = y[0:Lv, :] if Lv < Lp else y
    yz = yv * _silu(z_ref[0])
    outs = []
    for g in range(SSD_GROUPS):
        part = yz[:, g * half:(g + 1) * half]
        ms = jnp.mean(part * part, axis=1, keepdims=True)
        outs.append(part * lax.rsqrt(ms + EPS))
    y_ref[0] = jnp.concatenate(outs, axis=1) * nw_ref[...]

    @pl.when(c == nc - 1)
    def _():
        if native_out:
            for g in range(SSD_GROUPS):
                sgt = s_s[g].T
                for i in range(hpg):
                    hn_ref[g * hpg + i] = sgt[i * SSD_HEAD_DIM:(i + 1) * SSD_HEAD_DIM, :]
        else:
            hn_ref[...] = s_s[...]


def _state_specs(B, shape, in_layer, out_stack):
    zeros = (0,) * len(shape)
    if in_layer is None:
        in_spec = pl.BlockSpec((None,) + shape, lambda b, c: (b,) + zeros)
    else:
        in_spec = pl.BlockSpec((None, None) + shape, lambda b, c: (in_layer, b) + zeros)
    if out_stack is None:
        out_spec = pl.BlockSpec((None,) + shape, lambda b, c: (b,) + zeros)
        out_shape = jax.ShapeDtypeStruct((B,) + shape, F32)
    else:
        l = out_stack[0]
        out_spec = pl.BlockSpec((None, None) + shape, lambda b, c: (l, b) + zeros)
        out_shape = jax.ShapeDtypeStruct((DEPTH, B) + shape, F32)
    return in_spec, out_spec, out_shape


def _ssd(z, xbc, dt, h0, buf, p, consts, has_state, precise, h0_layer=None, native_out=False, out_stack=None):
    B, L, _ = z.shape
    Lv = min(CHUNK, L)
    Lp = _padded_rows(L)
    nc = L // Lv
    native_in = h0_layer is not None
    kern = functools.partial(_ssd_kernel, Lv=Lv, Lp=Lp, nc=nc, has_state=has_state, precise=precise,
                             native_in=native_in, native_out=native_out)
    row = lambda b, c: (b, c, 0)
    per_b3 = lambda b, c: (b, 0, 0)
    fixed = lambda b, c: (0, 0)
    half = D_SSD // SSD_GROUPS
    native = (SSD_HEADS, SSD_HEAD_DIM, SSD_STATE)
    packed = (SSD_GROUPS, SSD_STATE, half)
    h0_spec, _, _ = _state_specs(B, native if native_in else packed, h0_layer, None)
    _, hn_spec, hn_shape = _state_specs(B, native if native_out else packed, None, out_stack)
    prev = out_stack[1] if out_stack is not None else None
    extra_in, extra_specs, aliases = [], [], {}
    if prev is not None:
        extra_in, extra_specs, aliases = [prev], [pl.BlockSpec(memory_space=pl.ANY)], {13: 2}
    return pl.pallas_call(
        kern,
        grid=(B, nc),
        in_specs=[
            pl.BlockSpec((1, Lv, D_SSD), row),
            pl.BlockSpec((1, Lv, SSD_CONV_DIM), row),
            pl.BlockSpec((1, Lv, LANES), row),
            h0_spec,
            pl.BlockSpec((1, CONV_WIDTH - 1, SSD_CONV_DIM), per_b3),
            pl.BlockSpec((CONV_WIDTH, SSD_CONV_DIM), fixed),
            pl.BlockSpec((1, SSD_CONV_DIM), fixed),
            pl.BlockSpec((1, LANES), fixed),
            pl.BlockSpec((1, LANES), fixed),
            pl.BlockSpec((1, D_SSD), fixed),
            pl.BlockSpec((1, D_SSD), fixed),
            pl.BlockSpec((LANES, D_SSD), fixed),
            pl.BlockSpec((Lp, Lp), fixed),
        ] + extra_specs,
        out_specs=[
            pl.BlockSpec((1, Lv, D_SSD), row),
            pl.BlockSpec((1, CONV_WIDTH - 1, SSD_CONV_DIM), per_b3),
            hn_spec,
        ],
        out_shape=[
            jax.ShapeDtypeStruct((B, L, D_SSD), F32),
            jax.ShapeDtypeStruct((B, CONV_WIDTH - 1, SSD_CONV_DIM), F32),
            hn_shape,
        ],
        scratch_shapes=[
            pltpu.VMEM((SUBLANES + Lp, SSD_CONV_DIM), F32),
            pltpu.VMEM((Lp, LANES), F32),
            pltpu.VMEM((SSD_GROUPS, SSD_STATE, half), F32),
        ],
        input_output_aliases=aliases,
        compiler_params=_cparams(("arbitrary", "arbitrary")),
        name="ssd",
    )(z, xbc, dt, h0, buf, p["ssd_conv_w"], p["ssd_conv_b"], p["ssd_dt_bias"], p["ssd_a_log"],
      p["ssd_d_ch"], p["ssd_norm_w"], consts["head_expand"],
      jnp.tril(jnp.ones((Lp, Lp), F32)).astype(BF16), *extra_in)


def _rg_kernel(rg_ref, h0_ref, buf_ref, cw_ref, cb_ref, wa_ref, ba_ref, wx_ref, bx_ref, lam_ref,
               y_ref, nbuf_ref, hn_ref, xp_s, hc_s, *, Lv, Lp, nc, has_state, precise):
    c = pl.program_id(1)

    @pl.when(c == 0)
    def _():
        if has_state:
            hc_s[...] = h0_ref[0]
        else:
            hc_s[...] = jnp.zeros_like(hc_s)

    xc = _conv_step(rg_ref[0, :, 0:D_RG], buf_ref, cw_ref, cb_ref, nbuf_ref, xp_s,
                    c=c, nc=nc, Lv=Lv, Lp=Lp, has_state=has_state)
    r = _sigmoid(_mm_w(xc, wa_ref[0], wa_ref[1] if precise else None) + ba_ref[...])
    i = _sigmoid(_mm_w(xc, wx_ref[0], wx_ref[1] if precise else None) + bx_ref[...])
    la = (-RG_C * _softplus(-lam_ref[...])) * r
    u = jnp.sqrt(-jnp.tanh(la) * (jnp.exp(2.0 * la) + 1.0)) * (i * xc)
    row = lax.broadcasted_iota(jnp.int32, (Lp, D_RG), 0)
    if Lv < Lp:
        valid = row < Lv
        la = jnp.where(valid, la, 0.0)
        u = jnp.where(valid, u, 0.0)
    a = jnp.exp(la)
    s = 1
    while s < Lp:
        m = row >= s
        u_sh = jnp.where(m, pltpu.roll(u, s, 0), 0.0)
        a_sh = jnp.where(m, pltpu.roll(a, s, 0), 1.0)
        u = u + a * u_sh
        a = a * a_sh
        s *= 2
    h = u + a * hc_s[...]
    hc_s[...] = h[Lp - 1:Lp, :]
    hv = h[0:Lv, :] if Lv < Lp else h
    y_ref[0] = hv * jax.nn.gelu(rg_ref[0, :, D_RG:2 * D_RG])

    @pl.when(c == nc - 1)
    def _():
        hn_ref[0] = hc_s[...]


def _rg(rg, h0, buf, p, has_state, precise):
    B, L, _ = rg.shape
    Lv = min(CHUNK, L)
    Lp = _padded_rows(L)
    nc = L // Lv
    kern = functools.partial(_rg_kernel, Lv=Lv, Lp=Lp, nc=nc, has_state=has_state, precise=precise)
    row = lambda b, c: (b, c, 0)
    per_b3 = lambda b, c: (b, 0, 0)
    fixed = lambda b, c: (0, 0)
    fixed3 = lambda b, c: (0, 0, 0)
    return pl.pallas_call(
        kern,
        grid=(B, nc),
        in_specs=[
            pl.BlockSpec((1, Lv, 2 * D_RG), row),
            pl.BlockSpec((1, 1, D_RG), per_b3),
            pl.BlockSpec((1, CONV_WIDTH - 1, D_RG), per_b3),
            pl.BlockSpec((CONV_WIDTH, D_RG), fixed),
            pl.BlockSpec((1, D_RG), fixed),
            pl.BlockSpec((2, D_RG, D_RG), fixed3),
            pl.BlockSpec((1, D_RG), fixed),
            pl.BlockSpec((2, D_RG, D_RG), fixed3),
            pl.BlockSpec((1, D_RG), fixed),
            pl.BlockSpec((1, D_RG), fixed),
        ],
        out_specs=[
            pl.BlockSpec((1, Lv, D_RG), row),
            pl.BlockSpec((1, CONV_WIDTH - 1, D_RG), per_b3),
            pl.BlockSpec((1, 1, D_RG), per_b3),
        ],
        out_shape=[
            jax.ShapeDtypeStruct((B, L, D_RG), F32),
            jax.ShapeDtypeStruct((B, CONV_WIDTH - 1, D_RG), F32),
            jax.ShapeDtypeStruct((B, 1, D_RG), F32),
        ],
        scratch_shapes=[
            pltpu.VMEM((SUBLANES + Lp, D_RG), F32),
            pltpu.VMEM((1, D_RG), F32),
        ],
        compiler_params=_cparams(("arbitrary", "arbitrary")),
        name="rglru",
    )(rg, h0, buf, p["rg_conv_w"], p["rg_conv_b"], p["rg_wa_bd"], p["rg_ba"], p["rg_wx_bd"],
      p["rg_bx"], p["rg_lambda"])


def _ret_kernel(ret_ref, h0_ref, cos_ref, sin_ref, dm_ref, ea_ref, te_ref, cd_ref, gn_ref,
                *rest, Lv, Lp, nc, has_state, precise):
    y_ref, hn_ref, pad_s, s_s = rest[-4:]
    c = pl.program_id(1)

    @pl.when(c == 0)
    def _():
        if has_state:
            s_s[...] = h0_ref[...]
        else:
            s_s[...] = jnp.zeros_like(s_s)
        if Lv < Lp:
            pad_s[...] = jnp.zeros_like(pad_s)

    if Lv < Lp:
        pad_s[0:Lv, :] = ret_ref[0, :, 0:3 * D_RET]
        qkv = pad_s[...]
    else:
        qkv = ret_ref[0, :, 0:3 * D_RET]
    cosf = cos_ref[...]
    sinf = sin_ref[...]
    outs = []
    for h in range(RET_HEADS):
        sl = slice(h * RET_HEAD_DIM, (h + 1) * RET_HEAD_DIM)
        q = qkv[:, sl]
        k = qkv[:, D_RET + h * RET_HEAD_DIM:D_RET + (h + 1) * RET_HEAD_DIM]
        v = qkv[:, 2 * D_RET + h * RET_HEAD_DIM:2 * D_RET + (h + 1) * RET_HEAD_DIM]
        q = q * cosf + pltpu.roll(q, RET_HEAD_DIM // 2, 1) * sinf
        k = (k * cosf + pltpu.roll(k, RET_HEAD_DIM // 2, 1) * sinf) * (RET_HEAD_DIM ** -0.5)
        G = _mm_a(q, k, precise, nt=True)
        S = s_s[h]
        y = _mm_a(G * dm_ref[h], v, precise) + _mm_a(q * ea_ref[h], S, precise)
        s_s[h] = cd_ref[h] * S + _mm_a((k * te_ref[h]).T, v, precise)
        yv = y[0:Lv, :] if Lv < Lp else y
        mu = jnp.mean(yv, axis=1, keepdims=True)
        d = yv - mu
        var = jnp.mean(d * d, axis=1, keepdims=True)
        outs.append(d * lax.rsqrt(var + EPS))
    yn = jnp.concatenate(outs, axis=1) * gn_ref[...]
    y_ref[0] = _silu(ret_ref[0, :, 3 * D_RET:4 * D_RET]) * yn

    @pl.when(c == nc - 1)
    def _():
        hn_ref[...] = s_s[...]


def _ret(ret, h0, p, rc, has_state, precise, h0_layer=None, out_stack=None):
    B, L, _ = ret.shape
    Lv = min(CHUNK, L)
    Lp = _padded_rows(L)
    nc = L // Lv
    kern = functools.partial(_ret_kernel, Lv=Lv, Lp=Lp, nc=nc, has_state=has_state, precise=precise)
    row = lambda b, c: (b, c, 0)
    fixed2 = lambda b, c: (0, 0)
    fixed3 = lambda b, c: (0, 0, 0)
    hd = RET_HEAD_DIM
    h0_spec, _, _ = _state_specs(B, (RET_HEADS, hd, hd), h0_layer, None)
    _, hn_spec, hn_shape = _state_specs(B, (RET_HEADS, hd, hd), None, out_stack)
    prev = out_stack[1] if out_stack is not None else None
    extra_in, extra_specs, aliases = [], [], {}
    if prev is not None:
        extra_in, extra_specs, aliases = [prev], [pl.BlockSpec(memory_space=pl.ANY)], {9: 1}
    return pl.pallas_call(
        kern,
        grid=(B, nc),
        in_specs=[
            pl.BlockSpec((1, Lv, 4 * D_RET), row),
            h0_spec,
            pl.BlockSpec((Lp, hd), lambda b, c: (c, 0)),
            pl.BlockSpec((Lp, hd), lambda b, c: (c, 0)),
            pl.BlockSpec((RET_HEADS, Lp, Lp), fixed3),
            pl.BlockSpec((RET_HEADS, Lp, hd), fixed3),
            pl.BlockSpec((RET_HEADS, Lp, hd), fixed3),
            pl.BlockSpec((RET_HEADS, hd, hd), fixed3),
            pl.BlockSpec((1, D_RET), fixed2),
        ] + extra_specs,
        out_specs=[
            pl.BlockSpec((1, Lv, D_RET), row),
            hn_spec,
        ],
        out_shape=[
            jax.ShapeDtypeStruct((B, L, D_RET), F32),
            hn_shape,
        ],
        scratch_shapes=[
            pltpu.VMEM((Lp, 3 * D_RET), F32),
            pltpu.VMEM((RET_HEADS, hd, hd), F32),
        ],
        input_output_aliases=aliases,
        compiler_params=_cparams(("arbitrary", "arbitrary")),
        name="retention",
    )(ret, h0, rc["cos"], rc["sin"], rc["dmat"], rc["eacum"], rc["toend"], rc["cdec"], p["ret_gn_w"], *extra_in)


def _ret_consts(L, pos0):
    Lv = min(CHUNK, L)
    Lp = _padded_rows(L)
    nc = L // Lv
    half = RET_HEAD_DIM // 2
    inv = ROPE_BASE ** (-jnp.arange(half, dtype=F32) / half)
    pos = pos0 + jnp.arange(L, dtype=F32)
    ang = pos[:, None] * inv[None, :]
    cos = jnp.cos(ang)
    sin = jnp.sin(ang)
    cosf = jnp.concatenate([cos, cos], axis=1)
    sinf = jnp.concatenate([-sin, sin], axis=1)
    if Lv < Lp:
        cosf = jnp.pad(cosf, ((0, Lp - Lv), (0, 0)))
        sinf = jnp.pad(sinf, ((0, Lp - Lv), (0, 0)))
    assert cosf.shape[0] == nc * Lp
    log_gamma = jnp.log1p(-jnp.exp2(-5.0 - jnp.arange(RET_HEADS, dtype=F32)))
    steps = jnp.minimum(jnp.arange(Lp) + 1, Lv).astype(F32)
    acum = log_gamma[:, None] * steps[None, :]
    causal = jnp.tril(jnp.ones((Lp, Lp), bool))
    dmat = jnp.exp(jnp.where(causal[None], acum[:, :, None] - acum[:, None, :], -jnp.inf))
    ones = jnp.ones((RET_HEADS, Lp, RET_HEAD_DIM), F32)
    eacum = jnp.exp(acum)[:, :, None] * ones
    toend = jnp.exp(acum[:, -1:] - acum)[:, :, None] * ones
    rowvalid = (jnp.arange(Lp) < Lv).astype(F32)[None, :, None]
    toend = toend * rowvalid
    cdec = jnp.exp(acum[:, -1])[:, None, None] * jnp.ones((RET_HEADS, RET_HEAD_DIM, RET_HEAD_DIM), F32)
    return dict(cos=cosf, sin=sinf, dmat=dmat, eacum=eacum, toend=toend, cdec=cdec)


def _layernorm(v, g, b):
    mu = jnp.mean(v, axis=1, keepdims=True)
    d = v - mu
    var = jnp.mean(d * d, axis=1, keepdims=True)
    return d * lax.rsqrt(var + EPS) * g + b


def _route(logits, bias):
    tm = logits.shape[0]
    lane = lax.broadcasted_iota(jnp.int32, (tm, LANES), 1)
    lanef = lane.astype(F32)
    scores = _sigmoid(logits)
    choice = scores + bias
    neg = -jnp.inf
    best = jnp.full((tm, 1), neg, F32)
    e1 = jnp.zeros((tm, 1), F32)
    e2 = jnp.zeros((tm, 1), F32)
    for g in range(N_EXPERT_GROUPS):
        ing = (lane >= g * EXPERTS_PER_GROUP) & (lane < (g + 1) * EXPERTS_PER_GROUP)
        cg = jnp.where(ing, choice, neg)
        m1 = jnp.max(cg, axis=1, keepdims=True)
        i1 = jnp.min(jnp.where(cg == m1, lanef, float(LANES)), axis=1, keepdims=True)
        cg2 = jnp.where(lanef == i1, neg, cg)
        m2 = jnp.max(cg2, axis=1, keepdims=True)
        i2 = jnp.min(jnp.where(cg2 == m2, lanef, float(LANES)), axis=1, keepdims=True)
        gs = m1 + m2
        better = gs > best
        best = jnp.where(better, gs, best)
        e1 = jnp.where(better, i1, e1)
        e2 = jnp.where(better, i2, e2)
    w1 = jnp.sum(jnp.where(lanef == e1, scores, 0.0), axis=1, keepdims=True)
    w2 = jnp.sum(jnp.where(lanef == e2, scores, 0.0), axis=1, keepdims=True)
    den = w1 + w2
    out = jnp.where(lane == 0, e1, jnp.where(lane == 1, e2, jnp.where(lane == 2, w1 / den, jnp.where(lane == 3, w2 / den, 0.0))))
    return out


def _outproj_kernel(ys_ref, yr_ref, yt_ref, x_ref, g1_ref, sc2_ref, sh2_ref, w_ref, lng_ref, lnb_ref,
                    rwh_ref, rwl_ref, rb_ref, x1_ref, h2_ref, route_ref, *, precise):
    def part(y_ref, lo, hi):
        return _mm_w(y_ref[0], w_ref[0, lo:hi, :], w_ref[1, lo:hi, :] if precise else None)

    mix = part(ys_ref, 0, D_SSD) + part(yr_ref, D_SSD, D_SSD + D_RG) + part(yt_ref, D_SSD + D_RG, 2 * D_MODEL)
    x1 = _layernorm(ALPHA * x_ref[0] + (1.0 + g1_ref[0]) * mix, lng_ref[...], lnb_ref[...])
    x1_ref[0] = x1
    h2 = x1 * (1.0 + sc2_ref[0]) + sh2_ref[0]
    hi, lo = _split2(h2)
    h2_ref[0] = h2
    logits = _dot(hi, rwh_ref[...]) + (_dot(lo, rwh_ref[...]) + _dot(hi, rwl_ref[...]))
    route_ref[0] = _route(logits, rb_ref[...])


def _outproj(y_ssd, y_rg, y_ret, x, g1, sc2, sh2, p, consts, per_token_mod, precise):
    B, L, _ = x.shape
    tm = _pick_tm(L, LANES if precise else PROJ_TM)
    row = lambda b, i: (b, i, 0)
    fixed = lambda b, i: (0, 0)
    if per_token_mod:
        mod_spec = pl.BlockSpec((1, tm, D_MODEL), row)
    else:
        mod_spec = pl.BlockSpec((1, 1, D_MODEL), lambda b, i: (b, 0, 0))
    return pl.pallas_call(
        functools.partial(_outproj_kernel, precise=precise),
        grid=(B, L // tm),
        in_specs=[
            pl.BlockSpec((1, tm, D_SSD), row),
            pl.BlockSpec((1, tm, D_RG), row),
            pl.BlockSpec((1, tm, D_RET), row),
            pl.BlockSpec((1, tm, D_MODEL), row),
            mod_spec, mod_spec, mod_spec,
            pl.BlockSpec((2, 2 * D_MODEL, D_MODEL), lambda b, i: (0, 0, 0)),
            pl.BlockSpec((1, D_MODEL), fixed),
            pl.BlockSpec((1, D_MODEL), fixed),
            pl.BlockSpec((D_MODEL, LANES), fixed),
            pl.BlockSpec((D_MODEL, LANES), fixed),
            pl.BlockSpec((1, LANES), fixed),
        ],
        out_specs=[
            pl.BlockSpec((1, tm, D_MODEL), row),
            pl.BlockSpec((1, tm, D_MODEL), row),
            pl.BlockSpec((1, tm, LANES), row),
        ],
        out_shape=[
            jax.ShapeDtypeStruct((B, L, D_MODEL), F32),
            jax.ShapeDtypeStruct((B, L, D_MODEL), F32),
            jax.ShapeDtypeStruct((B, L, LANES), F32),
        ],
        compiler_params=_cparams(("arbitrary", "arbitrary")),
        name="outproj",
    )(y_ssd, y_rg, y_ret, x, g1, sc2, sh2, p["w_out"], p["ln1_g"], p["ln1_b"],
      consts["rw_hi"], consts["rw_lo"], consts["rbias"])


def _row_gather_copy(src_hbm, row, dst_vmem, slot, r, sem):
    return pltpu.make_async_copy(src_hbm.at[pl.ds(row, 1)], dst_vmem.at[slot, pl.ds(r, 1)], sem.at[slot])


def _moe_kernel(be_ref, nu_ref, *refs, precise, gather):
    if gather:
        rowc_ref, rown_ref, src_ref, wg_ref, wu_ref, wd_ref, o_ref = refs[:7]
        scr, xbuf, sem = refs[7:-2], refs[-2], refs[-1]
    else:
        x_ref, wg_ref, wu_ref, wd_ref, o_ref = refs[:5]
        scr = refs[5:]
    i = pl.program_id(0)
    n_used = nu_ref[0]
    used = i < n_used
    new_expert = jnp.logical_or(i == 0, be_ref[i] != be_ref[jnp.maximum(i - 1, 0)])

    if gather:
        slot = lax.rem(i, 2)

        def issue(rows_ref, s):
            def body(r, carry):
                _row_gather_copy(src_ref, rows_ref[0, 0, r], xbuf, s, r, sem).start()
                return carry
            lax.fori_loop(0, MOE_BM, body, 0, unroll=8)

        @pl.when(jnp.logical_and(used, i == 0))
        def _():
            issue(rowc_ref, 0)

        @pl.when(i + 1 < n_used)
        def _():
            issue(rown_ref, 1 - slot)

    @pl.when(jnp.logical_and(used, new_expert))
    def _():
        for k, w_ref in enumerate((wg_ref, wu_ref, wd_ref)):
            w = w_ref[...]
            if precise:
                hi, lo = _split2(w)
                scr[k][...] = hi
                scr[3 + k][...] = lo
            else:
                scr[k][...] = w.astype(BF16)

    @pl.when(used)
    def _():
        if gather:
            pltpu.make_async_copy(src_ref.at[pl.ds(0, MOE_BM)], xbuf.at[slot], sem.at[slot]).wait()
            x = xbuf[slot]
        else:
            x = x_ref[...]
        lo = (scr[3][...], scr[4][...], scr[5][...]) if precise else (None, None, None)
        g = _mm_w(x, scr[0][...], lo[0])
        u = _mm_w(x, scr[1][...], lo[1])
        o_ref[...] = _mm_w(_silu(g) * u, scr[2][...], lo[2])

    @pl.when(jnp.logical_not(used))
    def _():
        o_ref[...] = jnp.zeros_like(o_ref)


def _moe(xs, block_expert, n_used, weights, layer, precise, slot_rows=None):
    gather = slot_rows is not None
    P = slot_rows.shape[0] if gather else xs.shape[0]
    nblk = P // MOE_BM
    in_w = pl.BlockSpec((None, None, D_MODEL, D_EXPERT), lambda i, be, nu: (layer, be[i], 0, 0))
    out_w = pl.BlockSpec((None, None, D_EXPERT, D_MODEL), lambda i, be, nu: (layer, be[i], 0, 0))
    w_scratch = [pltpu.VMEM((D_MODEL, D_EXPERT), BF16), pltpu.VMEM((D_MODEL, D_EXPERT), BF16),
                 pltpu.VMEM((D_EXPERT, D_MODEL), BF16)] * (2 if precise else 1)
    if gather:
        rows3 = slot_rows.reshape(nblk, 1, MOE_BM)
        x_specs = [
            pl.BlockSpec((1, 1, MOE_BM), lambda i, be, nu: (i, 0, 0), memory_space=pltpu.SMEM),
            pl.BlockSpec((1, 1, MOE_BM), lambda i, be, nu: (jnp.minimum(i + 1, nblk - 1), 0, 0),
                         memory_space=pltpu.SMEM),
            pl.BlockSpec(memory_space=pl.ANY),
        ]
        x_args = [rows3, rows3, xs]
        extra_scratch = [pltpu.VMEM((2, MOE_BM, D_MODEL), xs.dtype), pltpu.SemaphoreType.DMA((2,))]
    else:
        x_specs = [pl.BlockSpec((MOE_BM, D_MODEL), lambda i, be, nu: (i, 0))]
        x_args = [xs]
        extra_scratch = []
    grid_spec = pltpu.PrefetchScalarGridSpec(
        num_scalar_prefetch=2,
        grid=(nblk,),
        in_specs=x_specs + [in_w, in_w, out_w],
        out_specs=pl.BlockSpec((MOE_BM, D_MODEL), lambda i, be, nu: (i, 0)),
        scratch_shapes=w_scratch + extra_scratch,
    )
    return pl.pallas_call(
        functools.partial(_moe_kernel, precise=precise, gather=gather),
        grid_spec=grid_spec,
        out_shape=jax.ShapeDtypeStruct((P, D_MODEL), F32),
        compiler_params=_cparams(("arbitrary",)),
        name="moe_gather" if gather else "moe",
    )(block_expert, n_used, *x_args, *weights)


def _combine_kernel(x1_ref, ya_ref, yb_ref, route_ref, g2_ref, lng_ref, lnb_ref, o_ref):
    tm = x1_ref.shape[1]
    lane = lax.broadcasted_iota(jnp.int32, (tm, LANES), 1)
    rt = route_ref[0]
    w1 = jnp.sum(jnp.where(lane == 2, rt, 0.0), axis=1, keepdims=True)
    w2 = jnp.sum(jnp.where(lane == 3, rt, 0.0), axis=1, keepdims=True)
    moe = ya_ref[...] * w1 + yb_ref[...] * w2
    o_ref[0] = _layernorm(ALPHA * x1_ref[0] + (1.0 + g2_ref[0]) * moe, lng_ref[...], lnb_ref[...])


def _combine(x1, ya, yb, row_off, route, g2, p, per_token_mod):
    B, L, _ = x1.shape
    tm = _pick_tm(L, PROJ_TM)
    assert row_off % tm == 0
    row = lambda b, i: (b, i, 0)
    fixed = lambda b, i: (0, 0)
    flat = lambda b, i: (row_off // tm + b * (L // tm) + i, 0)
    if per_token_mod:
        mod_spec = pl.BlockSpec((1, tm, D_MODEL), row)
    else:
        mod_spec = pl.BlockSpec((1, 1, D_MODEL), lambda b, i: (b, 0, 0))
    return pl.pallas_call(
        _combine_kernel,
        grid=(B, L // tm),
        in_specs=[
            pl.BlockSpec((1, tm, D_MODEL), row),
            pl.BlockSpec((tm, D_MODEL), flat),
            pl.BlockSpec((tm, D_MODEL), flat),
            pl.BlockSpec((1, tm, LANES), row),
            mod_spec,
            pl.BlockSpec((1, D_MODEL), fixed),
            pl.BlockSpec((1, D_MODEL), fixed),
        ],
        out_specs=pl.BlockSpec((1, tm, D_MODEL), row),
        out_shape=jax.ShapeDtypeStruct((B, L, D_MODEL), F32),
        compiler_params=_cparams(("arbitrary", "arbitrary")),
        name="combine",
    )(x1, ya, yb, route, g2, p["ln2_g"], p["ln2_b"])


def _dispatch_plan(eidx):
    T = eidx.shape[0]
    A = 2 * T
    flat_e = eidx.reshape(A)
    onehot = (flat_e[:, None] == jnp.arange(N_EXPERTS, dtype=jnp.int32)[None, :]).astype(jnp.int32)
    cum = jnp.cumsum(onehot, axis=0)
    rank = jnp.take_along_axis(cum, flat_e[:, None], axis=1)[:, 0] - 1
    counts = cum[-1]
    padded = (counts + MOE_BM - 1) // MOE_BM * MOE_BM
    pad_end = jnp.cumsum(padded)
    pad_start = pad_end - padded
    dest = pad_start[flat_e] + rank
    nblk = (A + N_EXPERTS * (MOE_BM - 1) + MOE_BM - 1) // MOE_BM
    block_pos = jnp.arange(nblk, dtype=jnp.int32) * MOE_BM
    block_expert = jnp.minimum(jnp.sum(pad_end[None, :] <= block_pos[:, None], axis=1), N_EXPERTS - 1).astype(jnp.int32)
    n_used = (pad_end[-1] // MOE_BM).astype(jnp.int32).reshape(1)
    return dest, block_expert, n_used, nblk * MOE_BM


def _mods(mod_l, n_prompt, dec_seq):
    mp = mod_l[:n_prompt].reshape(n_prompt, 6, D_MODEL)
    ms = mod_l[n_prompt:].reshape(-1, 6, D_MODEL)
    prompt = [mp[:, j][:, None, :] for j in range(6)]
    sample = [jnp.repeat(ms[:, j], dec_seq, axis=0)[None] for j in range(6)]
    return prompt, sample


def kernel(x_prompt, x_sample, c_prompt, c_sample, state_ssd, state_ssd_conv, state_rglru, state_rglru_conv, state_ret, w_ada, b_ada, w_in, ssd_conv_w, ssd_conv_b, ssd_dt_bias, ssd_a_log, ssd_d, ssd_norm_w, rg_conv_w, rg_conv_b, rg_wa, rg_ba, rg_wx, rg_bx, rg_lambda, ret_gn_w, w_out, ln1_g, ln1_b, router_w, router_bias, exp_w_gate, exp_w_up, exp_w_down, ln2_g, ln2_b):
    BP, LP, _ = x_prompt.shape
    BS, LS, _ = x_sample.shape
    TP, TS = BP * LP, BS * LS

    def pad_lanes(v):
        return jnp.pad(v, ((0, 0), (0, LANES - v.shape[-1])))

    eye = jnp.eye(RG_BLOCKS, dtype=F32)

    def block_diag(w):
        return jnp.stack(_hi_lo(jnp.einsum("njk,nm->njmk", w, eye).reshape(D_RG, D_RG)))

    params = []
    for l in range(DEPTH):
        wi = w_in[l]
        wi = jnp.concatenate([wi[:, :2560], wi[:, 2576:5648], wi[:, 2560:2576],
                              jnp.zeros((D_MODEL, LANES - SSD_HEADS), F32)], axis=1)
        params.append(dict(
            w_in=_hi_lo(wi),
            ssd_conv_w=ssd_conv_w[l], ssd_conv_b=ssd_conv_b[l][None],
            ssd_dt_bias=pad_lanes(ssd_dt_bias[l][None]), ssd_a_log=pad_lanes(ssd_a_log[l][None]),
            ssd_d_ch=jnp.repeat(ssd_d[l], SSD_HEAD_DIM)[None], ssd_norm_w=ssd_norm_w[l][None],
            rg_conv_w=rg_conv_w[l], rg_conv_b=rg_conv_b[l][None],
            rg_wa_bd=block_diag(rg_wa[l]), rg_ba=rg_ba[l][None],
            rg_wx_bd=block_diag(rg_wx[l]), rg_bx=rg_bx[l][None],
            rg_lambda=rg_lambda[l][None], ret_gn_w=ret_gn_w[l][None],
            w_out=jnp.stack(_hi_lo(w_out[l])), ln1_g=ln1_g[l][None], ln1_b=ln1_b[l][None],
            ln2_g=ln2_g[l][None], ln2_b=ln2_b[l][None],
        ))

    rw_hi, rw_lo = _hi_lo(pad_lanes(router_w))
    rbias = pad_lanes(router_bias[None])
    head_of_ch = jnp.arange(D_SSD) // SSD_HEAD_DIM
    consts = dict(
        rw_hi=rw_hi, rw_lo=rw_lo, rbias=rbias.astype(F32),
        head_expand=(jnp.arange(LANES)[:, None] == head_of_ch[None, :]).astype(BF16),
        tri=jnp.tril(jnp.ones((CHUNK, CHUNK), F32)).astype(BF16),
    )
    mod = _ada(jnp.concatenate([c_prompt, c_sample], axis=0), w_ada, b_ada)

    LM = LP - TAIL
    paths = [
        dict(x=x_prompt[:, :LM], B=BP, L=LM, per_token=False, has_state=False, precise=False, rc=_ret_consts(LM, 0.0)),
        dict(x=x_prompt[:, LM:], B=BP, L=TAIL, per_token=False, has_state=True, precise=True, rc=_ret_consts(TAIL, float(LM))),
        dict(x=x_sample.reshape(1, TS, D_MODEL), B=BS, L=LS, per_token=True, has_state=True, precise=False,
             rc=_ret_consts(LS, PAST_LEN)),
    ]
    zero_states = (jnp.zeros((BP, SSD_GROUPS, SSD_STATE, D_SSD // SSD_GROUPS), F32),
                   jnp.zeros((BP, CONV_WIDTH - 1, SSD_CONV_DIM), F32),
                   jnp.zeros((BP, 1, D_RG), F32),
                   jnp.zeros((BP, CONV_WIDTH - 1, D_RG), F32),
                   jnp.zeros((BP, RET_HEADS, RET_HEAD_DIM, RET_HEAD_DIM), F32))

    def mixer_stack(path, p, mods, states, ssd_kw, ret_kw):
        B, L, per_token, precise = path["B"], path["L"], path["per_token"], path["precise"]
        w_in_l = p["w_in"] if precise else p["w_in"][:1]
        z, xbc, rg, ret, dt = _inproj(path["x"], mods[1], mods[0], w_in_l, per_token, precise)
        if per_token:
            z, xbc, rg, ret, dt = [a.reshape(B, L, a.shape[-1]) for a in (z, xbc, rg, ret, dt)]
        s_ssd, s_cbuf, s_rg, s_rbuf, s_ret = states
        hs = path["has_state"]
        y_ssd, cbuf_n, ssd_n = _ssd(z, xbc, dt, s_ssd, s_cbuf, p, consts, hs, precise, **ssd_kw)
        y_rg, rbuf_n, rg_n = _rg(rg, s_rg, s_rbuf, p, hs, precise)
        y_ret, ret_n = _ret(ret, s_ret, p, path["rc"], hs, precise, **ret_kw)
        if per_token:
            y_ssd, y_rg, y_ret = [a.reshape(1, B * L, a.shape[-1]) for a in (y_ssd, y_rg, y_ret)]
        return (y_ssd, y_rg, y_ret), (ssd_n, cbuf_n, rg_n, rbuf_n, ret_n)

    expert_w = (exp_w_gate, exp_w_up, exp_w_down)

    def experts(h2_list, route_list, layer, precise):
        h2_all = jnp.concatenate([h.reshape(-1, D_MODEL) for h in h2_list], axis=0)
        route_all = jnp.concatenate([r.reshape(-1, LANES) for r in route_list], axis=0)
        T_all = h2_all.shape[0]
        dest, block_expert, n_used, P = _dispatch_plan(route_all[:, 0:2].astype(jnp.int32))
        tok = jnp.arange(2 * T_all, dtype=jnp.int32) // 2
        if precise:
            slot_tok = jnp.full((P,), T_all, jnp.int32).at[dest].set(tok)
            xs_sorted = jnp.concatenate([h2_all, jnp.zeros((1, D_MODEL), h2_all.dtype)], axis=0)[slot_tok]
            ys_sorted = _moe(xs_sorted, block_expert, n_used, expert_w, layer, True)
        else:
            slot_tok = jnp.zeros((P,), jnp.int32).at[dest].set(tok)
            ys_sorted = _moe(h2_all, block_expert, n_used, expert_w, layer, False, slot_rows=slot_tok)
        return ys_sorted[dest[0::2]], ys_sorted[dest[1::2]]

    new_tail = []
    state_ssd_t = jnp.swapaxes(state_ssd, -1, -2)
    ssd_s = ret_s = None
    cbuf_s, rg_s, rbuf_s = [], [], []
    for l in range(DEPTH):
        p = params[l]
        mods_p, mods_s = _mods(mod[l], BP, LS)
        mods = [mods_p, mods_p, mods_s]
        st_s = (state_ssd_t, state_ssd_conv[l], state_rglru[l][:, None, :], state_rglru_conv[l], state_ret)

        ys_m, st_m = mixer_stack(paths[0], p, mods[0], zero_states, {}, {})
        ys_t, st_t = mixer_stack(paths[1], p, mods[1], st_m, dict(native_out=True), {})
        ys_c, st_c = mixer_stack(paths[2], p, mods[2], st_s,
                                 dict(h0_layer=l, native_out=True, out_stack=(l, ssd_s)),
                                 dict(h0_layer=l, out_stack=(l, ret_s)))
        new_tail.append(st_t)
        ssd_s, ret_s = st_c[0], st_c[4]
        cbuf_s.append(st_c[1])
        rg_s.append(st_c[2])
        rbuf_s.append(st_c[3])

        post = []
        for path, ys, m in zip(paths, (ys_m, ys_t, ys_c), mods):
            post.append(_outproj(*ys, path["x"], m[2], m[4], m[3], p, consts, path["per_token"], path["precise"]))

        ya_b, yb_b = experts([post[0][1], post[2][1]], [post[0][2], post[2][2]], l, False)
        ya_t, yb_t = experts([post[1][1]], [post[1][2]], l, True)

        offs = (0, 0, BP * LM)
        for path, (x1, _, route), ya, yb, off, m in zip(paths, post, (ya_b, ya_t, ya_b), (yb_b, yb_t, yb_b), offs, mods):
            path["x"] = _combine(x1, ya, yb, off, route, m[5], p, path["per_token"])

    def stack(lst, k):
        return jnp.stack([s[k] for s in lst])

    y_prompt = jnp.concatenate([paths[0]["x"], paths[1]["x"]], axis=1)
    return (y_prompt, paths[2]["x"].reshape(BS, LS, D_MODEL),
            jnp.swapaxes(stack(new_tail, 0), -1, -2), stack(new_tail, 1), stack(new_tail, 2)[:, :, 0],
            stack(new_tail, 3), stack(new_tail, 4),
            jnp.swapaxes(ssd_s, -1, -2), jnp.stack(cbuf_s), jnp.stack(rg_s)[:, :, 0], jnp.stack(rbuf_s), ret_s)
```

```python
import functools
import math

import jax
import jax.numpy as jnp
from jax import lax
from jax.experimental import pallas as pl
from jax.experimental.pallas import tpu as pltpu

F32 = jnp.float32
BF16 = jnp.bfloat16

D_MODEL = 1024
DEPTH = 4
D_SSD = 1024
SSD_HEAD_DIM = 64
SSD_HEADS = 16
SSD_GROUPS = 2
SSD_STATE = 128
SSD_CONV_DIM = D_SSD + 2 * SSD_GROUPS * SSD_STATE
CONV_WIDTH = 4
D_RG = 512
RG_BLOCKS = 8
RG_C = 8.0
D_RET = 512
RET_HEADS = 4
RET_HEAD_DIM = 128
ROPE_BASE = 10000.0
N_EXPERTS = 32
EXPERTS_PER_GROUP = 8
N_EXPERT_GROUPS = 4
D_EXPERT = 512
ALPHA = (2 * DEPTH) ** 0.25
EPS = 1e-5
PAST_LEN = 16384.0

LANES = 128
SUBLANES = 8
CHUNK = 128
TAIL = CHUNK
SEQS_PER_STEP = 8
PROJ_TM = 384
MOE_BM = 256
VMEM_LIMIT = 56 * 1024 * 1024


def _cparams(sem):
    return pltpu.CompilerParams(dimension_semantics=sem, vmem_limit_bytes=VMEM_LIMIT)


def _padded_rows(L):
    return CHUNK if L >= CHUNK else -(-L // SUBLANES) * SUBLANES


def _pick_tm(L, cap):
    if L <= LANES:
        return L
    tm = cap - cap % LANES
    while L % tm:
        tm -= LANES
    return tm


def _sigmoid(x):
    return 1.0 / (1.0 + jnp.exp(-x))


def _silu(x):
    return x * _sigmoid(x)


def _softplus(x):
    return jnp.maximum(x, 0.0) + jnp.log1p(jnp.exp(-jnp.abs(x)))


def _split3(v):
    hi = v.astype(BF16)
    r = v - hi.astype(F32)
    mid = r.astype(BF16)
    lo = (r - mid.astype(F32)).astype(BF16)
    return hi, mid, lo


def _dot(a, b):
    return jnp.dot(a, b, preferred_element_type=F32)


def _split2(v):
    hi = v.astype(BF16)
    return hi, (v - hi.astype(F32)).astype(BF16)


def _mm_w(a, w_hi, w_lo):
    if w_lo is None:
        return _dot(a.astype(BF16), w_hi)
    ah, al = _split2(a)
    return _dot(ah, w_hi) + (_dot(al, w_hi) + _dot(ah, w_lo))


def _mm_a(a, b, precise, nt=False):
    dn = (((1,), (1,)), ((), ())) if nt else (((1,), (0,)), ((), ()))

    def d(x, y):
        return lax.dot_general(x, y, dn, preferred_element_type=F32)

    if not precise:
        return d(a.astype(BF16), b.astype(BF16))
    ah, al = _split2(a)
    bh, bl = _split2(b)
    return d(ah, bh) + (d(al, bh) + d(ah, bl))


def _hi_lo(w):
    hi32 = lax.reduce_precision(w, exponent_bits=8, mantissa_bits=7)
    return hi32.astype(BF16), (w - hi32).astype(BF16)


def _dot_exact_rhs(v, m):
    hi, mid, lo = _split3(v)
    return _dot(hi, m) + _dot(mid, m) + _dot(lo, m)


def _dot_exact_lhs(m, v):
    hi, mid, lo = _split3(v)
    return _dot(m, hi) + _dot(m, mid) + _dot(m, lo)


def _ada_kernel(c_ref, w_ref, b_ref, o_ref):
    o_ref[0] = _mm_a(_silu(c_ref[...]), w_ref[0], True) + b_ref[0]


def _ada(c_all, w_ada, b_ada):
    n = c_all.shape[0]
    tn = 1024
    return pl.pallas_call(
        _ada_kernel,
        grid=(DEPTH, 6 * D_MODEL // tn),
        in_specs=[
            pl.BlockSpec((n, D_MODEL), lambda l, j: (0, 0)),
            pl.BlockSpec((1, D_MODEL, tn), lambda l, j: (l, 0, j)),
            pl.BlockSpec((1, 1, tn), lambda l, j: (l, 0, j)),
        ],
        out_specs=pl.BlockSpec((1, n, tn), lambda l, j: (l, 0, j)),
        out_shape=jax.ShapeDtypeStruct((DEPTH, n, 6 * D_MODEL), F32),
        compiler_params=_cparams(("arbitrary", "arbitrary")),
        name="ada",
    )(c_all, w_ada, b_ada.reshape(DEPTH, 1, 6 * D_MODEL))


_IN_SEGS = (("z", 0, 1024), ("xbc", 1024, 2560), ("rg", 2560, 3584), ("ret", 3584, 5632), ("dt", 5632, 5760))
IN_PAD = 5760


def _inproj_kernel(x_ref, sc_ref, sh_ref, w_ref, *rest, precise):
    wl_ref = rest[0] if precise else None
    outs = rest[1:] if precise else rest
    h = x_ref[0] * (1.0 + sc_ref[0]) + sh_ref[0]
    if precise:
        hh, hl = _split2(h)
    else:
        hh = h.astype(BF16)
    for o_ref, (_, lo, hi) in zip(outs, _IN_SEGS):
        acc = _dot(hh, w_ref[:, lo:hi])
        if precise:
            acc = acc + (_dot(hl, w_ref[:, lo:hi]) + _dot(hh, wl_ref[:, lo:hi]))
        o_ref[0] = acc


def _inproj(x, sc, sh, w, per_token_mod, precise):
    B, L, _ = x.shape
    tm = _pick_tm(L, LANES if precise else PROJ_TM)
    if per_token_mod:
        mod_spec = pl.BlockSpec((1, tm, D_MODEL), lambda b, i: (b, i, 0))
    else:
        mod_spec = pl.BlockSpec((1, 1, D_MODEL), lambda b, i: (b, 0, 0))
    widths = [hi - lo for _, lo, hi in _IN_SEGS]
    w_spec = pl.BlockSpec((D_MODEL, IN_PAD), lambda b, i: (0, 0), pipeline_mode=pl.Buffered(1))
    ws = w if precise else w[:1]
    return pl.pallas_call(
        functools.partial(_inproj_kernel, precise=precise),
        grid=(B, L // tm),
        in_specs=[
            pl.BlockSpec((1, tm, D_MODEL), lambda b, i: (b, i, 0)),
            mod_spec,
            mod_spec,
        ] + [w_spec] * len(ws),
        out_specs=[pl.BlockSpec((1, tm, wd), lambda b, i: (b, i, 0)) for wd in widths],
        out_shape=[jax.ShapeDtypeStruct((B, L, wd), F32) for wd in widths],
        compiler_params=_cparams(("arbitrary", "arbitrary")),
        name="inproj",
    )(x, sc, sh, *ws)


def _conv_step(src_rows, buf_ref, cw_ref, cb_ref, nbuf_ref, xp_s, *, c, nc, Lv, Lp, has_state):
    @pl.when(c == 0)
    def _():
        xp_s[...] = jnp.zeros_like(xp_s)
        if has_state:
            xp_s[SUBLANES - (CONV_WIDTH - 1):SUBLANES, :] = buf_ref[0]

    xp_s[SUBLANES:SUBLANES + Lv, :] = src_rows
    base = SUBLANES - (CONV_WIDTH - 1)
    xc = cb_ref[...]
    for j in range(CONV_WIDTH):
        xc = xc + cw_ref[j:j + 1, :] * xp_s[base + j:base + j + Lp, :]

    @pl.when(c == nc - 1)
    def _():
        nbuf_ref[0] = xp_s[SUBLANES + Lv - (CONV_WIDTH - 1):SUBLANES + Lv, :]

    if nc > 1:
        xp_s[0:SUBLANES, :] = xp_s[Lp:Lp + SUBLANES, :]
    return xc


def _for_each_sequence(body, refs, bb, keep, squeeze):
    def one(s):
        v = list(refs)
        for k in keep:
            v[k] = refs[k].at[pl.ds(s, 1)]
        for k in squeeze:
            v[k] = refs[k].at[s]
        body(*v)

    if bb == 1:
        one(0)
    else:
        def step(s, carry):
            one(s)
            return carry
        lax.fori_loop(0, bb, step, 0)


def _ssd_kernel(*refs, bb, **kw):
    n = len(refs)
    _for_each_sequence(functools.partial(_ssd_body, **kw), refs, bb,
                       keep=(0, 1, 2, 4, n - 6, n - 5), squeeze=(3, n - 4))


def _ssd_body(z_ref, xbc_ref, dt_ref, h0_ref, buf_ref, cw_ref, cb_ref, dtb_ref, alog_ref, dch_ref,
              nw_ref, e_ref, tri_ref, *rest, Lv, Lp, nc, has_state, precise, native_in, native_out):
    y_ref, nbuf_ref, hn_ref, xp_s, dtp_s, s_s = rest[-6:]
    c = pl.program_id(1)
    hpg = SSD_HEADS // SSD_GROUPS

    @pl.when(c == 0)
    def _():
        if not has_state:
            s_s[...] = jnp.zeros_like(s_s)
        elif native_in:
            for g in range(SSD_GROUPS):
                s_s[g] = jnp.concatenate([h0_ref[g * hpg + i] for i in range(hpg)], axis=0).T
        else:
            s_s[...] = h0_ref[...]
        if Lv < Lp:
            dtp_s[...] = jnp.zeros_like(dtp_s)

    xc = _conv_step(xbc_ref[0], buf_ref, cw_ref, cb_ref, nbuf_ref, xp_s,
                    c=c, nc=nc, Lv=Lv, Lp=Lp, has_state=has_state)
    xbc = _silu(xc)
    xs = xbc[:, 0:D_SSD]
    Bm = xbc[:, D_SSD:D_SSD + 256]
    Cm = xbc[:, D_SSD + 256:D_SSD + 512]

    if Lv < Lp:
        dtp_s[0:Lv, :] = dt_ref[0]
        dtr = dtp_s[...]
    else:
        dtr = dt_ref[0]
    dt = _softplus(dtr + dtb_ref[...])
    if Lv < Lp:
        rowv = lax.broadcasted_iota(jnp.int32, (Lp, LANES), 0)
        dt = jnp.where(rowv < Lv, dt, 0.0)
    a = dt * (-jnp.exp(alog_ref[...]))
    acum = _dot_exact_lhs(tri_ref[...], a)
    dt_e = _dot_exact_rhs(dt, e_ref[...])
    ac_e = _dot_exact_rhs(acum, e_ref[...])
    acl_e = ac_e[Lp - 1:Lp, :]
    acum_t = acum.T

    vdt = xs * dt_e
    lane = lax.broadcasted_iota(jnp.int32, (Lp, LANES), 1)
    causal = lax.broadcasted_iota(jnp.int32, (Lp, Lp), 0) >= lax.broadcasted_iota(jnp.int32, (Lp, Lp), 1)
    mid_t = F32 if precise else BF16

    y_blocks = []
    for g in range(SSD_GROUPS):
        Cg = Cm[:, g * SSD_STATE:(g + 1) * SSD_STATE]
        Bg = Bm[:, g * SSD_STATE:(g + 1) * SSD_STATE]
        G = _mm_a(Cg, Bg, precise, nt=True)
        for jj in range(4):
            j = 4 * g + jj
            scs = []
            for h in (2 * j, 2 * j + 1):
                col = jnp.sum(jnp.where(lane == h, acum, 0.0), axis=1, keepdims=True)
                rw = acum_t[h:h + 1, :]
                dec = jnp.exp(jnp.where(causal, col - rw, -jnp.inf))
                scs.append((G * dec).astype(mid_t))
            sc = jnp.concatenate(scs, axis=1)
            vb = vdt[:, j * LANES:(j + 1) * LANES]
            v2 = jnp.concatenate([jnp.where(lane < SSD_HEAD_DIM, vb, 0.0),
                                  jnp.where(lane >= SSD_HEAD_DIM, vb, 0.0)], axis=0).astype(mid_t)
            y_blocks.append(_mm_a(sc, v2, precise))
    y = jnp.concatenate(y_blocks, axis=1)

    eac = jnp.exp(ac_e)
    vw = vdt * jnp.exp(acl_e - ac_e)
    cd = jnp.exp(acl_e)
    half = D_SSD // SSD_GROUPS
    ys_parts = []
    for g in range(SSD_GROUPS):
        Sg = s_s[g]
        ys_parts.append(_mm_a(Cm[:, g * SSD_STATE:(g + 1) * SSD_STATE], Sg, precise))
        BgT = Bm[:, g * SSD_STATE:(g + 1) * SSD_STATE].T
        s_s[g] = cd[:, g * half:(g + 1) * half] * Sg + _mm_a(BgT, vw[:, g * half:(g + 1) * half], precise)
    y = y + eac * jnp.concatenate(ys_parts, axis=1) + dch_ref[...] * xs

    yv = y[0:Lv, :] if Lv < Lp else y
    yz = yv * _silu(z_ref[0])
    outs = []
    for g in range(SSD_GROUPS):
        part = yz[:, g * half:(g + 1) * half]
        ms = jnp.mean(part * part, axis=1, keepdims=True)
        outs.append(part * lax.rsqrt(ms + EPS))
    y_ref[0] = jnp.concatenate(outs, axis=1) * nw_ref[...]

    @pl.when(c == nc - 1)
    def _():
        if native_out:
            for g in range(SSD_GROUPS):
                sgt = s_s[g].T
                for i in range(hpg):
                    hn_ref[g * hpg + i] = sgt[i * SSD_HEAD_DIM:(i + 1) * SSD_HEAD_DIM, :]
        else:
            hn_ref[...] = s_s[...]


def _state_specs(B, bb, shape, in_layer, out_stack):
    zeros = (0,) * len(shape)
    if in_layer is None:
        in_spec = pl.BlockSpec((bb,) + shape, lambda b, c: (b,) + zeros)
    else:
        in_spec = pl.BlockSpec((None, bb) + shape, lambda b, c: (in_layer, b) + zeros)
    if out_stack is None:
        out_spec = pl.BlockSpec((bb,) + shape, lambda b, c: (b,) + zeros)
        out_shape = jax.ShapeDtypeStruct((B,) + shape, F32)
    else:
        l = out_stack[0]
        out_spec = pl.BlockSpec((None, bb) + shape, lambda b, c: (l, b) + zeros)
        out_shape = jax.ShapeDtypeStruct((DEPTH, B) + shape, F32)
    return in_spec, out_spec, out_shape


def _seqs_per_step(B, L):
    return SEQS_PER_STEP if (L < CHUNK and B % SEQS_PER_STEP == 0) else 1


def _ssd(z, xbc, dt, h0, buf, p, consts, has_state, precise, h0_layer=None, native_out=False, out_stack=None):
    B, L, _ = z.shape
    Lv = min(CHUNK, L)
    Lp = _padded_rows(L)
    nc = L // Lv
    native_in = h0_layer is not None
    bb = _seqs_per_step(B, L)
    kern = functools.partial(_ssd_kernel, bb=bb, Lv=Lv, Lp=Lp, nc=nc, has_state=has_state, precise=precise,
                             native_in=native_in, native_out=native_out)
    row = lambda b, c: (b, c, 0)
    per_b3 = lambda b, c: (b, 0, 0)
    fixed = lambda b, c: (0, 0)
    half = D_SSD // SSD_GROUPS
    native = (SSD_HEADS, SSD_HEAD_DIM, SSD_STATE)
    packed = (SSD_GROUPS, SSD_STATE, half)
    h0_spec, _, _ = _state_specs(B, bb, native if native_in else packed, h0_layer, None)
    _, hn_spec, hn_shape = _state_specs(B, bb, native if native_out else packed, None, out_stack)
    prev = out_stack[1] if out_stack is not None else None
    extra_in, extra_specs, aliases = [], [], {}
    if prev is not None:
        extra_in, extra_specs, aliases = [prev], [pl.BlockSpec(memory_space=pl.ANY)], {13: 2}
    return pl.pallas_call(
        kern,
        grid=(B // bb, nc),
        in_specs=[
            pl.BlockSpec((bb, Lv, D_SSD), row),
            pl.BlockSpec((bb, Lv, SSD_CONV_DIM), row),
            pl.BlockSpec((bb, Lv, LANES), row),
            h0_spec,
            pl.BlockSpec((bb, CONV_WIDTH - 1, SSD_CONV_DIM), per_b3),
            pl.BlockSpec((CONV_WIDTH, SSD_CONV_DIM), fixed),
            pl.BlockSpec((1, SSD_CONV_DIM), fixed),
            pl.BlockSpec((1, LANES), fixed),
            pl.BlockSpec((1, LANES), fixed),
            pl.BlockSpec((1, D_SSD), fixed),
            pl.BlockSpec((1, D_SSD), fixed),
            pl.BlockSpec((LANES, D_SSD), fixed),
            pl.BlockSpec((Lp, Lp), fixed),
        ] + extra_specs,
        out_specs=[
            pl.BlockSpec((bb, Lv, D_SSD), row),
            pl.BlockSpec((bb, CONV_WIDTH - 1, SSD_CONV_DIM), per_b3),
            hn_spec,
        ],
        out_shape=[
            jax.ShapeDtypeStruct((B, L, D_SSD), F32),
            jax.ShapeDtypeStruct((B, CONV_WIDTH - 1, SSD_CONV_DIM), F32),
            hn_shape,
        ],
        scratch_shapes=[
            pltpu.VMEM((SUBLANES + Lp, SSD_CONV_DIM), F32),
            pltpu.VMEM((Lp, LANES), F32),
            pltpu.VMEM((SSD_GROUPS, SSD_STATE, half), F32),
        ],
        input_output_aliases=aliases,
        compiler_params=_cparams(("arbitrary", "arbitrary")),
        name="ssd",
    )(z, xbc, dt, h0, buf, p["ssd_conv_w"], p["ssd_conv_b"], p["ssd_dt_bias"], p["ssd_a_log"],
      p["ssd_d_ch"], p["ssd_norm_w"], consts["head_expand"],
      jnp.tril(jnp.ones((Lp, Lp), F32)).astype(BF16), *extra_in)


def _rg_kernel(*refs, bb, **kw):
    n = len(refs)
    _for_each_sequence(functools.partial(_rg_body, **kw), refs, bb,
                       keep=(0, 1, 2, n - 5, n - 4, n - 3), squeeze=())


def _rg_body(rg_ref, h0_ref, buf_ref, cw_ref, cb_ref, wa_ref, ba_ref, wx_ref, bx_ref, lam_ref,
             y_ref, nbuf_ref, hn_ref, xp_s, hc_s, *, Lv, Lp, nc, has_state, precise):
    c = pl.program_id(1)

    @pl.when(c == 0)
    def _():
        if has_state:
            hc_s[...] = h0_ref[0]
        else:
            hc_s[...] = jnp.zeros_like(hc_s)

    xc = _conv_step(rg_ref[0, :, 0:D_RG], buf_ref, cw_ref, cb_ref, nbuf_ref, xp_s,
                    c=c, nc=nc, Lv=Lv, Lp=Lp, has_state=has_state)
    r = _sigmoid(_mm_w(xc, wa_ref[0], wa_ref[1] if precise else None) + ba_ref[...])
    i = _sigmoid(_mm_w(xc, wx_ref[0], wx_ref[1] if precise else None) + bx_ref[...])
    la = (-RG_C * _softplus(-lam_ref[...])) * r
    u = jnp.sqrt(-jnp.tanh(la) * (jnp.exp(2.0 * la) + 1.0)) * (i * xc)
    row = lax.broadcasted_iota(jnp.int32, (Lp, D_RG), 0)
    if Lv < Lp:
        valid = row < Lv
        la = jnp.where(valid, la, 0.0)
        u = jnp.where(valid, u, 0.0)
    a = jnp.exp(la)
    s = 1
    while s < Lp:
        m = row >= s
        u_sh = jnp.where(m, pltpu.roll(u, s, 0), 0.0)
        a_sh = jnp.where(m, pltpu.roll(a, s, 0), 1.0)
        u = u + a * u_sh
        a = a * a_sh
        s *= 2
    h = u + a * hc_s[...]
    hc_s[...] = h[Lp - 1:Lp, :]
    hv = h[0:Lv, :] if Lv < Lp else h
    y_ref[0] = hv * jax.nn.gelu(rg_ref[0, :, D_RG:2 * D_RG])

    @pl.when(c == nc - 1)
    def _():
        hn_ref[0] = hc_s[...]


def _rg(rg, h0, buf, p, has_state, precise):
    B, L, _ = rg.shape
    Lv = min(CHUNK, L)
    Lp = _padded_rows(L)
    nc = L // Lv
    bb = _seqs_per_step(B, L)
    kern = functools.partial(_rg_kernel, bb=bb, Lv=Lv, Lp=Lp, nc=nc, has_state=has_state, precise=precise)
    row = lambda b, c: (b, c, 0)
    per_b3 = lambda b, c: (b, 0, 0)
    fixed = lambda b, c: (0, 0)
    fixed3 = lambda b, c: (0, 0, 0)
    return pl.pallas_call(
        kern,
        grid=(B // bb, nc),
        in_specs=[
            pl.BlockSpec((bb, Lv, 2 * D_RG), row),
            pl.BlockSpec((bb, 1, D_RG), per_b3),
            pl.BlockSpec((bb, CONV_WIDTH - 1, D_RG), per_b3),
            pl.BlockSpec((CONV_WIDTH, D_RG), fixed),
            pl.BlockSpec((1, D_RG), fixed),
            pl.BlockSpec((2, D_RG, D_RG), fixed3),
            pl.BlockSpec((1, D_RG), fixed),
            pl.BlockSpec((2, D_RG, D_RG), fixed3),
            pl.BlockSpec((1, D_RG), fixed),
            pl.BlockSpec((1, D_RG), fixed),
        ],
        out_specs=[
            pl.BlockSpec((bb, Lv, D_RG), row),
            pl.BlockSpec((bb, CONV_WIDTH - 1, D_RG), per_b3),
            pl.BlockSpec((bb, 1, D_RG), per_b3),
        ],
        out_shape=[
            jax.ShapeDtypeStruct((B, L, D_RG), F32),
            jax.ShapeDtypeStruct((B, CONV_WIDTH - 1, D_RG), F32),
            jax.ShapeDtypeStruct((B, 1, D_RG), F32),
        ],
        scratch_shapes=[
            pltpu.VMEM((SUBLANES + Lp, D_RG), F32),
            pltpu.VMEM((1, D_RG), F32),
        ],
        compiler_params=_cparams(("arbitrary", "arbitrary")),
        name="rglru",
    )(rg, h0, buf, p["rg_conv_w"], p["rg_conv_b"], p["rg_wa_bd"], p["rg_ba"], p["rg_wx_bd"],
      p["rg_bx"], p["rg_lambda"])


def _ret_kernel(*refs, bb, **kw):
    n = len(refs)
    _for_each_sequence(functools.partial(_ret_body, **kw), refs, bb, keep=(0, n - 4), squeeze=(1, n - 3))


def _ret_body(ret_ref, h0_ref, cos_ref, sin_ref, dm_ref, ea_ref, te_ref, cd_ref, gn_ref,
              *rest, Lv, Lp, nc, has_state, precise):
    y_ref, hn_ref, pad_s, s_s = rest[-4:]
    c = pl.program_id(1)

    @pl.when(c == 0)
    def _():
        if has_state:
            s_s[...] = h0_ref[...]
        else:
            s_s[...] = jnp.zeros_like(s_s)
        if Lv < Lp:
            pad_s[...] = jnp.zeros_like(pad_s)

    if Lv < Lp:
        pad_s[0:Lv, :] = ret_ref[0, :, 0:3 * D_RET]
        qkv = pad_s[...]
    else:
        qkv = ret_ref[0, :, 0:3 * D_RET]
    cosf = cos_ref[...]
    sinf = sin_ref[...]
    outs = []
    for h in range(RET_HEADS):
        sl = slice(h * RET_HEAD_DIM, (h + 1) * RET_HEAD_DIM)
        q = qkv[:, sl]
        k = qkv[:, D_RET + h * RET_HEAD_DIM:D_RET + (h + 1) * RET_HEAD_DIM]
        v = qkv[:, 2 * D_RET + h * RET_HEAD_DIM:2 * D_RET + (h + 1) * RET_HEAD_DIM]
        q = q * cosf + pltpu.roll(q, RET_HEAD_DIM // 2, 1) * sinf
        k = (k * cosf + pltpu.roll(k, RET_HEAD_DIM // 2, 1) * sinf) * (RET_HEAD_DIM ** -0.5)
        G = _mm_a(q, k, precise, nt=True)
        S = s_s[h]
        y = _mm_a(G * dm_ref[h], v, precise) + _mm_a(q * ea_ref[h], S, precise)
        s_s[h] = cd_ref[h] * S + _mm_a((k * te_ref[h]).T, v, precise)
        yv = y[0:Lv, :] if Lv < Lp else y
        mu = jnp.mean(yv, axis=1, keepdims=True)
        d = yv - mu
        var = jnp.mean(d * d, axis=1, keepdims=True)
        outs.append(d * lax.rsqrt(var + EPS))
    yn = jnp.concatenate(outs, axis=1) * gn_ref[...]
    y_ref[0] = _silu(ret_ref[0, :, 3 * D_RET:4 * D_RET]) * yn

    @pl.when(c == nc - 1)
    def _():
        hn_ref[...] = s_s[...]


def _ret(ret, h0, p, rc, has_state, precise, h0_layer=None, out_stack=None):
    B, L, _ = ret.shape
    Lv = min(CHUNK, L)
    Lp = _padded_rows(L)
    nc = L // Lv
    bb = _seqs_per_step(B, L)
    kern = functools.partial(_ret_kernel, bb=bb, Lv=Lv, Lp=Lp, nc=nc, has_state=has_state, precise=precise)
    row = lambda b, c: (b, c, 0)
    fixed2 = lambda b, c: (0, 0)
    fixed3 = lambda b, c: (0, 0, 0)
    hd = RET_HEAD_DIM
    h0_spec, _, _ = _state_specs(B, bb, (RET_HEADS, hd, hd), h0_layer, None)
    _, hn_spec, hn_shape = _state_specs(B, bb, (RET_HEADS, hd, hd), None, out_stack)
    prev = out_stack[1] if out_stack is not None else None
    extra_in, extra_specs, aliases = [], [], {}
    if prev is not None:
        extra_in, extra_specs, aliases = [prev], [pl.BlockSpec(memory_space=pl.ANY)], {9: 1}
    return pl.pallas_call(
        kern,
        grid=(B // bb, nc),
        in_specs=[
            pl.BlockSpec((bb, Lv, 4 * D_RET), row),
            h0_spec,
            pl.BlockSpec((Lp, hd), lambda b, c: (c, 0)),
            pl.BlockSpec((Lp, hd), lambda b, c: (c, 0)),
            pl.BlockSpec((RET_HEADS, Lp, Lp), fixed3),
            pl.BlockSpec((RET_HEADS, Lp, hd), fixed3),
            pl.BlockSpec((RET_HEADS, Lp, hd), fixed3),
            pl.BlockSpec((RET_HEADS, hd, hd), fixed3),
            pl.BlockSpec((1, D_RET), fixed2),
        ] + extra_specs,
        out_specs=[
            pl.BlockSpec((bb, Lv, D_RET), row),
            hn_spec,
        ],
        out_shape=[
            jax.ShapeDtypeStruct((B, L, D_RET), F32),
            hn_shape,
        ],
        scratch_shapes=[
            pltpu.VMEM((Lp, 3 * D_RET), F32),
            pltpu.VMEM((RET_HEADS, hd, hd), F32),
        ],
        input_output_aliases=aliases,
        compiler_params=_cparams(("arbitrary", "arbitrary")),
        name="retention",
    )(ret, h0, rc["cos"], rc["sin"], rc["dmat"], rc["eacum"], rc["toend"], rc["cdec"], p["ret_gn_w"], *extra_in)


def _ret_consts(L, pos0):
    Lv = min(CHUNK, L)
    Lp = _padded_rows(L)
    nc = L // Lv
    half = RET_HEAD_DIM // 2
    inv = ROPE_BASE ** (-jnp.arange(half, dtype=F32) / half)
    pos = pos0 + jnp.arange(L, dtype=F32)
    ang = pos[:, None] * inv[None, :]
    cos = jnp.cos(ang)
    sin = jnp.sin(ang)
    cosf = jnp.concatenate([cos, cos], axis=1)
    sinf = jnp.concatenate([-sin, sin], axis=1)
    if Lv < Lp:
        cosf = jnp.pad(cosf, ((0, Lp - Lv), (0, 0)))
        sinf = jnp.pad(sinf, ((0, Lp - Lv), (0, 0)))
    assert cosf.shape[0] == nc * Lp
    log_gamma = jnp.log1p(-jnp.exp2(-5.0 - jnp.arange(RET_HEADS, dtype=F32)))
    steps = jnp.minimum(jnp.arange(Lp) + 1, Lv).astype(F32)
    acum = log_gamma[:, None] * steps[None, :]
    causal = jnp.tril(jnp.ones((Lp, Lp), bool))
    dmat = jnp.exp(jnp.where(causal[None], acum[:, :, None] - acum[:, None, :], -jnp.inf))
    ones = jnp.ones((RET_HEADS, Lp, RET_HEAD_DIM), F32)
    eacum = jnp.exp(acum)[:, :, None] * ones
    toend = jnp.exp(acum[:, -1:] - acum)[:, :, None] * ones
    rowvalid = (jnp.arange(Lp) < Lv).astype(F32)[None, :, None]
    toend = toend * rowvalid
    cdec = jnp.exp(acum[:, -1])[:, None, None] * jnp.ones((RET_HEADS, RET_HEAD_DIM, RET_HEAD_DIM), F32)
    return dict(cos=cosf, sin=sinf, dmat=dmat, eacum=eacum, toend=toend, cdec=cdec)


def _layernorm(v, g, b):
    mu = jnp.mean(v, axis=1, keepdims=True)
    d = v - mu
    var = jnp.mean(d * d, axis=1, keepdims=True)
    return d * lax.rsqrt(var + EPS) * g + b


def _route(logits, bias):
    tm = logits.shape[0]
    lane = lax.broadcasted_iota(jnp.int32, (tm, LANES), 1)
    lanef = lane.astype(F32)
    scores = _sigmoid(logits)
    choice = scores + bias
    neg = -jnp.inf
    best = jnp.full((tm, 1), neg, F32)
    e1 = jnp.zeros((tm, 1), F32)
    e2 = jnp.zeros((tm, 1), F32)
    for g in range(N_EXPERT_GROUPS):
        ing = (lane >= g * EXPERTS_PER_GROUP) & (lane < (g + 1) * EXPERTS_PER_GROUP)
        cg = jnp.where(ing, choice, neg)
        m1 = jnp.max(cg, axis=1, keepdims=True)
        i1 = jnp.min(jnp.where(cg == m1, lanef, float(LANES)), axis=1, keepdims=True)
        cg2 = jnp.where(lanef == i1, neg, cg)
        m2 = jnp.max(cg2, axis=1, keepdims=True)
        i2 = jnp.min(jnp.where(cg2 == m2, lanef, float(LANES)), axis=1, keepdims=True)
        gs = m1 + m2
        better = gs > best
        best = jnp.where(better, gs, best)
        e1 = jnp.where(better, i1, e1)
        e2 = jnp.where(better, i2, e2)
    w1 = jnp.sum(jnp.where(lanef == e1, scores, 0.0), axis=1, keepdims=True)
    w2 = jnp.sum(jnp.where(lanef == e2, scores, 0.0), axis=1, keepdims=True)
    den = w1 + w2
    out = jnp.where(lane == 0, e1, jnp.where(lane == 1, e2, jnp.where(lane == 2, w1 / den, jnp.where(lane == 3, w2 / den, 0.0))))
    return out


def _outproj_kernel(ys_ref, yr_ref, yt_ref, x_ref, g1_ref, sc2_ref, sh2_ref, w_ref, lng_ref, lnb_ref,
                    rwh_ref, rwl_ref, rb_ref, x1_ref, h2_ref, route_ref, *, precise):
    def part(y_ref, lo, hi):
        return _mm_w(y_ref[0], w_ref[0, lo:hi, :], w_ref[1, lo:hi, :] if precise else None)

    mix = part(ys_ref, 0, D_SSD) + part(yr_ref, D_SSD, D_SSD + D_RG) + part(yt_ref, D_SSD + D_RG, 2 * D_MODEL)
    x1 = _layernorm(ALPHA * x_ref[0] + (1.0 + g1_ref[0]) * mix, lng_ref[...], lnb_ref[...])
    x1_ref[0] = x1
    h2 = x1 * (1.0 + sc2_ref[0]) + sh2_ref[0]
    hi, lo = _split2(h2)
    h2_ref[0] = h2
    logits = _dot(hi, rwh_ref[...]) + (_dot(lo, rwh_ref[...]) + _dot(hi, rwl_ref[...]))
    route_ref[0] = _route(logits, rb_ref[...])


def _outproj(y_ssd, y_rg, y_ret, x, g1, sc2, sh2, p, consts, per_token_mod, precise):
    B, L, _ = x.shape
    tm = _pick_tm(L, LANES if precise else PROJ_TM)
    row = lambda b, i: (b, i, 0)
    fixed = lambda b, i: (0, 0)
    if per_token_mod:
        mod_spec = pl.BlockSpec((1, tm, D_MODEL), row)
    else:
        mod_spec = pl.BlockSpec((1, 1, D_MODEL), lambda b, i: (b, 0, 0))
    return pl.pallas_call(
        functools.partial(_outproj_kernel, precise=precise),
        grid=(B, L // tm),
        in_specs=[
            pl.BlockSpec((1, tm, D_SSD), row),
            pl.BlockSpec((1, tm, D_RG), row),
            pl.BlockSpec((1, tm, D_RET), row),
            pl.BlockSpec((1, tm, D_MODEL), row),
            mod_spec, mod_spec, mod_spec,
            pl.BlockSpec((2, 2 * D_MODEL, D_MODEL), lambda b, i: (0, 0, 0)),
            pl.BlockSpec((1, D_MODEL), fixed),
            pl.BlockSpec((1, D_MODEL), fixed),
            pl.BlockSpec((D_MODEL, LANES), fixed),
            pl.BlockSpec((D_MODEL, LANES), fixed),
            pl.BlockSpec((1, LANES), fixed),
        ],
        out_specs=[
            pl.BlockSpec((1, tm, D_MODEL), row),
            pl.BlockSpec((1, tm, D_MODEL), row),
            pl.BlockSpec((1, tm, LANES), row),
        ],
        out_shape=[
            jax.ShapeDtypeStruct((B, L, D_MODEL), F32),
            jax.ShapeDtypeStruct((B, L, D_MODEL), F32),
            jax.ShapeDtypeStruct((B, L, LANES), F32),
        ],
        compiler_params=_cparams(("arbitrary", "arbitrary")),
        name="outproj",
    )(y_ssd, y_rg, y_ret, x, g1, sc2, sh2, p["w_out"], p["ln1_g"], p["ln1_b"],
      consts["rw_hi"], consts["rw_lo"], consts["rbias"])


def _row_gather_copy(src_hbm, row, dst_vmem, slot, r, sem):
    return pltpu.make_async_copy(src_hbm.at[pl.ds(row, 1)], dst_vmem.at[slot, pl.ds(r, 1)], sem.at[slot])


def _moe_kernel(be_ref, nu_ref, *refs, precise, gather):
    if gather:
        rowc_ref, rown_ref, src_ref, wg_ref, wu_ref, wd_ref, o_ref = refs[:7]
        scr, xbuf, sem = refs[7:-2], refs[-2], refs[-1]
    else:
        x_ref, wg_ref, wu_ref, wd_ref, o_ref = refs[:5]
        scr = refs[5:]
    i = pl.program_id(0)
    n_used = nu_ref[0]
    used = i < n_used
    new_expert = jnp.logical_or(i == 0, be_ref[i] != be_ref[jnp.maximum(i - 1, 0)])

    if gather:
        slot = lax.rem(i, 2)

        def issue(rows_ref, s):
            def body(r, carry):
                _row_gather_copy(src_ref, rows_ref[0, 0, r], xbuf, s, r, sem).start()
                return carry
            lax.fori_loop(0, MOE_BM, body, 0, unroll=True)

        @pl.when(jnp.logical_and(used, i == 0))
        def _():
            issue(rowc_ref, 0)

        @pl.when(i + 1 < n_used)
        def _():
            issue(rown_ref, 1 - slot)

    @pl.when(jnp.logical_and(used, new_expert))
    def _():
        for k, w_ref in enumerate((wg_ref, wu_ref, wd_ref)):
            w = w_ref[...]
            if precise:
                hi, lo = _split2(w)
                scr[k][...] = hi
                scr[3 + k][...] = lo
            else:
                scr[k][...] = w.astype(BF16)

    @pl.when(used)
    def _():
        if gather:
            pltpu.make_async_copy(src_ref.at[pl.ds(0, MOE_BM)], xbuf.at[slot], sem.at[slot]).wait()
            x = xbuf[slot]
        else:
            x = x_ref[...]
        lo = (scr[3][...], scr[4][...], scr[5][...]) if precise else (None, None, None)
        g = _mm_w(x, scr[0][...], lo[0])
        u = _mm_w(x, scr[1][...], lo[1])
        o_ref[...] = _mm_w(_silu(g) * u, scr[2][...], lo[2])

    @pl.when(jnp.logical_not(used))
    def _():
        o_ref[...] = jnp.zeros_like(o_ref)


def _moe(xs, block_expert, n_used, weights, layer, precise, slot_rows=None):
    gather = slot_rows is not None
    P = slot_rows.shape[0] if gather else xs.shape[0]
    nblk = P // MOE_BM
    in_w = pl.BlockSpec((None, None, D_MODEL, D_EXPERT), lambda i, be, nu: (layer, be[i], 0, 0))
    out_w = pl.BlockSpec((None, None, D_EXPERT, D_MODEL), lambda i, be, nu: (layer, be[i], 0, 0))
    w_scratch = [pltpu.VMEM((D_MODEL, D_EXPERT), BF16), pltpu.VMEM((D_MODEL, D_EXPERT), BF16),
                 pltpu.VMEM((D_EXPERT, D_MODEL), BF16)] * (2 if precise else 1)
    if gather:
        rows3 = slot_rows.reshape(nblk, 1, MOE_BM)
        x_specs = [
            pl.BlockSpec((1, 1, MOE_BM), lambda i, be, nu: (i, 0, 0), memory_space=pltpu.SMEM),
            pl.BlockSpec((1, 1, MOE_BM), lambda i, be, nu: (jnp.minimum(i + 1, nblk - 1), 0, 0),
                         memory_space=pltpu.SMEM),
            pl.BlockSpec(memory_space=pl.ANY),
        ]
        x_args = [rows3, rows3, xs]
        extra_scratch = [pltpu.VMEM((2, MOE_BM, D_MODEL), xs.dtype), pltpu.SemaphoreType.DMA((2,))]
    else:
        x_specs = [pl.BlockSpec((MOE_BM, D_MODEL), lambda i, be, nu: (i, 0))]
        x_args = [xs]
        extra_scratch = []
    grid_spec = pltpu.PrefetchScalarGridSpec(
        num_scalar_prefetch=2,
        grid=(nblk,),
        in_specs=x_specs + [in_w, in_w, out_w],
        out_specs=pl.BlockSpec((MOE_BM, D_MODEL), lambda i, be, nu: (i, 0)),
        scratch_shapes=w_scratch + extra_scratch,
    )
    return pl.pallas_call(
        functools.partial(_moe_kernel, precise=precise, gather=gather),
        grid_spec=grid_spec,
        out_shape=jax.ShapeDtypeStruct((P, D_MODEL), F32),
        compiler_params=_cparams(("arbitrary",)),
        name="moe_gather" if gather else "moe",
    )(block_expert, n_used, *x_args, *weights)


def _combine_kernel(x1_ref, ya_ref, yb_ref, route_ref, g2_ref, lng_ref, lnb_ref, o_ref):
    tm = x1_ref.shape[1]
    lane = lax.broadcasted_iota(jnp.int32, (tm, LANES), 1)
    rt = route_ref[0]
    w1 = jnp.sum(jnp.where(lane == 2, rt, 0.0), axis=1, keepdims=True)
    w2 = jnp.sum(jnp.where(lane == 3, rt, 0.0), axis=1, keepdims=True)
    moe = ya_ref[...] * w1 + yb_ref[...] * w2
    o_ref[0] = _layernorm(ALPHA * x1_ref[0] + (1.0 + g2_ref[0]) * moe, lng_ref[...], lnb_ref[...])


def _combine(x1, ya, yb, row_off, route, g2, p, per_token_mod):
    B, L, _ = x1.shape
    tm = _pick_tm(L, PROJ_TM)
    assert row_off % tm == 0
    row = lambda b, i: (b, i, 0)
    fixed = lambda b, i: (0, 0)
    flat = lambda b, i: (row_off // tm + b * (L // tm) + i, 0)
    if per_token_mod:
        mod_spec = pl.BlockSpec((1, tm, D_MODEL), row)
    else:
        mod_spec = pl.BlockSpec((1, 1, D_MODEL), lambda b, i: (b, 0, 0))
    return pl.pallas_call(
        _combine_kernel,
        grid=(B, L // tm),
        in_specs=[
            pl.BlockSpec((1, tm, D_MODEL), row),
            pl.BlockSpec((tm, D_MODEL), flat),
            pl.BlockSpec((tm, D_MODEL), flat),
            pl.BlockSpec((1, tm, LANES), row),
            mod_spec,
            pl.BlockSpec((1, D_MODEL), fixed),
            pl.BlockSpec((1, D_MODEL), fixed),
        ],
        out_specs=pl.BlockSpec((1, tm, D_MODEL), row),
        out_shape=jax.ShapeDtypeStruct((B, L, D_MODEL), F32),
        compiler_params=_cparams(("arbitrary", "arbitrary")),
        name="combine",
    )(x1, ya, yb, route, g2, p["ln2_g"], p["ln2_b"])


def _dispatch_plan(eidx):
    T = eidx.shape[0]
    A = 2 * T
    flat_e = eidx.reshape(A)
    onehot = (flat_e[:, None] == jnp.arange(N_EXPERTS, dtype=jnp.int32)[None, :]).astype(jnp.int32)
    cum = jnp.cumsum(onehot, axis=0)
    rank = jnp.take_along_axis(cum, flat_e[:, None], axis=1)[:, 0] - 1
    counts = cum[-1]
    padded = (counts + MOE_BM - 1) // MOE_BM * MOE_BM
    pad_end = jnp.cumsum(padded)
    pad_start = pad_end - padded
    dest = pad_start[flat_e] + rank
    nblk = (A + N_EXPERTS * (MOE_BM - 1) + MOE_BM - 1) // MOE_BM
    block_pos = jnp.arange(nblk, dtype=jnp.int32) * MOE_BM
    block_expert = jnp.minimum(jnp.sum(pad_end[None, :] <= block_pos[:, None], axis=1), N_EXPERTS - 1).astype(jnp.int32)
    n_used = (pad_end[-1] // MOE_BM).astype(jnp.int32).reshape(1)
    return dest, block_expert, n_used, nblk * MOE_BM


def _mods(mod_l, n_prompt, dec_seq):
    mp = mod_l[:n_prompt].reshape(n_prompt, 6, D_MODEL)
    ms = mod_l[n_prompt:].reshape(-1, 6, D_MODEL)
    prompt = [mp[:, j][:, None, :] for j in range(6)]
    sample = [jnp.repeat(ms[:, j], dec_seq, axis=0)[None] for j in range(6)]
    return prompt, sample


def kernel(x_prompt, x_sample, c_prompt, c_sample, state_ssd, state_ssd_conv, state_rglru, state_rglru_conv, state_ret, w_ada, b_ada, w_in, ssd_conv_w, ssd_conv_b, ssd_dt_bias, ssd_a_log, ssd_d, ssd_norm_w, rg_conv_w, rg_conv_b, rg_wa, rg_ba, rg_wx, rg_bx, rg_lambda, ret_gn_w, w_out, ln1_g, ln1_b, router_w, router_bias, exp_w_gate, exp_w_up, exp_w_down, ln2_g, ln2_b):
    BP, LP, _ = x_prompt.shape
    BS, LS, _ = x_sample.shape
    TP, TS = BP * LP, BS * LS

    def pad_lanes(v):
        return jnp.pad(v, ((0, 0), (0, LANES - v.shape[-1])))

    eye = jnp.eye(RG_BLOCKS, dtype=F32)

    def block_diag(w):
        return jnp.stack(_hi_lo(jnp.einsum("njk,nm->njmk", w, eye).reshape(D_RG, D_RG)))

    params = []
    for l in range(DEPTH):
        wi = w_in[l]
        wi = jnp.concatenate([wi[:, :2560], wi[:, 2576:5648], wi[:, 2560:2576],
                              jnp.zeros((D_MODEL, LANES - SSD_HEADS), F32)], axis=1)
        params.append(dict(
            w_in=_hi_lo(wi),
            ssd_conv_w=ssd_conv_w[l], ssd_conv_b=ssd_conv_b[l][None],
            ssd_dt_bias=pad_lanes(ssd_dt_bias[l][None]), ssd_a_log=pad_lanes(ssd_a_log[l][None]),
            ssd_d_ch=jnp.repeat(ssd_d[l], SSD_HEAD_DIM)[None], ssd_norm_w=ssd_norm_w[l][None],
            rg_conv_w=rg_conv_w[l], rg_conv_b=rg_conv_b[l][None],
            rg_wa_bd=block_diag(rg_wa[l]), rg_ba=rg_ba[l][None],
            rg_wx_bd=block_diag(rg_wx[l]), rg_bx=rg_bx[l][None],
            rg_lambda=rg_lambda[l][None], ret_gn_w=ret_gn_w[l][None],
            w_out=jnp.stack(_hi_lo(w_out[l])), ln1_g=ln1_g[l][None], ln1_b=ln1_b[l][None],
            ln2_g=ln2_g[l][None], ln2_b=ln2_b[l][None],
        ))

    rw_hi, rw_lo = _hi_lo(pad_lanes(router_w))
    rbias = pad_lanes(router_bias[None])
    head_of_ch = jnp.arange(D_SSD) // SSD_HEAD_DIM
    consts = dict(
        rw_hi=rw_hi, rw_lo=rw_lo, rbias=rbias.astype(F32),
        head_expand=(jnp.arange(LANES)[:, None] == head_of_ch[None, :]).astype(BF16),
        tri=jnp.tril(jnp.ones((CHUNK, CHUNK), F32)).astype(BF16),
    )
    mod = _ada(jnp.concatenate([c_prompt, c_sample], axis=0), w_ada, b_ada)

    LM = LP - TAIL
    paths = [
        dict(x=x_prompt[:, :LM], B=BP, L=LM, per_token=False, has_state=False, precise=False, rc=_ret_consts(LM, 0.0)),
        dict(x=x_prompt[:, LM:], B=BP, L=TAIL, per_token=False, has_state=True, precise=True, rc=_ret_consts(TAIL, float(LM))),
        dict(x=x_sample.reshape(1, TS, D_MODEL), B=BS, L=LS, per_token=True, has_state=True, precise=False,
             rc=_ret_consts(LS, PAST_LEN)),
    ]
    zero_states = (jnp.zeros((BP, SSD_GROUPS, SSD_STATE, D_SSD // SSD_GROUPS), F32),
                   jnp.zeros((BP, CONV_WIDTH - 1, SSD_CONV_DIM), F32),
                   jnp.zeros((BP, 1, D_RG), F32),
                   jnp.zeros((BP, CONV_WIDTH - 1, D_RG), F32),
                   jnp.zeros((BP, RET_HEADS, RET_HEAD_DIM, RET_HEAD_DIM), F32))

    def mixer_stack(path, p, mods, states, ssd_kw, ret_kw):
        B, L, per_token, precise = path["B"], path["L"], path["per_token"], path["precise"]
        w_in_l = p["w_in"] if precise else p["w_in"][:1]
        z, xbc, rg, ret, dt = _inproj(path["x"], mods[1], mods[0], w_in_l, per_token, precise)
        if per_token:
            z, xbc, rg, ret, dt = [a.reshape(B, L, a.shape[-1]) for a in (z, xbc, rg, ret, dt)]
        s_ssd, s_cbuf, s_rg, s_rbuf, s_ret = states
        hs = path["has_state"]
        y_ssd, cbuf_n, ssd_n = _ssd(z, xbc, dt, s_ssd, s_cbuf, p, consts, hs, precise, **ssd_kw)
        y_rg, rbuf_n, rg_n = _rg(rg, s_rg, s_rbuf, p, hs, precise)
        y_ret, ret_n = _ret(ret, s_ret, p, path["rc"], hs, precise, **ret_kw)
        if per_token:
            y_ssd, y_rg, y_ret = [a.reshape(1, B * L, a.shape[-1]) for a in (y_ssd, y_rg, y_ret)]
        return (y_ssd, y_rg, y_ret), (ssd_n, cbuf_n, rg_n, rbuf_n, ret_n)

    expert_w = (exp_w_gate, exp_w_up, exp_w_down)

    def experts(h2_list, route_list, layer, precise):
        h2_all = jnp.concatenate([h.reshape(-1, D_MODEL) for h in h2_list], axis=0)
        route_all = jnp.concatenate([r.reshape(-1, LANES) for r in route_list], axis=0)
        T_all = h2_all.shape[0]
        dest, block_expert, n_used, P = _dispatch_plan(route_all[:, 0:2].astype(jnp.int32))
        tok = jnp.arange(2 * T_all, dtype=jnp.int32) // 2
        if precise:
            slot_tok = jnp.full((P,), T_all, jnp.int32).at[dest].set(tok)
            xs_sorted = jnp.concatenate([h2_all, jnp.zeros((1, D_MODEL), h2_all.dtype)], axis=0)[slot_tok]
            ys_sorted = _moe(xs_sorted, block_expert, n_used, expert_w, layer, True)
        else:
            slot_tok = jnp.zeros((P,), jnp.int32).at[dest].set(tok)
            ys_sorted = _moe(h2_all, block_expert, n_used, expert_w, layer, False, slot_rows=slot_tok)
        return ys_sorted[dest[0::2]], ys_sorted[dest[1::2]]

    new_tail = []
    state_ssd_t = jnp.swapaxes(state_ssd, -1, -2)
    ssd_s = ret_s = None
    cbuf_s, rg_s, rbuf_s = [], [], []
    for l in range(DEPTH):
        p = params[l]
        mods_p, mods_s = _mods(mod[l], BP, LS)
        mods = [mods_p, mods_p, mods_s]
        st_s = (state_ssd_t, state_ssd_conv[l], state_rglru[l][:, None, :], state_rglru_conv[l], state_ret)

        ys_m, st_m = mixer_stack(paths[0], p, mods[0], zero_states, {}, {})
        ys_t, st_t = mixer_stack(paths[1], p, mods[1], st_m, dict(native_out=True), {})
        ys_c, st_c = mixer_stack(paths[2], p, mods[2], st_s,
                                 dict(h0_layer=l, native_out=True, out_stack=(l, ssd_s)),
                                 dict(h0_layer=l, out_stack=(l, ret_s)))
        new_tail.append(st_t)
        ssd_s, ret_s = st_c[0], st_c[4]
        cbuf_s.append(st_c[1])
        rg_s.append(st_c[2])
        rbuf_s.append(st_c[3])

        post = []
        for path, ys, m in zip(paths, (ys_m, ys_t, ys_c), mods):
            post.append(_outproj(*ys, path["x"], m[2], m[4], m[3], p, consts, path["per_token"], path["precise"]))

        ya_b, yb_b = experts([post[0][1], post[2][1]], [post[0][2], post[2][2]], l, False)
        ya_t, yb_t = experts([post[1][1]], [post[1][2]], l, True)

        offs = (0, 0, BP * LM)
        for path, (x1, _, route), ya, yb, off, m in zip(paths, post, (ya_b, ya_t, ya_b), (yb_b, yb_t, yb_b), offs, mods):
            path["x"] = _combine(x1, ya, yb, off, route, m[5], p, path["per_token"])

    def stack(lst, k):
        return jnp.stack([s[k] for s in lst])

    y_prompt = jnp.concatenate([paths[0]["x"], paths[1]["x"]], axis=1)
    return (y_prompt, paths[2]["x"].reshape(BS, LS, D_MODEL),
            jnp.swapaxes(stack(new_tail, 0), -1, -2), stack(new_tail, 1), stack(new_tail, 2)[:, :, 0],
            stack(new_tail, 3), stack(new_tail, 4),
            jnp.swapaxes(ssd_s, -1, -2), jnp.stack(cbuf_s), jnp.stack(rg_s)[:, :, 0], jnp.stack(rbuf_s), ret_s)
```

```python
import functools
import math

import jax
import jax.numpy as jnp
from jax import lax
from jax.experimental import pallas as pl
from jax.experimental.pallas import tpu as pltpu

F32 = jnp.float32
BF16 = jnp.bfloat16

D_MODEL = 1024
DEPTH = 4
D_SSD = 1024
SSD_HEAD_DIM = 64
SSD_HEADS = 16
SSD_GROUPS = 2
SSD_STATE = 128
SSD_CONV_DIM = D_SSD + 2 * SSD_GROUPS * SSD_STATE
CONV_WIDTH = 4
D_RG = 512
RG_BLOCKS = 8
RG_C = 8.0
D_RET = 512
RET_HEADS = 4
RET_HEAD_DIM = 128
ROPE_BASE = 10000.0
N_EXPERTS = 32
EXPERTS_PER_GROUP = 8
N_EXPERT_GROUPS = 4
D_EXPERT = 512
ALPHA = (2 * DEPTH) ** 0.25
EPS = 1e-5
PAST_LEN = 16384.0

LANES = 128
SUBLANES = 8
CHUNK = 128
TAIL = CHUNK
SEQS_PER_STEP = 8
ROW_TILES = D_MODEL // LANES
PROJ_TM = 384
MOE_BM = 256
VMEM_LIMIT = 56 * 1024 * 1024


def _cparams(sem):
    return pltpu.CompilerParams(dimension_semantics=sem, vmem_limit_bytes=VMEM_LIMIT)


def _padded_rows(L):
    return CHUNK if L >= CHUNK else -(-L // SUBLANES) * SUBLANES


def _pick_tm(L, cap):
    if L <= LANES:
        return L
    tm = cap - cap % LANES
    while L % tm:
        tm -= LANES
    return tm


def _sigmoid(x):
    return 1.0 / (1.0 + jnp.exp(-x))


def _silu(x):
    return x * _sigmoid(x)


def _softplus(x):
    return jnp.maximum(x, 0.0) + jnp.log1p(jnp.exp(-jnp.abs(x)))


def _split3(v):
    hi = v.astype(BF16)
    r = v - hi.astype(F32)
    mid = r.astype(BF16)
    lo = (r - mid.astype(F32)).astype(BF16)
    return hi, mid, lo


def _dot(a, b):
    return jnp.dot(a, b, preferred_element_type=F32)


def _split2(v):
    hi = v.astype(BF16)
    return hi, (v - hi.astype(F32)).astype(BF16)


def _mm_w(a, w_hi, w_lo):
    if w_lo is None:
        return _dot(a.astype(BF16), w_hi)
    ah, al = _split2(a)
    return _dot(ah, w_hi) + (_dot(al, w_hi) + _dot(ah, w_lo))


def _mm_a(a, b, precise, nt=False):
    dn = (((1,), (1,)), ((), ())) if nt else (((1,), (0,)), ((), ()))

    def d(x, y):
        return lax.dot_general(x, y, dn, preferred_element_type=F32)

    if not precise:
        return d(a.astype(BF16), b.astype(BF16))
    ah, al = _split2(a)
    bh, bl = _split2(b)
    return d(ah, bh) + (d(al, bh) + d(ah, bl))


def _hi_lo(w):
    hi32 = lax.reduce_precision(w, exponent_bits=8, mantissa_bits=7)
    return hi32.astype(BF16), (w - hi32).astype(BF16)


def _dot_exact_rhs(v, m):
    hi, mid, lo = _split3(v)
    return _dot(hi, m) + _dot(mid, m) + _dot(lo, m)


def _dot_exact_lhs(m, v):
    hi, mid, lo = _split3(v)
    return _dot(m, hi) + _dot(m, mid) + _dot(m, lo)


def _ada_kernel(c_ref, w_ref, b_ref, o_ref):
    o_ref[0] = _mm_a(_silu(c_ref[...]), w_ref[0], True) + b_ref[0]


def _ada(c_all, w_ada, b_ada):
    n = c_all.shape[0]
    tn = 1024
    return pl.pallas_call(
        _ada_kernel,
        grid=(DEPTH, 6 * D_MODEL // tn),
        in_specs=[
            pl.BlockSpec((n, D_MODEL), lambda l, j: (0, 0)),
            pl.BlockSpec((1, D_MODEL, tn), lambda l, j: (l, 0, j)),
            pl.BlockSpec((1, 1, tn), lambda l, j: (l, 0, j)),
        ],
        out_specs=pl.BlockSpec((1, n, tn), lambda l, j: (l, 0, j)),
        out_shape=jax.ShapeDtypeStruct((DEPTH, n, 6 * D_MODEL), F32),
        compiler_params=_cparams(("arbitrary", "arbitrary")),
        name="ada",
    )(c_all, w_ada, b_ada.reshape(DEPTH, 1, 6 * D_MODEL))


_IN_SEGS = (("z", 0, 1024), ("xbc", 1024, 2560), ("rg", 2560, 3584), ("ret", 3584, 5632), ("dt", 5632, 5760))
IN_PAD = 5760


def _inproj_kernel(x_ref, sc_ref, sh_ref, w_ref, *rest, precise):
    wl_ref = rest[0] if precise else None
    outs = rest[1:] if precise else rest
    h = x_ref[0] * (1.0 + sc_ref[0]) + sh_ref[0]
    if precise:
        hh, hl = _split2(h)
    else:
        hh = h.astype(BF16)
    for o_ref, (_, lo, hi) in zip(outs, _IN_SEGS):
        acc = _dot(hh, w_ref[:, lo:hi])
        if precise:
            acc = acc + (_dot(hl, w_ref[:, lo:hi]) + _dot(hh, wl_ref[:, lo:hi]))
        o_ref[0] = acc


def _inproj(x, sc, sh, w, per_token_mod, precise):
    B, L, _ = x.shape
    tm = _pick_tm(L, LANES if precise else PROJ_TM)
    if per_token_mod:
        mod_spec = pl.BlockSpec((1, tm, D_MODEL), lambda b, i: (b, i, 0))
    else:
        mod_spec = pl.BlockSpec((1, 1, D_MODEL), lambda b, i: (b, 0, 0))
    widths = [hi - lo for _, lo, hi in _IN_SEGS]
    w_spec = pl.BlockSpec((D_MODEL, IN_PAD), lambda b, i: (0, 0), pipeline_mode=pl.Buffered(1))
    ws = w if precise else w[:1]
    return pl.pallas_call(
        functools.partial(_inproj_kernel, precise=precise),
        grid=(B, L // tm),
        in_specs=[
            pl.BlockSpec((1, tm, D_MODEL), lambda b, i: (b, i, 0)),
            mod_spec,
            mod_spec,
        ] + [w_spec] * len(ws),
        out_specs=[pl.BlockSpec((1, tm, wd), lambda b, i: (b, i, 0)) for wd in widths],
        out_shape=[jax.ShapeDtypeStruct((B, L, wd), F32) for wd in widths],
        compiler_params=_cparams(("arbitrary", "arbitrary")),
        name="inproj",
    )(x, sc, sh, *ws)


def _conv_step(src_rows, buf_ref, cw_ref, cb_ref, nbuf_ref, xp_s, *, c, nc, Lv, Lp, has_state):
    @pl.when(c == 0)
    def _():
        xp_s[...] = jnp.zeros_like(xp_s)
        if has_state:
            xp_s[SUBLANES - (CONV_WIDTH - 1):SUBLANES, :] = buf_ref[0]

    xp_s[SUBLANES:SUBLANES + Lv, :] = src_rows
    base = SUBLANES - (CONV_WIDTH - 1)
    xc = cb_ref[...]
    for j in range(CONV_WIDTH):
        xc = xc + cw_ref[j:j + 1, :] * xp_s[base + j:base + j + Lp, :]

    @pl.when(c == nc - 1)
    def _():
        nbuf_ref[0] = xp_s[SUBLANES + Lv - (CONV_WIDTH - 1):SUBLANES + Lv, :]

    if nc > 1:
        xp_s[0:SUBLANES, :] = xp_s[Lp:Lp + SUBLANES, :]
    return xc


def _for_each_sequence(body, refs, bb, keep, squeeze):
    def one(s):
        v = list(refs)
        for k in keep:
            v[k] = refs[k].at[pl.ds(s, 1)]
        for k in squeeze:
            v[k] = refs[k].at[s]
        body(*v)

    if bb == 1:
        one(0)
    else:
        def step(s, carry):
            one(s)
            return carry
        lax.fori_loop(0, bb, step, 0)


def _ssd_kernel(*refs, bb, **kw):
    n = len(refs)
    _for_each_sequence(functools.partial(_ssd_body, **kw), refs, bb,
                       keep=(0, 1, 2, 4, n - 6, n - 5), squeeze=(3, n - 4))


def _ssd_body(z_ref, xbc_ref, dt_ref, h0_ref, buf_ref, cw_ref, cb_ref, dtb_ref, alog_ref, dch_ref,
              nw_ref, e_ref, tri_ref, *rest, Lv, Lp, nc, has_state, precise, native_in, native_out):
    y_ref, nbuf_ref, hn_ref, xp_s, dtp_s, s_s = rest[-6:]
    c = pl.program_id(1)
    hpg = SSD_HEADS // SSD_GROUPS

    @pl.when(c == 0)
    def _():
        if not has_state:
            s_s[...] = jnp.zeros_like(s_s)
        elif native_in:
            for g in range(SSD_GROUPS):
                s_s[g] = jnp.concatenate([h0_ref[g * hpg + i] for i in range(hpg)], axis=0).T
        else:
            s_s[...] = h0_ref[...]
        if Lv < Lp:
            dtp_s[...] = jnp.zeros_like(dtp_s)

    xc = _conv_step(xbc_ref[0], buf_ref, cw_ref, cb_ref, nbuf_ref, xp_s,
                    c=c, nc=nc, Lv=Lv, Lp=Lp, has_state=has_state)
    xbc = _silu(xc)
    xs = xbc[:, 0:D_SSD]
    Bm = xbc[:, D_SSD:D_SSD + 256]
    Cm = xbc[:, D_SSD + 256:D_SSD + 512]

    if Lv < Lp:
        dtp_s[0:Lv, :] = dt_ref[0]
        dtr = dtp_s[...]
    else:
        dtr = dt_ref[0]
    dt = _softplus(dtr + dtb_ref[...])
    if Lv < Lp:
        rowv = lax.broadcasted_iota(jnp.int32, (Lp, LANES), 0)
        dt = jnp.where(rowv < Lv, dt, 0.0)
    a = dt * (-jnp.exp(alog_ref[...]))
    acum = _dot_exact_lhs(tri_ref[...], a)
    dt_e = _dot_exact_rhs(dt, e_ref[...])
    ac_e = _dot_exact_rhs(acum, e_ref[...])
    acl_e = ac_e[Lp - 1:Lp, :]
    acum_t = acum.T

    vdt = xs * dt_e
    lane = lax.broadcasted_iota(jnp.int32, (Lp, LANES), 1)
    causal = lax.broadcasted_iota(jnp.int32, (Lp, Lp), 0) >= lax.broadcasted_iota(jnp.int32, (Lp, Lp), 1)
    mid_t = F32 if precise else BF16

    y_blocks = []
    for g in range(SSD_GROUPS):
        Cg = Cm[:, g * SSD_STATE:(g + 1) * SSD_STATE]
        Bg = Bm[:, g * SSD_STATE:(g + 1) * SSD_STATE]
        G = _mm_a(Cg, Bg, precise, nt=True)
        for jj in range(4):
            j = 4 * g + jj
            scs = []
            for h in (2 * j, 2 * j + 1):
                col = jnp.sum(jnp.where(lane == h, acum, 0.0), axis=1, keepdims=True)
                rw = acum_t[h:h + 1, :]
                dec = jnp.exp(jnp.where(causal, col - rw, -jnp.inf))
                scs.append((G * dec).astype(mid_t))
            sc = jnp.concatenate(scs, axis=1)
            vb = vdt[:, j * LANES:(j + 1) * LANES]
            v2 = jnp.concatenate([jnp.where(lane < SSD_HEAD_DIM, vb, 0.0),
                                  jnp.where(lane >= SSD_HEAD_DIM, vb, 0.0)], axis=0).astype(mid_t)
            y_blocks.append(_mm_a(sc, v2, precise))
    y = jnp.concatenate(y_blocks, axis=1)

    eac = jnp.exp(ac_e)
    vw = vdt * jnp.exp(acl_e - ac_e)
    cd = jnp.exp(acl_e)
    half = D_SSD // SSD_GROUPS
    ys_parts = []
    for g in range(SSD_GROUPS):
        Sg = s_s[g]
        ys_parts.append(_mm_a(Cm[:, g * SSD_STATE:(g + 1) * SSD_STATE], Sg, precise))
        BgT = Bm[:, g * SSD_STATE:(g + 1) * SSD_STATE].T
        s_s[g] = cd[:, g * half:(g + 1) * half] * Sg + _mm_a(BgT, vw[:, g * half:(g + 1) * half], precise)
    y = y + eac * jnp.concatenate(ys_parts, axis=1) + dch_ref[...] * xs

    yv = y[0:Lv, :] if Lv < Lp else y
    yz = yv * _silu(z_ref[0])
    outs = []
    for g in range(SSD_GROUPS):
        part = yz[:, g * half:(g + 1) * half]
        ms = jnp.mean(part * part, axis=1, keepdims=True)
        outs.append(part * lax.rsqrt(ms + EPS))
    y_ref[0] = jnp.concatenate(outs, axis=1) * nw_ref[...]

    @pl.when(c == nc - 1)
    def _():
        if native_out:
            for g in range(SSD_GROUPS):
                sgt = s_s[g].T
                for i in range(hpg):
                    hn_ref[g * hpg + i] = sgt[i * SSD_HEAD_DIM:(i + 1) * SSD_HEAD_DIM, :]
        else:
            hn_ref[...] = s_s[...]


def _state_specs(B, bb, shape, in_layer, out_stack):
    zeros = (0,) * len(shape)
    if in_layer is None:
        in_spec = pl.BlockSpec((bb,) + shape, lambda b, c: (b,) + zeros)
    else:
        in_spec = pl.BlockSpec((None, bb) + shape, lambda b, c: (in_layer, b) + zeros)
    if out_stack is None:
        out_spec = pl.BlockSpec((bb,) + shape, lambda b, c: (b,) + zeros)
        out_shape = jax.ShapeDtypeStruct((B,) + shape, F32)
    else:
        l = out_stack[0]
        out_spec = pl.BlockSpec((None, bb) + shape, lambda b, c: (l, b) + zeros)
        out_shape = jax.ShapeDtypeStruct((DEPTH, B) + shape, F32)
    return in_spec, out_spec, out_shape


def _seqs_per_step(B, L):
    return SEQS_PER_STEP if (L < CHUNK and B % SEQS_PER_STEP == 0) else 1


def _ssd(z, xbc, dt, h0, buf, p, consts, has_state, precise, h0_layer=None, native_out=False, out_stack=None):
    B, L, _ = z.shape
    Lv = min(CHUNK, L)
    Lp = _padded_rows(L)
    nc = L // Lv
    native_in = h0_layer is not None
    bb = _seqs_per_step(B, L)
    kern = functools.partial(_ssd_kernel, bb=bb, Lv=Lv, Lp=Lp, nc=nc, has_state=has_state, precise=precise,
                             native_in=native_in, native_out=native_out)
    row = lambda b, c: (b, c, 0)
    per_b3 = lambda b, c: (b, 0, 0)
    fixed = lambda b, c: (0, 0)
    half = D_SSD // SSD_GROUPS
    native = (SSD_HEADS, SSD_HEAD_DIM, SSD_STATE)
    packed = (SSD_GROUPS, SSD_STATE, half)
    h0_spec, _, _ = _state_specs(B, bb, native if native_in else packed, h0_layer, None)
    _, hn_spec, hn_shape = _state_specs(B, bb, native if native_out else packed, None, out_stack)
    prev = out_stack[1] if out_stack is not None else None
    extra_in, extra_specs, aliases = [], [], {}
    if prev is not None:
        extra_in, extra_specs, aliases = [prev], [pl.BlockSpec(memory_space=pl.ANY)], {13: 2}
    return pl.pallas_call(
        kern,
        grid=(B // bb, nc),
        in_specs=[
            pl.BlockSpec((bb, Lv, D_SSD), row),
            pl.BlockSpec((bb, Lv, SSD_CONV_DIM), row),
            pl.BlockSpec((bb, Lv, LANES), row),
            h0_spec,
            pl.BlockSpec((bb, CONV_WIDTH - 1, SSD_CONV_DIM), per_b3),
            pl.BlockSpec((CONV_WIDTH, SSD_CONV_DIM), fixed),
            pl.BlockSpec((1, SSD_CONV_DIM), fixed),
            pl.BlockSpec((1, LANES), fixed),
            pl.BlockSpec((1, LANES), fixed),
            pl.BlockSpec((1, D_SSD), fixed),
            pl.BlockSpec((1, D_SSD), fixed),
            pl.BlockSpec((LANES, D_SSD), fixed),
            pl.BlockSpec((Lp, Lp), fixed),
        ] + extra_specs,
        out_specs=[
            pl.BlockSpec((bb, Lv, D_SSD), row),
            pl.BlockSpec((bb, CONV_WIDTH - 1, SSD_CONV_DIM), per_b3),
            hn_spec,
        ],
        out_shape=[
            jax.ShapeDtypeStruct((B, L, D_SSD), F32),
            jax.ShapeDtypeStruct((B, CONV_WIDTH - 1, SSD_CONV_DIM), F32),
            hn_shape,
        ],
        scratch_shapes=[
            pltpu.VMEM((SUBLANES + Lp, SSD_CONV_DIM), F32),
            pltpu.VMEM((Lp, LANES), F32),
            pltpu.VMEM((SSD_GROUPS, SSD_STATE, half), F32),
        ],
        input_output_aliases=aliases,
        compiler_params=_cparams(("arbitrary", "arbitrary")),
        name="ssd",
    )(z, xbc, dt, h0, buf, p["ssd_conv_w"], p["ssd_conv_b"], p["ssd_dt_bias"], p["ssd_a_log"],
      p["ssd_d_ch"], p["ssd_norm_w"], consts["head_expand"],
      jnp.tril(jnp.ones((Lp, Lp), F32)).astype(BF16), *extra_in)


def _rg_kernel(*refs, bb, **kw):
    n = len(refs)
    _for_each_sequence(functools.partial(_rg_body, **kw), refs, bb,
                       keep=(0, 1, 2, n - 5, n - 4, n - 3), squeeze=())


def _rg_body(rg_ref, h0_ref, buf_ref, cw_ref, cb_ref, wa_ref, ba_ref, wx_ref, bx_ref, lam_ref,
             y_ref, nbuf_ref, hn_ref, xp_s, hc_s, *, Lv, Lp, nc, has_state, precise):
    c = pl.program_id(1)

    @pl.when(c == 0)
    def _():
        if has_state:
            hc_s[...] = h0_ref[0]
        else:
            hc_s[...] = jnp.zeros_like(hc_s)

    xc = _conv_step(rg_ref[0, :, 0:D_RG], buf_ref, cw_ref, cb_ref, nbuf_ref, xp_s,
                    c=c, nc=nc, Lv=Lv, Lp=Lp, has_state=has_state)
    r = _sigmoid(_mm_w(xc, wa_ref[0], wa_ref[1] if precise else None) + ba_ref[...])
    i = _sigmoid(_mm_w(xc, wx_ref[0], wx_ref[1] if precise else None) + bx_ref[...])
    la = (-RG_C * _softplus(-lam_ref[...])) * r
    u = jnp.sqrt(-jnp.tanh(la) * (jnp.exp(2.0 * la) + 1.0)) * (i * xc)
    row = lax.broadcasted_iota(jnp.int32, (Lp, D_RG), 0)
    if Lv < Lp:
        valid = row < Lv
        la = jnp.where(valid, la, 0.0)
        u = jnp.where(valid, u, 0.0)
    a = jnp.exp(la)
    s = 1
    while s < Lp:
        m = row >= s
        u_sh = jnp.where(m, pltpu.roll(u, s, 0), 0.0)
        a_sh = jnp.where(m, pltpu.roll(a, s, 0), 1.0)
        u = u + a * u_sh
        a = a * a_sh
        s *= 2
    h = u + a * hc_s[...]
    hc_s[...] = h[Lp - 1:Lp, :]
    hv = h[0:Lv, :] if Lv < Lp else h
    y_ref[0] = hv * jax.nn.gelu(rg_ref[0, :, D_RG:2 * D_RG])

    @pl.when(c == nc - 1)
    def _():
        hn_ref[0] = hc_s[...]


def _rg(rg, h0, buf, p, has_state, precise):
    B, L, _ = rg.shape
    Lv = min(CHUNK, L)
    Lp = _padded_rows(L)
    nc = L // Lv
    bb = _seqs_per_step(B, L)
    kern = functools.partial(_rg_kernel, bb=bb, Lv=Lv, Lp=Lp, nc=nc, has_state=has_state, precise=precise)
    row = lambda b, c: (b, c, 0)
    per_b3 = lambda b, c: (b, 0, 0)
    fixed = lambda b, c: (0, 0)
    fixed3 = lambda b, c: (0, 0, 0)
    return pl.pallas_call(
        kern,
        grid=(B // bb, nc),
        in_specs=[
            pl.BlockSpec((bb, Lv, 2 * D_RG), row),
            pl.BlockSpec((bb, 1, D_RG), per_b3),
            pl.BlockSpec((bb, CONV_WIDTH - 1, D_RG), per_b3),
            pl.BlockSpec((CONV_WIDTH, D_RG), fixed),
            pl.BlockSpec((1, D_RG), fixed),
            pl.BlockSpec((2, D_RG, D_RG), fixed3),
            pl.BlockSpec((1, D_RG), fixed),
            pl.BlockSpec((2, D_RG, D_RG), fixed3),
            pl.BlockSpec((1, D_RG), fixed),
            pl.BlockSpec((1, D_RG), fixed),
        ],
        out_specs=[
            pl.BlockSpec((bb, Lv, D_RG), row),
            pl.BlockSpec((bb, CONV_WIDTH - 1, D_RG), per_b3),
            pl.BlockSpec((bb, 1, D_RG), per_b3),
        ],
        out_shape=[
            jax.ShapeDtypeStruct((B, L, D_RG), F32),
            jax.ShapeDtypeStruct((B, CONV_WIDTH - 1, D_RG), F32),
            jax.ShapeDtypeStruct((B, 1, D_RG), F32),
        ],
        scratch_shapes=[
            pltpu.VMEM((SUBLANES + Lp, D_RG), F32),
            pltpu.VMEM((1, D_RG), F32),
        ],
        compiler_params=_cparams(("arbitrary", "arbitrary")),
        name="rglru",
    )(rg, h0, buf, p["rg_conv_w"], p["rg_conv_b"], p["rg_wa_bd"], p["rg_ba"], p["rg_wx_bd"],
      p["rg_bx"], p["rg_lambda"])


def _ret_kernel(*refs, bb, **kw):
    n = len(refs)
    _for_each_sequence(functools.partial(_ret_body, **kw), refs, bb, keep=(0, n - 4), squeeze=(1, n - 3))


def _ret_body(ret_ref, h0_ref, cos_ref, sin_ref, dm_ref, ea_ref, te_ref, cd_ref, gn_ref,
              *rest, Lv, Lp, nc, has_state, precise):
    y_ref, hn_ref, pad_s, s_s = rest[-4:]
    c = pl.program_id(1)

    @pl.when(c == 0)
    def _():
        if has_state:
            s_s[...] = h0_ref[...]
        else:
            s_s[...] = jnp.zeros_like(s_s)
        if Lv < Lp:
            pad_s[...] = jnp.zeros_like(pad_s)

    if Lv < Lp:
        pad_s[0:Lv, :] = ret_ref[0, :, 0:3 * D_RET]
        qkv = pad_s[...]
    else:
        qkv = ret_ref[0, :, 0:3 * D_RET]
    cosf = cos_ref[...]
    sinf = sin_ref[...]
    outs = []
    for h in range(RET_HEADS):
        sl = slice(h * RET_HEAD_DIM, (h + 1) * RET_HEAD_DIM)
        q = qkv[:, sl]
        k = qkv[:, D_RET + h * RET_HEAD_DIM:D_RET + (h + 1) * RET_HEAD_DIM]
        v = qkv[:, 2 * D_RET + h * RET_HEAD_DIM:2 * D_RET + (h + 1) * RET_HEAD_DIM]
        q = q * cosf + pltpu.roll(q, RET_HEAD_DIM // 2, 1) * sinf
        k = (k * cosf + pltpu.roll(k, RET_HEAD_DIM // 2, 1) * sinf) * (RET_HEAD_DIM ** -0.5)
        G = _mm_a(q, k, precise, nt=True)
        S = s_s[h]
        y = _mm_a(G * dm_ref[h], v, precise) + _mm_a(q * ea_ref[h], S, precise)
        s_s[h] = cd_ref[h] * S + _mm_a((k * te_ref[h]).T, v, precise)
        yv = y[0:Lv, :] if Lv < Lp else y
        mu = jnp.mean(yv, axis=1, keepdims=True)
        d = yv - mu
        var = jnp.mean(d * d, axis=1, keepdims=True)
        outs.append(d * lax.rsqrt(var + EPS))
    yn = jnp.concatenate(outs, axis=1) * gn_ref[...]
    y_ref[0] = _silu(ret_ref[0, :, 3 * D_RET:4 * D_RET]) * yn

    @pl.when(c == nc - 1)
    def _():
        hn_ref[...] = s_s[...]


def _ret(ret, h0, p, rc, has_state, precise, h0_layer=None, out_stack=None):
    B, L, _ = ret.shape
    Lv = min(CHUNK, L)
    Lp = _padded_rows(L)
    nc = L // Lv
    bb = _seqs_per_step(B, L)
    kern = functools.partial(_ret_kernel, bb=bb, Lv=Lv, Lp=Lp, nc=nc, has_state=has_state, precise=precise)
    row = lambda b, c: (b, c, 0)
    fixed2 = lambda b, c: (0, 0)
    fixed3 = lambda b, c: (0, 0, 0)
    hd = RET_HEAD_DIM
    h0_spec, _, _ = _state_specs(B, bb, (RET_HEADS, hd, hd), h0_layer, None)
    _, hn_spec, hn_shape = _state_specs(B, bb, (RET_HEADS, hd, hd), None, out_stack)
    prev = out_stack[1] if out_stack is not None else None
    extra_in, extra_specs, aliases = [], [], {}
    if prev is not None:
        extra_in, extra_specs, aliases = [prev], [pl.BlockSpec(memory_space=pl.ANY)], {9: 1}
    return pl.pallas_call(
        kern,
        grid=(B // bb, nc),
        in_specs=[
            pl.BlockSpec((bb, Lv, 4 * D_RET), row),
            h0_spec,
            pl.BlockSpec((Lp, hd), lambda b, c: (c, 0)),
            pl.BlockSpec((Lp, hd), lambda b, c: (c, 0)),
            pl.BlockSpec((RET_HEADS, Lp, Lp), fixed3),
            pl.BlockSpec((RET_HEADS, Lp, hd), fixed3),
            pl.BlockSpec((RET_HEADS, Lp, hd), fixed3),
            pl.BlockSpec((RET_HEADS, hd, hd), fixed3),
            pl.BlockSpec((1, D_RET), fixed2),
        ] + extra_specs,
        out_specs=[
            pl.BlockSpec((bb, Lv, D_RET), row),
            hn_spec,
        ],
        out_shape=[
            jax.ShapeDtypeStruct((B, L, D_RET), F32),
            hn_shape,
        ],
        scratch_shapes=[
            pltpu.VMEM((Lp, 3 * D_RET), F32),
            pltpu.VMEM((RET_HEADS, hd, hd), F32),
        ],
        input_output_aliases=aliases,
        compiler_params=_cparams(("arbitrary", "arbitrary")),
        name="retention",
    )(ret, h0, rc["cos"], rc["sin"], rc["dmat"], rc["eacum"], rc["toend"], rc["cdec"], p["ret_gn_w"], *extra_in)


def _ret_consts(L, pos0):
    Lv = min(CHUNK, L)
    Lp = _padded_rows(L)
    nc = L // Lv
    half = RET_HEAD_DIM // 2
    inv = ROPE_BASE ** (-jnp.arange(half, dtype=F32) / half)
    pos = pos0 + jnp.arange(L, dtype=F32)
    ang = pos[:, None] * inv[None, :]
    cos = jnp.cos(ang)
    sin = jnp.sin(ang)
    cosf = jnp.concatenate([cos, cos], axis=1)
    sinf = jnp.concatenate([-sin, sin], axis=1)
    if Lv < Lp:
        cosf = jnp.pad(cosf, ((0, Lp - Lv), (0, 0)))
        sinf = jnp.pad(sinf, ((0, Lp - Lv), (0, 0)))
    assert cosf.shape[0] == nc * Lp
    log_gamma = jnp.log1p(-jnp.exp2(-5.0 - jnp.arange(RET_HEADS, dtype=F32)))
    steps = jnp.minimum(jnp.arange(Lp) + 1, Lv).astype(F32)
    acum = log_gamma[:, None] * steps[None, :]
    causal = jnp.tril(jnp.ones((Lp, Lp), bool))
    dmat = jnp.exp(jnp.where(causal[None], acum[:, :, None] - acum[:, None, :], -jnp.inf))
    ones = jnp.ones((RET_HEADS, Lp, RET_HEAD_DIM), F32)
    eacum = jnp.exp(acum)[:, :, None] * ones
    toend = jnp.exp(acum[:, -1:] - acum)[:, :, None] * ones
    rowvalid = (jnp.arange(Lp) < Lv).astype(F32)[None, :, None]
    toend = toend * rowvalid
    cdec = jnp.exp(acum[:, -1])[:, None, None] * jnp.ones((RET_HEADS, RET_HEAD_DIM, RET_HEAD_DIM), F32)
    return dict(cos=cosf, sin=sinf, dmat=dmat, eacum=eacum, toend=toend, cdec=cdec)


def _layernorm(v, g, b):
    mu = jnp.mean(v, axis=1, keepdims=True)
    d = v - mu
    var = jnp.mean(d * d, axis=1, keepdims=True)
    return d * lax.rsqrt(var + EPS) * g + b


def _route(logits, bias):
    tm = logits.shape[0]
    lane = lax.broadcasted_iota(jnp.int32, (tm, LANES), 1)
    lanef = lane.astype(F32)
    scores = _sigmoid(logits)
    choice = scores + bias
    neg = -jnp.inf
    best = jnp.full((tm, 1), neg, F32)
    e1 = jnp.zeros((tm, 1), F32)
    e2 = jnp.zeros((tm, 1), F32)
    for g in range(N_EXPERT_GROUPS):
        ing = (lane >= g * EXPERTS_PER_GROUP) & (lane < (g + 1) * EXPERTS_PER_GROUP)
        cg = jnp.where(ing, choice, neg)
        m1 = jnp.max(cg, axis=1, keepdims=True)
        i1 = jnp.min(jnp.where(cg == m1, lanef, float(LANES)), axis=1, keepdims=True)
        cg2 = jnp.where(lanef == i1, neg, cg)
        m2 = jnp.max(cg2, axis=1, keepdims=True)
        i2 = jnp.min(jnp.where(cg2 == m2, lanef, float(LANES)), axis=1, keepdims=True)
        gs = m1 + m2
        better = gs > best
        best = jnp.where(better, gs, best)
        e1 = jnp.where(better, i1, e1)
        e2 = jnp.where(better, i2, e2)
    w1 = jnp.sum(jnp.where(lanef == e1, scores, 0.0), axis=1, keepdims=True)
    w2 = jnp.sum(jnp.where(lanef == e2, scores, 0.0), axis=1, keepdims=True)
    den = w1 + w2
    out = jnp.where(lane == 0, e1, jnp.where(lane == 1, e2, jnp.where(lane == 2, w1 / den, jnp.where(lane == 3, w2 / den, 0.0))))
    return out


def _outproj_kernel(ys_ref, yr_ref, yt_ref, x_ref, g1_ref, sc2_ref, sh2_ref, w_ref, lng_ref, lnb_ref,
                    rwh_ref, rwl_ref, rb_ref, x1_ref, h2_ref, route_ref, *, precise, tile_rows):
    def part(y_ref, lo, hi):
        return _mm_w(y_ref[0], w_ref[0, lo:hi, :], w_ref[1, lo:hi, :] if precise else None)

    mix = part(ys_ref, 0, D_SSD) + part(yr_ref, D_SSD, D_SSD + D_RG) + part(yt_ref, D_SSD + D_RG, 2 * D_MODEL)
    x1 = _layernorm(ALPHA * x_ref[0] + (1.0 + g1_ref[0]) * mix, lng_ref[...], lnb_ref[...])
    x1_ref[0] = x1
    h2 = x1 * (1.0 + sc2_ref[0]) + sh2_ref[0]
    hi, lo = _split2(h2)
    if tile_rows:
        tm = h2.shape[0]
        for s in range(ROW_TILES):
            h2_ref[0, pl.ds(s, tm, stride=ROW_TILES), :] = h2[:, s * LANES:(s + 1) * LANES]
    else:
        h2_ref[0] = h2
    logits = _dot(hi, rwh_ref[...]) + (_dot(lo, rwh_ref[...]) + _dot(hi, rwl_ref[...]))
    route_ref[0] = _route(logits, rb_ref[...])


def _outproj(y_ssd, y_rg, y_ret, x, g1, sc2, sh2, p, consts, per_token_mod, precise):
    B, L, _ = x.shape
    tm = _pick_tm(L, LANES if precise else PROJ_TM)
    tile_rows = not precise
    row = lambda b, i: (b, i, 0)
    fixed = lambda b, i: (0, 0)
    if per_token_mod:
        mod_spec = pl.BlockSpec((1, tm, D_MODEL), row)
    else:
        mod_spec = pl.BlockSpec((1, 1, D_MODEL), lambda b, i: (b, 0, 0))
    if tile_rows:
        h2_spec = pl.BlockSpec((1, ROW_TILES * tm, LANES), row)
        h2_shape = jax.ShapeDtypeStruct((B, ROW_TILES * L, LANES), F32)
    else:
        h2_spec = pl.BlockSpec((1, tm, D_MODEL), row)
        h2_shape = jax.ShapeDtypeStruct((B, L, D_MODEL), F32)
    return pl.pallas_call(
        functools.partial(_outproj_kernel, precise=precise, tile_rows=tile_rows),
        grid=(B, L // tm),
        in_specs=[
            pl.BlockSpec((1, tm, D_SSD), row),
            pl.BlockSpec((1, tm, D_RG), row),
            pl.BlockSpec((1, tm, D_RET), row),
            pl.BlockSpec((1, tm, D_MODEL), row),
            mod_spec, mod_spec, mod_spec,
            pl.BlockSpec((2, 2 * D_MODEL, D_MODEL), lambda b, i: (0, 0, 0)),
            pl.BlockSpec((1, D_MODEL), fixed),
            pl.BlockSpec((1, D_MODEL), fixed),
            pl.BlockSpec((D_MODEL, LANES), fixed),
            pl.BlockSpec((D_MODEL, LANES), fixed),
            pl.BlockSpec((1, LANES), fixed),
        ],
        out_specs=[
            pl.BlockSpec((1, tm, D_MODEL), row),
            h2_spec,
            pl.BlockSpec((1, tm, LANES), row),
        ],
        out_shape=[
            jax.ShapeDtypeStruct((B, L, D_MODEL), F32),
            h2_shape,
            jax.ShapeDtypeStruct((B, L, LANES), F32),
        ],
        compiler_params=_cparams(("arbitrary", "arbitrary")),
        name="outproj",
    )(y_ssd, y_rg, y_ret, x, g1, sc2, sh2, p["w_out"], p["ln1_g"], p["ln1_b"],
      consts["rw_hi"], consts["rw_lo"], consts["rbias"])


def _row_gather_copy(src_hbm, first_row, dst_vmem, slot, r, sem):
    return pltpu.make_async_copy(src_hbm.at[pl.ds(pl.multiple_of(first_row, ROW_TILES), ROW_TILES)],
                                 dst_vmem.at[slot, pl.ds(r * ROW_TILES, ROW_TILES)], sem.at[slot])


def _moe_kernel(be_ref, nu_ref, *refs, precise, gather):
    if gather:
        rowc_ref, rown_ref, src_ref, wg_ref, wu_ref, wd_ref, o_ref = refs[:7]
        scr, xbuf, sem = refs[7:-2], refs[-2], refs[-1]
    else:
        x_ref, wg_ref, wu_ref, wd_ref, o_ref = refs[:5]
        scr = refs[5:]
    i = pl.program_id(0)
    n_used = nu_ref[0]
    used = i < n_used
    new_expert = jnp.logical_or(i == 0, be_ref[i] != be_ref[jnp.maximum(i - 1, 0)])

    if gather:
        slot = lax.rem(i, 2)

        def issue(rows_ref, s):
            def body(r, carry):
                _row_gather_copy(src_ref, rows_ref[0, 0, r], xbuf, s, r, sem).start()
                return carry
            lax.fori_loop(0, MOE_BM, body, 0, unroll=True)

        @pl.when(jnp.logical_and(used, i == 0))
        def _():
            issue(rowc_ref, 0)

        @pl.when(i + 1 < n_used)
        def _():
            issue(rown_ref, 1 - slot)

    @pl.when(jnp.logical_and(used, new_expert))
    def _():
        for k, w_ref in enumerate((wg_ref, wu_ref, wd_ref)):
            w = w_ref[...]
            if precise:
                hi, lo = _split2(w)
                scr[k][...] = hi
                scr[3 + k][...] = lo
            else:
                scr[k][...] = w.astype(BF16)

    @pl.when(used)
    def _():
        if gather:
            pltpu.make_async_copy(src_ref.at[pl.ds(0, MOE_BM * ROW_TILES)], xbuf.at[slot], sem.at[slot]).wait()
            xb = xbuf.at[slot]
            x = jnp.concatenate([xb[pl.ds(s, MOE_BM, stride=ROW_TILES), :] for s in range(ROW_TILES)], axis=1)
        else:
            x = x_ref[...]
        lo = (scr[3][...], scr[4][...], scr[5][...]) if precise else (None, None, None)
        g = _mm_w(x, scr[0][...], lo[0])
        u = _mm_w(x, scr[1][...], lo[1])
        o_ref[...] = _mm_w(_silu(g) * u, scr[2][...], lo[2])

    @pl.when(jnp.logical_not(used))
    def _():
        o_ref[...] = jnp.zeros_like(o_ref)


def _moe(xs, block_expert, n_used, weights, layer, precise, slot_rows=None):
    gather = slot_rows is not None
    P = slot_rows.shape[0] if gather else xs.shape[0]
    nblk = P // MOE_BM
    in_w = pl.BlockSpec((None, None, D_MODEL, D_EXPERT), lambda i, be, nu: (layer, be[i], 0, 0))
    out_w = pl.BlockSpec((None, None, D_EXPERT, D_MODEL), lambda i, be, nu: (layer, be[i], 0, 0))
    w_scratch = [pltpu.VMEM((D_MODEL, D_EXPERT), BF16), pltpu.VMEM((D_MODEL, D_EXPERT), BF16),
                 pltpu.VMEM((D_EXPERT, D_MODEL), BF16)] * (2 if precise else 1)
    if gather:
        rows3 = slot_rows.reshape(nblk, 1, MOE_BM)
        x_specs = [
            pl.BlockSpec((1, 1, MOE_BM), lambda i, be, nu: (i, 0, 0), memory_space=pltpu.SMEM),
            pl.BlockSpec((1, 1, MOE_BM), lambda i, be, nu: (jnp.minimum(i + 1, nblk - 1), 0, 0),
                         memory_space=pltpu.SMEM),
            pl.BlockSpec(memory_space=pl.ANY),
        ]
        x_args = [rows3, rows3, xs]
        extra_scratch = [pltpu.VMEM((2, MOE_BM * ROW_TILES, LANES), xs.dtype), pltpu.SemaphoreType.DMA((2,))]
    else:
        x_specs = [pl.BlockSpec((MOE_BM, D_MODEL), lambda i, be, nu: (i, 0))]
        x_args = [xs]
        extra_scratch = []
    grid_spec = pltpu.PrefetchScalarGridSpec(
        num_scalar_prefetch=2,
        grid=(nblk,),
        in_specs=x_specs + [in_w, in_w, out_w],
        out_specs=pl.BlockSpec((MOE_BM, D_MODEL), lambda i, be, nu: (i, 0)),
        scratch_shapes=w_scratch + extra_scratch,
    )
    return pl.pallas_call(
        functools.partial(_moe_kernel, precise=precise, gather=gather),
        grid_spec=grid_spec,
        out_shape=jax.ShapeDtypeStruct((P, D_MODEL), F32),
        compiler_params=_cparams(("arbitrary",)),
        name="moe_gather" if gather else "moe",
    )(block_expert, n_used, *x_args, *weights)


def _combine_kernel(x1_ref, ya_ref, yb_ref, route_ref, g2_ref, lng_ref, lnb_ref, o_ref):
    tm = x1_ref.shape[1]
    lane = lax.broadcasted_iota(jnp.int32, (tm, LANES), 1)
    rt = route_ref[0]
    w1 = jnp.sum(jnp.where(lane == 2, rt, 0.0), axis=1, keepdims=True)
    w2 = jnp.sum(jnp.where(lane == 3, rt, 0.0), axis=1, keepdims=True)
    moe = ya_ref[...] * w1 + yb_ref[...] * w2
    o_ref[0] = _layernorm(ALPHA * x1_ref[0] + (1.0 + g2_ref[0]) * moe, lng_ref[...], lnb_ref[...])


def _combine(x1, ya, yb, row_off, route, g2, p, per_token_mod):
    B, L, _ = x1.shape
    tm = _pick_tm(L, PROJ_TM)
    assert row_off % tm == 0
    row = lambda b, i: (b, i, 0)
    fixed = lambda b, i: (0, 0)
    flat = lambda b, i: (row_off // tm + b * (L // tm) + i, 0)
    if per_token_mod:
        mod_spec = pl.BlockSpec((1, tm, D_MODEL), row)
    else:
        mod_spec = pl.BlockSpec((1, 1, D_MODEL), lambda b, i: (b, 0, 0))
    return pl.pallas_call(
        _combine_kernel,
        grid=(B, L // tm),
        in_specs=[
            pl.BlockSpec((1, tm, D_MODEL), row),
            pl.BlockSpec((tm, D_MODEL), flat),
            pl.BlockSpec((tm, D_MODEL), flat),
            pl.BlockSpec((1, tm, LANES), row),
            mod_spec,
            pl.BlockSpec((1, D_MODEL), fixed),
            pl.BlockSpec((1, D_MODEL), fixed),
        ],
        out_specs=pl.BlockSpec((1, tm, D_MODEL), row),
        out_shape=jax.ShapeDtypeStruct((B, L, D_MODEL), F32),
        compiler_params=_cparams(("arbitrary", "arbitrary")),
        name="combine",
    )(x1, ya, yb, route, g2, p["ln2_g"], p["ln2_b"])


def _dispatch_plan(eidx):
    T = eidx.shape[0]
    A = 2 * T
    flat_e = eidx.reshape(A)
    onehot = (flat_e[:, None] == jnp.arange(N_EXPERTS, dtype=jnp.int32)[None, :]).astype(jnp.int32)
    cum = jnp.cumsum(onehot, axis=0)
    rank = jnp.take_along_axis(cum, flat_e[:, None], axis=1)[:, 0] - 1
    counts = cum[-1]
    padded = (counts + MOE_BM - 1) // MOE_BM * MOE_BM
    pad_end = jnp.cumsum(padded)
    pad_start = pad_end - padded
    dest = pad_start[flat_e] + rank
    nblk = (A + N_EXPERTS * (MOE_BM - 1) + MOE_BM - 1) // MOE_BM
    block_pos = jnp.arange(nblk, dtype=jnp.int32) * MOE_BM
    block_expert = jnp.minimum(jnp.sum(pad_end[None, :] <= block_pos[:, None], axis=1), N_EXPERTS - 1).astype(jnp.int32)
    n_used = (pad_end[-1] // MOE_BM).astype(jnp.int32).reshape(1)
    return dest, block_expert, n_used, nblk * MOE_BM


def _mods(mod_l, n_prompt, dec_seq):
    mp = mod_l[:n_prompt].reshape(n_prompt, 6, D_MODEL)
    ms = mod_l[n_prompt:].reshape(-1, 6, D_MODEL)
    prompt = [mp[:, j][:, None, :] for j in range(6)]
    sample = [jnp.repeat(ms[:, j], dec_seq, axis=0)[None] for j in range(6)]
    return prompt, sample


def kernel(x_prompt, x_sample, c_prompt, c_sample, state_ssd, state_ssd_conv, state_rglru, state_rglru_conv, state_ret, w_ada, b_ada, w_in, ssd_conv_w, ssd_conv_b, ssd_dt_bias, ssd_a_log, ssd_d, ssd_norm_w, rg_conv_w, rg_conv_b, rg_wa, rg_ba, rg_wx, rg_bx, rg_lambda, ret_gn_w, w_out, ln1_g, ln1_b, router_w, router_bias, exp_w_gate, exp_w_up, exp_w_down, ln2_g, ln2_b):
    BP, LP, _ = x_prompt.shape
    BS, LS, _ = x_sample.shape
    TP, TS = BP * LP, BS * LS

    def pad_lanes(v):
        return jnp.pad(v, ((0, 0), (0, LANES - v.shape[-1])))

    eye = jnp.eye(RG_BLOCKS, dtype=F32)

    def block_diag(w):
        return jnp.stack(_hi_lo(jnp.einsum("njk,nm->njmk", w, eye).reshape(D_RG, D_RG)))

    params = []
    for l in range(DEPTH):
        wi = w_in[l]
        wi = jnp.concatenate([wi[:, :2560], wi[:, 2576:5648], wi[:, 2560:2576],
                              jnp.zeros((D_MODEL, LANES - SSD_HEADS), F32)], axis=1)
        params.append(dict(
            w_in=_hi_lo(wi),
            ssd_conv_w=ssd_conv_w[l], ssd_conv_b=ssd_conv_b[l][None],
            ssd_dt_bias=pad_lanes(ssd_dt_bias[l][None]), ssd_a_log=pad_lanes(ssd_a_log[l][None]),
            ssd_d_ch=jnp.repeat(ssd_d[l], SSD_HEAD_DIM)[None], ssd_norm_w=ssd_norm_w[l][None],
            rg_conv_w=rg_conv_w[l], rg_conv_b=rg_conv_b[l][None],
            rg_wa_bd=block_diag(rg_wa[l]), rg_ba=rg_ba[l][None],
            rg_wx_bd=block_diag(rg_wx[l]), rg_bx=rg_bx[l][None],
            rg_lambda=rg_lambda[l][None], ret_gn_w=ret_gn_w[l][None],
            w_out=jnp.stack(_hi_lo(w_out[l])), ln1_g=ln1_g[l][None], ln1_b=ln1_b[l][None],
            ln2_g=ln2_g[l][None], ln2_b=ln2_b[l][None],
        ))

    rw_hi, rw_lo = _hi_lo(pad_lanes(router_w))
    rbias = pad_lanes(router_bias[None])
    head_of_ch = jnp.arange(D_SSD) // SSD_HEAD_DIM
    consts = dict(
        rw_hi=rw_hi, rw_lo=rw_lo, rbias=rbias.astype(F32),
        head_expand=(jnp.arange(LANES)[:, None] == head_of_ch[None, :]).astype(BF16),
        tri=jnp.tril(jnp.ones((CHUNK, CHUNK), F32)).astype(BF16),
    )
    mod = _ada(jnp.concatenate([c_prompt, c_sample], axis=0), w_ada, b_ada)

    LM = LP - TAIL
    paths = [
        dict(x=x_prompt[:, :LM], B=BP, L=LM, per_token=False, has_state=False, precise=False, rc=_ret_consts(LM, 0.0)),
        dict(x=x_prompt[:, LM:], B=BP, L=TAIL, per_token=False, has_state=True, precise=True, rc=_ret_consts(TAIL, float(LM))),
        dict(x=x_sample.reshape(1, TS, D_MODEL), B=BS, L=LS, per_token=True, has_state=True, precise=False,
             rc=_ret_consts(LS, PAST_LEN)),
    ]
    zero_states = (jnp.zeros((BP, SSD_GROUPS, SSD_STATE, D_SSD // SSD_GROUPS), F32),
                   jnp.zeros((BP, CONV_WIDTH - 1, SSD_CONV_DIM), F32),
                   jnp.zeros((BP, 1, D_RG), F32),
                   jnp.zeros((BP, CONV_WIDTH - 1, D_RG), F32),
                   jnp.zeros((BP, RET_HEADS, RET_HEAD_DIM, RET_HEAD_DIM), F32))

    def mixer_stack(path, p, mods, states, ssd_kw, ret_kw):
        B, L, per_token, precise = path["B"], path["L"], path["per_token"], path["precise"]
        w_in_l = p["w_in"] if precise else p["w_in"][:1]
        z, xbc, rg, ret, dt = _inproj(path["x"], mods[1], mods[0], w_in_l, per_token, precise)
        if per_token:
            z, xbc, rg, ret, dt = [a.reshape(B, L, a.shape[-1]) for a in (z, xbc, rg, ret, dt)]
        s_ssd, s_cbuf, s_rg, s_rbuf, s_ret = states
        hs = path["has_state"]
        y_ssd, cbuf_n, ssd_n = _ssd(z, xbc, dt, s_ssd, s_cbuf, p, consts, hs, precise, **ssd_kw)
        y_rg, rbuf_n, rg_n = _rg(rg, s_rg, s_rbuf, p, hs, precise)
        y_ret, ret_n = _ret(ret, s_ret, p, path["rc"], hs, precise, **ret_kw)
        if per_token:
            y_ssd, y_rg, y_ret = [a.reshape(1, B * L, a.shape[-1]) for a in (y_ssd, y_rg, y_ret)]
        return (y_ssd, y_rg, y_ret), (ssd_n, cbuf_n, rg_n, rbuf_n, ret_n)

    expert_w = (exp_w_gate, exp_w_up, exp_w_down)

    def experts(h2_list, route_list, layer, precise):
        last = D_MODEL if precise else LANES
        h2_all = jnp.concatenate([h.reshape(-1, last) for h in h2_list], axis=0)
        route_all = jnp.concatenate([r.reshape(-1, LANES) for r in route_list], axis=0)
        T_all = route_all.shape[0]
        dest, block_expert, n_used, P = _dispatch_plan(route_all[:, 0:2].astype(jnp.int32))
        tok = jnp.arange(2 * T_all, dtype=jnp.int32) // 2
        if precise:
            slot_tok = jnp.full((P,), T_all, jnp.int32).at[dest].set(tok)
            xs_sorted = jnp.concatenate([h2_all, jnp.zeros((1, D_MODEL), h2_all.dtype)], axis=0)[slot_tok]
            ys_sorted = _moe(xs_sorted, block_expert, n_used, expert_w, layer, True)
        else:
            slot_row = jnp.zeros((P,), jnp.int32).at[dest].set(tok * ROW_TILES)
            ys_sorted = _moe(h2_all, block_expert, n_used, expert_w, layer, False, slot_rows=slot_row)
        return ys_sorted[dest[0::2]], ys_sorted[dest[1::2]]

    new_tail = []
    state_ssd_t = jnp.swapaxes(state_ssd, -1, -2)
    ssd_s = ret_s = None
    cbuf_s, rg_s, rbuf_s = [], [], []
    for l in range(DEPTH):
        p = params[l]
        mods_p, mods_s = _mods(mod[l], BP, LS)
        mods = [mods_p, mods_p, mods_s]
        st_s = (state_ssd_t, state_ssd_conv[l], state_rglru[l][:, None, :], state_rglru_conv[l], state_ret)

        ys_m, st_m = mixer_stack(paths[0], p, mods[0], zero_states, {}, {})
        ys_t, st_t = mixer_stack(paths[1], p, mods[1], st_m, dict(native_out=True), {})
        ys_c, st_c = mixer_stack(paths[2], p, mods[2], st_s,
                                 dict(h0_layer=l, native_out=True, out_stack=(l, ssd_s)),
                                 dict(h0_layer=l, out_stack=(l, ret_s)))
        new_tail.append(st_t)
        ssd_s, ret_s = st_c[0], st_c[4]
        cbuf_s.append(st_c[1])
        rg_s.append(st_c[2])
        rbuf_s.append(st_c[3])

        post = []
        for path, ys, m in zip(paths, (ys_m, ys_t, ys_c), mods):
            post.append(_outproj(*ys, path["x"], m[2], m[4], m[3], p, consts, path["per_token"], path["precise"]))

        ya_b, yb_b = experts([post[0][1], post[2][1]], [post[0][2], post[2][2]], l, False)
        ya_t, yb_t = experts([post[1][1]], [post[1][2]], l, True)

        offs = (0, 0, BP * LM)
        for path, (x1, _, route), ya, yb, off, m in zip(paths, post, (ya_b, ya_t, ya_b), (yb_b, yb_t, yb_b), offs, mods):
            path["x"] = _combine(x1, ya, yb, off, route, m[5], p, path["per_token"])

    def stack(lst, k):
        return jnp.stack([s[k] for s in lst])

    y_prompt = jnp.concatenate([paths[0]["x"], paths[1]["x"]], axis=1)
    return (y_prompt, paths[2]["x"].reshape(BS, LS, D_MODEL),
            jnp.swapaxes(stack(new_tail, 0), -1, -2), stack(new_tail, 1), stack(new_tail, 2)[:, :, 0],
            stack(new_tail, 3), stack(new_tail, 4),
            jnp.swapaxes(ssd_s, -1, -2), jnp.stack(cbuf_s), jnp.stack(rg_s)[:, :, 0], jnp.stack(rbuf_s), ret_s)
```

```python
import functools
import math

import jax
import jax.numpy as jnp
from jax import lax
from jax.experimental import pallas as pl
from jax.experimental.pallas import tpu as pltpu

F32 = jnp.float32
BF16 = jnp.bfloat16

D_MODEL = 1024
DEPTH = 4
D_SSD = 1024
SSD_HEAD_DIM = 64
SSD_HEADS = 16
SSD_GROUPS = 2
SSD_STATE = 128
SSD_CONV_DIM = D_SSD + 2 * SSD_GROUPS * SSD_STATE
CONV_WIDTH = 4
D_RG = 512
RG_BLOCKS = 8
RG_C = 8.0
D_RET = 512
RET_HEADS = 4
RET_HEAD_DIM = 128
ROPE_BASE = 10000.0
N_EXPERTS = 32
EXPERTS_PER_GROUP = 8
N_EXPERT_GROUPS = 4
D_EXPERT = 512
ALPHA = (2 * DEPTH) ** 0.25
EPS = 1e-5
PAST_LEN = 16384.0

LANES = 128
SUBLANES = 8
CHUNK = 128
TAIL = CHUNK
SEQS_PER_STEP = 8
SEQ_UNROLL = 2
ROW_TILES = D_MODEL // LANES
PROJ_TM = 384
MOE_BM = 256
VMEM_LIMIT = 56 * 1024 * 1024


def _cparams(sem):
    return pltpu.CompilerParams(dimension_semantics=sem, vmem_limit_bytes=VMEM_LIMIT)


def _padded_rows(L):
    return CHUNK if L >= CHUNK else -(-L // SUBLANES) * SUBLANES


def _pick_tm(L, cap):
    if L <= LANES:
        return L
    tm = cap - cap % LANES
    while L % tm:
        tm -= LANES
    return tm


def _sigmoid(x):
    return 1.0 / (1.0 + jnp.exp(-x))


def _silu(x):
    return x * _sigmoid(x)


def _softplus(x):
    return jnp.maximum(x, 0.0) + jnp.log1p(jnp.exp(-jnp.abs(x)))


def _split3(v):
    hi = v.astype(BF16)
    r = v - hi.astype(F32)
    mid = r.astype(BF16)
    lo = (r - mid.astype(F32)).astype(BF16)
    return hi, mid, lo


def _dot(a, b):
    return jnp.dot(a, b, preferred_element_type=F32)


def _split2(v):
    hi = v.astype(BF16)
    return hi, (v - hi.astype(F32)).astype(BF16)


def _mm_w(a, w_hi, w_lo):
    if w_lo is None:
        return _dot(a.astype(BF16), w_hi)
    ah, al = _split2(a)
    return _dot(ah, w_hi) + (_dot(al, w_hi) + _dot(ah, w_lo))


def _mm_a(a, b, precise, nt=False):
    dn = (((1,), (1,)), ((), ())) if nt else (((1,), (0,)), ((), ()))

    def d(x, y):
        return lax.dot_general(x, y, dn, preferred_element_type=F32)

    if not precise:
        return d(a.astype(BF16), b.astype(BF16))
    ah, al = _split2(a)
    bh, bl = _split2(b)
    return d(ah, bh) + (d(al, bh) + d(ah, bl))


def _hi_lo(w):
    hi32 = lax.reduce_precision(w, exponent_bits=8, mantissa_bits=7)
    return hi32.astype(BF16), (w - hi32).astype(BF16)


def _dot_exact_rhs(v, m):
    hi, mid, lo = _split3(v)
    return _dot(hi, m) + _dot(mid, m) + _dot(lo, m)


def _dot_exact_lhs(m, v):
    hi, mid, lo = _split3(v)
    return _dot(m, hi) + _dot(m, mid) + _dot(m, lo)


def _ada_kernel(c_ref, w_ref, b_ref, o_ref):
    o_ref[0] = _mm_a(_silu(c_ref[...]), w_ref[0], True) + b_ref[0]


def _ada(c_all, w_ada, b_ada):
    n = c_all.shape[0]
    tn = 1024
    return pl.pallas_call(
        _ada_kernel,
        grid=(DEPTH, 6 * D_MODEL // tn),
        in_specs=[
            pl.BlockSpec((n, D_MODEL), lambda l, j: (0, 0)),
            pl.BlockSpec((1, D_MODEL, tn), lambda l, j: (l, 0, j)),
            pl.BlockSpec((1, 1, tn), lambda l, j: (l, 0, j)),
        ],
        out_specs=pl.BlockSpec((1, n, tn), lambda l, j: (l, 0, j)),
        out_shape=jax.ShapeDtypeStruct((DEPTH, n, 6 * D_MODEL), F32),
        compiler_params=_cparams(("arbitrary", "arbitrary")),
        name="ada",
    )(c_all, w_ada, b_ada.reshape(DEPTH, 1, 6 * D_MODEL))


_IN_SEGS = (("z", 0, 1024), ("xbc", 1024, 2560), ("rg", 2560, 3584), ("ret", 3584, 5632), ("dt", 5632, 5760))
IN_PAD = 5760


def _inproj_kernel(x_ref, sc_ref, sh_ref, w_ref, *rest, precise):
    wl_ref = rest[0] if precise else None
    outs = rest[1:] if precise else rest
    h = x_ref[0] * (1.0 + sc_ref[0]) + sh_ref[0]
    if precise:
        hh, hl = _split2(h)
    else:
        hh = h.astype(BF16)
    for o_ref, (_, lo, hi) in zip(outs, _IN_SEGS):
        acc = _dot(hh, w_ref[:, lo:hi])
        if precise:
            acc = acc + (_dot(hl, w_ref[:, lo:hi]) + _dot(hh, wl_ref[:, lo:hi]))
        o_ref[0] = acc


def _inproj(x, sc, sh, w, per_token_mod, precise):
    B, L, _ = x.shape
    tm = _pick_tm(L, LANES if precise else PROJ_TM)
    if per_token_mod:
        mod_spec = pl.BlockSpec((1, tm, D_MODEL), lambda b, i: (b, i, 0))
    else:
        mod_spec = pl.BlockSpec((1, 1, D_MODEL), lambda b, i: (b, 0, 0))
    widths = [hi - lo for _, lo, hi in _IN_SEGS]
    w_spec = pl.BlockSpec((D_MODEL, IN_PAD), lambda b, i: (0, 0), pipeline_mode=pl.Buffered(1))
    ws = w if precise else w[:1]
    return pl.pallas_call(
        functools.partial(_inproj_kernel, precise=precise),
        grid=(B, L // tm),
        in_specs=[
            pl.BlockSpec((1, tm, D_MODEL), lambda b, i: (b, i, 0)),
            mod_spec,
            mod_spec,
        ] + [w_spec] * len(ws),
        out_specs=[pl.BlockSpec((1, tm, wd), lambda b, i: (b, i, 0)) for wd in widths],
        out_shape=[jax.ShapeDtypeStruct((B, L, wd), F32) for wd in widths],
        compiler_params=_cparams(("arbitrary", "arbitrary")),
        name="inproj",
    )(x, sc, sh, *ws)


def _conv_step(src_rows, buf_ref, cw_ref, cb_ref, nbuf_ref, xp_s, *, c, nc, Lv, Lp, has_state):
    @_when_unless(nc == 1, c == 0)
    def _():
        xp_s[...] = jnp.zeros_like(xp_s)
        if has_state:
            xp_s[SUBLANES - (CONV_WIDTH - 1):SUBLANES, :] = buf_ref[0]

    xp_s[SUBLANES:SUBLANES + Lv, :] = src_rows
    base = SUBLANES - (CONV_WIDTH - 1)
    xc = cb_ref[...]
    for j in range(CONV_WIDTH):
        xc = xc + cw_ref[j:j + 1, :] * xp_s[base + j:base + j + Lp, :]

    @_when_unless(nc == 1, c == nc - 1)
    def _():
        nbuf_ref[0] = xp_s[SUBLANES + Lv - (CONV_WIDTH - 1):SUBLANES + Lv, :]

    if nc > 1:
        xp_s[0:SUBLANES, :] = xp_s[Lp:Lp + SUBLANES, :]
    return xc


def _for_each_sequence(body, refs, bb, keep, squeeze, n_scratch):
    n = len(refs)

    def one(s, j):
        v = list(refs)
        for k in keep:
            v[k] = refs[k].at[pl.ds(s, 1)]
        for k in squeeze:
            v[k] = refs[k].at[s]
        if bb > 1:
            for k in range(n - n_scratch, n):
                v[k] = refs[k].at[j]
        body(*v)

    if bb == 1:
        one(0, 0)
    else:
        def step(i, carry):
            for j in range(SEQ_UNROLL):
                one(i * SEQ_UNROLL + j, j)
            return carry
        lax.fori_loop(0, bb // SEQ_UNROLL, step, 0)


def _when_unless(always, cond):
    def deco(fn):
        if always:
            fn()
        else:
            pl.when(cond)(fn)
        return fn
    return deco


def _seq_scratch(shape, bb):
    return pltpu.VMEM(((SEQ_UNROLL,) + shape) if bb > 1 else shape, F32)


def _ssd_kernel(*refs, bb, **kw):
    n = len(refs)
    _for_each_sequence(functools.partial(_ssd_body, **kw), refs, bb,
                       keep=(0, 1, 2, 4, n - 6, n - 5), squeeze=(3, n - 4), n_scratch=3)


def _ssd_body(z_ref, xbc_ref, dt_ref, h0_ref, buf_ref, cw_ref, cb_ref, dtb_ref, alog_ref, dch_ref,
              nw_ref, e_ref, tri_ref, *rest, Lv, Lp, nc, has_state, precise, native_in, native_out):
    y_ref, nbuf_ref, hn_ref, xp_s, dtp_s, s_s = rest[-6:]
    c = pl.program_id(1)
    hpg = SSD_HEADS // SSD_GROUPS

    @_when_unless(nc == 1, c == 0)
    def _():
        if not has_state:
            s_s[...] = jnp.zeros_like(s_s)
        elif native_in:
            for g in range(SSD_GROUPS):
                s_s[g] = jnp.concatenate([h0_ref[g * hpg + i] for i in range(hpg)], axis=0).T
        else:
            s_s[...] = h0_ref[...]
        if Lv < Lp:
            dtp_s[...] = jnp.zeros_like(dtp_s)

    xc = _conv_step(xbc_ref[0], buf_ref, cw_ref, cb_ref, nbuf_ref, xp_s,
                    c=c, nc=nc, Lv=Lv, Lp=Lp, has_state=has_state)
    xbc = _silu(xc)
    xs = xbc[:, 0:D_SSD]
    Bm = xbc[:, D_SSD:D_SSD + 256]
    Cm = xbc[:, D_SSD + 256:D_SSD + 512]

    if Lv < Lp:
        dtp_s[0:Lv, :] = dt_ref[0]
        dtr = dtp_s[...]
    else:
        dtr = dt_ref[0]
    dt = _softplus(dtr + dtb_ref[...])
    if Lv < Lp:
        rowv = lax.broadcasted_iota(jnp.int32, (Lp, LANES), 0)
        dt = jnp.where(rowv < Lv, dt, 0.0)
    a = dt * (-jnp.exp(alog_ref[...]))
    acum = _dot_exact_lhs(tri_ref[...], a)
    dt_e = _dot_exact_rhs(dt, e_ref[...])
    ac_e = _dot_exact_rhs(acum, e_ref[...])
    acl_e = ac_e[Lp - 1:Lp, :]
    acum_t = acum.T

    vdt = xs * dt_e
    lane = lax.broadcasted_iota(jnp.int32, (Lp, LANES), 1)
    causal = lax.broadcasted_iota(jnp.int32, (Lp, Lp), 0) >= lax.broadcasted_iota(jnp.int32, (Lp, Lp), 1)
    mid_t = F32 if precise else BF16

    y_blocks = []
    for g in range(SSD_GROUPS):
        Cg = Cm[:, g * SSD_STATE:(g + 1) * SSD_STATE]
        Bg = Bm[:, g * SSD_STATE:(g + 1) * SSD_STATE]
        G = _mm_a(Cg, Bg, precise, nt=True)
        for jj in range(4):
            j = 4 * g + jj
            scs = []
            for h in (2 * j, 2 * j + 1):
                col = jnp.sum(jnp.where(lane == h, acum, 0.0), axis=1, keepdims=True)
                rw = acum_t[h:h + 1, :]
                dec = jnp.exp(jnp.where(causal, col - rw, -jnp.inf))
                scs.append((G * dec).astype(mid_t))
            sc = jnp.concatenate(scs, axis=1)
            vb = vdt[:, j * LANES:(j + 1) * LANES]
            v2 = jnp.concatenate([jnp.where(lane < SSD_HEAD_DIM, vb, 0.0),
                                  jnp.where(lane >= SSD_HEAD_DIM, vb, 0.0)], axis=0).astype(mid_t)
            y_blocks.append(_mm_a(sc, v2, precise))
    y = jnp.concatenate(y_blocks, axis=1)

    eac = jnp.exp(ac_e)
    vw = vdt * jnp.exp(acl_e - ac_e)
    cd = jnp.exp(acl_e)
    half = D_SSD // SSD_GROUPS
    ys_parts = []
    for g in range(SSD_GROUPS):
        Sg = s_s[g]
        ys_parts.append(_mm_a(Cm[:, g * SSD_STATE:(g + 1) * SSD_STATE], Sg, precise))
        BgT = Bm[:, g * SSD_STATE:(g + 1) * SSD_STATE].T
        s_s[g] = cd[:, g * half:(g + 1) * half] * Sg + _mm_a(BgT, vw[:, g * half:(g + 1) * half], precise)
    y = y + eac * jnp.concatenate(ys_parts, axis=1) + dch_ref[...] * xs

    yv = y[0:Lv, :] if Lv < Lp else y
    yz = yv * _silu(z_ref[0])
    outs = []
    for g in range(SSD_GROUPS):
        part = yz[:, g * half:(g + 1) * half]
        ms = jnp.mean(part * part, axis=1, keepdims=True)
        outs.append(part * lax.rsqrt(ms + EPS))
    y_ref[0] = jnp.concatenate(outs, axis=1) * nw_ref[...]

    @_when_unless(nc == 1, c == nc - 1)
    def _():
        if native_out:
            for g in range(SSD_GROUPS):
                sgt = s_s[g].T
                for i in range(hpg):
                    hn_ref[g * hpg + i] = sgt[i * SSD_HEAD_DIM:(i + 1) * SSD_HEAD_DIM, :]
        else:
            hn_ref[...] = s_s[...]


def _state_specs(B, bb, shape, in_layer, out_stack):
    zeros = (0,) * len(shape)
    if in_layer is None:
        in_spec = pl.BlockSpec((bb,) + shape, lambda b, c: (b,) + zeros)
    else:
        in_spec = pl.BlockSpec((None, bb) + shape, lambda b, c: (in_layer, b) + zeros)
    if out_stack is None:
        out_spec = pl.BlockSpec((bb,) + shape, lambda b, c: (b,) + zeros)
        out_shape = jax.ShapeDtypeStruct((B,) + shape, F32)
    else:
        l = out_stack[0]
        out_spec = pl.BlockSpec((None, bb) + shape, lambda b, c: (l, b) + zeros)
        out_shape = jax.ShapeDtypeStruct((DEPTH, B) + shape, F32)
    return in_spec, out_spec, out_shape


def _seqs_per_step(B, L):
    return SEQS_PER_STEP if (L < CHUNK and B % SEQS_PER_STEP == 0) else 1


def _ssd(z, xbc, dt, h0, buf, p, consts, has_state, precise, h0_layer=None, native_out=False, out_stack=None):
    B, L, _ = z.shape
    Lv = min(CHUNK, L)
    Lp = _padded_rows(L)
    nc = L // Lv
    native_in = h0_layer is not None
    bb = _seqs_per_step(B, L)
    kern = functools.partial(_ssd_kernel, bb=bb, Lv=Lv, Lp=Lp, nc=nc, has_state=has_state, precise=precise,
                             native_in=native_in, native_out=native_out)
    row = lambda b, c: (b, c, 0)
    per_b3 = lambda b, c: (b, 0, 0)
    fixed = lambda b, c: (0, 0)
    half = D_SSD // SSD_GROUPS
    native = (SSD_HEADS, SSD_HEAD_DIM, SSD_STATE)
    packed = (SSD_GROUPS, SSD_STATE, half)
    h0_spec, _, _ = _state_specs(B, bb, native if native_in else packed, h0_layer, None)
    _, hn_spec, hn_shape = _state_specs(B, bb, native if native_out else packed, None, out_stack)
    prev = out_stack[1] if out_stack is not None else None
    extra_in, extra_specs, aliases = [], [], {}
    if prev is not None:
        extra_in, extra_specs, aliases = [prev], [pl.BlockSpec(memory_space=pl.ANY)], {13: 2}
    return pl.pallas_call(
        kern,
        grid=(B // bb, nc),
        in_specs=[
            pl.BlockSpec((bb, Lv, D_SSD), row),
            pl.BlockSpec((bb, Lv, SSD_CONV_DIM), row),
            pl.BlockSpec((bb, Lv, LANES), row),
            h0_spec,
            pl.BlockSpec((bb, CONV_WIDTH - 1, SSD_CONV_DIM), per_b3),
            pl.BlockSpec((CONV_WIDTH, SSD_CONV_DIM), fixed),
            pl.BlockSpec((1, SSD_CONV_DIM), fixed),
            pl.BlockSpec((1, LANES), fixed),
            pl.BlockSpec((1, LANES), fixed),
            pl.BlockSpec((1, D_SSD), fixed),
            pl.BlockSpec((1, D_SSD), fixed),
            pl.BlockSpec((LANES, D_SSD), fixed),
            pl.BlockSpec((Lp, Lp), fixed),
        ] + extra_specs,
        out_specs=[
            pl.BlockSpec((bb, Lv, D_SSD), row),
            pl.BlockSpec((bb, CONV_WIDTH - 1, SSD_CONV_DIM), per_b3),
            hn_spec,
        ],
        out_shape=[
            jax.ShapeDtypeStruct((B, L, D_SSD), F32),
            jax.ShapeDtypeStruct((B, CONV_WIDTH - 1, SSD_CONV_DIM), F32),
            hn_shape,
        ],
        scratch_shapes=[
            _seq_scratch((SUBLANES + Lp, SSD_CONV_DIM), bb),
            _seq_scratch((Lp, LANES), bb),
            _seq_scratch((SSD_GROUPS, SSD_STATE, half), bb),
        ],
        input_output_aliases=aliases,
        compiler_params=_cparams(("arbitrary", "arbitrary")),
        name="ssd",
    )(z, xbc, dt, h0, buf, p["ssd_conv_w"], p["ssd_conv_b"], p["ssd_dt_bias"], p["ssd_a_log"],
      p["ssd_d_ch"], p["ssd_norm_w"], consts["head_expand"],
      jnp.tril(jnp.ones((Lp, Lp), F32)).astype(BF16), *extra_in)


def _rg_kernel(*refs, bb, **kw):
    n = len(refs)
    _for_each_sequence(functools.partial(_rg_body, **kw), refs, bb,
                       keep=(0, 1, 2, n - 5, n - 4, n - 3), squeeze=(), n_scratch=2)


def _rg_body(rg_ref, h0_ref, buf_ref, cw_ref, cb_ref, wa_ref, ba_ref, wx_ref, bx_ref, lam_ref,
             y_ref, nbuf_ref, hn_ref, xp_s, hc_s, *, Lv, Lp, nc, has_state, precise):
    c = pl.program_id(1)

    @_when_unless(nc == 1, c == 0)
    def _():
        if has_state:
            hc_s[...] = h0_ref[0]
        else:
            hc_s[...] = jnp.zeros_like(hc_s)

    xc = _conv_step(rg_ref[0, :, 0:D_RG], buf_ref, cw_ref, cb_ref, nbuf_ref, xp_s,
                    c=c, nc=nc, Lv=Lv, Lp=Lp, has_state=has_state)
    r = _sigmoid(_mm_w(xc, wa_ref[0], wa_ref[1] if precise else None) + ba_ref[...])
    i = _sigmoid(_mm_w(xc, wx_ref[0], wx_ref[1] if precise else None) + bx_ref[...])
    la = (-RG_C * _softplus(-lam_ref[...])) * r
    u = jnp.sqrt(-jnp.tanh(la) * (jnp.exp(2.0 * la) + 1.0)) * (i * xc)
    row = lax.broadcasted_iota(jnp.int32, (Lp, D_RG), 0)
    if Lv < Lp:
        valid = row < Lv
        la = jnp.where(valid, la, 0.0)
        u = jnp.where(valid, u, 0.0)
    a = jnp.exp(la)
    s = 1
    while s < Lp:
        m = row >= s
        u_sh = jnp.where(m, pltpu.roll(u, s, 0), 0.0)
        a_sh = jnp.where(m, pltpu.roll(a, s, 0), 1.0)
        u = u + a * u_sh
        a = a * a_sh
        s *= 2
    h = u + a * hc_s[...]
    hc_s[...] = h[Lp - 1:Lp, :]
    hv = h[0:Lv, :] if Lv < Lp else h
    y_ref[0] = hv * jax.nn.gelu(rg_ref[0, :, D_RG:2 * D_RG])

    @_when_unless(nc == 1, c == nc - 1)
    def _():
        hn_ref[0] = hc_s[...]


def _rg(rg, h0, buf, p, has_state, precise):
    B, L, _ = rg.shape
    Lv = min(CHUNK, L)
    Lp = _padded_rows(L)
    nc = L // Lv
    bb = _seqs_per_step(B, L)
    kern = functools.partial(_rg_kernel, bb=bb, Lv=Lv, Lp=Lp, nc=nc, has_state=has_state, precise=precise)
    row = lambda b, c: (b, c, 0)
    per_b3 = lambda b, c: (b, 0, 0)
    fixed = lambda b, c: (0, 0)
    fixed3 = lambda b, c: (0, 0, 0)
    return pl.pallas_call(
        kern,
        grid=(B // bb, nc),
        in_specs=[
            pl.BlockSpec((bb, Lv, 2 * D_RG), row),
            pl.BlockSpec((bb, 1, D_RG), per_b3),
            pl.BlockSpec((bb, CONV_WIDTH - 1, D_RG), per_b3),
            pl.BlockSpec((CONV_WIDTH, D_RG), fixed),
            pl.BlockSpec((1, D_RG), fixed),
            pl.BlockSpec((2, D_RG, D_RG), fixed3),
            pl.BlockSpec((1, D_RG), fixed),
            pl.BlockSpec((2, D_RG, D_RG), fixed3),
            pl.BlockSpec((1, D_RG), fixed),
            pl.BlockSpec((1, D_RG), fixed),
        ],
        out_specs=[
            pl.BlockSpec((bb, Lv, D_RG), row),
            pl.BlockSpec((bb, CONV_WIDTH - 1, D_RG), per_b3),
            pl.BlockSpec((bb, 1, D_RG), per_b3),
        ],
        out_shape=[
            jax.ShapeDtypeStruct((B, L, D_RG), F32),
            jax.ShapeDtypeStruct((B, CONV_WIDTH - 1, D_RG), F32),
            jax.ShapeDtypeStruct((B, 1, D_RG), F32),
        ],
        scratch_shapes=[
            _seq_scratch((SUBLANES + Lp, D_RG), bb),
            _seq_scratch((1, D_RG), bb),
        ],
        compiler_params=_cparams(("arbitrary", "arbitrary")),
        name="rglru",
    )(rg, h0, buf, p["rg_conv_w"], p["rg_conv_b"], p["rg_wa_bd"], p["rg_ba"], p["rg_wx_bd"],
      p["rg_bx"], p["rg_lambda"])


def _ret_kernel(*refs, bb, **kw):
    n = len(refs)
    _for_each_sequence(functools.partial(_ret_body, **kw), refs, bb, keep=(0, n - 4), squeeze=(1, n - 3),
                       n_scratch=2)


def _ret_body(ret_ref, h0_ref, cos_ref, sin_ref, dm_ref, ea_ref, te_ref, cd_ref, gn_ref,
              *rest, Lv, Lp, nc, has_state, precise):
    y_ref, hn_ref, pad_s, s_s = rest[-4:]
    c = pl.program_id(1)

    @_when_unless(nc == 1, c == 0)
    def _():
        if has_state:
            s_s[...] = h0_ref[...]
        else:
            s_s[...] = jnp.zeros_like(s_s)
        if Lv < Lp:
            pad_s[...] = jnp.zeros_like(pad_s)

    if Lv < Lp:
        pad_s[0:Lv, :] = ret_ref[0, :, 0:3 * D_RET]
        qkv = pad_s[...]
    else:
        qkv = ret_ref[0, :, 0:3 * D_RET]
    cosf = cos_ref[...]
    sinf = sin_ref[...]
    outs = []
    for h in range(RET_HEADS):
        sl = slice(h * RET_HEAD_DIM, (h + 1) * RET_HEAD_DIM)
        q = qkv[:, sl]
        k = qkv[:, D_RET + h * RET_HEAD_DIM:D_RET + (h + 1) * RET_HEAD_DIM]
        v = qkv[:, 2 * D_RET + h * RET_HEAD_DIM:2 * D_RET + (h + 1) * RET_HEAD_DIM]
        q = q * cosf + pltpu.roll(q, RET_HEAD_DIM // 2, 1) * sinf
        k = (k * cosf + pltpu.roll(k, RET_HEAD_DIM // 2, 1) * sinf) * (RET_HEAD_DIM ** -0.5)
        G = _mm_a(q, k, precise, nt=True)
        S = s_s[h]
        y = _mm_a(G * dm_ref[h], v, precise) + _mm_a(q * ea_ref[h], S, precise)
        s_s[h] = cd_ref[h] * S + _mm_a((k * te_ref[h]).T, v, precise)
        yv = y[0:Lv, :] if Lv < Lp else y
        mu = jnp.mean(yv, axis=1, keepdims=True)
        d = yv - mu
        var = jnp.mean(d * d, axis=1, keepdims=True)
        outs.append(d * lax.rsqrt(var + EPS))
    yn = jnp.concatenate(outs, axis=1) * gn_ref[...]
    y_ref[0] = _silu(ret_ref[0, :, 3 * D_RET:4 * D_RET]) * yn

    @_when_unless(nc == 1, c == nc - 1)
    def _():
        hn_ref[...] = s_s[...]


def _ret(ret, h0, p, rc, has_state, precise, h0_layer=None, out_stack=None):
    B, L, _ = ret.shape
    Lv = min(CHUNK, L)
    Lp = _padded_rows(L)
    nc = L // Lv
    bb = _seqs_per_step(B, L)
    kern = functools.partial(_ret_kernel, bb=bb, Lv=Lv, Lp=Lp, nc=nc, has_state=has_state, precise=precise)
    row = lambda b, c: (b, c, 0)
    fixed2 = lambda b, c: (0, 0)
    fixed3 = lambda b, c: (0, 0, 0)
    hd = RET_HEAD_DIM
    h0_spec, _, _ = _state_specs(B, bb, (RET_HEADS, hd, hd), h0_layer, None)
    _, hn_spec, hn_shape = _state_specs(B, bb, (RET_HEADS, hd, hd), None, out_stack)
    prev = out_stack[1] if out_stack is not None else None
    extra_in, extra_specs, aliases = [], [], {}
    if prev is not None:
        extra_in, extra_specs, aliases = [prev], [pl.BlockSpec(memory_space=pl.ANY)], {9: 1}
    return pl.pallas_call(
        kern,
        grid=(B // bb, nc),
        in_specs=[
            pl.BlockSpec((bb, Lv, 4 * D_RET), row),
            h0_spec,
            pl.BlockSpec((Lp, hd), lambda b, c: (c, 0)),
            pl.BlockSpec((Lp, hd), lambda b, c: (c, 0)),
            pl.BlockSpec((RET_HEADS, Lp, Lp), fixed3),
            pl.BlockSpec((RET_HEADS, Lp, hd), fixed3),
            pl.BlockSpec((RET_HEADS, Lp, hd), fixed3),
            pl.BlockSpec((RET_HEADS, hd, hd), fixed3),
            pl.BlockSpec((1, D_RET), fixed2),
        ] + extra_specs,
        out_specs=[
            pl.BlockSpec((bb, Lv, D_RET), row),
            hn_spec,
        ],
        out_shape=[
            jax.ShapeDtypeStruct((B, L, D_RET), F32),
            hn_shape,
        ],
        scratch_shapes=[
            _seq_scratch((Lp, 3 * D_RET), bb),
            _seq_scratch((RET_HEADS, hd, hd), bb),
        ],
        input_output_aliases=aliases,
        compiler_params=_cparams(("arbitrary", "arbitrary")),
        name="retention",
    )(ret, h0, rc["cos"], rc["sin"], rc["dmat"], rc["eacum"], rc["toend"], rc["cdec"], p["ret_gn_w"], *extra_in)


def _ret_consts(L, pos0):
    Lv = min(CHUNK, L)
    Lp = _padded_rows(L)
    nc = L // Lv
    half = RET_HEAD_DIM // 2
    inv = ROPE_BASE ** (-jnp.arange(half, dtype=F32) / half)
    pos = pos0 + jnp.arange(L, dtype=F32)
    ang = pos[:, None] * inv[None, :]
    cos = jnp.cos(ang)
    sin = jnp.sin(ang)
    cosf = jnp.concatenate([cos, cos], axis=1)
    sinf = jnp.concatenate([-sin, sin], axis=1)
    if Lv < Lp:
        cosf = jnp.pad(cosf, ((0, Lp - Lv), (0, 0)))
        sinf = jnp.pad(sinf, ((0, Lp - Lv), (0, 0)))
    assert cosf.shape[0] == nc * Lp
    log_gamma = jnp.log1p(-jnp.exp2(-5.0 - jnp.arange(RET_HEADS, dtype=F32)))
    steps = jnp.minimum(jnp.arange(Lp) + 1, Lv).astype(F32)
    acum = log_gamma[:, None] * steps[None, :]
    causal = jnp.tril(jnp.ones((Lp, Lp), bool))
    dmat = jnp.exp(jnp.where(causal[None], acum[:, :, None] - acum[:, None, :], -jnp.inf))
    ones = jnp.ones((RET_HEADS, Lp, RET_HEAD_DIM), F32)
    eacum = jnp.exp(acum)[:, :, None] * ones
    toend = jnp.exp(acum[:, -1:] - acum)[:, :, None] * ones
    rowvalid = (jnp.arange(Lp) < Lv).astype(F32)[None, :, None]
    toend = toend * rowvalid
    cdec = jnp.exp(acum[:, -1])[:, None, None] * jnp.ones((RET_HEADS, RET_HEAD_DIM, RET_HEAD_DIM), F32)
    return dict(cos=cosf, sin=sinf, dmat=dmat, eacum=eacum, toend=toend, cdec=cdec)


def _layernorm(v, g, b):
    mu = jnp.mean(v, axis=1, keepdims=True)
    d = v - mu
    var = jnp.mean(d * d, axis=1, keepdims=True)
    return d * lax.rsqrt(var + EPS) * g + b


def _route(logits, bias):
    tm = logits.shape[0]
    lane = lax.broadcasted_iota(jnp.int32, (tm, LANES), 1)
    lanef = lane.astype(F32)
    scores = _sigmoid(logits)
    choice = scores + bias
    neg = -jnp.inf
    best = jnp.full((tm, 1), neg, F32)
    e1 = jnp.zeros((tm, 1), F32)
    e2 = jnp.zeros((tm, 1), F32)
    for g in range(N_EXPERT_GROUPS):
        ing = (lane >= g * EXPERTS_PER_GROUP) & (lane < (g + 1) * EXPERTS_PER_GROUP)
        cg = jnp.where(ing, choice, neg)
        m1 = jnp.max(cg, axis=1, keepdims=True)
        i1 = jnp.min(jnp.where(cg == m1, lanef, float(LANES)), axis=1, keepdims=True)
        cg2 = jnp.where(lanef == i1, neg, cg)
        m2 = jnp.max(cg2, axis=1, keepdims=True)
        i2 = jnp.min(jnp.where(cg2 == m2, lanef, float(LANES)), axis=1, keepdims=True)
        gs = m1 + m2
        better = gs > best
        best = jnp.where(better, gs, best)
        e1 = jnp.where(better, i1, e1)
        e2 = jnp.where(better, i2, e2)
    w1 = jnp.sum(jnp.where(lanef == e1, scores, 0.0), axis=1, keepdims=True)
    w2 = jnp.sum(jnp.where(lanef == e2, scores, 0.0), axis=1, keepdims=True)
    den = w1 + w2
    out = jnp.where(lane == 0, e1, jnp.where(lane == 1, e2, jnp.where(lane == 2, w1 / den, jnp.where(lane == 3, w2 / den, 0.0))))
    return out


def _outproj_kernel(ys_ref, yr_ref, yt_ref, x_ref, g1_ref, sc2_ref, sh2_ref, w_ref, lng_ref, lnb_ref,
                    rwh_ref, rwl_ref, rb_ref, x1_ref, h2_ref, route_ref, *, precise, tile_rows):
    def part(y_ref, lo, hi):
        return _mm_w(y_ref[0], w_ref[0, lo:hi, :], w_ref[1, lo:hi, :] if precise else None)

    mix = part(ys_ref, 0, D_SSD) + part(yr_ref, D_SSD, D_SSD + D_RG) + part(yt_ref, D_SSD + D_RG, 2 * D_MODEL)
    x1 = _layernorm(ALPHA * x_ref[0] + (1.0 + g1_ref[0]) * mix, lng_ref[...], lnb_ref[...])
    x1_ref[0] = x1
    h2 = x1 * (1.0 + sc2_ref[0]) + sh2_ref[0]
    hi, lo = _split2(h2)
    if tile_rows:
        tm = h2.shape[0]
        for s in range(ROW_TILES):
            h2_ref[0, pl.ds(s, tm, stride=ROW_TILES), :] = h2[:, s * LANES:(s + 1) * LANES]
    else:
        h2_ref[0] = h2
    logits = _dot(hi, rwh_ref[...]) + (_dot(lo, rwh_ref[...]) + _dot(hi, rwl_ref[...]))
    route_ref[0] = _route(logits, rb_ref[...])


def _outproj(y_ssd, y_rg, y_ret, x, g1, sc2, sh2, p, consts, per_token_mod, precise):
    B, L, _ = x.shape
    tm = _pick_tm(L, LANES if precise else PROJ_TM)
    tile_rows = not precise
    row = lambda b, i: (b, i, 0)
    fixed = lambda b, i: (0, 0)
    if per_token_mod:
        mod_spec = pl.BlockSpec((1, tm, D_MODEL), row)
    else:
        mod_spec = pl.BlockSpec((1, 1, D_MODEL), lambda b, i: (b, 0, 0))
    if tile_rows:
        h2_spec = pl.BlockSpec((1, ROW_TILES * tm, LANES), row)
        h2_shape = jax.ShapeDtypeStruct((B, ROW_TILES * L, LANES), F32)
    else:
        h2_spec = pl.BlockSpec((1, tm, D_MODEL), row)
        h2_shape = jax.ShapeDtypeStruct((B, L, D_MODEL), F32)
    return pl.pallas_call(
        functools.partial(_outproj_kernel, precise=precise, tile_rows=tile_rows),
        grid=(B, L // tm),
        in_specs=[
            pl.BlockSpec((1, tm, D_SSD), row),
            pl.BlockSpec((1, tm, D_RG), row),
            pl.BlockSpec((1, tm, D_RET), row),
            pl.BlockSpec((1, tm, D_MODEL), row),
            mod_spec, mod_spec, mod_spec,
            pl.BlockSpec((2, 2 * D_MODEL, D_MODEL), lambda b, i: (0, 0, 0)),
            pl.BlockSpec((1, D_MODEL), fixed),
            pl.BlockSpec((1, D_MODEL), fixed),
            pl.BlockSpec((D_MODEL, LANES), fixed),
            pl.BlockSpec((D_MODEL, LANES), fixed),
            pl.BlockSpec((1, LANES), fixed),
        ],
        out_specs=[
            pl.BlockSpec((1, tm, D_MODEL), row),
            h2_spec,
            pl.BlockSpec((1, tm, LANES), row),
        ],
        out_shape=[
            jax.ShapeDtypeStruct((B, L, D_MODEL), F32),
            h2_shape,
            jax.ShapeDtypeStruct((B, L, LANES), F32),
        ],
        compiler_params=_cparams(("arbitrary", "arbitrary")),
        name="outproj",
    )(y_ssd, y_rg, y_ret, x, g1, sc2, sh2, p["w_out"], p["ln1_g"], p["ln1_b"],
      consts["rw_hi"], consts["rw_lo"], consts["rbias"])


def _row_gather_copy(src_hbm, first_row, dst_vmem, slot, r, sem):
    return pltpu.make_async_copy(src_hbm.at[pl.ds(pl.multiple_of(first_row, ROW_TILES), ROW_TILES)],
                                 dst_vmem.at[slot, pl.ds(r * ROW_TILES, ROW_TILES)], sem.at[slot])


def _moe_kernel(be_ref, nu_ref, *refs, precise, gather):
    if gather:
        rowc_ref, rown_ref, src_ref, wg_ref, wu_ref, wd_ref, o_ref = refs[:7]
        scr, xbuf, sem = refs[7:-2], refs[-2], refs[-1]
    else:
        x_ref, wg_ref, wu_ref, wd_ref, o_ref = refs[:5]
        scr = refs[5:]
    i = pl.program_id(0)
    n_used = nu_ref[0]
    used = i < n_used
    new_expert = jnp.logical_or(i == 0, be_ref[i] != be_ref[jnp.maximum(i - 1, 0)])

    if gather:
        slot = lax.rem(i, 2)

        def issue(rows_ref, s):
            for r in range(MOE_BM):
                _row_gather_copy(src_ref, rows_ref[0, 0, r], xbuf, s, r, sem).start(priority=r % 2)

        @pl.when(jnp.logical_and(used, i == 0))
        def _():
            issue(rowc_ref, 0)

        @pl.when(i + 1 < n_used)
        def _():
            issue(rown_ref, 1 - slot)

    @pl.when(jnp.logical_and(used, new_expert))
    def _():
        for k, w_ref in enumerate((wg_ref, wu_ref, wd_ref)):
            w = w_ref[...]
            if precise:
                hi, lo = _split2(w)
                scr[k][...] = hi
                scr[3 + k][...] = lo
            else:
                scr[k][...] = w.astype(BF16)

    @pl.when(used)
    def _():
        if gather:
            pltpu.make_async_copy(src_ref.at[pl.ds(0, MOE_BM * ROW_TILES)], xbuf.at[slot], sem.at[slot]).wait()
            xb = xbuf.at[slot]
            x = jnp.concatenate([xb[pl.ds(s, MOE_BM, stride=ROW_TILES), :] for s in range(ROW_TILES)], axis=1)
        else:
            x = x_ref[...]
        lo = (scr[3][...], scr[4][...], scr[5][...]) if precise else (None, None, None)
        g = _mm_w(x, scr[0][...], lo[0])
        u = _mm_w(x, scr[1][...], lo[1])
        o_ref[...] = _mm_w(_silu(g) * u, scr[2][...], lo[2])

    @pl.when(jnp.logical_not(used))
    def _():
        o_ref[...] = jnp.zeros_like(o_ref)


def _moe(xs, block_expert, n_used, weights, layer, precise, slot_rows=None):
    gather = slot_rows is not None
    P = slot_rows.shape[0] if gather else xs.shape[0]
    nblk = P // MOE_BM
    in_w = pl.BlockSpec((None, None, D_MODEL, D_EXPERT), lambda i, be, nu: (layer, be[i], 0, 0))
    out_w = pl.BlockSpec((None, None, D_EXPERT, D_MODEL), lambda i, be, nu: (layer, be[i], 0, 0))
    w_scratch = [pltpu.VMEM((D_MODEL, D_EXPERT), BF16), pltpu.VMEM((D_MODEL, D_EXPERT), BF16),
                 pltpu.VMEM((D_EXPERT, D_MODEL), BF16)] * (2 if precise else 1)
    if gather:
        rows3 = slot_rows.reshape(nblk, 1, MOE_BM)
        x_specs = [
            pl.BlockSpec((1, 1, MOE_BM), lambda i, be, nu: (i, 0, 0), memory_space=pltpu.SMEM),
            pl.BlockSpec((1, 1, MOE_BM), lambda i, be, nu: (jnp.minimum(i + 1, nblk - 1), 0, 0),
                         memory_space=pltpu.SMEM),
            pl.BlockSpec(memory_space=pl.ANY),
        ]
        x_args = [rows3, rows3, xs]
        extra_scratch = [pltpu.VMEM((2, MOE_BM * ROW_TILES, LANES), xs.dtype), pltpu.SemaphoreType.DMA((2,))]
    else:
        x_specs = [pl.BlockSpec((MOE_BM, D_MODEL), lambda i, be, nu: (i, 0))]
        x_args = [xs]
        extra_scratch = []
    grid_spec = pltpu.PrefetchScalarGridSpec(
        num_scalar_prefetch=2,
        grid=(nblk,),
        in_specs=x_specs + [in_w, in_w, out_w],
        out_specs=pl.BlockSpec((MOE_BM, D_MODEL), lambda i, be, nu: (i, 0)),
        scratch_shapes=w_scratch + extra_scratch,
    )
    return pl.pallas_call(
        functools.partial(_moe_kernel, precise=precise, gather=gather),
        grid_spec=grid_spec,
        out_shape=jax.ShapeDtypeStruct((P, D_MODEL), F32),
        compiler_params=_cparams(("arbitrary",)),
        name="moe_gather" if gather else "moe",
    )(block_expert, n_used, *x_args, *weights)


def _combine_kernel(x1_ref, ya_ref, yb_ref, route_ref, g2_ref, lng_ref, lnb_ref, o_ref):
    tm = x1_ref.shape[1]
    lane = lax.broadcasted_iota(jnp.int32, (tm, LANES), 1)
    rt = route_ref[0]
    w1 = jnp.sum(jnp.where(lane == 2, rt, 0.0), axis=1, keepdims=True)
    w2 = jnp.sum(jnp.where(lane == 3, rt, 0.0), axis=1, keepdims=True)
    moe = ya_ref[...] * w1 + yb_ref[...] * w2
    o_ref[0] = _layernorm(ALPHA * x1_ref[0] + (1.0 + g2_ref[0]) * moe, lng_ref[...], lnb_ref[...])


def _combine(x1, ya, yb, row_off, route, g2, p, per_token_mod):
    B, L, _ = x1.shape
    tm = _pick_tm(L, PROJ_TM)
    assert row_off % tm == 0
    row = lambda b, i: (b, i, 0)
    fixed = lambda b, i: (0, 0)
    flat = lambda b, i: (row_off // tm + b * (L // tm) + i, 0)
    if per_token_mod:
        mod_spec = pl.BlockSpec((1, tm, D_MODEL), row)
    else:
        mod_spec = pl.BlockSpec((1, 1, D_MODEL), lambda b, i: (b, 0, 0))
    return pl.pallas_call(
        _combine_kernel,
        grid=(B, L // tm),
        in_specs=[
            pl.BlockSpec((1, tm, D_MODEL), row),
            pl.BlockSpec((tm, D_MODEL), flat),
            pl.BlockSpec((tm, D_MODEL), flat),
            pl.BlockSpec((1, tm, LANES), row),
            mod_spec,
            pl.BlockSpec((1, D_MODEL), fixed),
            pl.BlockSpec((1, D_MODEL), fixed),
        ],
        out_specs=pl.BlockSpec((1, tm, D_MODEL), row),
        out_shape=jax.ShapeDtypeStruct((B, L, D_MODEL), F32),
        compiler_params=_cparams(("arbitrary", "arbitrary")),
        name="combine",
    )(x1, ya, yb, route, g2, p["ln2_g"], p["ln2_b"])


def _dispatch_plan(eidx):
    T = eidx.shape[0]
    A = 2 * T
    flat_e = eidx.reshape(A)
    onehot = (flat_e[:, None] == jnp.arange(N_EXPERTS, dtype=jnp.int32)[None, :]).astype(jnp.int32)
    cum = jnp.cumsum(onehot, axis=0)
    rank = jnp.take_along_axis(cum, flat_e[:, None], axis=1)[:, 0] - 1
    counts = cum[-1]
    padded = (counts + MOE_BM - 1) // MOE_BM * MOE_BM
    pad_end = jnp.cumsum(padded)
    pad_start = pad_end - padded
    dest = pad_start[flat_e] + rank
    nblk = (A + N_EXPERTS * (MOE_BM - 1) + MOE_BM - 1) // MOE_BM
    block_pos = jnp.arange(nblk, dtype=jnp.int32) * MOE_BM
    block_expert = jnp.minimum(jnp.sum(pad_end[None, :] <= block_pos[:, None], axis=1), N_EXPERTS - 1).astype(jnp.int32)
    n_used = (pad_end[-1] // MOE_BM).astype(jnp.int32).reshape(1)
    return dest, block_expert, n_used, nblk * MOE_BM


def _mods(mod_l, n_prompt, dec_seq):
    mp = mod_l[:n_prompt].reshape(n_prompt, 6, D_MODEL)
    ms = mod_l[n_prompt:].reshape(-1, 6, D_MODEL)
    prompt = [mp[:, j][:, None, :] for j in range(6)]
    sample = [jnp.repeat(ms[:, j], dec_seq, axis=0)[None] for j in range(6)]
    return prompt, sample


def kernel(x_prompt, x_sample, c_prompt, c_sample, state_ssd, state_ssd_conv, state_rglru, state_rglru_conv, state_ret, w_ada, b_ada, w_in, ssd_conv_w, ssd_conv_b, ssd_dt_bias, ssd_a_log, ssd_d, ssd_norm_w, rg_conv_w, rg_conv_b, rg_wa, rg_ba, rg_wx, rg_bx, rg_lambda, ret_gn_w, w_out, ln1_g, ln1_b, router_w, router_bias, exp_w_gate, exp_w_up, exp_w_down, ln2_g, ln2_b):
    BP, LP, _ = x_prompt.shape
    BS, LS, _ = x_sample.shape
    TP, TS = BP * LP, BS * LS

    def pad_lanes(v):
        return jnp.pad(v, ((0, 0), (0, LANES - v.shape[-1])))

    eye = jnp.eye(RG_BLOCKS, dtype=F32)

    def block_diag(w):
        return jnp.stack(_hi_lo(jnp.einsum("njk,nm->njmk", w, eye).reshape(D_RG, D_RG)))

    params = []
    for l in range(DEPTH):
        wi = w_in[l]
        wi = jnp.concatenate([wi[:, :2560], wi[:, 2576:5648], wi[:, 2560:2576],
                              jnp.zeros((D_MODEL, LANES - SSD_HEADS), F32)], axis=1)
        params.append(dict(
            w_in=_hi_lo(wi),
            ssd_conv_w=ssd_conv_w[l], ssd_conv_b=ssd_conv_b[l][None],
            ssd_dt_bias=pad_lanes(ssd_dt_bias[l][None]), ssd_a_log=pad_lanes(ssd_a_log[l][None]),
            ssd_d_ch=jnp.repeat(ssd_d[l], SSD_HEAD_DIM)[None], ssd_norm_w=ssd_norm_w[l][None],
            rg_conv_w=rg_conv_w[l], rg_conv_b=rg_conv_b[l][None],
            rg_wa_bd=block_diag(rg_wa[l]), rg_ba=rg_ba[l][None],
            rg_wx_bd=block_diag(rg_wx[l]), rg_bx=rg_bx[l][None],
            rg_lambda=rg_lambda[l][None], ret_gn_w=ret_gn_w[l][None],
            w_out=jnp.stack(_hi_lo(w_out[l])), ln1_g=ln1_g[l][None], ln1_b=ln1_b[l][None],
            ln2_g=ln2_g[l][None], ln2_b=ln2_b[l][None],
        ))

    rw_hi, rw_lo = _hi_lo(pad_lanes(router_w))
    rbias = pad_lanes(router_bias[None])
    head_of_ch = jnp.arange(D_SSD) // SSD_HEAD_DIM
    consts = dict(
        rw_hi=rw_hi, rw_lo=rw_lo, rbias=rbias.astype(F32),
        head_expand=(jnp.arange(LANES)[:, None] == head_of_ch[None, :]).astype(BF16),
        tri=jnp.tril(jnp.ones((CHUNK, CHUNK), F32)).astype(BF16),
    )
    mod = _ada(jnp.concatenate([c_prompt, c_sample], axis=0), w_ada, b_ada)

    LM = LP - TAIL
    paths = [
        dict(x=x_prompt[:, :LM], B=BP, L=LM, per_token=False, has_state=False, precise=False, rc=_ret_consts(LM, 0.0)),
        dict(x=x_prompt[:, LM:], B=BP, L=TAIL, per_token=False, has_state=True, precise=True, rc=_ret_consts(TAIL, float(LM))),
        dict(x=x_sample.reshape(1, TS, D_MODEL), B=BS, L=LS, per_token=True, has_state=True, precise=False,
             rc=_ret_consts(LS, PAST_LEN)),
    ]
    zero_states = (jnp.zeros((BP, SSD_GROUPS, SSD_STATE, D_SSD // SSD_GROUPS), F32),
                   jnp.zeros((BP, CONV_WIDTH - 1, SSD_CONV_DIM), F32),
                   jnp.zeros((BP, 1, D_RG), F32),
                   jnp.zeros((BP, CONV_WIDTH - 1, D_RG), F32),
                   jnp.zeros((BP, RET_HEADS, RET_HEAD_DIM, RET_HEAD_DIM), F32))

    def mixer_stack(path, p, mods, states, ssd_kw, ret_kw):
        B, L, per_token, precise = path["B"], path["L"], path["per_token"], path["precise"]
        w_in_l = p["w_in"] if precise else p["w_in"][:1]
        z, xbc, rg, ret, dt = _inproj(path["x"], mods[1], mods[0], w_in_l, per_token, precise)
        if per_token:
            z, xbc, rg, ret, dt = [a.reshape(B, L, a.shape[-1]) for a in (z, xbc, rg, ret, dt)]
        s_ssd, s_cbuf, s_rg, s_rbuf, s_ret = states
        hs = path["has_state"]
        y_ssd, cbuf_n, ssd_n = _ssd(z, xbc, dt, s_ssd, s_cbuf, p, consts, hs, precise, **ssd_kw)
        y_rg, rbuf_n, rg_n = _rg(rg, s_rg, s_rbuf, p, hs, precise)
        y_ret, ret_n = _ret(ret, s_ret, p, path["rc"], hs, precise, **ret_kw)
        if per_token:
            y_ssd, y_rg, y_ret = [a.reshape(1, B * L, a.shape[-1]) for a in (y_ssd, y_rg, y_ret)]
        return (y_ssd, y_rg, y_ret), (ssd_n, cbuf_n, rg_n, rbuf_n, ret_n)

    expert_w = (exp_w_gate, exp_w_up, exp_w_down)

    def experts(h2_list, route_list, layer, precise):
        last = D_MODEL if precise else LANES
        h2_all = jnp.concatenate([h.reshape(-1, last) for h in h2_list], axis=0)
        route_all = jnp.concatenate([r.reshape(-1, LANES) for r in route_list], axis=0)
        T_all = route_all.shape[0]
        dest, block_expert, n_used, P = _dispatch_plan(route_all[:, 0:2].astype(jnp.int32))
        tok = jnp.arange(2 * T_all, dtype=jnp.int32) // 2
        if precise:
            slot_tok = jnp.full((P,), T_all, jnp.int32).at[dest].set(tok)
            xs_sorted = jnp.concatenate([h2_all, jnp.zeros((1, D_MODEL), h2_all.dtype)], axis=0)[slot_tok]
            ys_sorted = _moe(xs_sorted, block_expert, n_used, expert_w, layer, True)
        else:
            slot_row = jnp.zeros((P,), jnp.int32).at[dest].set(tok * ROW_TILES)
            ys_sorted = _moe(h2_all, block_expert, n_used, expert_w, layer, False, slot_rows=slot_row)
        return ys_sorted[dest[0::2]], ys_sorted[dest[1::2]]

    new_tail = []
    state_ssd_t = jnp.swapaxes(state_ssd, -1, -2)
    ssd_s = ret_s = None
    cbuf_s, rg_s, rbuf_s = [], [], []
    for l in range(DEPTH):
        p = params[l]
        mods_p, mods_s = _mods(mod[l], BP, LS)
        mods = [mods_p, mods_p, mods_s]
        st_s = (state_ssd_t, state_ssd_conv[l], state_rglru[l][:, None, :], state_rglru_conv[l], state_ret)

        ys_m, st_m = mixer_stack(paths[0], p, mods[0], zero_states, {}, {})
        ys_t, st_t = mixer_stack(paths[1], p, mods[1], st_m, dict(native_out=True), {})
        ys_c, st_c = mixer_stack(paths[2], p, mods[2], st_s,
                                 dict(h0_layer=l, native_out=True, out_stack=(l, ssd_s)),
                                 dict(h0_layer=l, out_stack=(l, ret_s)))
        new_tail.append(st_t)
        ssd_s, ret_s = st_c[0], st_c[4]
        cbuf_s.append(st_c[1])
        rg_s.append(st_c[2])
        rbuf_s.append(st_c[3])

        post = []
        for path, ys, m in zip(paths, (ys_m, ys_t, ys_c), mods):
            post.append(_outproj(*ys, path["x"], m[2], m[4], m[3], p, consts, path["per_token"], path["precise"]))

        ya_b, yb_b = experts([post[0][1], post[2][1]], [post[0][2], post[2][2]], l, False)
        ya_t, yb_t = experts([post[1][1]], [post[1][2]], l, True)

        offs = (0, 0, BP * LM)
        for path, (x1, _, route), ya, yb, off, m in zip(paths, post, (ya_b, ya_t, ya_b), (yb_b, yb_t, yb_b), offs, mods):
            path["x"] = _combine(x1, ya, yb, off, route, m[5], p, path["per_token"])

    def stack(lst, k):
        return jnp.stack([s[k] for s in lst])

    y_prompt = jnp.concatenate([paths[0]["x"], paths[1]["x"]], axis=1)
    return (y_prompt, paths[2]["x"].reshape(BS, LS, D_MODEL),
            jnp.swapaxes(stack(new_tail, 0), -1, -2), stack(new_tail, 1), stack(new_tail, 2)[:, :, 0],
            stack(new_tail, 3), stack(new_tail, 4),
            jnp.swapaxes(ssd_s, -1, -2), jnp.stack(cbuf_s), jnp.stack(rg_s)[:, :, 0], jnp.stack(rbuf_s), ret_s)
```

```python
import functools
import math

import jax
import jax.numpy as jnp
from jax import lax
from jax.experimental import pallas as pl
from jax.experimental.pallas import tpu as pltpu

F32 = jnp.float32
BF16 = jnp.bfloat16

D_MODEL = 1024
DEPTH = 4
D_SSD = 1024
SSD_HEAD_DIM = 64
SSD_HEADS = 16
SSD_GROUPS = 2
SSD_STATE = 128
SSD_CONV_DIM = D_SSD + 2 * SSD_GROUPS * SSD_STATE
CONV_WIDTH = 4
D_RG = 512
RG_BLOCKS = 8
RG_C = 8.0
D_RET = 512
RET_HEADS = 4
RET_HEAD_DIM = 128
ROPE_BASE = 10000.0
N_EXPERTS = 32
EXPERTS_PER_GROUP = 8
N_EXPERT_GROUPS = 4
D_EXPERT = 512
ALPHA = (2 * DEPTH) ** 0.25
EPS = 1e-5
PAST_LEN = 16384.0

LANES = 128
SUBLANES = 8
CHUNK = 128
TAIL = CHUNK
SEQS_PER_STEP = 8
SEQ_UNROLL = 2
ROW_TILES = D_MODEL // LANES
PROJ_TM = 384
MOE_BM = 256
VMEM_LIMIT = 56 * 1024 * 1024


def _cparams(sem):
    return pltpu.CompilerParams(dimension_semantics=sem, vmem_limit_bytes=VMEM_LIMIT)


def _padded_rows(L):
    return CHUNK if L >= CHUNK else -(-L // SUBLANES) * SUBLANES


def _pick_tm(L, cap):
    if L <= LANES:
        return L
    tm = cap - cap % LANES
    while L % tm:
        tm -= LANES
    return tm


def _sigmoid(x):
    return 1.0 / (1.0 + jnp.exp(-x))


def _silu(x):
    return x * _sigmoid(x)


def _softplus(x):
    return jnp.maximum(x, 0.0) + jnp.log1p(jnp.exp(-jnp.abs(x)))


def _split3(v):
    hi = v.astype(BF16)
    r = v - hi.astype(F32)
    mid = r.astype(BF16)
    lo = (r - mid.astype(F32)).astype(BF16)
    return hi, mid, lo


def _dot(a, b):
    return jnp.dot(a, b, preferred_element_type=F32)


def _split2(v):
    hi = v.astype(BF16)
    return hi, (v - hi.astype(F32)).astype(BF16)


def _mm_w(a, w_hi, w_lo):
    if w_lo is None:
        return _dot(a.astype(BF16), w_hi)
    ah, al = _split2(a)
    return _dot(ah, w_hi) + (_dot(al, w_hi) + _dot(ah, w_lo))


def _mm_a(a, b, precise, nt=False):
    dn = (((1,), (1,)), ((), ())) if nt else (((1,), (0,)), ((), ()))

    def d(x, y):
        return lax.dot_general(x, y, dn, preferred_element_type=F32)

    if not precise:
        return d(a.astype(BF16), b.astype(BF16))
    ah, al = _split2(a)
    bh, bl = _split2(b)
    return d(ah, bh) + (d(al, bh) + d(ah, bl))


def _hi_lo(w):
    hi32 = lax.reduce_precision(w, exponent_bits=8, mantissa_bits=7)
    return hi32.astype(BF16), (w - hi32).astype(BF16)


def _dot_exact_rhs(v, m):
    hi, mid, lo = _split3(v)
    return _dot(hi, m) + _dot(mid, m) + _dot(lo, m)


def _dot_exact_lhs(m, v):
    hi, mid, lo = _split3(v)
    return _dot(m, hi) + _dot(m, mid) + _dot(m, lo)


def _ada_kernel(c_ref, w_ref, b_ref, o_ref):
    o_ref[0] = _mm_a(_silu(c_ref[...]), w_ref[0], True) + b_ref[0]


def _ada(c_all, w_ada, b_ada):
    n = c_all.shape[0]
    tn = 1024
    return pl.pallas_call(
        _ada_kernel,
        grid=(DEPTH, 6 * D_MODEL // tn),
        in_specs=[
            pl.BlockSpec((n, D_MODEL), lambda l, j: (0, 0)),
            pl.BlockSpec((1, D_MODEL, tn), lambda l, j: (l, 0, j)),
            pl.BlockSpec((1, 1, tn), lambda l, j: (l, 0, j)),
        ],
        out_specs=pl.BlockSpec((1, n, tn), lambda l, j: (l, 0, j)),
        out_shape=jax.ShapeDtypeStruct((DEPTH, n, 6 * D_MODEL), F32),
        compiler_params=_cparams(("arbitrary", "arbitrary")),
        name="ada",
    )(c_all, w_ada, b_ada.reshape(DEPTH, 1, 6 * D_MODEL))


_IN_SEGS = (("z", 0, 1024), ("xbc", 1024, 2560), ("rg", 2560, 3584), ("ret", 3584, 5632), ("dt", 5632, 5760))
IN_PAD = 5760


def _inproj_kernel(x_ref, sc_ref, sh_ref, w_ref, *rest, precise):
    wl_ref = rest[0] if precise else None
    outs = rest[1:] if precise else rest
    h = x_ref[0] * (1.0 + sc_ref[0]) + sh_ref[0]
    if precise:
        hh, hl = _split2(h)
    else:
        hh = h.astype(BF16)
    for o_ref, (_, lo, hi) in zip(outs, _IN_SEGS):
        acc = _dot(hh, w_ref[:, lo:hi])
        if precise:
            acc = acc + (_dot(hl, w_ref[:, lo:hi]) + _dot(hh, wl_ref[:, lo:hi]))
        o_ref[0] = acc


def _inproj(x, sc, sh, w, layer, per_token_mod, precise):
    B, L, _ = x.shape
    tm = _pick_tm(L, LANES if precise else PROJ_TM)
    if per_token_mod:
        mod_spec = pl.BlockSpec((1, tm, D_MODEL), lambda b, i: (b, i, 0))
    else:
        mod_spec = pl.BlockSpec((1, 1, D_MODEL), lambda b, i: (b, 0, 0))
    widths = [hi - lo for _, lo, hi in _IN_SEGS]
    w_spec = pl.BlockSpec((None, D_MODEL, IN_PAD), lambda b, i: (layer, 0, 0), pipeline_mode=pl.Buffered(1))
    ws = w if precise else w[:1]
    return pl.pallas_call(
        functools.partial(_inproj_kernel, precise=precise),
        grid=(B, L // tm),
        in_specs=[
            pl.BlockSpec((1, tm, D_MODEL), lambda b, i: (b, i, 0)),
            mod_spec,
            mod_spec,
        ] + [w_spec] * len(ws),
        out_specs=[pl.BlockSpec((1, tm, wd), lambda b, i: (b, i, 0)) for wd in widths],
        out_shape=[jax.ShapeDtypeStruct((B, L, wd), F32) for wd in widths],
        compiler_params=_cparams(("arbitrary", "arbitrary")),
        name="inproj",
    )(x, sc, sh, *ws)


def _conv_step(src_rows, buf_ref, cw_ref, cb_ref, nbuf_ref, xp_s, *, c, nc, Lv, Lp, has_state):
    @_when_unless(nc == 1, c == 0)
    def _():
        xp_s[...] = jnp.zeros_like(xp_s)
        if has_state:
            xp_s[SUBLANES - (CONV_WIDTH - 1):SUBLANES, :] = buf_ref[0]

    xp_s[SUBLANES:SUBLANES + Lv, :] = src_rows
    base = SUBLANES - (CONV_WIDTH - 1)
    xc = cb_ref[...]
    for j in range(CONV_WIDTH):
        xc = xc + cw_ref[j:j + 1, :] * xp_s[base + j:base + j + Lp, :]

    @_when_unless(nc == 1, c == nc - 1)
    def _():
        nbuf_ref[0] = xp_s[SUBLANES + Lv - (CONV_WIDTH - 1):SUBLANES + Lv, :]

    if nc > 1:
        xp_s[0:SUBLANES, :] = xp_s[Lp:Lp + SUBLANES, :]
    return xc


def _for_each_sequence(body, refs, bb, keep, squeeze, n_scratch):
    n = len(refs)

    def one(s, j):
        v = list(refs)
        for k in keep:
            v[k] = refs[k].at[pl.ds(s, 1)]
        for k in squeeze:
            v[k] = refs[k].at[s]
        if bb > 1:
            for k in range(n - n_scratch, n):
                v[k] = refs[k].at[j]
        body(*v)

    if bb == 1:
        one(0, 0)
    else:
        def step(i, carry):
            for j in range(SEQ_UNROLL):
                one(i * SEQ_UNROLL + j, j)
            return carry
        lax.fori_loop(0, bb // SEQ_UNROLL, step, 0)


def _when_unless(always, cond):
    def deco(fn):
        if always:
            fn()
        else:
            pl.when(cond)(fn)
        return fn
    return deco


def _seq_scratch(shape, bb):
    return pltpu.VMEM(((SEQ_UNROLL,) + shape) if bb > 1 else shape, F32)


def _ssd_kernel(*refs, bb, **kw):
    n = len(refs)
    _for_each_sequence(functools.partial(_ssd_body, **kw), refs, bb,
                       keep=(0, 1, 2, 4, n - 6, n - 5), squeeze=(3, n - 4), n_scratch=3)


def _ssd_body(z_ref, xbc_ref, dt_ref, h0_ref, buf_ref, cw_ref, cb_ref, dtb_ref, alog_ref, dch_ref,
              nw_ref, e_ref, tri_ref, *rest, Lv, Lp, nc, has_state, precise, native_in, native_out):
    y_ref, nbuf_ref, hn_ref, xp_s, dtp_s, s_s = rest[-6:]
    c = pl.program_id(1)
    hpg = SSD_HEADS // SSD_GROUPS

    @_when_unless(nc == 1, c == 0)
    def _():
        if not has_state:
            s_s[...] = jnp.zeros_like(s_s)
        elif native_in:
            for g in range(SSD_GROUPS):
                s_s[g] = jnp.concatenate([h0_ref[g * hpg + i] for i in range(hpg)], axis=0).T
        else:
            s_s[...] = h0_ref[...]
        if Lv < Lp:
            dtp_s[...] = jnp.zeros_like(dtp_s)

    xc = _conv_step(xbc_ref[0], buf_ref, cw_ref, cb_ref, nbuf_ref, xp_s,
                    c=c, nc=nc, Lv=Lv, Lp=Lp, has_state=has_state)
    xbc = _silu(xc)
    xs = xbc[:, 0:D_SSD]
    Bm = xbc[:, D_SSD:D_SSD + 256]
    Cm = xbc[:, D_SSD + 256:D_SSD + 512]

    if Lv < Lp:
        dtp_s[0:Lv, :] = dt_ref[0]
        dtr = dtp_s[...]
    else:
        dtr = dt_ref[0]
    dt = _softplus(dtr + dtb_ref[...])
    if Lv < Lp:
        rowv = lax.broadcasted_iota(jnp.int32, (Lp, LANES), 0)
        dt = jnp.where(rowv < Lv, dt, 0.0)
    a = dt * (-jnp.exp(alog_ref[...]))
    acum = _dot_exact_lhs(tri_ref[...], a)
    dt_e = _dot_exact_rhs(dt, e_ref[...])
    ac_e = _dot_exact_rhs(acum, e_ref[...])
    acl_e = ac_e[Lp - 1:Lp, :]
    acum_t = acum.T

    vdt = xs * dt_e
    lane = lax.broadcasted_iota(jnp.int32, (Lp, LANES), 1)
    causal = lax.broadcasted_iota(jnp.int32, (Lp, Lp), 0) >= lax.broadcasted_iota(jnp.int32, (Lp, Lp), 1)
    mid_t = F32 if precise else BF16

    y_blocks = []
    for g in range(SSD_GROUPS):
        Cg = Cm[:, g * SSD_STATE:(g + 1) * SSD_STATE]
        Bg = Bm[:, g * SSD_STATE:(g + 1) * SSD_STATE]
        G = _mm_a(Cg, Bg, precise, nt=True)
        for jj in range(4):
            j = 4 * g + jj
            scs = []
            for h in (2 * j, 2 * j + 1):
                col = jnp.sum(jnp.where(lane == h, acum, 0.0), axis=1, keepdims=True)
                rw = acum_t[h:h + 1, :]
                dec = jnp.exp(jnp.where(causal, col - rw, -jnp.inf))
                scs.append((G * dec).astype(mid_t))
            sc = jnp.concatenate(scs, axis=1)
            vb = vdt[:, j * LANES:(j + 1) * LANES]
            v2 = jnp.concatenate([jnp.where(lane < SSD_HEAD_DIM, vb, 0.0),
                                  jnp.where(lane >= SSD_HEAD_DIM, vb, 0.0)], axis=0).astype(mid_t)
            y_blocks.append(_mm_a(sc, v2, precise))
    y = jnp.concatenate(y_blocks, axis=1)

    eac = jnp.exp(ac_e)
    vw = vdt * jnp.exp(acl_e - ac_e)
    cd = jnp.exp(acl_e)
    half = D_SSD // SSD_GROUPS
    ys_parts = []
    for g in range(SSD_GROUPS):
        Sg = s_s[g]
        ys_parts.append(_mm_a(Cm[:, g * SSD_STATE:(g + 1) * SSD_STATE], Sg, precise))
        BgT = Bm[:, g * SSD_STATE:(g + 1) * SSD_STATE].T
        s_s[g] = cd[:, g * half:(g + 1) * half] * Sg + _mm_a(BgT, vw[:, g * half:(g + 1) * half], precise)
    y = y + eac * jnp.concatenate(ys_parts, axis=1) + dch_ref[...] * xs

    yv = y[0:Lv, :] if Lv < Lp else y
    yz = yv * _silu(z_ref[0])
    outs = []
    for g in range(SSD_GROUPS):
        part = yz[:, g * half:(g + 1) * half]
        ms = jnp.mean(part * part, axis=1, keepdims=True)
        outs.append(part * lax.rsqrt(ms + EPS))
    y_ref[0] = jnp.concatenate(outs, axis=1) * nw_ref[...]

    @_when_unless(nc == 1, c == nc - 1)
    def _():
        if native_out:
            for g in range(SSD_GROUPS):
                sgt = s_s[g].T
                for i in range(hpg):
                    hn_ref[g * hpg + i] = sgt[i * SSD_HEAD_DIM:(i + 1) * SSD_HEAD_DIM, :]
        else:
            hn_ref[...] = s_s[...]


def _state_specs(B, bb, shape, in_layer, out_stack):
    zeros = (0,) * len(shape)
    if in_layer is None:
        in_spec = pl.BlockSpec((bb,) + shape, lambda b, c: (b,) + zeros)
    else:
        in_spec = pl.BlockSpec((None, bb) + shape, lambda b, c: (in_layer, b) + zeros)
    if out_stack is None:
        out_spec = pl.BlockSpec((bb,) + shape, lambda b, c: (b,) + zeros)
        out_shape = jax.ShapeDtypeStruct((B,) + shape, F32)
    else:
        l = out_stack[0]
        out_spec = pl.BlockSpec((None, bb) + shape, lambda b, c: (l, b) + zeros)
        out_shape = jax.ShapeDtypeStruct((DEPTH, B) + shape, F32)
    return in_spec, out_spec, out_shape


def _seqs_per_step(B, L):
    return SEQS_PER_STEP if (L < CHUNK and B % SEQS_PER_STEP == 0) else 1


def _ssd(z, xbc, dt, h0, buf, p, consts, has_state, precise, h0_layer=None, native_out=False, out_stack=None):
    B, L, _ = z.shape
    Lv = min(CHUNK, L)
    Lp = _padded_rows(L)
    nc = L // Lv
    native_in = h0_layer is not None
    bb = _seqs_per_step(B, L)
    kern = functools.partial(_ssd_kernel, bb=bb, Lv=Lv, Lp=Lp, nc=nc, has_state=has_state, precise=precise,
                             native_in=native_in, native_out=native_out)
    row = lambda b, c: (b, c, 0)
    per_b3 = lambda b, c: (b, 0, 0)
    fixed = lambda b, c: (0, 0)
    half = D_SSD // SSD_GROUPS
    native = (SSD_HEADS, SSD_HEAD_DIM, SSD_STATE)
    packed = (SSD_GROUPS, SSD_STATE, half)
    h0_spec, _, _ = _state_specs(B, bb, native if native_in else packed, h0_layer, None)
    _, hn_spec, hn_shape = _state_specs(B, bb, native if native_out else packed, None, out_stack)
    prev = out_stack[1] if out_stack is not None else None
    extra_in, extra_specs, aliases = [], [], {}
    if prev is not None:
        extra_in, extra_specs, aliases = [prev], [pl.BlockSpec(memory_space=pl.ANY)], {13: 2}
    return pl.pallas_call(
        kern,
        grid=(B // bb, nc),
        in_specs=[
            pl.BlockSpec((bb, Lv, D_SSD), row),
            pl.BlockSpec((bb, Lv, SSD_CONV_DIM), row),
            pl.BlockSpec((bb, Lv, LANES), row),
            h0_spec,
            pl.BlockSpec((bb, CONV_WIDTH - 1, SSD_CONV_DIM), per_b3),
            pl.BlockSpec((CONV_WIDTH, SSD_CONV_DIM), fixed),
            pl.BlockSpec((1, SSD_CONV_DIM), fixed),
            pl.BlockSpec((1, LANES), fixed),
            pl.BlockSpec((1, LANES), fixed),
            pl.BlockSpec((1, D_SSD), fixed),
            pl.BlockSpec((1, D_SSD), fixed),
            pl.BlockSpec((LANES, D_SSD), fixed),
            pl.BlockSpec((Lp, Lp), fixed),
        ] + extra_specs,
        out_specs=[
            pl.BlockSpec((bb, Lv, D_SSD), row),
            pl.BlockSpec((bb, CONV_WIDTH - 1, SSD_CONV_DIM), per_b3),
            hn_spec,
        ],
        out_shape=[
            jax.ShapeDtypeStruct((B, L, D_SSD), F32),
            jax.ShapeDtypeStruct((B, CONV_WIDTH - 1, SSD_CONV_DIM), F32),
            hn_shape,
        ],
        scratch_shapes=[
            _seq_scratch((SUBLANES + Lp, SSD_CONV_DIM), bb),
            _seq_scratch((Lp, LANES), bb),
            _seq_scratch((SSD_GROUPS, SSD_STATE, half), bb),
        ],
        input_output_aliases=aliases,
        compiler_params=_cparams(("arbitrary", "arbitrary")),
        name="ssd",
    )(z, xbc, dt, h0, buf, p["ssd_conv_w"], p["ssd_conv_b"], p["ssd_dt_bias"], p["ssd_a_log"],
      p["ssd_d_ch"], p["ssd_norm_w"], consts["head_expand"],
      jnp.tril(jnp.ones((Lp, Lp), F32)).astype(BF16), *extra_in)


def _rg_kernel(*refs, bb, **kw):
    n = len(refs)
    _for_each_sequence(functools.partial(_rg_body, **kw), refs, bb,
                       keep=(0, 1, 2, n - 5, n - 4, n - 3), squeeze=(), n_scratch=2)


def _rg_body(rg_ref, h0_ref, buf_ref, cw_ref, cb_ref, wa_ref, ba_ref, wx_ref, bx_ref, lam_ref,
             y_ref, nbuf_ref, hn_ref, xp_s, hc_s, *, Lv, Lp, nc, has_state, precise):
    c = pl.program_id(1)

    @_when_unless(nc == 1, c == 0)
    def _():
        if has_state:
            hc_s[...] = h0_ref[0]
        else:
            hc_s[...] = jnp.zeros_like(hc_s)

    xc = _conv_step(rg_ref[0, :, 0:D_RG], buf_ref, cw_ref, cb_ref, nbuf_ref, xp_s,
                    c=c, nc=nc, Lv=Lv, Lp=Lp, has_state=has_state)
    r = _sigmoid(_mm_w(xc, wa_ref[0], wa_ref[1] if precise else None) + ba_ref[...])
    i = _sigmoid(_mm_w(xc, wx_ref[0], wx_ref[1] if precise else None) + bx_ref[...])
    la = (-RG_C * _softplus(-lam_ref[...])) * r
    u = jnp.sqrt(-jnp.tanh(la) * (jnp.exp(2.0 * la) + 1.0)) * (i * xc)
    row = lax.broadcasted_iota(jnp.int32, (Lp, D_RG), 0)
    if Lv < Lp:
        valid = row < Lv
        la = jnp.where(valid, la, 0.0)
        u = jnp.where(valid, u, 0.0)
    a = jnp.exp(la)
    s = 1
    while s < Lp:
        m = row >= s
        u_sh = jnp.where(m, pltpu.roll(u, s, 0), 0.0)
        a_sh = jnp.where(m, pltpu.roll(a, s, 0), 1.0)
        u = u + a * u_sh
        a = a * a_sh
        s *= 2
    h = u + a * hc_s[...]
    hc_s[...] = h[Lp - 1:Lp, :]
    hv = h[0:Lv, :] if Lv < Lp else h
    y_ref[0] = hv * jax.nn.gelu(rg_ref[0, :, D_RG:2 * D_RG])

    @_when_unless(nc == 1, c == nc - 1)
    def _():
        hn_ref[0] = hc_s[...]


def _rg(rg, h0, buf, p, has_state, precise):
    B, L, _ = rg.shape
    Lv = min(CHUNK, L)
    Lp = _padded_rows(L)
    nc = L // Lv
    bb = _seqs_per_step(B, L)
    kern = functools.partial(_rg_kernel, bb=bb, Lv=Lv, Lp=Lp, nc=nc, has_state=has_state, precise=precise)
    row = lambda b, c: (b, c, 0)
    per_b3 = lambda b, c: (b, 0, 0)
    fixed = lambda b, c: (0, 0)
    fixed3 = lambda b, c: (0, 0, 0)
    return pl.pallas_call(
        kern,
        grid=(B // bb, nc),
        in_specs=[
            pl.BlockSpec((bb, Lv, 2 * D_RG), row),
            pl.BlockSpec((bb, 1, D_RG), per_b3),
            pl.BlockSpec((bb, CONV_WIDTH - 1, D_RG), per_b3),
            pl.BlockSpec((CONV_WIDTH, D_RG), fixed),
            pl.BlockSpec((1, D_RG), fixed),
            pl.BlockSpec((2, D_RG, D_RG), fixed3),
            pl.BlockSpec((1, D_RG), fixed),
            pl.BlockSpec((2, D_RG, D_RG), fixed3),
            pl.BlockSpec((1, D_RG), fixed),
            pl.BlockSpec((1, D_RG), fixed),
        ],
        out_specs=[
            pl.BlockSpec((bb, Lv, D_RG), row),
            pl.BlockSpec((bb, CONV_WIDTH - 1, D_RG), per_b3),
            pl.BlockSpec((bb, 1, D_RG), per_b3),
        ],
        out_shape=[
            jax.ShapeDtypeStruct((B, L, D_RG), F32),
            jax.ShapeDtypeStruct((B, CONV_WIDTH - 1, D_RG), F32),
            jax.ShapeDtypeStruct((B, 1, D_RG), F32),
        ],
        scratch_shapes=[
            _seq_scratch((SUBLANES + Lp, D_RG), bb),
            _seq_scratch((1, D_RG), bb),
        ],
        compiler_params=_cparams(("arbitrary", "arbitrary")),
        name="rglru",
    )(rg, h0, buf, p["rg_conv_w"], p["rg_conv_b"], p["rg_wa_bd"], p["rg_ba"], p["rg_wx_bd"],
      p["rg_bx"], p["rg_lambda"])


def _ret_kernel(*refs, bb, **kw):
    n = len(refs)
    _for_each_sequence(functools.partial(_ret_body, **kw), refs, bb, keep=(0, n - 4), squeeze=(1, n - 3),
                       n_scratch=2)


def _ret_body(ret_ref, h0_ref, cos_ref, sin_ref, dm_ref, ea_ref, te_ref, cd_ref, gn_ref,
              *rest, Lv, Lp, nc, has_state, precise):
    y_ref, hn_ref, pad_s, s_s = rest[-4:]
    c = pl.program_id(1)

    @_when_unless(nc == 1, c == 0)
    def _():
        if has_state:
            s_s[...] = h0_ref[...]
        else:
            s_s[...] = jnp.zeros_like(s_s)
        if Lv < Lp:
            pad_s[...] = jnp.zeros_like(pad_s)

    if Lv < Lp:
        pad_s[0:Lv, :] = ret_ref[0, :, 0:3 * D_RET]
        qkv = pad_s[...]
    else:
        qkv = ret_ref[0, :, 0:3 * D_RET]
    cosf = cos_ref[...]
    sinf = sin_ref[...]
    outs = []
    for h in range(RET_HEADS):
        sl = slice(h * RET_HEAD_DIM, (h + 1) * RET_HEAD_DIM)
        q = qkv[:, sl]
        k = qkv[:, D_RET + h * RET_HEAD_DIM:D_RET + (h + 1) * RET_HEAD_DIM]
        v = qkv[:, 2 * D_RET + h * RET_HEAD_DIM:2 * D_RET + (h + 1) * RET_HEAD_DIM]
        q = q * cosf + pltpu.roll(q, RET_HEAD_DIM // 2, 1) * sinf
        k = (k * cosf + pltpu.roll(k, RET_HEAD_DIM // 2, 1) * sinf) * (RET_HEAD_DIM ** -0.5)
        G = _mm_a(q, k, precise, nt=True)
        S = s_s[h]
        y = _mm_a(G * dm_ref[h], v, precise) + _mm_a(q * ea_ref[h], S, precise)
        s_s[h] = cd_ref[h] * S + _mm_a((k * te_ref[h]).T, v, precise)
        yv = y[0:Lv, :] if Lv < Lp else y
        mu = jnp.mean(yv, axis=1, keepdims=True)
        d = yv - mu
        var = jnp.mean(d * d, axis=1, keepdims=True)
        outs.append(d * lax.rsqrt(var + EPS))
    yn = jnp.concatenate(outs, axis=1) * gn_ref[...]
    y_ref[0] = _silu(ret_ref[0, :, 3 * D_RET:4 * D_RET]) * yn

    @_when_unless(nc == 1, c == nc - 1)
    def _():
        hn_ref[...] = s_s[...]


def _ret(ret, h0, p, rc, has_state, precise, h0_layer=None, out_stack=None):
    B, L, _ = ret.shape
    Lv = min(CHUNK, L)
    Lp = _padded_rows(L)
    nc = L // Lv
    bb = _seqs_per_step(B, L)
    kern = functools.partial(_ret_kernel, bb=bb, Lv=Lv, Lp=Lp, nc=nc, has_state=has_state, precise=precise)
    row = lambda b, c: (b, c, 0)
    fixed2 = lambda b, c: (0, 0)
    fixed3 = lambda b, c: (0, 0, 0)
    hd = RET_HEAD_DIM
    h0_spec, _, _ = _state_specs(B, bb, (RET_HEADS, hd, hd), h0_layer, None)
    _, hn_spec, hn_shape = _state_specs(B, bb, (RET_HEADS, hd, hd), None, out_stack)
    prev = out_stack[1] if out_stack is not None else None
    extra_in, extra_specs, aliases = [], [], {}
    if prev is not None:
        extra_in, extra_specs, aliases = [prev], [pl.BlockSpec(memory_space=pl.ANY)], {9: 1}
    return pl.pallas_call(
        kern,
        grid=(B // bb, nc),
        in_specs=[
            pl.BlockSpec((bb, Lv, 4 * D_RET), row),
            h0_spec,
            pl.BlockSpec((Lp, hd), lambda b, c: (c, 0)),
            pl.BlockSpec((Lp, hd), lambda b, c: (c, 0)),
            pl.BlockSpec((RET_HEADS, Lp, Lp), fixed3),
            pl.BlockSpec((RET_HEADS, Lp, hd), fixed3),
            pl.BlockSpec((RET_HEADS, Lp, hd), fixed3),
            pl.BlockSpec((RET_HEADS, hd, hd), fixed3),
            pl.BlockSpec((1, D_RET), fixed2),
        ] + extra_specs,
        out_specs=[
            pl.BlockSpec((bb, Lv, D_RET), row),
            hn_spec,
        ],
        out_shape=[
            jax.ShapeDtypeStruct((B, L, D_RET), F32),
            hn_shape,
        ],
        scratch_shapes=[
            _seq_scratch((Lp, 3 * D_RET), bb),
            _seq_scratch((RET_HEADS, hd, hd), bb),
        ],
        input_output_aliases=aliases,
        compiler_params=_cparams(("arbitrary", "arbitrary")),
        name="retention",
    )(ret, h0, rc["cos"], rc["sin"], rc["dmat"], rc["eacum"], rc["toend"], rc["cdec"], p["ret_gn_w"], *extra_in)


def _ret_consts(L, pos0):
    Lv = min(CHUNK, L)
    Lp = _padded_rows(L)
    nc = L // Lv
    half = RET_HEAD_DIM // 2
    inv = ROPE_BASE ** (-jnp.arange(half, dtype=F32) / half)
    pos = pos0 + jnp.arange(L, dtype=F32)
    ang = pos[:, None] * inv[None, :]
    cos = jnp.cos(ang)
    sin = jnp.sin(ang)
    cosf = jnp.concatenate([cos, cos], axis=1)
    sinf = jnp.concatenate([-sin, sin], axis=1)
    if Lv < Lp:
        cosf = jnp.pad(cosf, ((0, Lp - Lv), (0, 0)))
        sinf = jnp.pad(sinf, ((0, Lp - Lv), (0, 0)))
    assert cosf.shape[0] == nc * Lp
    log_gamma = jnp.log1p(-jnp.exp2(-5.0 - jnp.arange(RET_HEADS, dtype=F32)))
    steps = jnp.minimum(jnp.arange(Lp) + 1, Lv).astype(F32)
    acum = log_gamma[:, None] * steps[None, :]
    causal = jnp.tril(jnp.ones((Lp, Lp), bool))
    dmat = jnp.exp(jnp.where(causal[None], acum[:, :, None] - acum[:, None, :], -jnp.inf))
    ones = jnp.ones((RET_HEADS, Lp, RET_HEAD_DIM), F32)
    eacum = jnp.exp(acum)[:, :, None] * ones
    toend = jnp.exp(acum[:, -1:] - acum)[:, :, None] * ones
    rowvalid = (jnp.arange(Lp) < Lv).astype(F32)[None, :, None]
    toend = toend * rowvalid
    cdec = jnp.exp(acum[:, -1])[:, None, None] * jnp.ones((RET_HEADS, RET_HEAD_DIM, RET_HEAD_DIM), F32)
    return dict(cos=cosf, sin=sinf, dmat=dmat, eacum=eacum, toend=toend, cdec=cdec)


def _layernorm(v, g, b):
    mu = jnp.mean(v, axis=1, keepdims=True)
    d = v - mu
    var = jnp.mean(d * d, axis=1, keepdims=True)
    return d * lax.rsqrt(var + EPS) * g + b


def _route(logits, bias):
    tm = logits.shape[0]
    lane = lax.broadcasted_iota(jnp.int32, (tm, LANES), 1)
    lanef = lane.astype(F32)
    scores = _sigmoid(logits)
    choice = scores + bias
    neg = -jnp.inf
    best = jnp.full((tm, 1), neg, F32)
    e1 = jnp.zeros((tm, 1), F32)
    e2 = jnp.zeros((tm, 1), F32)
    for g in range(N_EXPERT_GROUPS):
        ing = (lane >= g * EXPERTS_PER_GROUP) & (lane < (g + 1) * EXPERTS_PER_GROUP)
        cg = jnp.where(ing, choice, neg)
        m1 = jnp.max(cg, axis=1, keepdims=True)
        i1 = jnp.min(jnp.where(cg == m1, lanef, float(LANES)), axis=1, keepdims=True)
        cg2 = jnp.where(lanef == i1, neg, cg)
        m2 = jnp.max(cg2, axis=1, keepdims=True)
        i2 = jnp.min(jnp.where(cg2 == m2, lanef, float(LANES)), axis=1, keepdims=True)
        gs = m1 + m2
        better = gs > best
        best = jnp.where(better, gs, best)
        e1 = jnp.where(better, i1, e1)
        e2 = jnp.where(better, i2, e2)
    w1 = jnp.sum(jnp.where(lanef == e1, scores, 0.0), axis=1, keepdims=True)
    w2 = jnp.sum(jnp.where(lanef == e2, scores, 0.0), axis=1, keepdims=True)
    den = w1 + w2
    out = jnp.where(lane == 0, e1, jnp.where(lane == 1, e2, jnp.where(lane == 2, w1 / den, jnp.where(lane == 3, w2 / den, 0.0))))
    return out


def _outproj_kernel(ys_ref, yr_ref, yt_ref, x_ref, g1_ref, sc2_ref, sh2_ref, w_ref, lng_ref, lnb_ref,
                    rwh_ref, rwl_ref, rb_ref, *rest, precise, tile_rows):
    x1_ref, h2_ref, route_ref = rest[-3:]
    def part(y_ref, lo, hi):
        return _mm_w(y_ref[0], w_ref[0, lo:hi, :], w_ref[1, lo:hi, :] if precise else None)

    mix = part(ys_ref, 0, D_SSD) + part(yr_ref, D_SSD, D_SSD + D_RG) + part(yt_ref, D_SSD + D_RG, 2 * D_MODEL)
    x1 = _layernorm(ALPHA * x_ref[0] + (1.0 + g1_ref[0]) * mix, lng_ref[...], lnb_ref[...])
    x1_ref[0] = x1
    h2 = x1 * (1.0 + sc2_ref[0]) + sh2_ref[0]
    hi, lo = _split2(h2)
    if tile_rows:
        tm = h2.shape[0]
        for s in range(ROW_TILES):
            h2_ref[pl.ds(s, tm, stride=ROW_TILES), :] = h2[:, s * LANES:(s + 1) * LANES]
    else:
        h2_ref[0] = h2
    logits = _dot(hi, rwh_ref[...]) + (_dot(lo, rwh_ref[...]) + _dot(hi, rwl_ref[...]))
    route_ref[0] = _route(logits, rb_ref[...])


def _outproj(y_ssd, y_rg, y_ret, x, g1, sc2, sh2, p, consts, per_token_mod, precise, shared=None):
    B, L, _ = x.shape
    tm = _pick_tm(L, LANES if precise else PROJ_TM)
    tile_rows = not precise
    row = lambda b, i: (b, i, 0)
    fixed = lambda b, i: (0, 0)
    if per_token_mod:
        mod_spec = pl.BlockSpec((1, tm, D_MODEL), row)
    else:
        mod_spec = pl.BlockSpec((1, 1, D_MODEL), lambda b, i: (b, 0, 0))
    extra_in, extra_specs, aliases = [], [], {}
    if tile_rows:
        total, off, prev = shared
        assert off % tm == 0
        h2_spec = pl.BlockSpec((ROW_TILES * tm, LANES), lambda b, i: (off // tm + b * (L // tm) + i, 0))
        h2_shape = jax.ShapeDtypeStruct((ROW_TILES * total, LANES), F32)
        if prev is not None:
            extra_in, extra_specs, aliases = [prev], [pl.BlockSpec(memory_space=pl.ANY)], {13: 1}
    else:
        h2_spec = pl.BlockSpec((1, tm, D_MODEL), row)
        h2_shape = jax.ShapeDtypeStruct((B, L, D_MODEL), F32)
    return pl.pallas_call(
        functools.partial(_outproj_kernel, precise=precise, tile_rows=tile_rows),
        grid=(B, L // tm),
        in_specs=[
            pl.BlockSpec((1, tm, D_SSD), row),
            pl.BlockSpec((1, tm, D_RG), row),
            pl.BlockSpec((1, tm, D_RET), row),
            pl.BlockSpec((1, tm, D_MODEL), row),
            mod_spec, mod_spec, mod_spec,
            pl.BlockSpec((2, 2 * D_MODEL, D_MODEL), lambda b, i: (0, 0, 0)),
            pl.BlockSpec((1, D_MODEL), fixed),
            pl.BlockSpec((1, D_MODEL), fixed),
            pl.BlockSpec((D_MODEL, LANES), fixed),
            pl.BlockSpec((D_MODEL, LANES), fixed),
            pl.BlockSpec((1, LANES), fixed),
        ] + extra_specs,
        out_specs=[
            pl.BlockSpec((1, tm, D_MODEL), row),
            h2_spec,
            pl.BlockSpec((1, tm, LANES), row),
        ],
        out_shape=[
            jax.ShapeDtypeStruct((B, L, D_MODEL), F32),
            h2_shape,
            jax.ShapeDtypeStruct((B, L, LANES), F32),
        ],
        input_output_aliases=aliases,
        compiler_params=_cparams(("arbitrary", "arbitrary")),
        name="outproj",
    )(y_ssd, y_rg, y_ret, x, g1, sc2, sh2, p["w_out"], p["ln1_g"], p["ln1_b"],
      consts["rw_hi"], consts["rw_lo"], consts["rbias"], *extra_in)


def _row_gather_copy(src_hbm, first_row, dst_vmem, slot, r, sem):
    return pltpu.make_async_copy(src_hbm.at[pl.ds(pl.multiple_of(first_row, ROW_TILES), ROW_TILES)],
                                 dst_vmem.at[slot, pl.ds(r * ROW_TILES, ROW_TILES)], sem.at[slot])


def _moe_kernel(be_ref, nu_ref, *refs, precise, gather):
    if gather:
        rowc_ref, rown_ref, src_ref, wg_ref, wu_ref, wd_ref, o_ref = refs[:7]
        scr, xbuf, sem = refs[7:-2], refs[-2], refs[-1]
    else:
        x_ref, wg_ref, wu_ref, wd_ref, o_ref = refs[:5]
        scr = refs[5:]
    i = pl.program_id(0)
    n_used = nu_ref[0]
    used = i < n_used
    new_expert = jnp.logical_or(i == 0, be_ref[i] != be_ref[jnp.maximum(i - 1, 0)])

    if gather:
        slot = lax.rem(i, 2)

        def issue(rows_ref, s):
            for r in range(MOE_BM):
                _row_gather_copy(src_ref, rows_ref[0, 0, r], xbuf, s, r, sem).start(priority=r % 2)

        @pl.when(jnp.logical_and(used, i == 0))
        def _():
            issue(rowc_ref, 0)

        @pl.when(i + 1 < n_used)
        def _():
            issue(rown_ref, 1 - slot)

    @pl.when(jnp.logical_and(used, new_expert))
    def _():
        for k, w_ref in enumerate((wg_ref, wu_ref, wd_ref)):
            w = w_ref[...]
            if precise:
                hi, lo = _split2(w)
                scr[k][...] = hi
                scr[3 + k][...] = lo
            else:
                scr[k][...] = w.astype(BF16)

    @pl.when(used)
    def _():
        if gather:
            pltpu.make_async_copy(src_ref.at[pl.ds(0, MOE_BM * ROW_TILES)], xbuf.at[slot], sem.at[slot]).wait()
            xb = xbuf.at[slot]
            x = jnp.concatenate([xb[pl.ds(s, MOE_BM, stride=ROW_TILES), :] for s in range(ROW_TILES)], axis=1)
        else:
            x = x_ref[...]
        lo = (scr[3][...], scr[4][...], scr[5][...]) if precise else (None, None, None)
        g = _mm_w(x, scr[0][...], lo[0])
        u = _mm_w(x, scr[1][...], lo[1])
        o_ref[...] = _mm_w(_silu(g) * u, scr[2][...], lo[2])

    @pl.when(jnp.logical_not(used))
    def _():
        o_ref[...] = jnp.zeros_like(o_ref)


def _moe(xs, block_expert, n_used, weights, layer, precise, slot_rows=None):
    gather = slot_rows is not None
    P = slot_rows.shape[0] if gather else xs.shape[0]
    nblk = P // MOE_BM
    in_w = pl.BlockSpec((None, None, D_MODEL, D_EXPERT), lambda i, be, nu: (layer, be[i], 0, 0))
    out_w = pl.BlockSpec((None, None, D_EXPERT, D_MODEL), lambda i, be, nu: (layer, be[i], 0, 0))
    w_scratch = [pltpu.VMEM((D_MODEL, D_EXPERT), BF16), pltpu.VMEM((D_MODEL, D_EXPERT), BF16),
                 pltpu.VMEM((D_EXPERT, D_MODEL), BF16)] * (2 if precise else 1)
    if gather:
        rows3 = slot_rows.reshape(nblk, 1, MOE_BM)
        x_specs = [
            pl.BlockSpec((1, 1, MOE_BM), lambda i, be, nu: (i, 0, 0), memory_space=pltpu.SMEM),
            pl.BlockSpec((1, 1, MOE_BM), lambda i, be, nu: (jnp.minimum(i + 1, nblk - 1), 0, 0),
                         memory_space=pltpu.SMEM),
            pl.BlockSpec(memory_space=pl.ANY),
        ]
        x_args = [rows3, rows3, xs]
        extra_scratch = [pltpu.VMEM((2, MOE_BM * ROW_TILES, LANES), xs.dtype), pltpu.SemaphoreType.DMA((2,))]
    else:
        x_specs = [pl.BlockSpec((MOE_BM, D_MODEL), lambda i, be, nu: (i, 0))]
        x_args = [xs]
        extra_scratch = []
    grid_spec = pltpu.PrefetchScalarGridSpec(
        num_scalar_prefetch=2,
        grid=(nblk,),
        in_specs=x_specs + [in_w, in_w, out_w],
        out_specs=pl.BlockSpec((MOE_BM, D_MODEL), lambda i, be, nu: (i, 0)),
        scratch_shapes=w_scratch + extra_scratch,
    )
    return pl.pallas_call(
        functools.partial(_moe_kernel, precise=precise, gather=gather),
        grid_spec=grid_spec,
        out_shape=jax.ShapeDtypeStruct((P, D_MODEL), F32),
        compiler_params=_cparams(("arbitrary",)),
        name="moe_gather" if gather else "moe",
    )(block_expert, n_used, *x_args, *weights)


def _combine_kernel(x1_ref, ya_ref, yb_ref, route_ref, g2_ref, lng_ref, lnb_ref, o_ref):
    tm = x1_ref.shape[1]
    lane = lax.broadcasted_iota(jnp.int32, (tm, LANES), 1)
    rt = route_ref[0]
    w1 = jnp.sum(jnp.where(lane == 2, rt, 0.0), axis=1, keepdims=True)
    w2 = jnp.sum(jnp.where(lane == 3, rt, 0.0), axis=1, keepdims=True)
    moe = ya_ref[...] * w1 + yb_ref[...] * w2
    o_ref[0] = _layernorm(ALPHA * x1_ref[0] + (1.0 + g2_ref[0]) * moe, lng_ref[...], lnb_ref[...])


def _combine(x1, ya, yb, row_off, route, g2, p, per_token_mod):
    B, L, _ = x1.shape
    tm = _pick_tm(L, PROJ_TM)
    assert row_off % tm == 0
    row = lambda b, i: (b, i, 0)
    fixed = lambda b, i: (0, 0)
    flat = lambda b, i: (row_off // tm + b * (L // tm) + i, 0)
    if per_token_mod:
        mod_spec = pl.BlockSpec((1, tm, D_MODEL), row)
    else:
        mod_spec = pl.BlockSpec((1, 1, D_MODEL), lambda b, i: (b, 0, 0))
    return pl.pallas_call(
        _combine_kernel,
        grid=(B, L // tm),
        in_specs=[
            pl.BlockSpec((1, tm, D_MODEL), row),
            pl.BlockSpec((tm, D_MODEL), flat),
            pl.BlockSpec((tm, D_MODEL), flat),
            pl.BlockSpec((1, tm, LANES), row),
            mod_spec,
            pl.BlockSpec((1, D_MODEL), fixed),
            pl.BlockSpec((1, D_MODEL), fixed),
        ],
        out_specs=pl.BlockSpec((1, tm, D_MODEL), row),
        out_shape=jax.ShapeDtypeStruct((B, L, D_MODEL), F32),
        compiler_params=_cparams(("arbitrary", "arbitrary")),
        name="combine",
    )(x1, ya, yb, route, g2, p["ln2_g"], p["ln2_b"])


def _dispatch_plan(eidx):
    T = eidx.shape[0]
    A = 2 * T
    flat_e = eidx.reshape(A)
    onehot = (flat_e[:, None] == jnp.arange(N_EXPERTS, dtype=jnp.int32)[None, :]).astype(jnp.int32)
    cum = jnp.cumsum(onehot, axis=0)
    rank = jnp.take_along_axis(cum, flat_e[:, None], axis=1)[:, 0] - 1
    counts = cum[-1]
    padded = (counts + MOE_BM - 1) // MOE_BM * MOE_BM
    pad_end = jnp.cumsum(padded)
    pad_start = pad_end - padded
    dest = pad_start[flat_e] + rank
    nblk = (A + N_EXPERTS * (MOE_BM - 1) + MOE_BM - 1) // MOE_BM
    block_pos = jnp.arange(nblk, dtype=jnp.int32) * MOE_BM
    block_expert = jnp.minimum(jnp.sum(pad_end[None, :] <= block_pos[:, None], axis=1), N_EXPERTS - 1).astype(jnp.int32)
    n_used = (pad_end[-1] // MOE_BM).astype(jnp.int32).reshape(1)
    return dest, block_expert, n_used, nblk * MOE_BM


def _mods(mod_l, n_prompt, dec_seq):
    mp = mod_l[:n_prompt].reshape(n_prompt, 6, D_MODEL)
    ms = mod_l[n_prompt:].reshape(-1, 6, D_MODEL)
    prompt = [mp[:, j][:, None, :] for j in range(6)]
    sample = [jnp.repeat(ms[:, j], dec_seq, axis=0)[None] for j in range(6)]
    return prompt, sample


def kernel(x_prompt, x_sample, c_prompt, c_sample, state_ssd, state_ssd_conv, state_rglru, state_rglru_conv, state_ret, w_ada, b_ada, w_in, ssd_conv_w, ssd_conv_b, ssd_dt_bias, ssd_a_log, ssd_d, ssd_norm_w, rg_conv_w, rg_conv_b, rg_wa, rg_ba, rg_wx, rg_bx, rg_lambda, ret_gn_w, w_out, ln1_g, ln1_b, router_w, router_bias, exp_w_gate, exp_w_up, exp_w_down, ln2_g, ln2_b):
    BP, LP, _ = x_prompt.shape
    BS, LS, _ = x_sample.shape
    TP, TS = BP * LP, BS * LS

    def pad_lanes(v):
        return jnp.pad(v, ((0, 0), (0, LANES - v.shape[-1])))

    eye = jnp.eye(RG_BLOCKS, dtype=F32)

    def block_diag(w):
        return jnp.stack(_hi_lo(jnp.einsum("njk,nm->njmk", w, eye).reshape(D_RG, D_RG)))

    w_in_parts = _hi_lo(jnp.concatenate(
        [w_in[:, :, :2560], w_in[:, :, 2576:5648], w_in[:, :, 2560:2576],
         jnp.zeros((DEPTH, D_MODEL, LANES - SSD_HEADS), F32)], axis=2))

    params = []
    for l in range(DEPTH):
        params.append(dict(
            ssd_conv_w=ssd_conv_w[l], ssd_conv_b=ssd_conv_b[l][None],
            ssd_dt_bias=pad_lanes(ssd_dt_bias[l][None]), ssd_a_log=pad_lanes(ssd_a_log[l][None]),
            ssd_d_ch=jnp.repeat(ssd_d[l], SSD_HEAD_DIM)[None], ssd_norm_w=ssd_norm_w[l][None],
            rg_conv_w=rg_conv_w[l], rg_conv_b=rg_conv_b[l][None],
            rg_wa_bd=block_diag(rg_wa[l]), rg_ba=rg_ba[l][None],
            rg_wx_bd=block_diag(rg_wx[l]), rg_bx=rg_bx[l][None],
            rg_lambda=rg_lambda[l][None], ret_gn_w=ret_gn_w[l][None],
            w_out=jnp.stack(_hi_lo(w_out[l])), ln1_g=ln1_g[l][None], ln1_b=ln1_b[l][None],
            ln2_g=ln2_g[l][None], ln2_b=ln2_b[l][None],
        ))

    rw_hi, rw_lo = _hi_lo(pad_lanes(router_w))
    rbias = pad_lanes(router_bias[None])
    head_of_ch = jnp.arange(D_SSD) // SSD_HEAD_DIM
    consts = dict(
        rw_hi=rw_hi, rw_lo=rw_lo, rbias=rbias.astype(F32),
        head_expand=(jnp.arange(LANES)[:, None] == head_of_ch[None, :]).astype(BF16),
        tri=jnp.tril(jnp.ones((CHUNK, CHUNK), F32)).astype(BF16),
    )
    mod = _ada(jnp.concatenate([c_prompt, c_sample], axis=0), w_ada, b_ada)

    LM = LP - TAIL
    paths = [
        dict(x=x_prompt[:, :LM], B=BP, L=LM, per_token=False, has_state=False, precise=False, rc=_ret_consts(LM, 0.0)),
        dict(x=x_prompt[:, LM:], B=BP, L=TAIL, per_token=False, has_state=True, precise=True, rc=_ret_consts(TAIL, float(LM))),
        dict(x=x_sample.reshape(1, TS, D_MODEL), B=BS, L=LS, per_token=True, has_state=True, precise=False,
             rc=_ret_consts(LS, PAST_LEN)),
    ]
    zero_states = (jnp.zeros((BP, SSD_GROUPS, SSD_STATE, D_SSD // SSD_GROUPS), F32),
                   jnp.zeros((BP, CONV_WIDTH - 1, SSD_CONV_DIM), F32),
                   jnp.zeros((BP, 1, D_RG), F32),
                   jnp.zeros((BP, CONV_WIDTH - 1, D_RG), F32),
                   jnp.zeros((BP, RET_HEADS, RET_HEAD_DIM, RET_HEAD_DIM), F32))

    def mixer_stack(path, p, layer, mods, states, ssd_kw, ret_kw):
        B, L, per_token, precise = path["B"], path["L"], path["per_token"], path["precise"]
        z, xbc, rg, ret, dt = _inproj(path["x"], mods[1], mods[0], w_in_parts, layer, per_token, precise)
        if per_token:
            z, xbc, rg, ret, dt = [a.reshape(B, L, a.shape[-1]) for a in (z, xbc, rg, ret, dt)]
        s_ssd, s_cbuf, s_rg, s_rbuf, s_ret = states
        hs = path["has_state"]
        y_ssd, cbuf_n, ssd_n = _ssd(z, xbc, dt, s_ssd, s_cbuf, p, consts, hs, precise, **ssd_kw)
        y_rg, rbuf_n, rg_n = _rg(rg, s_rg, s_rbuf, p, hs, precise)
        y_ret, ret_n = _ret(ret, s_ret, p, path["rc"], hs, precise, **ret_kw)
        if per_token:
            y_ssd, y_rg, y_ret = [a.reshape(1, B * L, a.shape[-1]) for a in (y_ssd, y_rg, y_ret)]
        return (y_ssd, y_rg, y_ret), (ssd_n, cbuf_n, rg_n, rbuf_n, ret_n)

    expert_w = (exp_w_gate, exp_w_up, exp_w_down)

    def experts(h2_all, route_list, layer, precise):
        route_all = jnp.concatenate([r.reshape(-1, LANES) for r in route_list], axis=0)
        T_all = route_all.shape[0]
        dest, block_expert, n_used, P = _dispatch_plan(route_all[:, 0:2].astype(jnp.int32))
        tok = jnp.arange(2 * T_all, dtype=jnp.int32) // 2
        if precise:
            slot_tok = jnp.full((P,), T_all, jnp.int32).at[dest].set(tok)
            xs_sorted = jnp.concatenate([h2_all, jnp.zeros((1, D_MODEL), h2_all.dtype)], axis=0)[slot_tok]
            ys_sorted = _moe(xs_sorted, block_expert, n_used, expert_w, layer, True)
        else:
            slot_row = jnp.zeros((P,), jnp.int32).at[dest].set(tok * ROW_TILES)
            ys_sorted = _moe(h2_all, block_expert, n_used, expert_w, layer, False, slot_rows=slot_row)
        return ys_sorted[dest[0::2]], ys_sorted[dest[1::2]]

    new_tail = []
    state_ssd_t = jnp.swapaxes(state_ssd, -1, -2)
    ssd_s = ret_s = None
    cbuf_s, rg_s, rbuf_s = [], [], []
    for l in range(DEPTH):
        p = params[l]
        mods_p, mods_s = _mods(mod[l], BP, LS)
        mods = [mods_p, mods_p, mods_s]
        st_s = (state_ssd_t, state_ssd_conv[l], state_rglru[l][:, None, :], state_rglru_conv[l], state_ret)

        ys_m, st_m = mixer_stack(paths[0], p, l, mods[0], zero_states, {}, {})
        ys_t, st_t = mixer_stack(paths[1], p, l, mods[1], st_m, dict(native_out=True), {})
        ys_c, st_c = mixer_stack(paths[2], p, l, mods[2], st_s,
                                 dict(h0_layer=l, native_out=True, out_stack=(l, ssd_s)),
                                 dict(h0_layer=l, out_stack=(l, ret_s)))
        new_tail.append(st_t)
        ssd_s, ret_s = st_c[0], st_c[4]
        cbuf_s.append(st_c[1])
        rg_s.append(st_c[2])
        rbuf_s.append(st_c[3])

        offs = (0, 0, BP * LM)
        group_tokens = BP * LM + TS
        post, h2_group = [None] * 3, None
        for k in (0, 2, 1):
            path, ys, m = paths[k], (ys_m, ys_t, ys_c)[k], mods[k]
            shared = None if path["precise"] else (group_tokens, offs[k], h2_group)
            post[k] = _outproj(*ys, path["x"], m[2], m[4], m[3], p, consts, path["per_token"], path["precise"], shared)
            if not path["precise"]:
                h2_group = post[k][1]

        ya_b, yb_b = experts(h2_group, [post[0][2], post[2][2]], l, False)
        ya_t, yb_t = experts(post[1][1].reshape(-1, D_MODEL), [post[1][2]], l, True)

        for path, (x1, _, route), ya, yb, off, m in zip(paths, post, (ya_b, ya_t, ya_b), (yb_b, yb_t, yb_b), offs, mods):
            path["x"] = _combine(x1, ya, yb, off, route, m[5], p, path["per_token"])

    def stack(lst, k):
        return jnp.stack([s[k] for s in lst])

    y_prompt = jnp.concatenate([paths[0]["x"], paths[1]["x"]], axis=1)
    return (y_prompt, paths[2]["x"].reshape(BS, LS, D_MODEL),
            jnp.swapaxes(stack(new_tail, 0), -1, -2), stack(new_tail, 1), stack(new_tail, 2)[:, :, 0],
            stack(new_tail, 3), stack(new_tail, 4),
            jnp.swapaxes(ssd_s, -1, -2), jnp.stack(cbuf_s), jnp.stack(rg_s)[:, :, 0], jnp.stack(rbuf_s), ret_s)
```

```python
import functools
import math

import jax
import jax.numpy as jnp
from jax import lax
from jax.experimental import pallas as pl
from jax.experimental.pallas import tpu as pltpu

F32 = jnp.float32
BF16 = jnp.bfloat16

D_MODEL = 1024
DEPTH = 4
D_SSD = 1024
SSD_HEAD_DIM = 64
SSD_HEADS = 16
SSD_GROUPS = 2
SSD_STATE = 128
SSD_CONV_DIM = D_SSD + 2 * SSD_GROUPS * SSD_STATE
CONV_WIDTH = 4
D_RG = 512
RG_BLOCKS = 8
RG_C = 8.0
D_RET = 512
RET_HEADS = 4
RET_HEAD_DIM = 128
ROPE_BASE = 10000.0
N_EXPERTS = 32
EXPERTS_PER_GROUP = 8
N_EXPERT_GROUPS = 4
D_EXPERT = 512
ALPHA = (2 * DEPTH) ** 0.25
EPS = 1e-5
PAST_LEN = 16384.0

LANES = 128
SUBLANES = 8
CHUNK = 128
TAIL = CHUNK
SEQS_PER_STEP = 8
SEQ_UNROLL = 4
ROW_TILES = D_MODEL // LANES
PROJ_TM = 384
MOE_BM = 256
MOE_BM_TAIL = 64
VMEM_LIMIT = 56 * 1024 * 1024


def _cparams(sem):
    return pltpu.CompilerParams(dimension_semantics=sem, vmem_limit_bytes=VMEM_LIMIT)


def _padded_rows(L):
    return CHUNK if L >= CHUNK else -(-L // SUBLANES) * SUBLANES


def _pick_tm(L, cap):
    if L <= LANES:
        return L
    tm = cap - cap % LANES
    while L % tm:
        tm -= LANES
    return tm


def _sigmoid(x):
    return 1.0 / (1.0 + jnp.exp(-x))


def _silu(x):
    return x * _sigmoid(x)


def _softplus(x):
    return jnp.maximum(x, 0.0) + jnp.log1p(jnp.exp(-jnp.abs(x)))


def _split3(v):
    hi = v.astype(BF16)
    r = v - hi.astype(F32)
    mid = r.astype(BF16)
    lo = (r - mid.astype(F32)).astype(BF16)
    return hi, mid, lo


def _dot(a, b):
    return jnp.dot(a, b, preferred_element_type=F32)


def _split2(v):
    hi = v.astype(BF16)
    return hi, (v - hi.astype(F32)).astype(BF16)


def _mm_w(a, w_hi, w_lo):
    if w_lo is None:
        return _dot(a.astype(BF16), w_hi)
    ah, al = _split2(a)
    return _dot(ah, w_hi) + (_dot(al, w_hi) + _dot(ah, w_lo))


def _mm_a(a, b, precise, nt=False):
    dn = (((1,), (1,)), ((), ())) if nt else (((1,), (0,)), ((), ()))

    def d(x, y):
        return lax.dot_general(x, y, dn, preferred_element_type=F32)

    if not precise:
        return d(a.astype(BF16), b.astype(BF16))
    ah, al = _split2(a)
    bh, bl = _split2(b)
    return d(ah, bh) + (d(al, bh) + d(ah, bl))


def _hi_lo(w):
    hi32 = lax.reduce_precision(w, exponent_bits=8, mantissa_bits=7)
    return hi32.astype(BF16), (w - hi32).astype(BF16)


def _dot_exact_rhs(v, m):
    hi, mid, lo = _split3(v)
    return _dot(hi, m) + _dot(mid, m) + _dot(lo, m)


def _dot_exact_lhs(m, v):
    hi, mid, lo = _split3(v)
    return _dot(m, hi) + _dot(m, mid) + _dot(m, lo)


def _ada_kernel(c_ref, w_ref, b_ref, o_ref):
    o_ref[0] = _mm_a(_silu(c_ref[...]), w_ref[0], True) + b_ref[0]


def _ada(c_all, w_ada, b_ada):
    n = c_all.shape[0]
    tn = 1024
    return pl.pallas_call(
        _ada_kernel,
        grid=(DEPTH, 6 * D_MODEL // tn),
        in_specs=[
            pl.BlockSpec((n, D_MODEL), lambda l, j: (0, 0)),
            pl.BlockSpec((1, D_MODEL, tn), lambda l, j: (l, 0, j)),
            pl.BlockSpec((1, 1, tn), lambda l, j: (l, 0, j)),
        ],
        out_specs=pl.BlockSpec((1, n, tn), lambda l, j: (l, 0, j)),
        out_shape=jax.ShapeDtypeStruct((DEPTH, n, 6 * D_MODEL), F32),
        compiler_params=_cparams(("arbitrary", "arbitrary")),
        name="ada",
    )(c_all, w_ada, b_ada.reshape(DEPTH, 1, 6 * D_MODEL))


_IN_SEGS = (("z", 0, 1024), ("xbc", 1024, 2560), ("rg", 2560, 3584), ("ret", 3584, 5632), ("dt", 5632, 5760))
IN_PAD = 5760


def _inproj_kernel(x_ref, sc_ref, sh_ref, w_ref, *rest, precise):
    wl_ref = rest[0] if precise else None
    outs = rest[1:] if precise else rest
    h = x_ref[0] * (1.0 + sc_ref[0]) + sh_ref[0]
    if precise:
        hh, hl = _split2(h)
    else:
        hh = h.astype(BF16)
    for o_ref, (_, lo, hi) in zip(outs, _IN_SEGS):
        acc = _dot(hh, w_ref[:, lo:hi])
        if precise:
            acc = acc + (_dot(hl, w_ref[:, lo:hi]) + _dot(hh, wl_ref[:, lo:hi]))
        o_ref[0] = acc


def _inproj(x, sc, sh, w, layer, per_token_mod, precise):
    B, L, _ = x.shape
    tm = _pick_tm(L, LANES if precise else PROJ_TM)
    if per_token_mod:
        mod_spec = pl.BlockSpec((1, tm, D_MODEL), lambda b, i: (b, i, 0))
    else:
        mod_spec = pl.BlockSpec((1, 1, D_MODEL), lambda b, i: (b, 0, 0))
    widths = [hi - lo for _, lo, hi in _IN_SEGS]
    w_spec = pl.BlockSpec((None, D_MODEL, IN_PAD), lambda b, i: (layer, 0, 0), pipeline_mode=pl.Buffered(1))
    ws = w if precise else w[:1]
    return pl.pallas_call(
        functools.partial(_inproj_kernel, precise=precise),
        grid=(B, L // tm),
        in_specs=[
            pl.BlockSpec((1, tm, D_MODEL), lambda b, i: (b, i, 0)),
            mod_spec,
            mod_spec,
        ] + [w_spec] * len(ws),
        out_specs=[pl.BlockSpec((1, tm, wd), lambda b, i: (b, i, 0)) for wd in widths],
        out_shape=[jax.ShapeDtypeStruct((B, L, wd), F32) for wd in widths],
        compiler_params=_cparams(("arbitrary", "arbitrary")),
        name="inproj",
    )(x, sc, sh, *ws)


def _conv_step(src_rows, buf_ref, cw_ref, cb_ref, nbuf_ref, xp_s, *, c, nc, Lv, Lp, has_state):
    @_when_unless(nc == 1, c == 0)
    def _():
        xp_s[...] = jnp.zeros_like(xp_s)
        if has_state:
            xp_s[SUBLANES - (CONV_WIDTH - 1):SUBLANES, :] = buf_ref[0]

    xp_s[SUBLANES:SUBLANES + Lv, :] = src_rows
    base = SUBLANES - (CONV_WIDTH - 1)
    xc = cb_ref[...]
    for j in range(CONV_WIDTH):
        xc = xc + cw_ref[j:j + 1, :] * xp_s[base + j:base + j + Lp, :]

    @_when_unless(nc == 1, c == nc - 1)
    def _():
        nbuf_ref[0] = xp_s[SUBLANES + Lv - (CONV_WIDTH - 1):SUBLANES + Lv, :]

    if nc > 1:
        xp_s[0:SUBLANES, :] = xp_s[Lp:Lp + SUBLANES, :]
    return xc


def _for_each_sequence(body, refs, bb, keep, squeeze, n_scratch):
    n = len(refs)

    def one(s, j):
        v = list(refs)
        for k in keep:
            v[k] = refs[k].at[pl.ds(s, 1)]
        for k in squeeze:
            v[k] = refs[k].at[s]
        if bb > 1:
            for k in range(n - n_scratch, n):
                v[k] = refs[k].at[j]
        body(*v)

    if bb == 1:
        one(0, 0)
    else:
        def step(i, carry):
            for j in range(SEQ_UNROLL):
                one(i * SEQ_UNROLL + j, j)
            return carry
        lax.fori_loop(0, bb // SEQ_UNROLL, step, 0)


def _when_unless(always, cond):
    def deco(fn):
        if always:
            fn()
        else:
            pl.when(cond)(fn)
        return fn
    return deco


def _seq_scratch(shape, bb):
    return pltpu.VMEM(((SEQ_UNROLL,) + shape) if bb > 1 else shape, F32)


def _ssd_kernel(*refs, bb, **kw):
    n = len(refs)
    _for_each_sequence(functools.partial(_ssd_body, **kw), refs, bb,
                       keep=(0, 1, 2, 4, n - 6, n - 5), squeeze=(3, n - 4), n_scratch=3)


def _ssd_body(z_ref, xbc_ref, dt_ref, h0_ref, buf_ref, cw_ref, cb_ref, dtb_ref, alog_ref, dch_ref,
              nw_ref, e_ref, tri_ref, *rest, Lv, Lp, nc, has_state, precise, native_in, native_out):
    y_ref, nbuf_ref, hn_ref, xp_s, dtp_s, s_s = rest[-6:]
    c = pl.program_id(1)
    hpg = SSD_HEADS // SSD_GROUPS

    @_when_unless(nc == 1, c == 0)
    def _():
        if not has_state:
            s_s[...] = jnp.zeros_like(s_s)
        elif native_in:
            for g in range(SSD_GROUPS):
                s_s[g] = jnp.concatenate([h0_ref[g * hpg + i] for i in range(hpg)], axis=0).T
        else:
            s_s[...] = h0_ref[...]
        if Lv < Lp:
            dtp_s[...] = jnp.zeros_like(dtp_s)

    xc = _conv_step(xbc_ref[0], buf_ref, cw_ref, cb_ref, nbuf_ref, xp_s,
                    c=c, nc=nc, Lv=Lv, Lp=Lp, has_state=has_state)
    xbc = _silu(xc)
    xs = xbc[:, 0:D_SSD]
    Bm = xbc[:, D_SSD:D_SSD + 256]
    Cm = xbc[:, D_SSD + 256:D_SSD + 512]

    if Lv < Lp:
        dtp_s[0:Lv, :] = dt_ref[0]
        dtr = dtp_s[...]
    else:
        dtr = dt_ref[0]
    dt = _softplus(dtr + dtb_ref[...])
    if Lv < Lp:
        rowv = lax.broadcasted_iota(jnp.int32, (Lp, LANES), 0)
        dt = jnp.where(rowv < Lv, dt, 0.0)
    a = dt * (-jnp.exp(alog_ref[...]))
    acum = _dot_exact_lhs(tri_ref[...], a)
    dt_e = _dot_exact_rhs(dt, e_ref[...])
    ac_e = _dot_exact_rhs(acum, e_ref[...])
    acl_e = ac_e[Lp - 1:Lp, :]
    acum_t = acum.T

    vdt = xs * dt_e
    lane = lax.broadcasted_iota(jnp.int32, (Lp, LANES), 1)
    causal = lax.broadcasted_iota(jnp.int32, (Lp, Lp), 0) >= lax.broadcasted_iota(jnp.int32, (Lp, Lp), 1)
    mid_t = F32 if precise else BF16

    y_blocks = []
    for g in range(SSD_GROUPS):
        Cg = Cm[:, g * SSD_STATE:(g + 1) * SSD_STATE]
        Bg = Bm[:, g * SSD_STATE:(g + 1) * SSD_STATE]
        G = _mm_a(Cg, Bg, precise, nt=True)
        for jj in range(4):
            j = 4 * g + jj
            scs = []
            for h in (2 * j, 2 * j + 1):
                col = jnp.sum(jnp.where(lane == h, acum, 0.0), axis=1, keepdims=True)
                rw = acum_t[h:h + 1, :]
                dec = jnp.exp(jnp.where(causal, col - rw, -jnp.inf))
                scs.append((G * dec).astype(mid_t))
            sc = jnp.concatenate(scs, axis=1)
            vb = vdt[:, j * LANES:(j + 1) * LANES]
            v2 = jnp.concatenate([jnp.where(lane < SSD_HEAD_DIM, vb, 0.0),
                                  jnp.where(lane >= SSD_HEAD_DIM, vb, 0.0)], axis=0).astype(mid_t)
            y_blocks.append(_mm_a(sc, v2, precise))
    y = jnp.concatenate(y_blocks, axis=1)

    eac = jnp.exp(ac_e)
    vw = vdt * jnp.exp(acl_e - ac_e)
    cd = jnp.exp(acl_e)
    half = D_SSD // SSD_GROUPS
    ys_parts = []
    for g in range(SSD_GROUPS):
        Sg = s_s[g]
        ys_parts.append(_mm_a(Cm[:, g * SSD_STATE:(g + 1) * SSD_STATE], Sg, precise))
        BgT = Bm[:, g * SSD_STATE:(g + 1) * SSD_STATE].T
        s_s[g] = cd[:, g * half:(g + 1) * half] * Sg + _mm_a(BgT, vw[:, g * half:(g + 1) * half], precise)
    y = y + eac * jnp.concatenate(ys_parts, axis=1) + dch_ref[...] * xs

    yv = y[0:Lv, :] if Lv < Lp else y
    yz = yv * _silu(z_ref[0])
    outs = []
    for g in range(SSD_GROUPS):
        part = yz[:, g * half:(g + 1) * half]
        ms = jnp.mean(part * part, axis=1, keepdims=True)
        outs.append(part * lax.rsqrt(ms + EPS))
    y_ref[0] = jnp.concatenate(outs, axis=1) * nw_ref[...]

    @_when_unless(nc == 1, c == nc - 1)
    def _():
        if native_out:
            for g in range(SSD_GROUPS):
                sgt = s_s[g].T
                for i in range(hpg):
                    hn_ref[g * hpg + i] = sgt[i * SSD_HEAD_DIM:(i + 1) * SSD_HEAD_DIM, :]
        else:
            hn_ref[...] = s_s[...]


def _state_specs(B, bb, shape, in_layer, out_stack):
    zeros = (0,) * len(shape)
    if in_layer is None:
        in_spec = pl.BlockSpec((bb,) + shape, lambda b, c: (b,) + zeros)
    else:
        in_spec = pl.BlockSpec((None, bb) + shape, lambda b, c: (in_layer, b) + zeros)
    if out_stack is None:
        out_spec = pl.BlockSpec((bb,) + shape, lambda b, c: (b,) + zeros)
        out_shape = jax.ShapeDtypeStruct((B,) + shape, F32)
    else:
        l = out_stack[0]
        out_spec = pl.BlockSpec((None, bb) + shape, lambda b, c: (l, b) + zeros)
        out_shape = jax.ShapeDtypeStruct((DEPTH, B) + shape, F32)
    return in_spec, out_spec, out_shape


def _seqs_per_step(B, L):
    return SEQS_PER_STEP if (L < CHUNK and B % SEQS_PER_STEP == 0) else 1


def _ssd(z, xbc, dt, h0, buf, p, consts, has_state, precise, h0_layer=None, native_out=False, out_stack=None):
    B, L, _ = z.shape
    Lv = min(CHUNK, L)
    Lp = _padded_rows(L)
    nc = L // Lv
    native_in = h0_layer is not None
    bb = _seqs_per_step(B, L)
    kern = functools.partial(_ssd_kernel, bb=bb, Lv=Lv, Lp=Lp, nc=nc, has_state=has_state, precise=precise,
                             native_in=native_in, native_out=native_out)
    row = lambda b, c: (b, c, 0)
    per_b3 = lambda b, c: (b, 0, 0)
    fixed = lambda b, c: (0, 0)
    half = D_SSD // SSD_GROUPS
    native = (SSD_HEADS, SSD_HEAD_DIM, SSD_STATE)
    packed = (SSD_GROUPS, SSD_STATE, half)
    h0_spec, _, _ = _state_specs(B, bb, native if native_in else packed, h0_layer, None)
    _, hn_spec, hn_shape = _state_specs(B, bb, native if native_out else packed, None, out_stack)
    prev = out_stack[1] if out_stack is not None else None
    extra_in, extra_specs, aliases = [], [], {}
    if prev is not None:
        extra_in, extra_specs, aliases = [prev], [pl.BlockSpec(memory_space=pl.ANY)], {13: 2}
    return pl.pallas_call(
        kern,
        grid=(B // bb, nc),
        in_specs=[
            pl.BlockSpec((bb, Lv, D_SSD), row),
            pl.BlockSpec((bb, Lv, SSD_CONV_DIM), row),
            pl.BlockSpec((bb, Lv, LANES), row),
            h0_spec,
            pl.BlockSpec((bb, CONV_WIDTH - 1, SSD_CONV_DIM), per_b3),
            pl.BlockSpec((CONV_WIDTH, SSD_CONV_DIM), fixed),
            pl.BlockSpec((1, SSD_CONV_DIM), fixed),
            pl.BlockSpec((1, LANES), fixed),
            pl.BlockSpec((1, LANES), fixed),
            pl.BlockSpec((1, D_SSD), fixed),
            pl.BlockSpec((1, D_SSD), fixed),
            pl.BlockSpec((LANES, D_SSD), fixed),
            pl.BlockSpec((Lp, Lp), fixed),
        ] + extra_specs,
        out_specs=[
            pl.BlockSpec((bb, Lv, D_SSD), row),
            pl.BlockSpec((bb, CONV_WIDTH - 1, SSD_CONV_DIM), per_b3),
            hn_spec,
        ],
        out_shape=[
            jax.ShapeDtypeStruct((B, L, D_SSD), F32),
            jax.ShapeDtypeStruct((B, CONV_WIDTH - 1, SSD_CONV_DIM), F32),
            hn_shape,
        ],
        scratch_shapes=[
            _seq_scratch((SUBLANES + Lp, SSD_CONV_DIM), bb),
            _seq_scratch((Lp, LANES), bb),
            _seq_scratch((SSD_GROUPS, SSD_STATE, half), bb),
        ],
        input_output_aliases=aliases,
        compiler_params=_cparams(("arbitrary", "arbitrary")),
        name="ssd",
    )(z, xbc, dt, h0, buf, p["ssd_conv_w"], p["ssd_conv_b"], p["ssd_dt_bias"], p["ssd_a_log"],
      p["ssd_d_ch"], p["ssd_norm_w"], consts["head_expand"],
      jnp.tril(jnp.ones((Lp, Lp), F32)).astype(BF16), *extra_in)


def _rg_kernel(*refs, bb, **kw):
    n = len(refs)
    _for_each_sequence(functools.partial(_rg_body, **kw), refs, bb,
                       keep=(0, 1, 2, n - 5, n - 4, n - 3), squeeze=(), n_scratch=2)


def _rg_body(rg_ref, h0_ref, buf_ref, cw_ref, cb_ref, wa_ref, ba_ref, wx_ref, bx_ref, lam_ref,
             y_ref, nbuf_ref, hn_ref, xp_s, hc_s, *, Lv, Lp, nc, has_state, precise):
    c = pl.program_id(1)

    @_when_unless(nc == 1, c == 0)
    def _():
        if has_state:
            hc_s[...] = h0_ref[0]
        else:
            hc_s[...] = jnp.zeros_like(hc_s)

    xc = _conv_step(rg_ref[0, :, 0:D_RG], buf_ref, cw_ref, cb_ref, nbuf_ref, xp_s,
                    c=c, nc=nc, Lv=Lv, Lp=Lp, has_state=has_state)
    r = _sigmoid(_mm_w(xc, wa_ref[0], wa_ref[1] if precise else None) + ba_ref[...])
    i = _sigmoid(_mm_w(xc, wx_ref[0], wx_ref[1] if precise else None) + bx_ref[...])
    la = (-RG_C * _softplus(-lam_ref[...])) * r
    u = jnp.sqrt(-jnp.tanh(la) * (jnp.exp(2.0 * la) + 1.0)) * (i * xc)
    row = lax.broadcasted_iota(jnp.int32, (Lp, D_RG), 0)
    if Lv < Lp:
        valid = row < Lv
        la = jnp.where(valid, la, 0.0)
        u = jnp.where(valid, u, 0.0)
    a = jnp.exp(la)
    s = 1
    while s < Lp:
        m = row >= s
        u_sh = jnp.where(m, pltpu.roll(u, s, 0), 0.0)
        a_sh = jnp.where(m, pltpu.roll(a, s, 0), 1.0)
        u = u + a * u_sh
        a = a * a_sh
        s *= 2
    h = u + a * hc_s[...]
    hc_s[...] = h[Lp - 1:Lp, :]
    hv = h[0:Lv, :] if Lv < Lp else h
    y_ref[0] = hv * jax.nn.gelu(rg_ref[0, :, D_RG:2 * D_RG])

    @_when_unless(nc == 1, c == nc - 1)
    def _():
        hn_ref[0] = hc_s[...]


def _rg(rg, h0, buf, p, has_state, precise):
    B, L, _ = rg.shape
    Lv = min(CHUNK, L)
    Lp = _padded_rows(L)
    nc = L // Lv
    bb = _seqs_per_step(B, L)
    kern = functools.partial(_rg_kernel, bb=bb, Lv=Lv, Lp=Lp, nc=nc, has_state=has_state, precise=precise)
    row = lambda b, c: (b, c, 0)
    per_b3 = lambda b, c: (b, 0, 0)
    fixed = lambda b, c: (0, 0)
    fixed3 = lambda b, c: (0, 0, 0)
    return pl.pallas_call(
        kern,
        grid=(B // bb, nc),
        in_specs=[
            pl.BlockSpec((bb, Lv, 2 * D_RG), row),
            pl.BlockSpec((bb, 1, D_RG), per_b3),
            pl.BlockSpec((bb, CONV_WIDTH - 1, D_RG), per_b3),
            pl.BlockSpec((CONV_WIDTH, D_RG), fixed),
            pl.BlockSpec((1, D_RG), fixed),
            pl.BlockSpec((2, D_RG, D_RG), fixed3),
            pl.BlockSpec((1, D_RG), fixed),
            pl.BlockSpec((2, D_RG, D_RG), fixed3),
            pl.BlockSpec((1, D_RG), fixed),
            pl.BlockSpec((1, D_RG), fixed),
        ],
        out_specs=[
            pl.BlockSpec((bb, Lv, D_RG), row),
            pl.BlockSpec((bb, CONV_WIDTH - 1, D_RG), per_b3),
            pl.BlockSpec((bb, 1, D_RG), per_b3),
        ],
        out_shape=[
            jax.ShapeDtypeStruct((B, L, D_RG), F32),
            jax.ShapeDtypeStruct((B, CONV_WIDTH - 1, D_RG), F32),
            jax.ShapeDtypeStruct((B, 1, D_RG), F32),
        ],
        scratch_shapes=[
            _seq_scratch((SUBLANES + Lp, D_RG), bb),
            _seq_scratch((1, D_RG), bb),
        ],
        compiler_params=_cparams(("arbitrary", "arbitrary")),
        name="rglru",
    )(rg, h0, buf, p["rg_conv_w"], p["rg_conv_b"], p["rg_wa_bd"], p["rg_ba"], p["rg_wx_bd"],
      p["rg_bx"], p["rg_lambda"])


def _ret_kernel(*refs, bb, **kw):
    n = len(refs)
    _for_each_sequence(functools.partial(_ret_body, **kw), refs, bb, keep=(0, n - 4), squeeze=(1, n - 3),
                       n_scratch=2)


def _ret_body(ret_ref, h0_ref, cos_ref, sin_ref, dm_ref, ea_ref, te_ref, cd_ref, gn_ref,
              *rest, Lv, Lp, nc, has_state, precise):
    y_ref, hn_ref, pad_s, s_s = rest[-4:]
    c = pl.program_id(1)

    @_when_unless(nc == 1, c == 0)
    def _():
        if has_state:
            s_s[...] = h0_ref[...]
        else:
            s_s[...] = jnp.zeros_like(s_s)
        if Lv < Lp:
            pad_s[...] = jnp.zeros_like(pad_s)

    if Lv < Lp:
        pad_s[0:Lv, :] = ret_ref[0, :, 0:3 * D_RET]
        qkv = pad_s[...]
    else:
        qkv = ret_ref[0, :, 0:3 * D_RET]
    cosf = cos_ref[...]
    sinf = sin_ref[...]
    outs = []
    for h in range(RET_HEADS):
        sl = slice(h * RET_HEAD_DIM, (h + 1) * RET_HEAD_DIM)
        q = qkv[:, sl]
        k = qkv[:, D_RET + h * RET_HEAD_DIM:D_RET + (h + 1) * RET_HEAD_DIM]
        v = qkv[:, 2 * D_RET + h * RET_HEAD_DIM:2 * D_RET + (h + 1) * RET_HEAD_DIM]
        q = q * cosf + pltpu.roll(q, RET_HEAD_DIM // 2, 1) * sinf
        k = (k * cosf + pltpu.roll(k, RET_HEAD_DIM // 2, 1) * sinf) * (RET_HEAD_DIM ** -0.5)
        G = _mm_a(q, k, precise, nt=True)
        S = s_s[h]
        y = _mm_a(G * dm_ref[h], v, precise) + _mm_a(q * ea_ref[h], S, precise)
        s_s[h] = cd_ref[h] * S + _mm_a((k * te_ref[h]).T, v, precise)
        yv = y[0:Lv, :] if Lv < Lp else y
        mu = jnp.mean(yv, axis=1, keepdims=True)
        d = yv - mu
        var = jnp.mean(d * d, axis=1, keepdims=True)
        outs.append(d * lax.rsqrt(var + EPS))
    yn = jnp.concatenate(outs, axis=1) * gn_ref[...]
    y_ref[0] = _silu(ret_ref[0, :, 3 * D_RET:4 * D_RET]) * yn

    @_when_unless(nc == 1, c == nc - 1)
    def _():
        hn_ref[...] = s_s[...]


def _ret(ret, h0, p, rc, has_state, precise, h0_layer=None, out_stack=None):
    B, L, _ = ret.shape
    Lv = min(CHUNK, L)
    Lp = _padded_rows(L)
    nc = L // Lv
    bb = _seqs_per_step(B, L)
    kern = functools.partial(_ret_kernel, bb=bb, Lv=Lv, Lp=Lp, nc=nc, has_state=has_state, precise=precise)
    row = lambda b, c: (b, c, 0)
    fixed2 = lambda b, c: (0, 0)
    fixed3 = lambda b, c: (0, 0, 0)
    hd = RET_HEAD_DIM
    h0_spec, _, _ = _state_specs(B, bb, (RET_HEADS, hd, hd), h0_layer, None)
    _, hn_spec, hn_shape = _state_specs(B, bb, (RET_HEADS, hd, hd), None, out_stack)
    prev = out_stack[1] if out_stack is not None else None
    extra_in, extra_specs, aliases = [], [], {}
    if prev is not None:
        extra_in, extra_specs, aliases = [prev], [pl.BlockSpec(memory_space=pl.ANY)], {9: 1}
    return pl.pallas_call(
        kern,
        grid=(B // bb, nc),
        in_specs=[
            pl.BlockSpec((bb, Lv, 4 * D_RET), row),
            h0_spec,
            pl.BlockSpec((Lp, hd), lambda b, c: (c, 0)),
            pl.BlockSpec((Lp, hd), lambda b, c: (c, 0)),
            pl.BlockSpec((RET_HEADS, Lp, Lp), fixed3),
            pl.BlockSpec((RET_HEADS, Lp, hd), fixed3),
            pl.BlockSpec((RET_HEADS, Lp, hd), fixed3),
            pl.BlockSpec((RET_HEADS, hd, hd), fixed3),
            pl.BlockSpec((1, D_RET), fixed2),
        ] + extra_specs,
        out_specs=[
            pl.BlockSpec((bb, Lv, D_RET), row),
            hn_spec,
        ],
        out_shape=[
            jax.ShapeDtypeStruct((B, L, D_RET), F32),
            hn_shape,
        ],
        scratch_shapes=[
            _seq_scratch((Lp, 3 * D_RET), bb),
            _seq_scratch((RET_HEADS, hd, hd), bb),
        ],
        input_output_aliases=aliases,
        compiler_params=_cparams(("arbitrary", "arbitrary")),
        name="retention",
    )(ret, h0, rc["cos"], rc["sin"], rc["dmat"], rc["eacum"], rc["toend"], rc["cdec"], p["ret_gn_w"], *extra_in)


def _ret_consts(L, pos0):
    Lv = min(CHUNK, L)
    Lp = _padded_rows(L)
    nc = L // Lv
    half = RET_HEAD_DIM // 2
    inv = ROPE_BASE ** (-jnp.arange(half, dtype=F32) / half)
    pos = pos0 + jnp.arange(L, dtype=F32)
    ang = pos[:, None] * inv[None, :]
    cos = jnp.cos(ang)
    sin = jnp.sin(ang)
    cosf = jnp.concatenate([cos, cos], axis=1)
    sinf = jnp.concatenate([-sin, sin], axis=1)
    if Lv < Lp:
        cosf = jnp.pad(cosf, ((0, Lp - Lv), (0, 0)))
        sinf = jnp.pad(sinf, ((0, Lp - Lv), (0, 0)))
    assert cosf.shape[0] == nc * Lp
    log_gamma = jnp.log1p(-jnp.exp2(-5.0 - jnp.arange(RET_HEADS, dtype=F32)))
    steps = jnp.minimum(jnp.arange(Lp) + 1, Lv).astype(F32)
    acum = log_gamma[:, None] * steps[None, :]
    causal = jnp.tril(jnp.ones((Lp, Lp), bool))
    dmat = jnp.exp(jnp.where(causal[None], acum[:, :, None] - acum[:, None, :], -jnp.inf))
    ones = jnp.ones((RET_HEADS, Lp, RET_HEAD_DIM), F32)
    eacum = jnp.exp(acum)[:, :, None] * ones
    toend = jnp.exp(acum[:, -1:] - acum)[:, :, None] * ones
    rowvalid = (jnp.arange(Lp) < Lv).astype(F32)[None, :, None]
    toend = toend * rowvalid
    cdec = jnp.exp(acum[:, -1])[:, None, None] * jnp.ones((RET_HEADS, RET_HEAD_DIM, RET_HEAD_DIM), F32)
    return dict(cos=cosf, sin=sinf, dmat=dmat, eacum=eacum, toend=toend, cdec=cdec)


def _layernorm(v, g, b):
    mu = jnp.mean(v, axis=1, keepdims=True)
    d = v - mu
    var = jnp.mean(d * d, axis=1, keepdims=True)
    return d * lax.rsqrt(var + EPS) * g + b


def _route(logits, bias):
    tm = logits.shape[0]
    lane = lax.broadcasted_iota(jnp.int32, (tm, LANES), 1)
    lanef = lane.astype(F32)
    scores = _sigmoid(logits)
    choice = scores + bias
    neg = -jnp.inf
    best = jnp.full((tm, 1), neg, F32)
    e1 = jnp.zeros((tm, 1), F32)
    e2 = jnp.zeros((tm, 1), F32)
    for g in range(N_EXPERT_GROUPS):
        ing = (lane >= g * EXPERTS_PER_GROUP) & (lane < (g + 1) * EXPERTS_PER_GROUP)
        cg = jnp.where(ing, choice, neg)
        m1 = jnp.max(cg, axis=1, keepdims=True)
        i1 = jnp.min(jnp.where(cg == m1, lanef, float(LANES)), axis=1, keepdims=True)
        cg2 = jnp.where(lanef == i1, neg, cg)
        m2 = jnp.max(cg2, axis=1, keepdims=True)
        i2 = jnp.min(jnp.where(cg2 == m2, lanef, float(LANES)), axis=1, keepdims=True)
        gs = m1 + m2
        better = gs > best
        best = jnp.where(better, gs, best)
        e1 = jnp.where(better, i1, e1)
        e2 = jnp.where(better, i2, e2)
    w1 = jnp.sum(jnp.where(lanef == e1, scores, 0.0), axis=1, keepdims=True)
    w2 = jnp.sum(jnp.where(lanef == e2, scores, 0.0), axis=1, keepdims=True)
    den = w1 + w2
    out = jnp.where(lane == 0, e1, jnp.where(lane == 1, e2, jnp.where(lane == 2, w1 / den, jnp.where(lane == 3, w2 / den, 0.0))))
    return out


def _outproj_kernel(ys_ref, yr_ref, yt_ref, x_ref, g1_ref, sc2_ref, sh2_ref, w_ref, lng_ref, lnb_ref,
                    rwh_ref, rwl_ref, rb_ref, *rest, precise, tile_rows):
    x1_ref, h2_ref, route_ref = rest[-3:]
    def part(y_ref, lo, hi):
        return _mm_w(y_ref[0], w_ref[0, lo:hi, :], w_ref[1, lo:hi, :] if precise else None)

    mix = part(ys_ref, 0, D_SSD) + part(yr_ref, D_SSD, D_SSD + D_RG) + part(yt_ref, D_SSD + D_RG, 2 * D_MODEL)
    x1 = _layernorm(ALPHA * x_ref[0] + (1.0 + g1_ref[0]) * mix, lng_ref[...], lnb_ref[...])
    x1_ref[0] = x1
    h2 = x1 * (1.0 + sc2_ref[0]) + sh2_ref[0]
    hi, lo = _split2(h2)
    if tile_rows:
        tm = h2.shape[0]
        for s in range(ROW_TILES):
            h2_ref[pl.ds(s, tm, stride=ROW_TILES), :] = h2[:, s * LANES:(s + 1) * LANES]
    else:
        h2_ref[0] = h2
    logits = _dot(hi, rwh_ref[...]) + (_dot(lo, rwh_ref[...]) + _dot(hi, rwl_ref[...]))
    route_ref[0] = _route(logits, rb_ref[...])


def _outproj(y_ssd, y_rg, y_ret, x, g1, sc2, sh2, p, consts, per_token_mod, precise, shared=None):
    B, L, _ = x.shape
    tm = _pick_tm(L, LANES if precise else PROJ_TM)
    tile_rows = not precise
    row = lambda b, i: (b, i, 0)
    fixed = lambda b, i: (0, 0)
    if per_token_mod:
        mod_spec = pl.BlockSpec((1, tm, D_MODEL), row)
    else:
        mod_spec = pl.BlockSpec((1, 1, D_MODEL), lambda b, i: (b, 0, 0))
    extra_in, extra_specs, aliases = [], [], {}
    if tile_rows:
        total, off, prev = shared
        assert off % tm == 0
        h2_spec = pl.BlockSpec((ROW_TILES * tm, LANES), lambda b, i: (off // tm + b * (L // tm) + i, 0))
        h2_shape = jax.ShapeDtypeStruct((ROW_TILES * total, LANES), F32)
        if prev is not None:
            extra_in, extra_specs, aliases = [prev], [pl.BlockSpec(memory_space=pl.ANY)], {13: 1}
    else:
        h2_spec = pl.BlockSpec((1, tm, D_MODEL), row)
        h2_shape = jax.ShapeDtypeStruct((B, L, D_MODEL), F32)
    return pl.pallas_call(
        functools.partial(_outproj_kernel, precise=precise, tile_rows=tile_rows),
        grid=(B, L // tm),
        in_specs=[
            pl.BlockSpec((1, tm, D_SSD), row),
            pl.BlockSpec((1, tm, D_RG), row),
            pl.BlockSpec((1, tm, D_RET), row),
            pl.BlockSpec((1, tm, D_MODEL), row),
            mod_spec, mod_spec, mod_spec,
            pl.BlockSpec((2, 2 * D_MODEL, D_MODEL), lambda b, i: (0, 0, 0)),
            pl.BlockSpec((1, D_MODEL), fixed),
            pl.BlockSpec((1, D_MODEL), fixed),
            pl.BlockSpec((D_MODEL, LANES), fixed),
            pl.BlockSpec((D_MODEL, LANES), fixed),
            pl.BlockSpec((1, LANES), fixed),
        ] + extra_specs,
        out_specs=[
            pl.BlockSpec((1, tm, D_MODEL), row),
            h2_spec,
            pl.BlockSpec((1, tm, LANES), row),
        ],
        out_shape=[
            jax.ShapeDtypeStruct((B, L, D_MODEL), F32),
            h2_shape,
            jax.ShapeDtypeStruct((B, L, LANES), F32),
        ],
        input_output_aliases=aliases,
        compiler_params=_cparams(("arbitrary", "arbitrary")),
        name="outproj",
    )(y_ssd, y_rg, y_ret, x, g1, sc2, sh2, p["w_out"], p["ln1_g"], p["ln1_b"],
      consts["rw_hi"], consts["rw_lo"], consts["rbias"], *extra_in)


def _row_gather_copy(src_hbm, first_row, dst_vmem, slot, r, sem):
    return pltpu.make_async_copy(src_hbm.at[pl.ds(pl.multiple_of(first_row, ROW_TILES), ROW_TILES)],
                                 dst_vmem.at[slot, pl.ds(r * ROW_TILES, ROW_TILES)], sem.at[slot])


def _moe_kernel(be_ref, nu_ref, *refs, precise, gather):
    if gather:
        rowc_ref, rown_ref, src_ref, wg_ref, wu_ref, wd_ref, o_ref = refs[:7]
        scr, xbuf, sem = refs[7:-2], refs[-2], refs[-1]
    else:
        x_ref, wg_ref, wu_ref, wd_ref, o_ref = refs[:5]
        scr = refs[5:]
    i = pl.program_id(0)
    n_used = nu_ref[0]
    used = i < n_used
    new_expert = jnp.logical_or(i == 0, be_ref[i] != be_ref[jnp.maximum(i - 1, 0)])

    if gather:
        slot = lax.rem(i, 2)

        def issue(rows_ref, s):
            for r in range(MOE_BM):
                _row_gather_copy(src_ref, rows_ref[0, 0, r], xbuf, s, r, sem).start(priority=r % 2)

        @pl.when(jnp.logical_and(used, i == 0))
        def _():
            issue(rowc_ref, 0)

        @pl.when(i + 1 < n_used)
        def _():
            issue(rown_ref, 1 - slot)

    @pl.when(jnp.logical_and(used, new_expert))
    def _():
        for k, w_ref in enumerate((wg_ref, wu_ref, wd_ref)):
            w = w_ref[...]
            if precise:
                hi, lo = _split2(w)
                scr[k][...] = hi
                scr[3 + k][...] = lo
            else:
                scr[k][...] = w.astype(BF16)

    @pl.when(used)
    def _():
        if gather:
            pltpu.make_async_copy(src_ref.at[pl.ds(0, MOE_BM * ROW_TILES)], xbuf.at[slot], sem.at[slot]).wait()
            xb = xbuf.at[slot]
            x = jnp.concatenate([xb[pl.ds(s, MOE_BM, stride=ROW_TILES), :] for s in range(ROW_TILES)], axis=1)
        else:
            x = x_ref[...]
        lo = (scr[3][...], scr[4][...], scr[5][...]) if precise else (None, None, None)
        g = _mm_w(x, scr[0][...], lo[0])
        u = _mm_w(x, scr[1][...], lo[1])
        o_ref[...] = _mm_w(_silu(g) * u, scr[2][...], lo[2])

    @pl.when(jnp.logical_not(used))
    def _():
        o_ref[...] = jnp.zeros_like(o_ref)


def _moe(xs, block_expert, n_used, weights, layer, precise, slot_rows=None, bm=MOE_BM):
    gather = slot_rows is not None
    P = slot_rows.shape[0] if gather else xs.shape[0]
    bm = MOE_BM if gather else bm
    nblk = P // bm
    in_w = pl.BlockSpec((None, None, D_MODEL, D_EXPERT), lambda i, be, nu: (layer, be[i], 0, 0))
    out_w = pl.BlockSpec((None, None, D_EXPERT, D_MODEL), lambda i, be, nu: (layer, be[i], 0, 0))
    w_scratch = [pltpu.VMEM((D_MODEL, D_EXPERT), BF16), pltpu.VMEM((D_MODEL, D_EXPERT), BF16),
                 pltpu.VMEM((D_EXPERT, D_MODEL), BF16)] * (2 if precise else 1)
    if gather:
        rows3 = slot_rows.reshape(nblk, 1, MOE_BM)
        x_specs = [
            pl.BlockSpec((1, 1, MOE_BM), lambda i, be, nu: (i, 0, 0), memory_space=pltpu.SMEM),
            pl.BlockSpec((1, 1, MOE_BM), lambda i, be, nu: (jnp.minimum(i + 1, nblk - 1), 0, 0),
                         memory_space=pltpu.SMEM),
            pl.BlockSpec(memory_space=pl.ANY),
        ]
        x_args = [rows3, rows3, xs]
        extra_scratch = [pltpu.VMEM((2, MOE_BM * ROW_TILES, LANES), xs.dtype), pltpu.SemaphoreType.DMA((2,))]
    else:
        x_specs = [pl.BlockSpec((bm, D_MODEL), lambda i, be, nu: (i, 0))]
        x_args = [xs]
        extra_scratch = []
    grid_spec = pltpu.PrefetchScalarGridSpec(
        num_scalar_prefetch=2,
        grid=(nblk,),
        in_specs=x_specs + [in_w, in_w, out_w],
        out_specs=pl.BlockSpec((bm, D_MODEL), lambda i, be, nu: (i, 0)),
        scratch_shapes=w_scratch + extra_scratch,
    )
    return pl.pallas_call(
        functools.partial(_moe_kernel, precise=precise, gather=gather),
        grid_spec=grid_spec,
        out_shape=jax.ShapeDtypeStruct((P, D_MODEL), F32),
        compiler_params=_cparams(("arbitrary",)),
        name="moe_gather" if gather else "moe",
    )(block_expert, n_used, *x_args, *weights)


def _combine_kernel(x1_ref, ya_ref, yb_ref, route_ref, g2_ref, lng_ref, lnb_ref, o_ref):
    tm = x1_ref.shape[1]
    lane = lax.broadcasted_iota(jnp.int32, (tm, LANES), 1)
    rt = route_ref[0]
    w1 = jnp.sum(jnp.where(lane == 2, rt, 0.0), axis=1, keepdims=True)
    w2 = jnp.sum(jnp.where(lane == 3, rt, 0.0), axis=1, keepdims=True)
    moe = ya_ref[...] * w1 + yb_ref[...] * w2
    o_ref[0] = _layernorm(ALPHA * x1_ref[0] + (1.0 + g2_ref[0]) * moe, lng_ref[...], lnb_ref[...])


def _combine(x1, ya, yb, row_off, route, g2, p, per_token_mod):
    B, L, _ = x1.shape
    tm = _pick_tm(L, PROJ_TM)
    assert row_off % tm == 0
    row = lambda b, i: (b, i, 0)
    fixed = lambda b, i: (0, 0)
    flat = lambda b, i: (row_off // tm + b * (L // tm) + i, 0)
    if per_token_mod:
        mod_spec = pl.BlockSpec((1, tm, D_MODEL), row)
    else:
        mod_spec = pl.BlockSpec((1, 1, D_MODEL), lambda b, i: (b, 0, 0))
    return pl.pallas_call(
        _combine_kernel,
        grid=(B, L // tm),
        in_specs=[
            pl.BlockSpec((1, tm, D_MODEL), row),
            pl.BlockSpec((tm, D_MODEL), flat),
            pl.BlockSpec((tm, D_MODEL), flat),
            pl.BlockSpec((1, tm, LANES), row),
            mod_spec,
            pl.BlockSpec((1, D_MODEL), fixed),
            pl.BlockSpec((1, D_MODEL), fixed),
        ],
        out_specs=pl.BlockSpec((1, tm, D_MODEL), row),
        out_shape=jax.ShapeDtypeStruct((B, L, D_MODEL), F32),
        compiler_params=_cparams(("arbitrary", "arbitrary")),
        name="combine",
    )(x1, ya, yb, route, g2, p["ln2_g"], p["ln2_b"])


def _dispatch_plan(eidx, bm):
    T = eidx.shape[0]
    A = 2 * T
    flat_e = eidx.reshape(A)
    onehot = (flat_e[:, None] == jnp.arange(N_EXPERTS, dtype=jnp.int32)[None, :]).astype(jnp.int32)
    cum = jnp.cumsum(onehot, axis=0)
    rank = jnp.take_along_axis(cum, flat_e[:, None], axis=1)[:, 0] - 1
    counts = cum[-1]
    padded = (counts + bm - 1) // bm * bm
    pad_end = jnp.cumsum(padded)
    pad_start = pad_end - padded
    dest = pad_start[flat_e] + rank
    nblk = (A + N_EXPERTS * (bm - 1) + bm - 1) // bm
    block_pos = jnp.arange(nblk, dtype=jnp.int32) * bm
    block_expert = jnp.minimum(jnp.sum(pad_end[None, :] <= block_pos[:, None], axis=1), N_EXPERTS - 1).astype(jnp.int32)
    n_used = (pad_end[-1] // bm).astype(jnp.int32).reshape(1)
    return dest, block_expert, n_used, nblk * bm


def _mods(mod_l, n_prompt, dec_seq):
    mp = mod_l[:n_prompt].reshape(n_prompt, 6, D_MODEL)
    ms = mod_l[n_prompt:].reshape(-1, 6, D_MODEL)
    prompt = [mp[:, j][:, None, :] for j in range(6)]
    sample = [jnp.repeat(ms[:, j], dec_seq, axis=0)[None] for j in range(6)]
    return prompt, sample


def kernel(x_prompt, x_sample, c_prompt, c_sample, state_ssd, state_ssd_conv, state_rglru, state_rglru_conv, state_ret, w_ada, b_ada, w_in, ssd_conv_w, ssd_conv_b, ssd_dt_bias, ssd_a_log, ssd_d, ssd_norm_w, rg_conv_w, rg_conv_b, rg_wa, rg_ba, rg_wx, rg_bx, rg_lambda, ret_gn_w, w_out, ln1_g, ln1_b, router_w, router_bias, exp_w_gate, exp_w_up, exp_w_down, ln2_g, ln2_b):
    BP, LP, _ = x_prompt.shape
    BS, LS, _ = x_sample.shape
    TP, TS = BP * LP, BS * LS

    def pad_lanes(v):
        return jnp.pad(v, ((0, 0), (0, LANES - v.shape[-1])))

    eye = jnp.eye(RG_BLOCKS, dtype=F32)

    def block_diag(w):
        return jnp.stack(_hi_lo(jnp.einsum("njk,nm->njmk", w, eye).reshape(D_RG, D_RG)))

    w_in_parts = _hi_lo(jnp.concatenate(
        [w_in[:, :, :2560], w_in[:, :, 2576:5648], w_in[:, :, 2560:2576],
         jnp.zeros((DEPTH, D_MODEL, LANES - SSD_HEADS), F32)], axis=2))

    params = []
    for l in range(DEPTH):
        params.append(dict(
            ssd_conv_w=ssd_conv_w[l], ssd_conv_b=ssd_conv_b[l][None],
            ssd_dt_bias=pad_lanes(ssd_dt_bias[l][None]), ssd_a_log=pad_lanes(ssd_a_log[l][None]),
            ssd_d_ch=jnp.repeat(ssd_d[l], SSD_HEAD_DIM)[None], ssd_norm_w=ssd_norm_w[l][None],
            rg_conv_w=rg_conv_w[l], rg_conv_b=rg_conv_b[l][None],
            rg_wa_bd=block_diag(rg_wa[l]), rg_ba=rg_ba[l][None],
            rg_wx_bd=block_diag(rg_wx[l]), rg_bx=rg_bx[l][None],
            rg_lambda=rg_lambda[l][None], ret_gn_w=ret_gn_w[l][None],
            w_out=jnp.stack(_hi_lo(w_out[l])), ln1_g=ln1_g[l][None], ln1_b=ln1_b[l][None],
            ln2_g=ln2_g[l][None], ln2_b=ln2_b[l][None],
        ))

    rw_hi, rw_lo = _hi_lo(pad_lanes(router_w))
    rbias = pad_lanes(router_bias[None])
    head_of_ch = jnp.arange(D_SSD) // SSD_HEAD_DIM
    consts = dict(
        rw_hi=rw_hi, rw_lo=rw_lo, rbias=rbias.astype(F32),
        head_expand=(jnp.arange(LANES)[:, None] == head_of_ch[None, :]).astype(BF16),
        tri=jnp.tril(jnp.ones((CHUNK, CHUNK), F32)).astype(BF16),
    )
    mod = _ada(jnp.concatenate([c_prompt, c_sample], axis=0), w_ada, b_ada)

    LM = LP - TAIL
    paths = [
        dict(x=x_prompt[:, :LM], B=BP, L=LM, per_token=False, has_state=False, precise=False, rc=_ret_consts(LM, 0.0)),
        dict(x=x_prompt[:, LM:], B=BP, L=TAIL, per_token=False, has_state=True, precise=True, rc=_ret_consts(TAIL, float(LM))),
        dict(x=x_sample.reshape(1, TS, D_MODEL), B=BS, L=LS, per_token=True, has_state=True, precise=False,
             rc=_ret_consts(LS, PAST_LEN)),
    ]
    zero_states = (jnp.zeros((BP, SSD_GROUPS, SSD_STATE, D_SSD // SSD_GROUPS), F32),
                   jnp.zeros((BP, CONV_WIDTH - 1, SSD_CONV_DIM), F32),
                   jnp.zeros((BP, 1, D_RG), F32),
                   jnp.zeros((BP, CONV_WIDTH - 1, D_RG), F32),
                   jnp.zeros((BP, RET_HEADS, RET_HEAD_DIM, RET_HEAD_DIM), F32))

    def mixer_stack(path, p, layer, mods, states, ssd_kw, ret_kw):
        B, L, per_token, precise = path["B"], path["L"], path["per_token"], path["precise"]
        z, xbc, rg, ret, dt = _inproj(path["x"], mods[1], mods[0], w_in_parts, layer, per_token, precise)
        if per_token:
            z, xbc, rg, ret, dt = [a.reshape(B, L, a.shape[-1]) for a in (z, xbc, rg, ret, dt)]
        s_ssd, s_cbuf, s_rg, s_rbuf, s_ret = states
        hs = path["has_state"]
        y_ssd, cbuf_n, ssd_n = _ssd(z, xbc, dt, s_ssd, s_cbuf, p, consts, hs, precise, **ssd_kw)
        y_rg, rbuf_n, rg_n = _rg(rg, s_rg, s_rbuf, p, hs, precise)
        y_ret, ret_n = _ret(ret, s_ret, p, path["rc"], hs, precise, **ret_kw)
        if per_token:
            y_ssd, y_rg, y_ret = [a.reshape(1, B * L, a.shape[-1]) for a in (y_ssd, y_rg, y_ret)]
        return (y_ssd, y_rg, y_ret), (ssd_n, cbuf_n, rg_n, rbuf_n, ret_n)

    expert_w = (exp_w_gate, exp_w_up, exp_w_down)

    def experts(h2_all, route_list, layer, precise):
        route_all = jnp.concatenate([r.reshape(-1, LANES) for r in route_list], axis=0)
        T_all = route_all.shape[0]
        bm = MOE_BM_TAIL if precise else MOE_BM
        dest, block_expert, n_used, P = _dispatch_plan(route_all[:, 0:2].astype(jnp.int32), bm)
        tok = jnp.arange(2 * T_all, dtype=jnp.int32) // 2
        if precise:
            slot_tok = jnp.full((P,), T_all, jnp.int32).at[dest].set(tok)
            xs_sorted = jnp.concatenate([h2_all, jnp.zeros((1, D_MODEL), h2_all.dtype)], axis=0)[slot_tok]
            ys_sorted = _moe(xs_sorted, block_expert, n_used, expert_w, layer, True, bm=bm)
        else:
            slot_row = jnp.zeros((P,), jnp.int32).at[dest].set(tok * ROW_TILES)
            ys_sorted = _moe(h2_all, block_expert, n_used, expert_w, layer, False, slot_rows=slot_row)
        return ys_sorted[dest[0::2]], ys_sorted[dest[1::2]]

    new_tail = []
    state_ssd_t = jnp.swapaxes(state_ssd, -1, -2)
    ssd_s = ret_s = None
    cbuf_s, rg_s, rbuf_s = [], [], []
    for l in range(DEPTH):
        p = params[l]
        mods_p, mods_s = _mods(mod[l], BP, LS)
        mods = [mods_p, mods_p, mods_s]
        st_s = (state_ssd_t, state_ssd_conv[l], state_rglru[l][:, None, :], state_rglru_conv[l], state_ret)

        ys_m, st_m = mixer_stack(paths[0], p, l, mods[0], zero_states, {}, {})
        ys_t, st_t = mixer_stack(paths[1], p, l, mods[1], st_m, dict(native_out=True), {})
        ys_c, st_c = mixer_stack(paths[2], p, l, mods[2], st_s,
                                 dict(h0_layer=l, native_out=True, out_stack=(l, ssd_s)),
                                 dict(h0_layer=l, out_stack=(l, ret_s)))
        new_tail.append(st_t)
        ssd_s, ret_s = st_c[0], st_c[4]
        cbuf_s.append(st_c[1])
        rg_s.append(st_c[2])
        rbuf_s.append(st_c[3])

        offs = (0, 0, BP * LM)
        group_tokens = BP * LM + TS
        post, h2_group = [None] * 3, None
        for k in (0, 2, 1):
            path, ys, m = paths[k], (ys_m, ys_t, ys_c)[k], mods[k]
            shared = None if path["precise"] else (group_tokens, offs[k], h2_group)
            post[k] = _outproj(*ys, path["x"], m[2], m[4], m[3], p, consts, path["per_token"], path["precise"], shared)
            if not path["precise"]:
                h2_group = post[k][1]

        ya_b, yb_b = experts(h2_group, [post[0][2], post[2][2]], l, False)
        ya_t, yb_t = experts(post[1][1].reshape(-1, D_MODEL), [post[1][2]], l, True)

        for path, (x1, _, route), ya, yb, off, m in zip(paths, post, (ya_b, ya_t, ya_b), (yb_b, yb_t, yb_b), offs, mods):
            path["x"] = _combine(x1, ya, yb, off, route, m[5], p, path["per_token"])

    def stack(lst, k):
        return jnp.stack([s[k] for s in lst])

    y_prompt = jnp.concatenate([paths[0]["x"], paths[1]["x"]], axis=1)
    return (y_prompt, paths[2]["x"].reshape(BS, LS, D_MODEL),
            jnp.swapaxes(stack(new_tail, 0), -1, -2), stack(new_tail, 1), stack(new_tail, 2)[:, :, 0],
            stack(new_tail, 3), stack(new_tail, 4),
            jnp.swapaxes(ssd_s, -1, -2), jnp.stack(cbuf_s), jnp.stack(rg_s)[:, :, 0], jnp.stack(rbuf_s), ret_s)
```

```python
import functools
import math

import jax
import jax.numpy as jnp
from jax import lax
from jax.experimental import pallas as pl
from jax.experimental.pallas import tpu as pltpu

F32 = jnp.float32
BF16 = jnp.bfloat16

D_MODEL = 1024
DEPTH = 4
D_SSD = 1024
SSD_HEAD_DIM = 64
SSD_HEADS = 16
SSD_GROUPS = 2
SSD_STATE = 128
SSD_CONV_DIM = D_SSD + 2 * SSD_GROUPS * SSD_STATE
CONV_WIDTH = 4
D_RG = 512
RG_BLOCKS = 8
RG_C = 8.0
D_RET = 512
RET_HEADS = 4
RET_HEAD_DIM = 128
ROPE_BASE = 10000.0
N_EXPERTS = 32
EXPERTS_PER_GROUP = 8
N_EXPERT_GROUPS = 4
D_EXPERT = 512
ALPHA = (2 * DEPTH) ** 0.25
EPS = 1e-5
PAST_LEN = 16384.0

LANES = 128
SUBLANES = 8
CHUNK = 128
TAIL = CHUNK
SEQS_PER_STEP = 8
SEQ_UNROLL = 4
ROW_TILES = D_MODEL // LANES
PROJ_TM = 384
MOE_BM = 256
MOE_BM_TAIL = 256
VMEM_LIMIT = 56 * 1024 * 1024


def _cparams(sem):
    return pltpu.CompilerParams(dimension_semantics=sem, vmem_limit_bytes=VMEM_LIMIT)


def _padded_rows(L):
    return CHUNK if L >= CHUNK else -(-L // SUBLANES) * SUBLANES


def _pick_tm(L, cap):
    if L <= LANES:
        return L
    tm = cap - cap % LANES
    while L % tm:
        tm -= LANES
    return tm


def _sigmoid(x):
    return 1.0 / (1.0 + jnp.exp(-x))


def _silu(x):
    return x * _sigmoid(x)


def _softplus(x):
    return jnp.maximum(x, 0.0) + jnp.log1p(jnp.exp(-jnp.abs(x)))


def _split3(v):
    hi = v.astype(BF16)
    r = v - hi.astype(F32)
    mid = r.astype(BF16)
    lo = (r - mid.astype(F32)).astype(BF16)
    return hi, mid, lo


def _dot(a, b):
    return jnp.dot(a, b, preferred_element_type=F32)


def _split2(v):
    hi = v.astype(BF16)
    return hi, (v - hi.astype(F32)).astype(BF16)


def _mm_w(a, w_hi, w_lo):
    if w_lo is None:
        return _dot(a.astype(BF16), w_hi)
    ah, al = _split2(a)
    return _dot(ah, w_hi) + (_dot(al, w_hi) + _dot(ah, w_lo))


def _mm_a(a, b, precise, nt=False):
    dn = (((1,), (1,)), ((), ())) if nt else (((1,), (0,)), ((), ()))

    def d(x, y):
        return lax.dot_general(x, y, dn, preferred_element_type=F32)

    if not precise:
        return d(a.astype(BF16), b.astype(BF16))
    ah, al = _split2(a)
    bh, bl = _split2(b)
    return d(ah, bh) + (d(al, bh) + d(ah, bl))


def _hi_lo(w):
    hi32 = lax.reduce_precision(w, exponent_bits=8, mantissa_bits=7)
    return hi32.astype(BF16), (w - hi32).astype(BF16)


def _dot_exact_rhs(v, m):
    hi, mid, lo = _split3(v)
    return _dot(hi, m) + _dot(mid, m) + _dot(lo, m)


def _dot_exact_lhs(m, v):
    hi, mid, lo = _split3(v)
    return _dot(m, hi) + _dot(m, mid) + _dot(m, lo)


def _ada_kernel(c_ref, w_ref, b_ref, o_ref):
    o_ref[0] = _mm_a(_silu(c_ref[...]), w_ref[0], True) + b_ref[0]


def _ada(c_all, w_ada, b_ada):
    n = c_all.shape[0]
    tn = 1024
    return pl.pallas_call(
        _ada_kernel,
        grid=(DEPTH, 6 * D_MODEL // tn),
        in_specs=[
            pl.BlockSpec((n, D_MODEL), lambda l, j: (0, 0)),
            pl.BlockSpec((1, D_MODEL, tn), lambda l, j: (l, 0, j)),
            pl.BlockSpec((1, 1, tn), lambda l, j: (l, 0, j)),
        ],
        out_specs=pl.BlockSpec((1, n, tn), lambda l, j: (l, 0, j)),
        out_shape=jax.ShapeDtypeStruct((DEPTH, n, 6 * D_MODEL), F32),
        compiler_params=_cparams(("arbitrary", "arbitrary")),
        name="ada",
    )(c_all, w_ada, b_ada.reshape(DEPTH, 1, 6 * D_MODEL))


_IN_SEGS = (("z", 0, 1024), ("xbc", 1024, 2560), ("rg", 2560, 3584), ("ret", 3584, 5632), ("dt", 5632, 5760))
IN_PAD = 5760


def _inproj_kernel(x_ref, sc_ref, sh_ref, w_ref, *rest, precise):
    wl_ref = rest[0] if precise else None
    outs = rest[1:] if precise else rest
    h = x_ref[0] * (1.0 + sc_ref[0]) + sh_ref[0]
    if precise:
        hh, hl = _split2(h)
    else:
        hh = h.astype(BF16)
    for o_ref, (_, lo, hi) in zip(outs, _IN_SEGS):
        acc = _dot(hh, w_ref[:, lo:hi])
        if precise:
            acc = acc + (_dot(hl, w_ref[:, lo:hi]) + _dot(hh, wl_ref[:, lo:hi]))
        o_ref[0] = acc


def _inproj(x, sc, sh, w, layer, per_token_mod, precise):
    B, L, _ = x.shape
    tm = _pick_tm(L, LANES if precise else PROJ_TM)
    if per_token_mod:
        mod_spec = pl.BlockSpec((1, tm, D_MODEL), lambda b, i: (b, i, 0))
    else:
        mod_spec = pl.BlockSpec((1, 1, D_MODEL), lambda b, i: (b, 0, 0))
    widths = [hi - lo for _, lo, hi in _IN_SEGS]
    w_spec = pl.BlockSpec((None, D_MODEL, IN_PAD), lambda b, i: (layer, 0, 0), pipeline_mode=pl.Buffered(1))
    ws = w if precise else w[:1]
    return pl.pallas_call(
        functools.partial(_inproj_kernel, precise=precise),
        grid=(B, L // tm),
        in_specs=[
            pl.BlockSpec((1, tm, D_MODEL), lambda b, i: (b, i, 0)),
            mod_spec,
            mod_spec,
        ] + [w_spec] * len(ws),
        out_specs=[pl.BlockSpec((1, tm, wd), lambda b, i: (b, i, 0)) for wd in widths],
        out_shape=[jax.ShapeDtypeStruct((B, L, wd), F32) for wd in widths],
        compiler_params=_cparams(("arbitrary", "arbitrary")),
        name="inproj",
    )(x, sc, sh, *ws)


def _conv_step(src_rows, buf_ref, cw_ref, cb_ref, nbuf_ref, xp_s, *, c, nc, Lv, Lp, has_state):
    @_when_unless(nc == 1, c == 0)
    def _():
        xp_s[...] = jnp.zeros_like(xp_s)
        if has_state:
            xp_s[SUBLANES - (CONV_WIDTH - 1):SUBLANES, :] = buf_ref[0]

    xp_s[SUBLANES:SUBLANES + Lv, :] = src_rows
    base = SUBLANES - (CONV_WIDTH - 1)
    xc = cb_ref[...]
    for j in range(CONV_WIDTH):
        xc = xc + cw_ref[j:j + 1, :] * xp_s[base + j:base + j + Lp, :]

    @_when_unless(nc == 1, c == nc - 1)
    def _():
        nbuf_ref[0] = xp_s[SUBLANES + Lv - (CONV_WIDTH - 1):SUBLANES + Lv, :]

    if nc > 1:
        xp_s[0:SUBLANES, :] = xp_s[Lp:Lp + SUBLANES, :]
    return xc


def _for_each_sequence(body, refs, bb, keep, squeeze, n_scratch):
    n = len(refs)

    def one(s, j):
        v = list(refs)
        for k in keep:
            v[k] = refs[k].at[pl.ds(s, 1)]
        for k in squeeze:
            v[k] = refs[k].at[s]
        if bb > 1:
            for k in range(n - n_scratch, n):
                v[k] = refs[k].at[j]
        body(*v)

    if bb == 1:
        one(0, 0)
    else:
        def step(i, carry):
            for j in range(SEQ_UNROLL):
                one(i * SEQ_UNROLL + j, j)
            return carry
        lax.fori_loop(0, bb // SEQ_UNROLL, step, 0)


def _when_unless(always, cond):
    def deco(fn):
        if always:
            fn()
        else:
            pl.when(cond)(fn)
        return fn
    return deco


def _seq_scratch(shape, bb):
    return pltpu.VMEM(((SEQ_UNROLL,) + shape) if bb > 1 else shape, F32)


def _ssd_kernel(*refs, bb, **kw):
    n = len(refs)
    _for_each_sequence(functools.partial(_ssd_body, **kw), refs, bb,
                       keep=(0, 1, 2, 4, n - 6, n - 5), squeeze=(3, n - 4), n_scratch=3)


def _ssd_body(z_ref, xbc_ref, dt_ref, h0_ref, buf_ref, cw_ref, cb_ref, dtb_ref, alog_ref, dch_ref,
              nw_ref, e_ref, tri_ref, *rest, Lv, Lp, nc, has_state, precise, native_in, native_out):
    y_ref, nbuf_ref, hn_ref, xp_s, dtp_s, s_s = rest[-6:]
    c = pl.program_id(1)
    hpg = SSD_HEADS // SSD_GROUPS

    @_when_unless(nc == 1, c == 0)
    def _():
        if not has_state:
            s_s[...] = jnp.zeros_like(s_s)
        elif native_in:
            for g in range(SSD_GROUPS):
                s_s[g] = jnp.concatenate([h0_ref[g * hpg + i] for i in range(hpg)], axis=0).T
        else:
            s_s[...] = h0_ref[...]
        if Lv < Lp:
            dtp_s[...] = jnp.zeros_like(dtp_s)

    xc = _conv_step(xbc_ref[0], buf_ref, cw_ref, cb_ref, nbuf_ref, xp_s,
                    c=c, nc=nc, Lv=Lv, Lp=Lp, has_state=has_state)
    xbc = _silu(xc)
    xs = xbc[:, 0:D_SSD]
    Bm = xbc[:, D_SSD:D_SSD + 256]
    Cm = xbc[:, D_SSD + 256:D_SSD + 512]

    if Lv < Lp:
        dtp_s[0:Lv, :] = dt_ref[0]
        dtr = dtp_s[...]
    else:
        dtr = dt_ref[0]
    dt = _softplus(dtr + dtb_ref[...])
    if Lv < Lp:
        rowv = lax.broadcasted_iota(jnp.int32, (Lp, LANES), 0)
        dt = jnp.where(rowv < Lv, dt, 0.0)
    a = dt * (-jnp.exp(alog_ref[...]))
    acum = _dot_exact_lhs(tri_ref[...], a)
    dt_e = _dot_exact_rhs(dt, e_ref[...])
    ac_e = _dot_exact_rhs(acum, e_ref[...])
    acl_e = ac_e[Lp - 1:Lp, :]
    acum_t = acum.T

    vdt = xs * dt_e
    lane = lax.broadcasted_iota(jnp.int32, (Lp, LANES), 1)
    causal = lax.broadcasted_iota(jnp.int32, (Lp, Lp), 0) >= lax.broadcasted_iota(jnp.int32, (Lp, Lp), 1)
    mid_t = F32 if precise else BF16

    y_blocks = []
    for g in range(SSD_GROUPS):
        Cg = Cm[:, g * SSD_STATE:(g + 1) * SSD_STATE]
        Bg = Bm[:, g * SSD_STATE:(g + 1) * SSD_STATE]
        G = _mm_a(Cg, Bg, precise, nt=True)
        for jj in range(4):
            j = 4 * g + jj
            scs = []
            for h in (2 * j, 2 * j + 1):
                col = jnp.sum(jnp.where(lane == h, acum, 0.0), axis=1, keepdims=True)
                rw = acum_t[h:h + 1, :]
                dec = jnp.exp(jnp.where(causal, col - rw, -jnp.inf))
                scs.append((G * dec).astype(mid_t))
            sc = jnp.concatenate(scs, axis=1)
            vb = vdt[:, j * LANES:(j + 1) * LANES]
            v2 = jnp.concatenate([jnp.where(lane < SSD_HEAD_DIM, vb, 0.0),
                                  jnp.where(lane >= SSD_HEAD_DIM, vb, 0.0)], axis=0).astype(mid_t)
            y_blocks.append(_mm_a(sc, v2, precise))
    y = jnp.concatenate(y_blocks, axis=1)

    eac = jnp.exp(ac_e)
    vw = vdt * jnp.exp(acl_e - ac_e)
    cd = jnp.exp(acl_e)
    half = D_SSD // SSD_GROUPS
    ys_parts = []
    for g in range(SSD_GROUPS):
        Sg = s_s[g]
        ys_parts.append(_mm_a(Cm[:, g * SSD_STATE:(g + 1) * SSD_STATE], Sg, precise))
        BgT = Bm[:, g * SSD_STATE:(g + 1) * SSD_STATE].T
        s_s[g] = cd[:, g * half:(g + 1) * half] * Sg + _mm_a(BgT, vw[:, g * half:(g + 1) * half], precise)
    y = y + eac * jnp.concatenate(ys_parts, axis=1) + dch_ref[...] * xs

    yv = y[0:Lv, :] if Lv < Lp else y
    yz = yv * _silu(z_ref[0])
    outs = []
    for g in range(SSD_GROUPS):
        part = yz[:, g * half:(g + 1) * half]
        ms = jnp.mean(part * part, axis=1, keepdims=True)
        outs.append(part * lax.rsqrt(ms + EPS))
    y_ref[0] = jnp.concatenate(outs, axis=1) * nw_ref[...]

    @_when_unless(nc == 1, c == nc - 1)
    def _():
        if native_out:
            for g in range(SSD_GROUPS):
                sgt = s_s[g].T
                for i in range(hpg):
                    hn_ref[g * hpg + i] = sgt[i * SSD_HEAD_DIM:(i + 1) * SSD_HEAD_DIM, :]
        else:
            hn_ref[...] = s_s[...]


def _state_specs(B, bb, shape, in_layer, out_stack):
    zeros = (0,) * len(shape)
    if in_layer is None:
        in_spec = pl.BlockSpec((bb,) + shape, lambda b, c: (b,) + zeros)
    else:
        in_spec = pl.BlockSpec((None, bb) + shape, lambda b, c: (in_layer, b) + zeros)
    if out_stack is None:
        out_spec = pl.BlockSpec((bb,) + shape, lambda b, c: (b,) + zeros)
        out_shape = jax.ShapeDtypeStruct((B,) + shape, F32)
    else:
        l = out_stack[0]
        out_spec = pl.BlockSpec((None, bb) + shape, lambda b, c: (l, b) + zeros)
        out_shape = jax.ShapeDtypeStruct((DEPTH, B) + shape, F32)
    return in_spec, out_spec, out_shape


def _seqs_per_step(B, L):
    return SEQS_PER_STEP if (L < CHUNK and B % SEQS_PER_STEP == 0) else 1


def _ssd(z, xbc, dt, h0, buf, p, consts, has_state, precise, h0_layer=None, native_out=False, out_stack=None):
    B, L, _ = z.shape
    Lv = min(CHUNK, L)
    Lp = _padded_rows(L)
    nc = L // Lv
    native_in = h0_layer is not None
    bb = _seqs_per_step(B, L)
    kern = functools.partial(_ssd_kernel, bb=bb, Lv=Lv, Lp=Lp, nc=nc, has_state=has_state, precise=precise,
                             native_in=native_in, native_out=native_out)
    row = lambda b, c: (b, c, 0)
    per_b3 = lambda b, c: (b, 0, 0)
    fixed = lambda b, c: (0, 0)
    half = D_SSD // SSD_GROUPS
    native = (SSD_HEADS, SSD_HEAD_DIM, SSD_STATE)
    packed = (SSD_GROUPS, SSD_STATE, half)
    h0_spec, _, _ = _state_specs(B, bb, native if native_in else packed, h0_layer, None)
    _, hn_spec, hn_shape = _state_specs(B, bb, native if native_out else packed, None, out_stack)
    prev = out_stack[1] if out_stack is not None else None
    extra_in, extra_specs, aliases = [], [], {}
    if prev is not None:
        extra_in, extra_specs, aliases = [prev], [pl.BlockSpec(memory_space=pl.ANY)], {13: 2}
    return pl.pallas_call(
        kern,
        grid=(B // bb, nc),
        in_specs=[
            pl.BlockSpec((bb, Lv, D_SSD), row),
            pl.BlockSpec((bb, Lv, SSD_CONV_DIM), row),
            pl.BlockSpec((bb, Lv, LANES), row),
            h0_spec,
            pl.BlockSpec((bb, CONV_WIDTH - 1, SSD_CONV_DIM), per_b3),
            pl.BlockSpec((CONV_WIDTH, SSD_CONV_DIM), fixed),
            pl.BlockSpec((1, SSD_CONV_DIM), fixed),
            pl.BlockSpec((1, LANES), fixed),
            pl.BlockSpec((1, LANES), fixed),
            pl.BlockSpec((1, D_SSD), fixed),
            pl.BlockSpec((1, D_SSD), fixed),
            pl.BlockSpec((LANES, D_SSD), fixed),
            pl.BlockSpec((Lp, Lp), fixed),
        ] + extra_specs,
        out_specs=[
            pl.BlockSpec((bb, Lv, D_SSD), row),
            pl.BlockSpec((bb, CONV_WIDTH - 1, SSD_CONV_DIM), per_b3),
            hn_spec,
        ],
        out_shape=[
            jax.ShapeDtypeStruct((B, L, D_SSD), F32),
            jax.ShapeDtypeStruct((B, CONV_WIDTH - 1, SSD_CONV_DIM), F32),
            hn_shape,
        ],
        scratch_shapes=[
            _seq_scratch((SUBLANES + Lp, SSD_CONV_DIM), bb),
            _seq_scratch((Lp, LANES), bb),
            _seq_scratch((SSD_GROUPS, SSD_STATE, half), bb),
        ],
        input_output_aliases=aliases,
        compiler_params=_cparams(("arbitrary", "arbitrary")),
        name="ssd",
    )(z, xbc, dt, h0, buf, p["ssd_conv_w"], p["ssd_conv_b"], p["ssd_dt_bias"], p["ssd_a_log"],
      p["ssd_d_ch"], p["ssd_norm_w"], consts["head_expand"],
      jnp.tril(jnp.ones((Lp, Lp), F32)).astype(BF16), *extra_in)


def _rg_kernel(*refs, bb, **kw):
    n = len(refs)
    _for_each_sequence(functools.partial(_rg_body, **kw), refs, bb,
                       keep=(0, 1, 2, n - 5, n - 4, n - 3), squeeze=(), n_scratch=2)


def _rg_body(rg_ref, h0_ref, buf_ref, cw_ref, cb_ref, wa_ref, ba_ref, wx_ref, bx_ref, lam_ref,
             y_ref, nbuf_ref, hn_ref, xp_s, hc_s, *, Lv, Lp, nc, has_state, precise):
    c = pl.program_id(1)

    @_when_unless(nc == 1, c == 0)
    def _():
        if has_state:
            hc_s[...] = h0_ref[0]
        else:
            hc_s[...] = jnp.zeros_like(hc_s)

    xc = _conv_step(rg_ref[0, :, 0:D_RG], buf_ref, cw_ref, cb_ref, nbuf_ref, xp_s,
                    c=c, nc=nc, Lv=Lv, Lp=Lp, has_state=has_state)
    r = _sigmoid(_mm_w(xc, wa_ref[0], wa_ref[1] if precise else None) + ba_ref[...])
    i = _sigmoid(_mm_w(xc, wx_ref[0], wx_ref[1] if precise else None) + bx_ref[...])
    la = (-RG_C * _softplus(-lam_ref[...])) * r
    u = jnp.sqrt(-jnp.tanh(la) * (jnp.exp(2.0 * la) + 1.0)) * (i * xc)
    row = lax.broadcasted_iota(jnp.int32, (Lp, D_RG), 0)
    if Lv < Lp:
        valid = row < Lv
        la = jnp.where(valid, la, 0.0)
        u = jnp.where(valid, u, 0.0)
    a = jnp.exp(la)
    s = 1
    while s < Lp:
        m = row >= s
        u_sh = jnp.where(m, pltpu.roll(u, s, 0), 0.0)
        a_sh = jnp.where(m, pltpu.roll(a, s, 0), 1.0)
        u = u + a * u_sh
        a = a * a_sh
        s *= 2
    h = u + a * hc_s[...]
    hc_s[...] = h[Lp - 1:Lp, :]
    hv = h[0:Lv, :] if Lv < Lp else h
    y_ref[0] = hv * jax.nn.gelu(rg_ref[0, :, D_RG:2 * D_RG])

    @_when_unless(nc == 1, c == nc - 1)
    def _():
        hn_ref[0] = hc_s[...]


def _rg(rg, h0, buf, p, has_state, precise):
    B, L, _ = rg.shape
    Lv = min(CHUNK, L)
    Lp = _padded_rows(L)
    nc = L // Lv
    bb = _seqs_per_step(B, L)
    kern = functools.partial(_rg_kernel, bb=bb, Lv=Lv, Lp=Lp, nc=nc, has_state=has_state, precise=precise)
    row = lambda b, c: (b, c, 0)
    per_b3 = lambda b, c: (b, 0, 0)
    fixed = lambda b, c: (0, 0)
    fixed3 = lambda b, c: (0, 0, 0)
    return pl.pallas_call(
        kern,
        grid=(B // bb, nc),
        in_specs=[
            pl.BlockSpec((bb, Lv, 2 * D_RG), row),
            pl.BlockSpec((bb, 1, D_RG), per_b3),
            pl.BlockSpec((bb, CONV_WIDTH - 1, D_RG), per_b3),
            pl.BlockSpec((CONV_WIDTH, D_RG), fixed),
            pl.BlockSpec((1, D_RG), fixed),
            pl.BlockSpec((2, D_RG, D_RG), fixed3),
            pl.BlockSpec((1, D_RG), fixed),
            pl.BlockSpec((2, D_RG, D_RG), fixed3),
            pl.BlockSpec((1, D_RG), fixed),
            pl.BlockSpec((1, D_RG), fixed),
        ],
        out_specs=[
            pl.BlockSpec((bb, Lv, D_RG), row),
            pl.BlockSpec((bb, CONV_WIDTH - 1, D_RG), per_b3),
            pl.BlockSpec((bb, 1, D_RG), per_b3),
        ],
        out_shape=[
            jax.ShapeDtypeStruct((B, L, D_RG), F32),
            jax.ShapeDtypeStruct((B, CONV_WIDTH - 1, D_RG), F32),
            jax.ShapeDtypeStruct((B, 1, D_RG), F32),
        ],
        scratch_shapes=[
            _seq_scratch((SUBLANES + Lp, D_RG), bb),
            _seq_scratch((1, D_RG), bb),
        ],
        compiler_params=_cparams(("arbitrary", "arbitrary")),
        name="rglru",
    )(rg, h0, buf, p["rg_conv_w"], p["rg_conv_b"], p["rg_wa_bd"], p["rg_ba"], p["rg_wx_bd"],
      p["rg_bx"], p["rg_lambda"])


def _ret_kernel(*refs, bb, **kw):
    n = len(refs)
    _for_each_sequence(functools.partial(_ret_body, **kw), refs, bb, keep=(0, n - 4), squeeze=(1, n - 3),
                       n_scratch=2)


def _ret_body(ret_ref, h0_ref, cos_ref, sin_ref, dm_ref, ea_ref, te_ref, cd_ref, gn_ref,
              *rest, Lv, Lp, nc, has_state, precise):
    y_ref, hn_ref, pad_s, s_s = rest[-4:]
    c = pl.program_id(1)

    @_when_unless(nc == 1, c == 0)
    def _():
        if has_state:
            s_s[...] = h0_ref[...]
        else:
            s_s[...] = jnp.zeros_like(s_s)
        if Lv < Lp:
            pad_s[...] = jnp.zeros_like(pad_s)

    if Lv < Lp:
        pad_s[0:Lv, :] = ret_ref[0, :, 0:3 * D_RET]
        qkv = pad_s[...]
    else:
        qkv = ret_ref[0, :, 0:3 * D_RET]
    cosf = cos_ref[...]
    sinf = sin_ref[...]
    outs = []
    for h in range(RET_HEADS):
        sl = slice(h * RET_HEAD_DIM, (h + 1) * RET_HEAD_DIM)
        q = qkv[:, sl]
        k = qkv[:, D_RET + h * RET_HEAD_DIM:D_RET + (h + 1) * RET_HEAD_DIM]
        v = qkv[:, 2 * D_RET + h * RET_HEAD_DIM:2 * D_RET + (h + 1) * RET_HEAD_DIM]
        q = q * cosf + pltpu.roll(q, RET_HEAD_DIM // 2, 1) * sinf
        k = (k * cosf + pltpu.roll(k, RET_HEAD_DIM // 2, 1) * sinf) * (RET_HEAD_DIM ** -0.5)
        G = _mm_a(q, k, precise, nt=True)
        S = s_s[h]
        y = _mm_a(G * dm_ref[h], v, precise) + _mm_a(q * ea_ref[h], S, precise)
        s_s[h] = cd_ref[h] * S + _mm_a((k * te_ref[h]).T, v, precise)
        yv = y[0:Lv, :] if Lv < Lp else y
        mu = jnp.mean(yv, axis=1, keepdims=True)
        d = yv - mu
        var = jnp.mean(d * d, axis=1, keepdims=True)
        outs.append(d * lax.rsqrt(var + EPS))
    yn = jnp.concatenate(outs, axis=1) * gn_ref[...]
    y_ref[0] = _silu(ret_ref[0, :, 3 * D_RET:4 * D_RET]) * yn

    @_when_unless(nc == 1, c == nc - 1)
    def _():
        hn_ref[...] = s_s[...]


def _ret(ret, h0, p, rc, has_state, precise, h0_layer=None, out_stack=None):
    B, L, _ = ret.shape
    Lv = min(CHUNK, L)
    Lp = _padded_rows(L)
    nc = L // Lv
    bb = _seqs_per_step(B, L)
    kern = functools.partial(_ret_kernel, bb=bb, Lv=Lv, Lp=Lp, nc=nc, has_state=has_state, precise=precise)
    row = lambda b, c: (b, c, 0)
    fixed2 = lambda b, c: (0, 0)
    fixed3 = lambda b, c: (0, 0, 0)
    hd = RET_HEAD_DIM
    h0_spec, _, _ = _state_specs(B, bb, (RET_HEADS, hd, hd), h0_layer, None)
    _, hn_spec, hn_shape = _state_specs(B, bb, (RET_HEADS, hd, hd), None, out_stack)
    prev = out_stack[1] if out_stack is not None else None
    extra_in, extra_specs, aliases = [], [], {}
    if prev is not None:
        extra_in, extra_specs, aliases = [prev], [pl.BlockSpec(memory_space=pl.ANY)], {9: 1}
    return pl.pallas_call(
        kern,
        grid=(B // bb, nc),
        in_specs=[
            pl.BlockSpec((bb, Lv, 4 * D_RET), row),
            h0_spec,
            pl.BlockSpec((Lp, hd), lambda b, c: (c, 0)),
            pl.BlockSpec((Lp, hd), lambda b, c: (c, 0)),
            pl.BlockSpec((RET_HEADS, Lp, Lp), fixed3),
            pl.BlockSpec((RET_HEADS, Lp, hd), fixed3),
            pl.BlockSpec((RET_HEADS, Lp, hd), fixed3),
            pl.BlockSpec((RET_HEADS, hd, hd), fixed3),
            pl.BlockSpec((1, D_RET), fixed2),
        ] + extra_specs,
        out_specs=[
            pl.BlockSpec((bb, Lv, D_RET), row),
            hn_spec,
        ],
        out_shape=[
            jax.ShapeDtypeStruct((B, L, D_RET), F32),
            hn_shape,
        ],
        scratch_shapes=[
            _seq_scratch((Lp, 3 * D_RET), bb),
            _seq_scratch((RET_HEADS, hd, hd), bb),
        ],
        input_output_aliases=aliases,
        compiler_params=_cparams(("arbitrary", "arbitrary")),
        name="retention",
    )(ret, h0, rc["cos"], rc["sin"], rc["dmat"], rc["eacum"], rc["toend"], rc["cdec"], p["ret_gn_w"], *extra_in)


def _ret_consts(L, pos0):
    Lv = min(CHUNK, L)
    Lp = _padded_rows(L)
    nc = L // Lv
    half = RET_HEAD_DIM // 2
    inv = ROPE_BASE ** (-jnp.arange(half, dtype=F32) / half)
    pos = pos0 + jnp.arange(L, dtype=F32)
    ang = pos[:, None] * inv[None, :]
    cos = jnp.cos(ang)
    sin = jnp.sin(ang)
    cosf = jnp.concatenate([cos, cos], axis=1)
    sinf = jnp.concatenate([-sin, sin], axis=1)
    if Lv < Lp:
        cosf = jnp.pad(cosf, ((0, Lp - Lv), (0, 0)))
        sinf = jnp.pad(sinf, ((0, Lp - Lv), (0, 0)))
    assert cosf.shape[0] == nc * Lp
    log_gamma = jnp.log1p(-jnp.exp2(-5.0 - jnp.arange(RET_HEADS, dtype=F32)))
    steps = jnp.minimum(jnp.arange(Lp) + 1, Lv).astype(F32)
    acum = log_gamma[:, None] * steps[None, :]
    causal = jnp.tril(jnp.ones((Lp, Lp), bool))
    dmat = jnp.exp(jnp.where(causal[None], acum[:, :, None] - acum[:, None, :], -jnp.inf))
    ones = jnp.ones((RET_HEADS, Lp, RET_HEAD_DIM), F32)
    eacum = jnp.exp(acum)[:, :, None] * ones
    toend = jnp.exp(acum[:, -1:] - acum)[:, :, None] * ones
    rowvalid = (jnp.arange(Lp) < Lv).astype(F32)[None, :, None]
    toend = toend * rowvalid
    cdec = jnp.exp(acum[:, -1])[:, None, None] * jnp.ones((RET_HEADS, RET_HEAD_DIM, RET_HEAD_DIM), F32)
    return dict(cos=cosf, sin=sinf, dmat=dmat, eacum=eacum, toend=toend, cdec=cdec)


def _layernorm(v, g, b):
    mu = jnp.mean(v, axis=1, keepdims=True)
    d = v - mu
    var = jnp.mean(d * d, axis=1, keepdims=True)
    return d * lax.rsqrt(var + EPS) * g + b


def _route(logits, bias):
    tm = logits.shape[0]
    lane = lax.broadcasted_iota(jnp.int32, (tm, LANES), 1)
    lanef = lane.astype(F32)
    scores = _sigmoid(logits)
    choice = scores + bias
    neg = -jnp.inf
    best = jnp.full((tm, 1), neg, F32)
    e1 = jnp.zeros((tm, 1), F32)
    e2 = jnp.zeros((tm, 1), F32)
    for g in range(N_EXPERT_GROUPS):
        ing = (lane >= g * EXPERTS_PER_GROUP) & (lane < (g + 1) * EXPERTS_PER_GROUP)
        cg = jnp.where(ing, choice, neg)
        m1 = jnp.max(cg, axis=1, keepdims=True)
        i1 = jnp.min(jnp.where(cg == m1, lanef, float(LANES)), axis=1, keepdims=True)
        cg2 = jnp.where(lanef == i1, neg, cg)
        m2 = jnp.max(cg2, axis=1, keepdims=True)
        i2 = jnp.min(jnp.where(cg2 == m2, lanef, float(LANES)), axis=1, keepdims=True)
        gs = m1 + m2
        better = gs > best
        best = jnp.where(better, gs, best)
        e1 = jnp.where(better, i1, e1)
        e2 = jnp.where(better, i2, e2)
    w1 = jnp.sum(jnp.where(lanef == e1, scores, 0.0), axis=1, keepdims=True)
    w2 = jnp.sum(jnp.where(lanef == e2, scores, 0.0), axis=1, keepdims=True)
    den = w1 + w2
    out = jnp.where(lane == 0, e1, jnp.where(lane == 1, e2, jnp.where(lane == 2, w1 / den, jnp.where(lane == 3, w2 / den, 0.0))))
    return out


def _outproj_kernel(ys_ref, yr_ref, yt_ref, x_ref, g1_ref, sc2_ref, sh2_ref, w_ref, lng_ref, lnb_ref,
                    rwh_ref, rwl_ref, rb_ref, *rest, precise, tile_rows):
    x1_ref, h2_ref, route_ref = rest[-3:]
    def part(y_ref, lo, hi):
        return _mm_w(y_ref[0], w_ref[0, lo:hi, :], w_ref[1, lo:hi, :] if precise else None)

    mix = part(ys_ref, 0, D_SSD) + part(yr_ref, D_SSD, D_SSD + D_RG) + part(yt_ref, D_SSD + D_RG, 2 * D_MODEL)
    x1 = _layernorm(ALPHA * x_ref[0] + (1.0 + g1_ref[0]) * mix, lng_ref[...], lnb_ref[...])
    x1_ref[0] = x1
    h2 = x1 * (1.0 + sc2_ref[0]) + sh2_ref[0]
    hi, lo = _split2(h2)
    if tile_rows:
        tm = h2.shape[0]
        for s in range(ROW_TILES):
            h2_ref[pl.ds(s, tm, stride=ROW_TILES), :] = h2[:, s * LANES:(s + 1) * LANES]
    else:
        h2_ref[0] = h2
    logits = _dot(hi, rwh_ref[...]) + (_dot(lo, rwh_ref[...]) + _dot(hi, rwl_ref[...]))
    route_ref[0] = _route(logits, rb_ref[...])


def _outproj(y_ssd, y_rg, y_ret, x, g1, sc2, sh2, p, consts, per_token_mod, precise, shared=None):
    B, L, _ = x.shape
    tm = _pick_tm(L, LANES if precise else PROJ_TM)
    tile_rows = not precise
    row = lambda b, i: (b, i, 0)
    fixed = lambda b, i: (0, 0)
    if per_token_mod:
        mod_spec = pl.BlockSpec((1, tm, D_MODEL), row)
    else:
        mod_spec = pl.BlockSpec((1, 1, D_MODEL), lambda b, i: (b, 0, 0))
    extra_in, extra_specs, aliases = [], [], {}
    if tile_rows:
        total, off, prev = shared
        assert off % tm == 0
        h2_spec = pl.BlockSpec((ROW_TILES * tm, LANES), lambda b, i: (off // tm + b * (L // tm) + i, 0))
        h2_shape = jax.ShapeDtypeStruct((ROW_TILES * total, LANES), F32)
        if prev is not None:
            extra_in, extra_specs, aliases = [prev], [pl.BlockSpec(memory_space=pl.ANY)], {13: 1}
    else:
        h2_spec = pl.BlockSpec((1, tm, D_MODEL), row)
        h2_shape = jax.ShapeDtypeStruct((B, L, D_MODEL), F32)
    return pl.pallas_call(
        functools.partial(_outproj_kernel, precise=precise, tile_rows=tile_rows),
        grid=(B, L // tm),
        in_specs=[
            pl.BlockSpec((1, tm, D_SSD), row),
            pl.BlockSpec((1, tm, D_RG), row),
            pl.BlockSpec((1, tm, D_RET), row),
            pl.BlockSpec((1, tm, D_MODEL), row),
            mod_spec, mod_spec, mod_spec,
            pl.BlockSpec((2, 2 * D_MODEL, D_MODEL), lambda b, i: (0, 0, 0)),
            pl.BlockSpec((1, D_MODEL), fixed),
            pl.BlockSpec((1, D_MODEL), fixed),
            pl.BlockSpec((D_MODEL, LANES), fixed),
            pl.BlockSpec((D_MODEL, LANES), fixed),
            pl.BlockSpec((1, LANES), fixed),
        ] + extra_specs,
        out_specs=[
            pl.BlockSpec((1, tm, D_MODEL), row),
            h2_spec,
            pl.BlockSpec((1, tm, LANES), row),
        ],
        out_shape=[
            jax.ShapeDtypeStruct((B, L, D_MODEL), F32),
            h2_shape,
            jax.ShapeDtypeStruct((B, L, LANES), F32),
        ],
        input_output_aliases=aliases,
        compiler_params=_cparams(("arbitrary", "arbitrary")),
        name="outproj",
    )(y_ssd, y_rg, y_ret, x, g1, sc2, sh2, p["w_out"], p["ln1_g"], p["ln1_b"],
      consts["rw_hi"], consts["rw_lo"], consts["rbias"], *extra_in)


def _row_gather_copy(src_hbm, first_row, dst_vmem, slot, r, sem):
    return pltpu.make_async_copy(src_hbm.at[pl.ds(pl.multiple_of(first_row, ROW_TILES), ROW_TILES)],
                                 dst_vmem.at[slot, pl.ds(r * ROW_TILES, ROW_TILES)], sem.at[slot])


def _moe_kernel(be_ref, nu_ref, *refs, precise, gather):
    if gather:
        rowc_ref, rown_ref, src_ref, wg_ref, wu_ref, wd_ref, o_ref = refs[:7]
        scr, xbuf, sem = refs[7:-2], refs[-2], refs[-1]
    else:
        x_ref, wg_ref, wu_ref, wd_ref, o_ref = refs[:5]
        scr = refs[5:]
    i = pl.program_id(0)
    n_used = nu_ref[0]
    used = i < n_used
    new_expert = jnp.logical_or(i == 0, be_ref[i] != be_ref[jnp.maximum(i - 1, 0)])

    if gather:
        slot = lax.rem(i, 2)

        def issue(rows_ref, s):
            for r in range(MOE_BM):
                _row_gather_copy(src_ref, rows_ref[0, 0, r], xbuf, s, r, sem).start(priority=r % 2)

        @pl.when(jnp.logical_and(used, i == 0))
        def _():
            issue(rowc_ref, 0)

        @pl.when(i + 1 < n_used)
        def _():
            issue(rown_ref, 1 - slot)

    @pl.when(jnp.logical_and(used, new_expert))
    def _():
        for k, w_ref in enumerate((wg_ref, wu_ref, wd_ref)):
            w = w_ref[...]
            if precise:
                hi, lo = _split2(w)
                scr[k][...] = hi
                scr[3 + k][...] = lo
            else:
                scr[k][...] = w.astype(BF16)

    @pl.when(used)
    def _():
        if gather:
            pltpu.make_async_copy(src_ref.at[pl.ds(0, MOE_BM * ROW_TILES)], xbuf.at[slot], sem.at[slot]).wait()
            xb = xbuf.at[slot]
            x = jnp.concatenate([xb[pl.ds(s, MOE_BM, stride=ROW_TILES), :] for s in range(ROW_TILES)], axis=1)
        else:
            x = x_ref[...]
        lo = (scr[3][...], scr[4][...], scr[5][...]) if precise else (None, None, None)
        g = _mm_w(x, scr[0][...], lo[0])
        u = _mm_w(x, scr[1][...], lo[1])
        o_ref[...] = _mm_w(_silu(g) * u, scr[2][...], lo[2])

    @pl.when(jnp.logical_not(used))
    def _():
        o_ref[...] = jnp.zeros_like(o_ref)


def _moe(xs, block_expert, n_used, weights, layer, precise, slot_rows=None, bm=MOE_BM):
    gather = slot_rows is not None
    P = slot_rows.shape[0] if gather else xs.shape[0]
    bm = MOE_BM if gather else bm
    nblk = P // bm
    in_w = pl.BlockSpec((None, None, D_MODEL, D_EXPERT), lambda i, be, nu: (layer, be[i], 0, 0))
    out_w = pl.BlockSpec((None, None, D_EXPERT, D_MODEL), lambda i, be, nu: (layer, be[i], 0, 0))
    w_scratch = [pltpu.VMEM((D_MODEL, D_EXPERT), BF16), pltpu.VMEM((D_MODEL, D_EXPERT), BF16),
                 pltpu.VMEM((D_EXPERT, D_MODEL), BF16)] * (2 if precise else 1)
    if gather:
        rows3 = slot_rows.reshape(nblk, 1, MOE_BM)
        x_specs = [
            pl.BlockSpec((1, 1, MOE_BM), lambda i, be, nu: (i, 0, 0), memory_space=pltpu.SMEM),
            pl.BlockSpec((1, 1, MOE_BM), lambda i, be, nu: (jnp.minimum(i + 1, nblk - 1), 0, 0),
                         memory_space=pltpu.SMEM),
            pl.BlockSpec(memory_space=pl.ANY),
        ]
        x_args = [rows3, rows3, xs]
        extra_scratch = [pltpu.VMEM((2, MOE_BM * ROW_TILES, LANES), xs.dtype), pltpu.SemaphoreType.DMA((2,))]
    else:
        x_specs = [pl.BlockSpec((bm, D_MODEL), lambda i, be, nu: (i, 0))]
        x_args = [xs]
        extra_scratch = []
    grid_spec = pltpu.PrefetchScalarGridSpec(
        num_scalar_prefetch=2,
        grid=(nblk,),
        in_specs=x_specs + [in_w, in_w, out_w],
        out_specs=pl.BlockSpec((bm, D_MODEL), lambda i, be, nu: (i, 0)),
        scratch_shapes=w_scratch + extra_scratch,
    )
    return pl.pallas_call(
        functools.partial(_moe_kernel, precise=precise, gather=gather),
        grid_spec=grid_spec,
        out_shape=jax.ShapeDtypeStruct((P, D_MODEL), F32),
        compiler_params=_cparams(("arbitrary",)),
        name="moe_gather" if gather else "moe",
    )(block_expert, n_used, *x_args, *weights)


def _combine_kernel(x1_ref, ya_ref, yb_ref, route_ref, g2_ref, lng_ref, lnb_ref, o_ref):
    tm = x1_ref.shape[1]
    lane = lax.broadcasted_iota(jnp.int32, (tm, LANES), 1)
    rt = route_ref[0]
    w1 = jnp.sum(jnp.where(lane == 2, rt, 0.0), axis=1, keepdims=True)
    w2 = jnp.sum(jnp.where(lane == 3, rt, 0.0), axis=1, keepdims=True)
    moe = ya_ref[...] * w1 + yb_ref[...] * w2
    o_ref[0] = _layernorm(ALPHA * x1_ref[0] + (1.0 + g2_ref[0]) * moe, lng_ref[...], lnb_ref[...])


def _combine(x1, ya, yb, row_off, route, g2, p, per_token_mod):
    B, L, _ = x1.shape
    tm = _pick_tm(L, PROJ_TM)
    assert row_off % tm == 0
    row = lambda b, i: (b, i, 0)
    fixed = lambda b, i: (0, 0)
    flat = lambda b, i: (row_off // tm + b * (L // tm) + i, 0)
    if per_token_mod:
        mod_spec = pl.BlockSpec((1, tm, D_MODEL), row)
    else:
        mod_spec = pl.BlockSpec((1, 1, D_MODEL), lambda b, i: (b, 0, 0))
    return pl.pallas_call(
        _combine_kernel,
        grid=(B, L // tm),
        in_specs=[
            pl.BlockSpec((1, tm, D_MODEL), row),
            pl.BlockSpec((tm, D_MODEL), flat),
            pl.BlockSpec((tm, D_MODEL), flat),
            pl.BlockSpec((1, tm, LANES), row),
            mod_spec,
            pl.BlockSpec((1, D_MODEL), fixed),
            pl.BlockSpec((1, D_MODEL), fixed),
        ],
        out_specs=pl.BlockSpec((1, tm, D_MODEL), row),
        out_shape=jax.ShapeDtypeStruct((B, L, D_MODEL), F32),
        compiler_params=_cparams(("arbitrary", "arbitrary")),
        name="combine",
    )(x1, ya, yb, route, g2, p["ln2_g"], p["ln2_b"])


def _dispatch_plan(eidx, bm):
    T = eidx.shape[0]
    A = 2 * T
    flat_e = eidx.reshape(A)
    onehot = (flat_e[:, None] == jnp.arange(N_EXPERTS, dtype=jnp.int32)[None, :]).astype(jnp.int32)
    cum = jnp.cumsum(onehot, axis=0)
    rank = jnp.take_along_axis(cum, flat_e[:, None], axis=1)[:, 0] - 1
    counts = cum[-1]
    padded = (counts + bm - 1) // bm * bm
    pad_end = jnp.cumsum(padded)
    pad_start = pad_end - padded
    dest = pad_start[flat_e] + rank
    nblk = (A + N_EXPERTS * (bm - 1) + bm - 1) // bm
    block_pos = jnp.arange(nblk, dtype=jnp.int32) * bm
    block_expert = jnp.minimum(jnp.sum(pad_end[None, :] <= block_pos[:, None], axis=1), N_EXPERTS - 1).astype(jnp.int32)
    n_used = (pad_end[-1] // bm).astype(jnp.int32).reshape(1)
    return dest, block_expert, n_used, nblk * bm


def _mods(mod_l, n_prompt, dec_seq):
    mp = mod_l[:n_prompt].reshape(n_prompt, 6, D_MODEL)
    ms = mod_l[n_prompt:].reshape(-1, 6, D_MODEL)
    prompt = [mp[:, j][:, None, :] for j in range(6)]
    sample = [jnp.repeat(ms[:, j], dec_seq, axis=0)[None] for j in range(6)]
    return prompt, sample


def kernel(x_prompt, x_sample, c_prompt, c_sample, state_ssd, state_ssd_conv, state_rglru, state_rglru_conv, state_ret, w_ada, b_ada, w_in, ssd_conv_w, ssd_conv_b, ssd_dt_bias, ssd_a_log, ssd_d, ssd_norm_w, rg_conv_w, rg_conv_b, rg_wa, rg_ba, rg_wx, rg_bx, rg_lambda, ret_gn_w, w_out, ln1_g, ln1_b, router_w, router_bias, exp_w_gate, exp_w_up, exp_w_down, ln2_g, ln2_b):
    BP, LP, _ = x_prompt.shape
    BS, LS, _ = x_sample.shape
    TP, TS = BP * LP, BS * LS

    def pad_lanes(v):
        return jnp.pad(v, ((0, 0), (0, LANES - v.shape[-1])))

    eye = jnp.eye(RG_BLOCKS, dtype=F32)

    def block_diag(w):
        return jnp.stack(_hi_lo(jnp.einsum("njk,nm->njmk", w, eye).reshape(D_RG, D_RG)))

    w_in_parts = _hi_lo(jnp.concatenate(
        [w_in[:, :, :2560], w_in[:, :, 2576:5648], w_in[:, :, 2560:2576],
         jnp.zeros((DEPTH, D_MODEL, LANES - SSD_HEADS), F32)], axis=2))

    params = []
    for l in range(DEPTH):
        params.append(dict(
            ssd_conv_w=ssd_conv_w[l], ssd_conv_b=ssd_conv_b[l][None],
            ssd_dt_bias=pad_lanes(ssd_dt_bias[l][None]), ssd_a_log=pad_lanes(ssd_a_log[l][None]),
            ssd_d_ch=jnp.repeat(ssd_d[l], SSD_HEAD_DIM)[None], ssd_norm_w=ssd_norm_w[l][None],
            rg_conv_w=rg_conv_w[l], rg_conv_b=rg_conv_b[l][None],
            rg_wa_bd=block_diag(rg_wa[l]), rg_ba=rg_ba[l][None],
            rg_wx_bd=block_diag(rg_wx[l]), rg_bx=rg_bx[l][None],
            rg_lambda=rg_lambda[l][None], ret_gn_w=ret_gn_w[l][None],
            w_out=jnp.stack(_hi_lo(w_out[l])), ln1_g=ln1_g[l][None], ln1_b=ln1_b[l][None],
            ln2_g=ln2_g[l][None], ln2_b=ln2_b[l][None],
        ))

    rw_hi, rw_lo = _hi_lo(pad_lanes(router_w))
    rbias = pad_lanes(router_bias[None])
    head_of_ch = jnp.arange(D_SSD) // SSD_HEAD_DIM
    consts = dict(
        rw_hi=rw_hi, rw_lo=rw_lo, rbias=rbias.astype(F32),
        head_expand=(jnp.arange(LANES)[:, None] == head_of_ch[None, :]).astype(BF16),
        tri=jnp.tril(jnp.ones((CHUNK, CHUNK), F32)).astype(BF16),
    )
    mod = _ada(jnp.concatenate([c_prompt, c_sample], axis=0), w_ada, b_ada)

    LM = LP - TAIL
    paths = [
        dict(x=x_prompt[:, :LM], B=BP, L=LM, per_token=False, has_state=False, precise=False, rc=_ret_consts(LM, 0.0)),
        dict(x=x_prompt[:, LM:], B=BP, L=TAIL, per_token=False, has_state=True, precise=True, rc=_ret_consts(TAIL, float(LM))),
        dict(x=x_sample.reshape(1, TS, D_MODEL), B=BS, L=LS, per_token=True, has_state=True, precise=False,
             rc=_ret_consts(LS, PAST_LEN)),
    ]
    zero_states = (jnp.zeros((BP, SSD_GROUPS, SSD_STATE, D_SSD // SSD_GROUPS), F32),
                   jnp.zeros((BP, CONV_WIDTH - 1, SSD_CONV_DIM), F32),
                   jnp.zeros((BP, 1, D_RG), F32),
                   jnp.zeros((BP, CONV_WIDTH - 1, D_RG), F32),
                   jnp.zeros((BP, RET_HEADS, RET_HEAD_DIM, RET_HEAD_DIM), F32))

    def mixer_stack(path, p, layer, mods, states, ssd_kw, ret_kw):
        B, L, per_token, precise = path["B"], path["L"], path["per_token"], path["precise"]
        z, xbc, rg, ret, dt = _inproj(path["x"], mods[1], mods[0], w_in_parts, layer, per_token, precise)
        if per_token:
            z, xbc, rg, ret, dt = [a.reshape(B, L, a.shape[-1]) for a in (z, xbc, rg, ret, dt)]
        s_ssd, s_cbuf, s_rg, s_rbuf, s_ret = states
        hs = path["has_state"]
        y_ssd, cbuf_n, ssd_n = _ssd(z, xbc, dt, s_ssd, s_cbuf, p, consts, hs, precise, **ssd_kw)
        y_rg, rbuf_n, rg_n = _rg(rg, s_rg, s_rbuf, p, hs, precise)
        y_ret, ret_n = _ret(ret, s_ret, p, path["rc"], hs, precise, **ret_kw)
        if per_token:
            y_ssd, y_rg, y_ret = [a.reshape(1, B * L, a.shape[-1]) for a in (y_ssd, y_rg, y_ret)]
        return (y_ssd, y_rg, y_ret), (ssd_n, cbuf_n, rg_n, rbuf_n, ret_n)

    expert_w = (exp_w_gate, exp_w_up, exp_w_down)

    def experts(h2_all, route_list, layer, precise):
        route_all = jnp.concatenate([r.reshape(-1, LANES) for r in route_list], axis=0)
        T_all = route_all.shape[0]
        bm = MOE_BM_TAIL if precise else MOE_BM
        dest, block_expert, n_used, P = _dispatch_plan(route_all[:, 0:2].astype(jnp.int32), bm)
        tok = jnp.arange(2 * T_all, dtype=jnp.int32) // 2
        if precise:
            slot_tok = jnp.full((P,), T_all, jnp.int32).at[dest].set(tok)
            xs_sorted = jnp.concatenate([h2_all, jnp.zeros((1, D_MODEL), h2_all.dtype)], axis=0)[slot_tok]
            ys_sorted = _moe(xs_sorted, block_expert, n_used, expert_w, layer, True, bm=bm)
        else:
            slot_row = jnp.zeros((P,), jnp.int32).at[dest].set(tok * ROW_TILES)
            ys_sorted = _moe(h2_all, block_expert, n_used, expert_w, layer, False, slot_rows=slot_row)
        return ys_sorted[dest[0::2]], ys_sorted[dest[1::2]]

    new_tail = []
    state_ssd_t = jnp.swapaxes(state_ssd, -1, -2)
    ssd_s = ret_s = None
    cbuf_s, rg_s, rbuf_s = [], [], []
    for l in range(DEPTH):
        p = params[l]
        mods_p, mods_s = _mods(mod[l], BP, LS)
        mods = [mods_p, mods_p, mods_s]
        st_s = (state_ssd_t, state_ssd_conv[l], state_rglru[l][:, None, :], state_rglru_conv[l], state_ret)

        ys_m, st_m = mixer_stack(paths[0], p, l, mods[0], zero_states, {}, {})
        ys_t, st_t = mixer_stack(paths[1], p, l, mods[1], st_m, dict(native_out=True), {})
        ys_c, st_c = mixer_stack(paths[2], p, l, mods[2], st_s,
                                 dict(h0_layer=l, native_out=True, out_stack=(l, ssd_s)),
                                 dict(h0_layer=l, out_stack=(l, ret_s)))
        new_tail.append(st_t)
        ssd_s, ret_s = st_c[0], st_c[4]
        cbuf_s.append(st_c[1])
        rg_s.append(st_c[2])
        rbuf_s.append(st_c[3])

        offs = (0, 0, BP * LM)
        group_tokens = BP * LM + TS
        post, h2_group = [None] * 3, None
        for k in (0, 2, 1):
            path, ys, m = paths[k], (ys_m, ys_t, ys_c)[k], mods[k]
            shared = None if path["precise"] else (group_tokens, offs[k], h2_group)
            post[k] = _outproj(*ys, path["x"], m[2], m[4], m[3], p, consts, path["per_token"], path["precise"], shared)
            if not path["precise"]:
                h2_group = post[k][1]

        ya_b, yb_b = experts(h2_group, [post[0][2], post[2][2]], l, False)
        ya_t, yb_t = experts(post[1][1].reshape(-1, D_MODEL), [post[1][2]], l, True)

        for path, (x1, _, route), ya, yb, off, m in zip(paths, post, (ya_b, ya_t, ya_b), (yb_b, yb_t, yb_b), offs, mods):
            path["x"] = _combine(x1, ya, yb, off, route, m[5], p, path["per_token"])

    def stack(lst, k):
        return jnp.stack([s[k] for s in lst])

    y_prompt = jnp.concatenate([paths[0]["x"], paths[1]["x"]], axis=1)
    return (y_prompt, paths[2]["x"].reshape(BS, LS, D_MODEL),
            jnp.swapaxes(stack(new_tail, 0), -1, -2), stack(new_tail, 1), stack(new_tail, 2)[:, :, 0],
            stack(new_tail, 3), stack(new_tail, 4),
            jnp.swapaxes(ssd_s, -1, -2), jnp.stack(cbuf_s), jnp.stack(rg_s)[:, :, 0], jnp.stack(rbuf_s), ret_s)
```

```python
import functools
import math

import jax
import jax.numpy as jnp
from jax import lax
from jax.experimental import pallas as pl
from jax.experimental.pallas import tpu as pltpu

F32 = jnp.float32
BF16 = jnp.bfloat16

D_MODEL = 1024
DEPTH = 4
D_SSD = 1024
SSD_HEAD_DIM = 64
SSD_HEADS = 16
SSD_GROUPS = 2
SSD_STATE = 128
SSD_CONV_DIM = D_SSD + 2 * SSD_GROUPS * SSD_STATE
CONV_WIDTH = 4
D_RG = 512
RG_BLOCKS = 8
RG_C = 8.0
D_RET = 512
RET_HEADS = 4
RET_HEAD_DIM = 128
ROPE_BASE = 10000.0
N_EXPERTS = 32
EXPERTS_PER_GROUP = 8
N_EXPERT_GROUPS = 4
D_EXPERT = 512
ALPHA = (2 * DEPTH) ** 0.25
EPS = 1e-5
PAST_LEN = 16384.0

LANES = 128
SUBLANES = 8
CHUNK = 128
TAIL = CHUNK
SEQS_PER_STEP = 8
SEQ_UNROLL = 4
ROW_TILES = D_MODEL // LANES
PROJ_TM = 384
MOE_BM = 256
MOE_BM_TAIL = 128
VMEM_LIMIT = 56 * 1024 * 1024


def _cparams(sem):
    return pltpu.CompilerParams(dimension_semantics=sem, vmem_limit_bytes=VMEM_LIMIT)


def _padded_rows(L):
    return CHUNK if L >= CHUNK else -(-L // SUBLANES) * SUBLANES


def _pick_tm(L, cap):
    if L <= LANES:
        return L
    tm = cap - cap % LANES
    while L % tm:
        tm -= LANES
    return tm


def _sigmoid(x):
    return 1.0 / (1.0 + jnp.exp(-x))


def _silu(x):
    return x * _sigmoid(x)


def _softplus(x):
    return jnp.maximum(x, 0.0) + jnp.log1p(jnp.exp(-jnp.abs(x)))


def _split3(v):
    hi = v.astype(BF16)
    r = v - hi.astype(F32)
    mid = r.astype(BF16)
    lo = (r - mid.astype(F32)).astype(BF16)
    return hi, mid, lo


def _dot(a, b):
    return jnp.dot(a, b, preferred_element_type=F32)


def _split2(v):
    hi = v.astype(BF16)
    return hi, (v - hi.astype(F32)).astype(BF16)


def _mm_w(a, w_hi, w_lo):
    if w_lo is None:
        return _dot(a.astype(BF16), w_hi)
    ah, al = _split2(a)
    return _dot(ah, w_hi) + (_dot(al, w_hi) + _dot(ah, w_lo))


def _mm_a(a, b, precise, nt=False):
    dn = (((1,), (1,)), ((), ())) if nt else (((1,), (0,)), ((), ()))

    def d(x, y):
        return lax.dot_general(x, y, dn, preferred_element_type=F32)

    if not precise:
        return d(a.astype(BF16), b.astype(BF16))
    ah, al = _split2(a)
    bh, bl = _split2(b)
    return d(ah, bh) + (d(al, bh) + d(ah, bl))


def _hi_lo(w):
    hi32 = lax.reduce_precision(w, exponent_bits=8, mantissa_bits=7)
    return hi32.astype(BF16), (w - hi32).astype(BF16)


def _dot_exact_rhs(v, m):
    hi, mid, lo = _split3(v)
    return _dot(hi, m) + _dot(mid, m) + _dot(lo, m)


def _dot_exact_lhs(m, v):
    hi, mid, lo = _split3(v)
    return _dot(m, hi) + _dot(m, mid) + _dot(m, lo)


def _ada_kernel(c_ref, w_ref, b_ref, o_ref):
    o_ref[0] = _mm_a(_silu(c_ref[...]), w_ref[0], True) + b_ref[0]


def _ada(c_all, w_ada, b_ada):
    n = c_all.shape[0]
    tn = 1024
    return pl.pallas_call(
        _ada_kernel,
        grid=(DEPTH, 6 * D_MODEL // tn),
        in_specs=[
            pl.BlockSpec((n, D_MODEL), lambda l, j: (0, 0)),
            pl.BlockSpec((1, D_MODEL, tn), lambda l, j: (l, 0, j)),
            pl.BlockSpec((1, 1, tn), lambda l, j: (l, 0, j)),
        ],
        out_specs=pl.BlockSpec((1, n, tn), lambda l, j: (l, 0, j)),
        out_shape=jax.ShapeDtypeStruct((DEPTH, n, 6 * D_MODEL), F32),
        compiler_params=_cparams(("arbitrary", "arbitrary")),
        name="ada",
    )(c_all, w_ada, b_ada.reshape(DEPTH, 1, 6 * D_MODEL))


_IN_SEGS = (("z", 0, 1024), ("xbc", 1024, 2560), ("rg", 2560, 3584), ("ret", 3584, 5632), ("dt", 5632, 5760))
IN_PAD = 5760


def _inproj_kernel(x_ref, sc_ref, sh_ref, w_ref, *rest, precise):
    wl_ref = rest[0] if precise else None
    outs = rest[1:] if precise else rest
    h = x_ref[0] * (1.0 + sc_ref[0]) + sh_ref[0]
    if precise:
        hh, hl = _split2(h)
    else:
        hh = h.astype(BF16)
    for o_ref, (_, lo, hi) in zip(outs, _IN_SEGS):
        acc = _dot(hh, w_ref[:, lo:hi])
        if precise:
            acc = acc + (_dot(hl, w_ref[:, lo:hi]) + _dot(hh, wl_ref[:, lo:hi]))
        o_ref[0] = acc


def _inproj(x, sc, sh, w, layer, per_token_mod, precise):
    B, L, _ = x.shape
    tm = _pick_tm(L, LANES if precise else PROJ_TM)
    if per_token_mod:
        mod_spec = pl.BlockSpec((1, tm, D_MODEL), lambda b, i: (b, i, 0))
    else:
        mod_spec = pl.BlockSpec((1, 1, D_MODEL), lambda b, i: (b, 0, 0))
    widths = [hi - lo for _, lo, hi in _IN_SEGS]
    w_spec = pl.BlockSpec((None, D_MODEL, IN_PAD), lambda b, i: (layer, 0, 0), pipeline_mode=pl.Buffered(1))
    ws = w if precise else w[:1]
    return pl.pallas_call(
        functools.partial(_inproj_kernel, precise=precise),
        grid=(B, L // tm),
        in_specs=[
            pl.BlockSpec((1, tm, D_MODEL), lambda b, i: (b, i, 0)),
            mod_spec,
            mod_spec,
        ] + [w_spec] * len(ws),
        out_specs=[pl.BlockSpec((1, tm, wd), lambda b, i: (b, i, 0)) for wd in widths],
        out_shape=[jax.ShapeDtypeStruct((B, L, wd), F32) for wd in widths],
        compiler_params=_cparams(("arbitrary", "arbitrary")),
        name="inproj",
    )(x, sc, sh, *ws)


def _conv_step(src_rows, buf_ref, cw_ref, cb_ref, nbuf_ref, xp_s, *, c, nc, Lv, Lp, has_state):
    @_when_unless(nc == 1, c == 0)
    def _():
        xp_s[...] = jnp.zeros_like(xp_s)
        if has_state:
            xp_s[SUBLANES - (CONV_WIDTH - 1):SUBLANES, :] = buf_ref[0]

    xp_s[SUBLANES:SUBLANES + Lv, :] = src_rows
    base = SUBLANES - (CONV_WIDTH - 1)
    xc = cb_ref[...]
    for j in range(CONV_WIDTH):
        xc = xc + cw_ref[j:j + 1, :] * xp_s[base + j:base + j + Lp, :]

    @_when_unless(nc == 1, c == nc - 1)
    def _():
        nbuf_ref[0] = xp_s[SUBLANES + Lv - (CONV_WIDTH - 1):SUBLANES + Lv, :]

    if nc > 1:
        xp_s[0:SUBLANES, :] = xp_s[Lp:Lp + SUBLANES, :]
    return xc


def _for_each_sequence(body, refs, bb, keep, squeeze, n_scratch):
    n = len(refs)

    def one(s, j):
        v = list(refs)
        for k in keep:
            v[k] = refs[k].at[pl.ds(s, 1)]
        for k in squeeze:
            v[k] = refs[k].at[s]
        if bb > 1:
            for k in range(n - n_scratch, n):
                v[k] = refs[k].at[j]
        body(*v)

    if bb == 1:
        one(0, 0)
    else:
        def step(i, carry):
            for j in range(SEQ_UNROLL):
                one(i * SEQ_UNROLL + j, j)
            return carry
        lax.fori_loop(0, bb // SEQ_UNROLL, step, 0)


def _when_unless(always, cond):
    def deco(fn):
        if always:
            fn()
        else:
            pl.when(cond)(fn)
        return fn
    return deco


def _seq_scratch(shape, bb):
    return pltpu.VMEM(((SEQ_UNROLL,) + shape) if bb > 1 else shape, F32)


def _ssd_kernel(*refs, bb, **kw):
    n = len(refs)
    _for_each_sequence(functools.partial(_ssd_body, **kw), refs, bb,
                       keep=(0, 1, 2, 4, n - 6, n - 5), squeeze=(3, n - 4), n_scratch=3)


def _ssd_body(z_ref, xbc_ref, dt_ref, h0_ref, buf_ref, cw_ref, cb_ref, dtb_ref, alog_ref, dch_ref,
              nw_ref, e_ref, tri_ref, *rest, Lv, Lp, nc, has_state, precise, native_in, native_out):
    y_ref, nbuf_ref, hn_ref, xp_s, dtp_s, s_s = rest[-6:]
    c = pl.program_id(1)
    hpg = SSD_HEADS // SSD_GROUPS

    @_when_unless(nc == 1, c == 0)
    def _():
        if not has_state:
            s_s[...] = jnp.zeros_like(s_s)
        elif native_in:
            for g in range(SSD_GROUPS):
                s_s[g] = jnp.concatenate([h0_ref[g * hpg + i] for i in range(hpg)], axis=0).T
        else:
            s_s[...] = h0_ref[...]
        if Lv < Lp:
            dtp_s[...] = jnp.zeros_like(dtp_s)

    xc = _conv_step(xbc_ref[0], buf_ref, cw_ref, cb_ref, nbuf_ref, xp_s,
                    c=c, nc=nc, Lv=Lv, Lp=Lp, has_state=has_state)
    xbc = _silu(xc)
    xs = xbc[:, 0:D_SSD]
    Bm = xbc[:, D_SSD:D_SSD + 256]
    Cm = xbc[:, D_SSD + 256:D_SSD + 512]

    if Lv < Lp:
        dtp_s[0:Lv, :] = dt_ref[0]
        dtr = dtp_s[...]
    else:
        dtr = dt_ref[0]
    dt = _softplus(dtr + dtb_ref[...])
    if Lv < Lp:
        rowv = lax.broadcasted_iota(jnp.int32, (Lp, LANES), 0)
        dt = jnp.where(rowv < Lv, dt, 0.0)
    a = dt * (-jnp.exp(alog_ref[...]))
    acum = _dot_exact_lhs(tri_ref[...], a)
    dt_e = _dot_exact_rhs(dt, e_ref[...])
    ac_e = _dot_exact_rhs(acum, e_ref[...])
    acl_e = ac_e[Lp - 1:Lp, :]
    acum_t = acum.T

    vdt = xs * dt_e
    lane = lax.broadcasted_iota(jnp.int32, (Lp, LANES), 1)
    causal = lax.broadcasted_iota(jnp.int32, (Lp, Lp), 0) >= lax.broadcasted_iota(jnp.int32, (Lp, Lp), 1)
    mid_t = F32 if precise else BF16

    y_blocks = []
    for g in range(SSD_GROUPS):
        Cg = Cm[:, g * SSD_STATE:(g + 1) * SSD_STATE]
        Bg = Bm[:, g * SSD_STATE:(g + 1) * SSD_STATE]
        G = _mm_a(Cg, Bg, precise, nt=True)
        for jj in range(4):
            j = 4 * g + jj
            scs = []
            for h in (2 * j, 2 * j + 1):
                col = jnp.sum(jnp.where(lane == h, acum, 0.0), axis=1, keepdims=True)
                rw = acum_t[h:h + 1, :]
                dec = jnp.exp(jnp.where(causal, col - rw, -jnp.inf))
                scs.append((G * dec).astype(mid_t))
            sc = jnp.concatenate(scs, axis=1)
            vb = vdt[:, j * LANES:(j + 1) * LANES]
            v2 = jnp.concatenate([jnp.where(lane < SSD_HEAD_DIM, vb, 0.0),
                                  jnp.where(lane >= SSD_HEAD_DIM, vb, 0.0)], axis=0).astype(mid_t)
            y_blocks.append(_mm_a(sc, v2, precise))
    y = jnp.concatenate(y_blocks, axis=1)

    eac = jnp.exp(ac_e)
    vw = vdt * jnp.exp(acl_e - ac_e)
    cd = jnp.exp(acl_e)
    half = D_SSD // SSD_GROUPS
    ys_parts = []
    for g in range(SSD_GROUPS):
        Sg = s_s[g]
        ys_parts.append(_mm_a(Cm[:, g * SSD_STATE:(g + 1) * SSD_STATE], Sg, precise))
        BgT = Bm[:, g * SSD_STATE:(g + 1) * SSD_STATE].T
        s_s[g] = cd[:, g * half:(g + 1) * half] * Sg + _mm_a(BgT, vw[:, g * half:(g + 1) * half], precise)
    y = y + eac * jnp.concatenate(ys_parts, axis=1) + dch_ref[...] * xs

    yv = y[0:Lv, :] if Lv < Lp else y
    yz = yv * _silu(z_ref[0])
    outs = []
    for g in range(SSD_GROUPS):
        part = yz[:, g * half:(g + 1) * half]
        ms = jnp.mean(part * part, axis=1, keepdims=True)
        outs.append(part * lax.rsqrt(ms + EPS))
    y_ref[0] = jnp.concatenate(outs, axis=1) * nw_ref[...]

    @_when_unless(nc == 1, c == nc - 1)
    def _():
        if native_out:
            for g in range(SSD_GROUPS):
                sgt = s_s[g].T
                for i in range(hpg):
                    hn_ref[g * hpg + i] = sgt[i * SSD_HEAD_DIM:(i + 1) * SSD_HEAD_DIM, :]
        else:
            hn_ref[...] = s_s[...]


def _state_specs(B, bb, shape, in_layer, out_stack):
    zeros = (0,) * len(shape)
    if in_layer is None:
        in_spec = pl.BlockSpec((bb,) + shape, lambda b, c: (b,) + zeros)
    else:
        in_spec = pl.BlockSpec((None, bb) + shape, lambda b, c: (in_layer, b) + zeros)
    if out_stack is None:
        out_spec = pl.BlockSpec((bb,) + shape, lambda b, c: (b,) + zeros)
        out_shape = jax.ShapeDtypeStruct((B,) + shape, F32)
    else:
        l = out_stack[0]
        out_spec = pl.BlockSpec((None, bb) + shape, lambda b, c: (l, b) + zeros)
        out_shape = jax.ShapeDtypeStruct((DEPTH, B) + shape, F32)
    return in_spec, out_spec, out_shape


def _seqs_per_step(B, L):
    return SEQS_PER_STEP if (L < CHUNK and B % SEQS_PER_STEP == 0) else 1


def _ssd(z, xbc, dt, h0, buf, p, consts, has_state, precise, h0_layer=None, native_out=False, out_stack=None):
    B, L, _ = z.shape
    Lv = min(CHUNK, L)
    Lp = _padded_rows(L)
    nc = L // Lv
    native_in = h0_layer is not None
    bb = _seqs_per_step(B, L)
    kern = functools.partial(_ssd_kernel, bb=bb, Lv=Lv, Lp=Lp, nc=nc, has_state=has_state, precise=precise,
                             native_in=native_in, native_out=native_out)
    row = lambda b, c: (b, c, 0)
    per_b3 = lambda b, c: (b, 0, 0)
    fixed = lambda b, c: (0, 0)
    half = D_SSD // SSD_GROUPS
    native = (SSD_HEADS, SSD_HEAD_DIM, SSD_STATE)
    packed = (SSD_GROUPS, SSD_STATE, half)
    h0_spec, _, _ = _state_specs(B, bb, native if native_in else packed, h0_layer, None)
    _, hn_spec, hn_shape = _state_specs(B, bb, native if native_out else packed, None, out_stack)
    prev = out_stack[1] if out_stack is not None else None
    extra_in, extra_specs, aliases = [], [], {}
    if prev is not None:
        extra_in, extra_specs, aliases = [prev], [pl.BlockSpec(memory_space=pl.ANY)], {13: 2}
    return pl.pallas_call(
        kern,
        grid=(B // bb, nc),
        in_specs=[
            pl.BlockSpec((bb, Lv, D_SSD), row),
            pl.BlockSpec((bb, Lv, SSD_CONV_DIM), row),
            pl.BlockSpec((bb, Lv, LANES), row),
            h0_spec,
            pl.BlockSpec((bb, CONV_WIDTH - 1, SSD_CONV_DIM), per_b3),
            pl.BlockSpec((CONV_WIDTH, SSD_CONV_DIM), fixed),
            pl.BlockSpec((1, SSD_CONV_DIM), fixed),
            pl.BlockSpec((1, LANES), fixed),
            pl.BlockSpec((1, LANES), fixed),
            pl.BlockSpec((1, D_SSD), fixed),
            pl.BlockSpec((1, D_SSD), fixed),
            pl.BlockSpec((LANES, D_SSD), fixed),
            pl.BlockSpec((Lp, Lp), fixed),
        ] + extra_specs,
        out_specs=[
            pl.BlockSpec((bb, Lv, D_SSD), row),
            pl.BlockSpec((bb, CONV_WIDTH - 1, SSD_CONV_DIM), per_b3),
            hn_spec,
        ],
        out_shape=[
            jax.ShapeDtypeStruct((B, L, D_SSD), F32),
            jax.ShapeDtypeStruct((B, CONV_WIDTH - 1, SSD_CONV_DIM), F32),
            hn_shape,
        ],
        scratch_shapes=[
            _seq_scratch((SUBLANES + Lp, SSD_CONV_DIM), bb),
            _seq_scratch((Lp, LANES), bb),
            _seq_scratch((SSD_GROUPS, SSD_STATE, half), bb),
        ],
        input_output_aliases=aliases,
        compiler_params=_cparams(("arbitrary", "arbitrary")),
        name="ssd",
    )(z, xbc, dt, h0, buf, p["ssd_conv_w"], p["ssd_conv_b"], p["ssd_dt_bias"], p["ssd_a_log"],
      p["ssd_d_ch"], p["ssd_norm_w"], consts["head_expand"],
      jnp.tril(jnp.ones((Lp, Lp), F32)).astype(BF16), *extra_in)


def _rg_kernel(*refs, bb, **kw):
    n = len(refs)
    _for_each_sequence(functools.partial(_rg_body, **kw), refs, bb,
                       keep=(0, 1, 2, n - 5, n - 4, n - 3), squeeze=(), n_scratch=2)


def _rg_body(rg_ref, h0_ref, buf_ref, cw_ref, cb_ref, wa_ref, ba_ref, wx_ref, bx_ref, lam_ref,
             y_ref, nbuf_ref, hn_ref, xp_s, hc_s, *, Lv, Lp, nc, has_state, precise):
    c = pl.program_id(1)

    @_when_unless(nc == 1, c == 0)
    def _():
        if has_state:
            hc_s[...] = h0_ref[0]
        else:
            hc_s[...] = jnp.zeros_like(hc_s)

    xc = _conv_step(rg_ref[0, :, 0:D_RG], buf_ref, cw_ref, cb_ref, nbuf_ref, xp_s,
                    c=c, nc=nc, Lv=Lv, Lp=Lp, has_state=has_state)
    r = _sigmoid(_mm_w(xc, wa_ref[0], wa_ref[1] if precise else None) + ba_ref[...])
    i = _sigmoid(_mm_w(xc, wx_ref[0], wx_ref[1] if precise else None) + bx_ref[...])
    la = (-RG_C * _softplus(-lam_ref[...])) * r
    u = jnp.sqrt(-jnp.tanh(la) * (jnp.exp(2.0 * la) + 1.0)) * (i * xc)
    row = lax.broadcasted_iota(jnp.int32, (Lp, D_RG), 0)
    if Lv < Lp:
        valid = row < Lv
        la = jnp.where(valid, la, 0.0)
        u = jnp.where(valid, u, 0.0)
    a = jnp.exp(la)
    s = 1
    while s < Lp:
        m = row >= s
        u_sh = jnp.where(m, pltpu.roll(u, s, 0), 0.0)
        a_sh = jnp.where(m, pltpu.roll(a, s, 0), 1.0)
        u = u + a * u_sh
        a = a * a_sh
        s *= 2
    h = u + a * hc_s[...]
    hc_s[...] = h[Lp - 1:Lp, :]
    hv = h[0:Lv, :] if Lv < Lp else h
    y_ref[0] = hv * jax.nn.gelu(rg_ref[0, :, D_RG:2 * D_RG])

    @_when_unless(nc == 1, c == nc - 1)
    def _():
        hn_ref[0] = hc_s[...]


def _rg(rg, h0, buf, p, has_state, precise):
    B, L, _ = rg.shape
    Lv = min(CHUNK, L)
    Lp = _padded_rows(L)
    nc = L // Lv
    bb = _seqs_per_step(B, L)
    kern = functools.partial(_rg_kernel, bb=bb, Lv=Lv, Lp=Lp, nc=nc, has_state=has_state, precise=precise)
    row = lambda b, c: (b, c, 0)
    per_b3 = lambda b, c: (b, 0, 0)
    fixed = lambda b, c: (0, 0)
    fixed3 = lambda b, c: (0, 0, 0)
    return pl.pallas_call(
        kern,
        grid=(B // bb, nc),
        in_specs=[
            pl.BlockSpec((bb, Lv, 2 * D_RG), row),
            pl.BlockSpec((bb, 1, D_RG), per_b3),
            pl.BlockSpec((bb, CONV_WIDTH - 1, D_RG), per_b3),
            pl.BlockSpec((CONV_WIDTH, D_RG), fixed),
            pl.BlockSpec((1, D_RG), fixed),
            pl.BlockSpec((2, D_RG, D_RG), fixed3),
            pl.BlockSpec((1, D_RG), fixed),
            pl.BlockSpec((2, D_RG, D_RG), fixed3),
            pl.BlockSpec((1, D_RG), fixed),
            pl.BlockSpec((1, D_RG), fixed),
        ],
        out_specs=[
            pl.BlockSpec((bb, Lv, D_RG), row),
            pl.BlockSpec((bb, CONV_WIDTH - 1, D_RG), per_b3),
            pl.BlockSpec((bb, 1, D_RG), per_b3),
        ],
        out_shape=[
            jax.ShapeDtypeStruct((B, L, D_RG), F32),
            jax.ShapeDtypeStruct((B, CONV_WIDTH - 1, D_RG), F32),
            jax.ShapeDtypeStruct((B, 1, D_RG), F32),
        ],
        scratch_shapes=[
            _seq_scratch((SUBLANES + Lp, D_RG), bb),
            _seq_scratch((1, D_RG), bb),
        ],
        compiler_params=_cparams(("arbitrary", "arbitrary")),
        name="rglru",
    )(rg, h0, buf, p["rg_conv_w"], p["rg_conv_b"], p["rg_wa_bd"], p["rg_ba"], p["rg_wx_bd"],
      p["rg_bx"], p["rg_lambda"])


def _ret_kernel(*refs, bb, **kw):
    n = len(refs)
    _for_each_sequence(functools.partial(_ret_body, **kw), refs, bb, keep=(0, n - 4), squeeze=(1, n - 3),
                       n_scratch=2)


def _ret_body(ret_ref, h0_ref, cos_ref, sin_ref, dm_ref, ea_ref, te_ref, cd_ref, gn_ref,
              *rest, Lv, Lp, nc, has_state, precise):
    y_ref, hn_ref, pad_s, s_s = rest[-4:]
    c = pl.program_id(1)

    @_when_unless(nc == 1, c == 0)
    def _():
        if has_state:
            s_s[...] = h0_ref[...]
        else:
            s_s[...] = jnp.zeros_like(s_s)
        if Lv < Lp:
            pad_s[...] = jnp.zeros_like(pad_s)

    if Lv < Lp:
        pad_s[0:Lv, :] = ret_ref[0, :, 0:3 * D_RET]
        qkv = pad_s[...]
    else:
        qkv = ret_ref[0, :, 0:3 * D_RET]
    cosf = cos_ref[...]
    sinf = sin_ref[...]
    outs = []
    for h in range(RET_HEADS):
        sl = slice(h * RET_HEAD_DIM, (h + 1) * RET_HEAD_DIM)
        q = qkv[:, sl]
        k = qkv[:, D_RET + h * RET_HEAD_DIM:D_RET + (h + 1) * RET_HEAD_DIM]
        v = qkv[:, 2 * D_RET + h * RET_HEAD_DIM:2 * D_RET + (h + 1) * RET_HEAD_DIM]
        q = q * cosf + pltpu.roll(q, RET_HEAD_DIM // 2, 1) * sinf
        k = (k * cosf + pltpu.roll(k, RET_HEAD_DIM // 2, 1) * sinf) * (RET_HEAD_DIM ** -0.5)
        G = _mm_a(q, k, precise, nt=True)
        S = s_s[h]
        y = _mm_a(G * dm_ref[h], v, precise) + _mm_a(q * ea_ref[h], S, precise)
        s_s[h] = cd_ref[h] * S + _mm_a((k * te_ref[h]).T, v, precise)
        yv = y[0:Lv, :] if Lv < Lp else y
        mu = jnp.mean(yv, axis=1, keepdims=True)
        d = yv - mu
        var = jnp.mean(d * d, axis=1, keepdims=True)
        outs.append(d * lax.rsqrt(var + EPS))
    yn = jnp.concatenate(outs, axis=1) * gn_ref[...]
    y_ref[0] = _silu(ret_ref[0, :, 3 * D_RET:4 * D_RET]) * yn

    @_when_unless(nc == 1, c == nc - 1)
    def _():
        hn_ref[...] = s_s[...]


def _ret(ret, h0, p, rc, has_state, precise, h0_layer=None, out_stack=None):
    B, L, _ = ret.shape
    Lv = min(CHUNK, L)
    Lp = _padded_rows(L)
    nc = L // Lv
    bb = _seqs_per_step(B, L)
    kern = functools.partial(_ret_kernel, bb=bb, Lv=Lv, Lp=Lp, nc=nc, has_state=has_state, precise=precise)
    row = lambda b, c: (b, c, 0)
    fixed2 = lambda b, c: (0, 0)
    fixed3 = lambda b, c: (0, 0, 0)
    hd = RET_HEAD_DIM
    h0_spec, _, _ = _state_specs(B, bb, (RET_HEADS, hd, hd), h0_layer, None)
    _, hn_spec, hn_shape = _state_specs(B, bb, (RET_HEADS, hd, hd), None, out_stack)
    prev = out_stack[1] if out_stack is not None else None
    extra_in, extra_specs, aliases = [], [], {}
    if prev is not None:
        extra_in, extra_specs, aliases = [prev], [pl.BlockSpec(memory_space=pl.ANY)], {9: 1}
    return pl.pallas_call(
        kern,
        grid=(B // bb, nc),
        in_specs=[
            pl.BlockSpec((bb, Lv, 4 * D_RET), row),
            h0_spec,
            pl.BlockSpec((Lp, hd), lambda b, c: (c, 0)),
            pl.BlockSpec((Lp, hd), lambda b, c: (c, 0)),
            pl.BlockSpec((RET_HEADS, Lp, Lp), fixed3),
            pl.BlockSpec((RET_HEADS, Lp, hd), fixed3),
            pl.BlockSpec((RET_HEADS, Lp, hd), fixed3),
            pl.BlockSpec((RET_HEADS, hd, hd), fixed3),
            pl.BlockSpec((1, D_RET), fixed2),
        ] + extra_specs,
        out_specs=[
            pl.BlockSpec((bb, Lv, D_RET), row),
            hn_spec,
        ],
        out_shape=[
            jax.ShapeDtypeStruct((B, L, D_RET), F32),
            hn_shape,
        ],
        scratch_shapes=[
            _seq_scratch((Lp, 3 * D_RET), bb),
            _seq_scratch((RET_HEADS, hd, hd), bb),
        ],
        input_output_aliases=aliases,
        compiler_params=_cparams(("arbitrary", "arbitrary")),
        name="retention",
    )(ret, h0, rc["cos"], rc["sin"], rc["dmat"], rc["eacum"], rc["toend"], rc["cdec"], p["ret_gn_w"], *extra_in)


def _ret_consts(L, pos0):
    Lv = min(CHUNK, L)
    Lp = _padded_rows(L)
    nc = L // Lv
    half = RET_HEAD_DIM // 2
    inv = ROPE_BASE ** (-jnp.arange(half, dtype=F32) / half)
    pos = pos0 + jnp.arange(L, dtype=F32)
    ang = pos[:, None] * inv[None, :]
    cos = jnp.cos(ang)
    sin = jnp.sin(ang)
    cosf = jnp.concatenate([cos, cos], axis=1)
    sinf = jnp.concatenate([-sin, sin], axis=1)
    if Lv < Lp:
        cosf = jnp.pad(cosf, ((0, Lp - Lv), (0, 0)))
        sinf = jnp.pad(sinf, ((0, Lp - Lv), (0, 0)))
    assert cosf.shape[0] == nc * Lp
    log_gamma = jnp.log1p(-jnp.exp2(-5.0 - jnp.arange(RET_HEADS, dtype=F32)))
    steps = jnp.minimum(jnp.arange(Lp) + 1, Lv).astype(F32)
    acum = log_gamma[:, None] * steps[None, :]
    causal = jnp.tril(jnp.ones((Lp, Lp), bool))
    dmat = jnp.exp(jnp.where(causal[None], acum[:, :, None] - acum[:, None, :], -jnp.inf))
    ones = jnp.ones((RET_HEADS, Lp, RET_HEAD_DIM), F32)
    eacum = jnp.exp(acum)[:, :, None] * ones
    toend = jnp.exp(acum[:, -1:] - acum)[:, :, None] * ones
    rowvalid = (jnp.arange(Lp) < Lv).astype(F32)[None, :, None]
    toend = toend * rowvalid
    cdec = jnp.exp(acum[:, -1])[:, None, None] * jnp.ones((RET_HEADS, RET_HEAD_DIM, RET_HEAD_DIM), F32)
    return dict(cos=cosf, sin=sinf, dmat=dmat, eacum=eacum, toend=toend, cdec=cdec)


def _layernorm(v, g, b):
    mu = jnp.mean(v, axis=1, keepdims=True)
    d = v - mu
    var = jnp.mean(d * d, axis=1, keepdims=True)
    return d * lax.rsqrt(var + EPS) * g + b


def _route(logits, bias):
    tm = logits.shape[0]
    lane = lax.broadcasted_iota(jnp.int32, (tm, LANES), 1)
    lanef = lane.astype(F32)
    scores = _sigmoid(logits)
    choice = scores + bias
    neg = -jnp.inf
    best = jnp.full((tm, 1), neg, F32)
    e1 = jnp.zeros((tm, 1), F32)
    e2 = jnp.zeros((tm, 1), F32)
    for g in range(N_EXPERT_GROUPS):
        ing = (lane >= g * EXPERTS_PER_GROUP) & (lane < (g + 1) * EXPERTS_PER_GROUP)
        cg = jnp.where(ing, choice, neg)
        m1 = jnp.max(cg, axis=1, keepdims=True)
        i1 = jnp.min(jnp.where(cg == m1, lanef, float(LANES)), axis=1, keepdims=True)
        cg2 = jnp.where(lanef == i1, neg, cg)
        m2 = jnp.max(cg2, axis=1, keepdims=True)
        i2 = jnp.min(jnp.where(cg2 == m2, lanef, float(LANES)), axis=1, keepdims=True)
        gs = m1 + m2
        better = gs > best
        best = jnp.where(better, gs, best)
        e1 = jnp.where(better, i1, e1)
        e2 = jnp.where(better, i2, e2)
    w1 = jnp.sum(jnp.where(lanef == e1, scores, 0.0), axis=1, keepdims=True)
    w2 = jnp.sum(jnp.where(lanef == e2, scores, 0.0), axis=1, keepdims=True)
    den = w1 + w2
    out = jnp.where(lane == 0, e1, jnp.where(lane == 1, e2, jnp.where(lane == 2, w1 / den, jnp.where(lane == 3, w2 / den, 0.0))))
    return out


def _outproj_kernel(ys_ref, yr_ref, yt_ref, x_ref, g1_ref, sc2_ref, sh2_ref, w_ref, lng_ref, lnb_ref,
                    rwh_ref, rwl_ref, rb_ref, *rest, precise, tile_rows):
    x1_ref, h2_ref, route_ref = rest[-3:]
    def part(y_ref, lo, hi):
        return _mm_w(y_ref[0], w_ref[0, lo:hi, :], w_ref[1, lo:hi, :] if precise else None)

    mix = part(ys_ref, 0, D_SSD) + part(yr_ref, D_SSD, D_SSD + D_RG) + part(yt_ref, D_SSD + D_RG, 2 * D_MODEL)
    x1 = _layernorm(ALPHA * x_ref[0] + (1.0 + g1_ref[0]) * mix, lng_ref[...], lnb_ref[...])
    x1_ref[0] = x1
    h2 = x1 * (1.0 + sc2_ref[0]) + sh2_ref[0]
    hi, lo = _split2(h2)
    if tile_rows:
        tm = h2.shape[0]
        for s in range(ROW_TILES):
            h2_ref[pl.ds(s, tm, stride=ROW_TILES), :] = h2[:, s * LANES:(s + 1) * LANES]
    else:
        h2_ref[0] = h2
    logits = _dot(hi, rwh_ref[...]) + (_dot(lo, rwh_ref[...]) + _dot(hi, rwl_ref[...]))
    route_ref[0] = _route(logits, rb_ref[...])


def _outproj(y_ssd, y_rg, y_ret, x, g1, sc2, sh2, p, consts, per_token_mod, precise, shared=None):
    B, L, _ = x.shape
    tm = _pick_tm(L, LANES if precise else PROJ_TM)
    tile_rows = not precise
    row = lambda b, i: (b, i, 0)
    fixed = lambda b, i: (0, 0)
    if per_token_mod:
        mod_spec = pl.BlockSpec((1, tm, D_MODEL), row)
    else:
        mod_spec = pl.BlockSpec((1, 1, D_MODEL), lambda b, i: (b, 0, 0))
    extra_in, extra_specs, aliases = [], [], {}
    if tile_rows:
        total, off, prev = shared
        assert off % tm == 0
        h2_spec = pl.BlockSpec((ROW_TILES * tm, LANES), lambda b, i: (off // tm + b * (L // tm) + i, 0))
        h2_shape = jax.ShapeDtypeStruct((ROW_TILES * total, LANES), F32)
        if prev is not None:
            extra_in, extra_specs, aliases = [prev], [pl.BlockSpec(memory_space=pl.ANY)], {13: 1}
    else:
        h2_spec = pl.BlockSpec((1, tm, D_MODEL), row)
        h2_shape = jax.ShapeDtypeStruct((B, L, D_MODEL), F32)
    return pl.pallas_call(
        functools.partial(_outproj_kernel, precise=precise, tile_rows=tile_rows),
        grid=(B, L // tm),
        in_specs=[
            pl.BlockSpec((1, tm, D_SSD), row),
            pl.BlockSpec((1, tm, D_RG), row),
            pl.BlockSpec((1, tm, D_RET), row),
            pl.BlockSpec((1, tm, D_MODEL), row),
            mod_spec, mod_spec, mod_spec,
            pl.BlockSpec((2, 2 * D_MODEL, D_MODEL), lambda b, i: (0, 0, 0)),
            pl.BlockSpec((1, D_MODEL), fixed),
            pl.BlockSpec((1, D_MODEL), fixed),
            pl.BlockSpec((D_MODEL, LANES), fixed),
            pl.BlockSpec((D_MODEL, LANES), fixed),
            pl.BlockSpec((1, LANES), fixed),
        ] + extra_specs,
        out_specs=[
            pl.BlockSpec((1, tm, D_MODEL), row),
            h2_spec,
            pl.BlockSpec((1, tm, LANES), row),
        ],
        out_shape=[
            jax.ShapeDtypeStruct((B, L, D_MODEL), F32),
            h2_shape,
            jax.ShapeDtypeStruct((B, L, LANES), F32),
        ],
        input_output_aliases=aliases,
        compiler_params=_cparams(("arbitrary", "arbitrary")),
        name="outproj",
    )(y_ssd, y_rg, y_ret, x, g1, sc2, sh2, p["w_out"], p["ln1_g"], p["ln1_b"],
      consts["rw_hi"], consts["rw_lo"], consts["rbias"], *extra_in)


def _row_gather_copy(src_hbm, first_row, dst_vmem, slot, r, sem):
    return pltpu.make_async_copy(src_hbm.at[pl.ds(pl.multiple_of(first_row, ROW_TILES), ROW_TILES)],
                                 dst_vmem.at[slot, pl.ds(r * ROW_TILES, ROW_TILES)], sem.at[slot])


def _moe_kernel(be_ref, nu_ref, *refs, precise, gather):
    if gather:
        rowc_ref, rown_ref, src_ref, wg_ref, wu_ref, wd_ref, o_ref = refs[:7]
        scr, xbuf, sem = refs[7:-2], refs[-2], refs[-1]
    else:
        x_ref, wg_ref, wu_ref, wd_ref, o_ref = refs[:5]
        scr = refs[5:]
    i = pl.program_id(0)
    n_used = nu_ref[0]
    used = i < n_used
    new_expert = jnp.logical_or(i == 0, be_ref[i] != be_ref[jnp.maximum(i - 1, 0)])

    if gather:
        slot = lax.rem(i, 2)

        def issue(rows_ref, s):
            for r in range(MOE_BM):
                _row_gather_copy(src_ref, rows_ref[0, 0, r], xbuf, s, r, sem).start(priority=r % 2)

        @pl.when(jnp.logical_and(used, i == 0))
        def _():
            issue(rowc_ref, 0)

        @pl.when(i + 1 < n_used)
        def _():
            issue(rown_ref, 1 - slot)

    @pl.when(jnp.logical_and(used, new_expert))
    def _():
        for k, w_ref in enumerate((wg_ref, wu_ref, wd_ref)):
            w = w_ref[...]
            if precise:
                hi, lo = _split2(w)
                scr[k][...] = hi
                scr[3 + k][...] = lo
            else:
                scr[k][...] = w.astype(BF16)

    @pl.when(used)
    def _():
        if gather:
            pltpu.make_async_copy(src_ref.at[pl.ds(0, MOE_BM * ROW_TILES)], xbuf.at[slot], sem.at[slot]).wait()
            xb = xbuf.at[slot]
            x = jnp.concatenate([xb[pl.ds(s, MOE_BM, stride=ROW_TILES), :] for s in range(ROW_TILES)], axis=1)
        else:
            x = x_ref[...]
        lo = (scr[3][...], scr[4][...], scr[5][...]) if precise else (None, None, None)
        g = _mm_w(x, scr[0][...], lo[0])
        u = _mm_w(x, scr[1][...], lo[1])
        o_ref[...] = _mm_w(_silu(g) * u, scr[2][...], lo[2])

    @pl.when(jnp.logical_not(used))
    def _():
        o_ref[...] = jnp.zeros_like(o_ref)


def _moe(xs, block_expert, n_used, weights, layer, precise, slot_rows=None, bm=MOE_BM):
    gather = slot_rows is not None
    P = slot_rows.shape[0] if gather else xs.shape[0]
    bm = MOE_BM if gather else bm
    nblk = P // bm
    in_w = pl.BlockSpec((None, None, D_MODEL, D_EXPERT), lambda i, be, nu: (layer, be[i], 0, 0))
    out_w = pl.BlockSpec((None, None, D_EXPERT, D_MODEL), lambda i, be, nu: (layer, be[i], 0, 0))
    w_scratch = [pltpu.VMEM((D_MODEL, D_EXPERT), BF16), pltpu.VMEM((D_MODEL, D_EXPERT), BF16),
                 pltpu.VMEM((D_EXPERT, D_MODEL), BF16)] * (2 if precise else 1)
    if gather:
        rows3 = slot_rows.reshape(nblk, 1, MOE_BM)
        x_specs = [
            pl.BlockSpec((1, 1, MOE_BM), lambda i, be, nu: (i, 0, 0), memory_space=pltpu.SMEM),
            pl.BlockSpec((1, 1, MOE_BM), lambda i, be, nu: (jnp.minimum(i + 1, nblk - 1), 0, 0),
                         memory_space=pltpu.SMEM),
            pl.BlockSpec(memory_space=pl.ANY),
        ]
        x_args = [rows3, rows3, xs]
        extra_scratch = [pltpu.VMEM((2, MOE_BM * ROW_TILES, LANES), xs.dtype), pltpu.SemaphoreType.DMA((2,))]
    else:
        x_specs = [pl.BlockSpec((bm, D_MODEL), lambda i, be, nu: (i, 0))]
        x_args = [xs]
        extra_scratch = []
    grid_spec = pltpu.PrefetchScalarGridSpec(
        num_scalar_prefetch=2,
        grid=(nblk,),
        in_specs=x_specs + [in_w, in_w, out_w],
        out_specs=pl.BlockSpec((bm, D_MODEL), lambda i, be, nu: (i, 0)),
        scratch_shapes=w_scratch + extra_scratch,
    )
    return pl.pallas_call(
        functools.partial(_moe_kernel, precise=precise, gather=gather),
        grid_spec=grid_spec,
        out_shape=jax.ShapeDtypeStruct((P, D_MODEL), F32),
        compiler_params=_cparams(("arbitrary",)),
        name="moe_gather" if gather else "moe",
    )(block_expert, n_used, *x_args, *weights)


def _combine_kernel(x1_ref, ya_ref, yb_ref, route_ref, g2_ref, lng_ref, lnb_ref, o_ref):
    tm = x1_ref.shape[1]
    lane = lax.broadcasted_iota(jnp.int32, (tm, LANES), 1)
    rt = route_ref[0]
    w1 = jnp.sum(jnp.where(lane == 2, rt, 0.0), axis=1, keepdims=True)
    w2 = jnp.sum(jnp.where(lane == 3, rt, 0.0), axis=1, keepdims=True)
    moe = ya_ref[...] * w1 + yb_ref[...] * w2
    o_ref[0] = _layernorm(ALPHA * x1_ref[0] + (1.0 + g2_ref[0]) * moe, lng_ref[...], lnb_ref[...])


def _combine(x1, ya, yb, row_off, route, g2, p, per_token_mod):
    B, L, _ = x1.shape
    tm = _pick_tm(L, PROJ_TM)
    assert row_off % tm == 0
    row = lambda b, i: (b, i, 0)
    fixed = lambda b, i: (0, 0)
    flat = lambda b, i: (row_off // tm + b * (L // tm) + i, 0)
    if per_token_mod:
        mod_spec = pl.BlockSpec((1, tm, D_MODEL), row)
    else:
        mod_spec = pl.BlockSpec((1, 1, D_MODEL), lambda b, i: (b, 0, 0))
    return pl.pallas_call(
        _combine_kernel,
        grid=(B, L // tm),
        in_specs=[
            pl.BlockSpec((1, tm, D_MODEL), row),
            pl.BlockSpec((tm, D_MODEL), flat),
            pl.BlockSpec((tm, D_MODEL), flat),
            pl.BlockSpec((1, tm, LANES), row),
            mod_spec,
            pl.BlockSpec((1, D_MODEL), fixed),
            pl.BlockSpec((1, D_MODEL), fixed),
        ],
        out_specs=pl.BlockSpec((1, tm, D_MODEL), row),
        out_shape=jax.ShapeDtypeStruct((B, L, D_MODEL), F32),
        compiler_params=_cparams(("arbitrary", "arbitrary")),
        name="combine",
    )(x1, ya, yb, route, g2, p["ln2_g"], p["ln2_b"])


def _dispatch_plan(eidx, bm):
    T = eidx.shape[0]
    A = 2 * T
    flat_e = eidx.reshape(A)
    onehot = (flat_e[:, None] == jnp.arange(N_EXPERTS, dtype=jnp.int32)[None, :]).astype(jnp.int32)
    cum = jnp.cumsum(onehot, axis=0)
    rank = jnp.take_along_axis(cum, flat_e[:, None], axis=1)[:, 0] - 1
    counts = cum[-1]
    padded = (counts + bm - 1) // bm * bm
    pad_end = jnp.cumsum(padded)
    pad_start = pad_end - padded
    dest = pad_start[flat_e] + rank
    nblk = (A + N_EXPERTS * (bm - 1) + bm - 1) // bm
    block_pos = jnp.arange(nblk, dtype=jnp.int32) * bm
    block_expert = jnp.minimum(jnp.sum(pad_end[None, :] <= block_pos[:, None], axis=1), N_EXPERTS - 1).astype(jnp.int32)
    n_used = (pad_end[-1] // bm).astype(jnp.int32).reshape(1)
    return dest, block_expert, n_used, nblk * bm


def _mods(mod_l, n_prompt, dec_seq):
    mp = mod_l[:n_prompt].reshape(n_prompt, 6, D_MODEL)
    ms = mod_l[n_prompt:].reshape(-1, 6, D_MODEL)
    prompt = [mp[:, j][:, None, :] for j in range(6)]
    sample = [jnp.repeat(ms[:, j], dec_seq, axis=0)[None] for j in range(6)]
    return prompt, sample


def kernel(x_prompt, x_sample, c_prompt, c_sample, state_ssd, state_ssd_conv, state_rglru, state_rglru_conv, state_ret, w_ada, b_ada, w_in, ssd_conv_w, ssd_conv_b, ssd_dt_bias, ssd_a_log, ssd_d, ssd_norm_w, rg_conv_w, rg_conv_b, rg_wa, rg_ba, rg_wx, rg_bx, rg_lambda, ret_gn_w, w_out, ln1_g, ln1_b, router_w, router_bias, exp_w_gate, exp_w_up, exp_w_down, ln2_g, ln2_b):
    BP, LP, _ = x_prompt.shape
    BS, LS, _ = x_sample.shape
    TP, TS = BP * LP, BS * LS

    def pad_lanes(v):
        return jnp.pad(v, ((0, 0), (0, LANES - v.shape[-1])))

    eye = jnp.eye(RG_BLOCKS, dtype=F32)

    def block_diag(w):
        return jnp.stack(_hi_lo(jnp.einsum("njk,nm->njmk", w, eye).reshape(D_RG, D_RG)))

    w_in_parts = _hi_lo(jnp.concatenate(
        [w_in[:, :, :2560], w_in[:, :, 2576:5648], w_in[:, :, 2560:2576],
         jnp.zeros((DEPTH, D_MODEL, LANES - SSD_HEADS), F32)], axis=2))

    params = []
    for l in range(DEPTH):
        params.append(dict(
            ssd_conv_w=ssd_conv_w[l], ssd_conv_b=ssd_conv_b[l][None],
            ssd_dt_bias=pad_lanes(ssd_dt_bias[l][None]), ssd_a_log=pad_lanes(ssd_a_log[l][None]),
            ssd_d_ch=jnp.repeat(ssd_d[l], SSD_HEAD_DIM)[None], ssd_norm_w=ssd_norm_w[l][None],
            rg_conv_w=rg_conv_w[l], rg_conv_b=rg_conv_b[l][None],
            rg_wa_bd=block_diag(rg_wa[l]), rg_ba=rg_ba[l][None],
            rg_wx_bd=block_diag(rg_wx[l]), rg_bx=rg_bx[l][None],
            rg_lambda=rg_lambda[l][None], ret_gn_w=ret_gn_w[l][None],
            w_out=jnp.stack(_hi_lo(w_out[l])), ln1_g=ln1_g[l][None], ln1_b=ln1_b[l][None],
            ln2_g=ln2_g[l][None], ln2_b=ln2_b[l][None],
        ))

    rw_hi, rw_lo = _hi_lo(pad_lanes(router_w))
    rbias = pad_lanes(router_bias[None])
    head_of_ch = jnp.arange(D_SSD) // SSD_HEAD_DIM
    consts = dict(
        rw_hi=rw_hi, rw_lo=rw_lo, rbias=rbias.astype(F32),
        head_expand=(jnp.arange(LANES)[:, None] == head_of_ch[None, :]).astype(BF16),
        tri=jnp.tril(jnp.ones((CHUNK, CHUNK), F32)).astype(BF16),
    )
    mod = _ada(jnp.concatenate([c_prompt, c_sample], axis=0), w_ada, b_ada)

    LM = LP - TAIL
    paths = [
        dict(x=x_prompt[:, :LM], B=BP, L=LM, per_token=False, has_state=False, precise=False, rc=_ret_consts(LM, 0.0)),
        dict(x=x_prompt[:, LM:], B=BP, L=TAIL, per_token=False, has_state=True, precise=True, rc=_ret_consts(TAIL, float(LM))),
        dict(x=x_sample.reshape(1, TS, D_MODEL), B=BS, L=LS, per_token=True, has_state=True, precise=False,
             rc=_ret_consts(LS, PAST_LEN)),
    ]
    zero_states = (jnp.zeros((BP, SSD_GROUPS, SSD_STATE, D_SSD // SSD_GROUPS), F32),
                   jnp.zeros((BP, CONV_WIDTH - 1, SSD_CONV_DIM), F32),
                   jnp.zeros((BP, 1, D_RG), F32),
                   jnp.zeros((BP, CONV_WIDTH - 1, D_RG), F32),
                   jnp.zeros((BP, RET_HEADS, RET_HEAD_DIM, RET_HEAD_DIM), F32))

    def mixer_stack(path, p, layer, mods, states, ssd_kw, ret_kw):
        B, L, per_token, precise = path["B"], path["L"], path["per_token"], path["precise"]
        z, xbc, rg, ret, dt = _inproj(path["x"], mods[1], mods[0], w_in_parts, layer, per_token, precise)
        if per_token:
            z, xbc, rg, ret, dt = [a.reshape(B, L, a.shape[-1]) for a in (z, xbc, rg, ret, dt)]
        s_ssd, s_cbuf, s_rg, s_rbuf, s_ret = states
        hs = path["has_state"]
        y_ssd, cbuf_n, ssd_n = _ssd(z, xbc, dt, s_ssd, s_cbuf, p, consts, hs, precise, **ssd_kw)
        y_rg, rbuf_n, rg_n = _rg(rg, s_rg, s_rbuf, p, hs, precise)
        y_ret, ret_n = _ret(ret, s_ret, p, path["rc"], hs, precise, **ret_kw)
        if per_token:
            y_ssd, y_rg, y_ret = [a.reshape(1, B * L, a.shape[-1]) for a in (y_ssd, y_rg, y_ret)]
        return (y_ssd, y_rg, y_ret), (ssd_n, cbuf_n, rg_n, rbuf_n, ret_n)

    expert_w = (exp_w_gate, exp_w_up, exp_w_down)

    def experts(h2_all, route_list, layer, precise):
        route_all = jnp.concatenate([r.reshape(-1, LANES) for r in route_list], axis=0)
        T_all = route_all.shape[0]
        bm = MOE_BM_TAIL if precise else MOE_BM
        dest, block_expert, n_used, P = _dispatch_plan(route_all[:, 0:2].astype(jnp.int32), bm)
        tok = jnp.arange(2 * T_all, dtype=jnp.int32) // 2
        if precise:
            slot_tok = jnp.full((P,), T_all, jnp.int32).at[dest].set(tok)
            xs_sorted = jnp.concatenate([h2_all, jnp.zeros((1, D_MODEL), h2_all.dtype)], axis=0)[slot_tok]
            ys_sorted = _moe(xs_sorted, block_expert, n_used, expert_w, layer, True, bm=bm)
        else:
            slot_row = jnp.zeros((P,), jnp.int32).at[dest].set(tok * ROW_TILES)
            ys_sorted = _moe(h2_all, block_expert, n_used, expert_w, layer, False, slot_rows=slot_row)
        return ys_sorted[dest[0::2]], ys_sorted[dest[1::2]]

    new_tail = []
    state_ssd_t = jnp.swapaxes(state_ssd, -1, -2)
    ssd_s = ret_s = None
    cbuf_s, rg_s, rbuf_s = [], [], []
    for l in range(DEPTH):
        p = params[l]
        mods_p, mods_s = _mods(mod[l], BP, LS)
        mods = [mods_p, mods_p, mods_s]
        st_s = (state_ssd_t, state_ssd_conv[l], state_rglru[l][:, None, :], state_rglru_conv[l], state_ret)

        ys_m, st_m = mixer_stack(paths[0], p, l, mods[0], zero_states, {}, {})
        ys_t, st_t = mixer_stack(paths[1], p, l, mods[1], st_m, dict(native_out=True), {})
        ys_c, st_c = mixer_stack(paths[2], p, l, mods[2], st_s,
                                 dict(h0_layer=l, native_out=True, out_stack=(l, ssd_s)),
                                 dict(h0_layer=l, out_stack=(l, ret_s)))
        new_tail.append(st_t)
        ssd_s, ret_s = st_c[0], st_c[4]
        cbuf_s.append(st_c[1])
        rg_s.append(st_c[2])
        rbuf_s.append(st_c[3])

        offs = (0, 0, BP * LM)
        group_tokens = BP * LM + TS
        post, h2_group = [None] * 3, None
        for k in (0, 2, 1):
            path, ys, m = paths[k], (ys_m, ys_t, ys_c)[k], mods[k]
            shared = None if path["precise"] else (group_tokens, offs[k], h2_group)
            post[k] = _outproj(*ys, path["x"], m[2], m[4], m[3], p, consts, path["per_token"], path["precise"], shared)
            if not path["precise"]:
                h2_group = post[k][1]

        ya_b, yb_b = experts(h2_group, [post[0][2], post[2][2]], l, False)
        ya_t, yb_t = experts(post[1][1].reshape(-1, D_MODEL), [post[1][2]], l, True)

        for path, (x1, _, route), ya, yb, off, m in zip(paths, post, (ya_b, ya_t, ya_b), (yb_b, yb_t, yb_b), offs, mods):
            path["x"] = _combine(x1, ya, yb, off, route, m[5], p, path["per_token"])

    def stack(lst, k):
        return jnp.stack([s[k] for s in lst])

    y_prompt = jnp.concatenate([paths[0]["x"], paths[1]["x"]], axis=1)
    return (y_prompt, paths[2]["x"].reshape(BS, LS, D_MODEL),
            jnp.swapaxes(stack(new_tail, 0), -1, -2), stack(new_tail, 1), stack(new_tail, 2)[:, :, 0],
            stack(new_tail, 3), stack(new_tail, 4),
            jnp.swapaxes(ssd_s, -1, -2), jnp.stack(cbuf_s), jnp.stack(rg_s)[:, :, 0], jnp.stack(rbuf_s), ret_s)
```

```python
import functools
import math

import jax
import jax.numpy as jnp
from jax import lax
from jax.experimental import pallas as pl
from jax.experimental.pallas import tpu as pltpu

F32 = jnp.float32
BF16 = jnp.bfloat16

D_MODEL = 1024
DEPTH = 4
D_SSD = 1024
SSD_HEAD_DIM = 64
SSD_HEADS = 16
SSD_GROUPS = 2
SSD_STATE = 128
SSD_CONV_DIM = D_SSD + 2 * SSD_GROUPS * SSD_STATE
CONV_WIDTH = 4
D_RG = 512
RG_BLOCKS = 8
RG_C = 8.0
D_RET = 512
RET_HEADS = 4
RET_HEAD_DIM = 128
ROPE_BASE = 10000.0
N_EXPERTS = 32
EXPERTS_PER_GROUP = 8
N_EXPERT_GROUPS = 4
D_EXPERT = 512
ALPHA = (2 * DEPTH) ** 0.25
EPS = 1e-5
PAST_LEN = 16384.0

LANES = 128
SUBLANES = 8
CHUNK = 128
TAIL = CHUNK
SEQS_PER_STEP = 16
SEQ_UNROLL = 4
ROW_TILES = D_MODEL // LANES
PROJ_TM = 384
MOE_BM = 256
MOE_BM_TAIL = 128
VMEM_LIMIT = 56 * 1024 * 1024


def _cparams(sem):
    return pltpu.CompilerParams(dimension_semantics=sem, vmem_limit_bytes=VMEM_LIMIT)


def _padded_rows(L):
    return CHUNK if L >= CHUNK else -(-L // SUBLANES) * SUBLANES


def _pick_tm(L, cap):
    if L <= LANES:
        return L
    tm = cap - cap % LANES
    while L % tm:
        tm -= LANES
    return tm


def _sigmoid(x):
    return 1.0 / (1.0 + jnp.exp(-x))


def _silu(x):
    return x * _sigmoid(x)


def _softplus(x):
    return jnp.maximum(x, 0.0) + jnp.log1p(jnp.exp(-jnp.abs(x)))


def _split3(v):
    hi = v.astype(BF16)
    r = v - hi.astype(F32)
    mid = r.astype(BF16)
    lo = (r - mid.astype(F32)).astype(BF16)
    return hi, mid, lo


def _dot(a, b):
    return jnp.dot(a, b, preferred_element_type=F32)


def _split2(v):
    hi = v.astype(BF16)
    return hi, (v - hi.astype(F32)).astype(BF16)


def _mm_w(a, w_hi, w_lo):
    if w_lo is None:
        return _dot(a.astype(BF16), w_hi)
    ah, al = _split2(a)
    return _dot(ah, w_hi) + (_dot(al, w_hi) + _dot(ah, w_lo))


def _mm_a(a, b, precise, nt=False):
    dn = (((1,), (1,)), ((), ())) if nt else (((1,), (0,)), ((), ()))

    def d(x, y):
        return lax.dot_general(x, y, dn, preferred_element_type=F32)

    if not precise:
        return d(a.astype(BF16), b.astype(BF16))
    ah, al = _split2(a)
    bh, bl = _split2(b)
    return d(ah, bh) + (d(al, bh) + d(ah, bl))


def _hi_lo(w):
    hi32 = lax.reduce_precision(w, exponent_bits=8, mantissa_bits=7)
    return hi32.astype(BF16), (w - hi32).astype(BF16)


def _dot_exact_rhs(v, m):
    hi, mid, lo = _split3(v)
    return _dot(hi, m) + _dot(mid, m) + _dot(lo, m)


def _dot_exact_lhs(m, v):
    hi, mid, lo = _split3(v)
    return _dot(m, hi) + _dot(m, mid) + _dot(m, lo)


def _ada_kernel(c_ref, w_ref, b_ref, o_ref):
    o_ref[0] = _mm_a(_silu(c_ref[...]), w_ref[0], True) + b_ref[0]


def _ada(c_all, w_ada, b_ada):
    n = c_all.shape[0]
    tn = 1024
    return pl.pallas_call(
        _ada_kernel,
        grid=(DEPTH, 6 * D_MODEL // tn),
        in_specs=[
            pl.BlockSpec((n, D_MODEL), lambda l, j: (0, 0)),
            pl.BlockSpec((1, D_MODEL, tn), lambda l, j: (l, 0, j)),
            pl.BlockSpec((1, 1, tn), lambda l, j: (l, 0, j)),
        ],
        out_specs=pl.BlockSpec((1, n, tn), lambda l, j: (l, 0, j)),
        out_shape=jax.ShapeDtypeStruct((DEPTH, n, 6 * D_MODEL), F32),
        compiler_params=_cparams(("arbitrary", "arbitrary")),
        name="ada",
    )(c_all, w_ada, b_ada.reshape(DEPTH, 1, 6 * D_MODEL))


_IN_SEGS = (("z", 0, 1024), ("xbc", 1024, 2560), ("rg", 2560, 3584), ("ret", 3584, 5632), ("dt", 5632, 5760))
IN_PAD = 5760


def _inproj_kernel(x_ref, sc_ref, sh_ref, w_ref, *rest, precise):
    wl_ref = rest[0] if precise else None
    outs = rest[1:] if precise else rest
    h = x_ref[0] * (1.0 + sc_ref[0]) + sh_ref[0]
    if precise:
        hh, hl = _split2(h)
    else:
        hh = h.astype(BF16)
    for o_ref, (_, lo, hi) in zip(outs, _IN_SEGS):
        acc = _dot(hh, w_ref[:, lo:hi])
        if precise:
            acc = acc + (_dot(hl, w_ref[:, lo:hi]) + _dot(hh, wl_ref[:, lo:hi]))
        o_ref[0] = acc


def _inproj(x, sc, sh, w, layer, per_token_mod, precise):
    B, L, _ = x.shape
    tm = _pick_tm(L, LANES if precise else PROJ_TM)
    if per_token_mod:
        mod_spec = pl.BlockSpec((1, tm, D_MODEL), lambda b, i: (b, i, 0))
    else:
        mod_spec = pl.BlockSpec((1, 1, D_MODEL), lambda b, i: (b, 0, 0))
    widths = [hi - lo for _, lo, hi in _IN_SEGS]
    w_spec = pl.BlockSpec((None, D_MODEL, IN_PAD), lambda b, i: (layer, 0, 0), pipeline_mode=pl.Buffered(1))
    ws = w if precise else w[:1]
    return pl.pallas_call(
        functools.partial(_inproj_kernel, precise=precise),
        grid=(B, L // tm),
        in_specs=[
            pl.BlockSpec((1, tm, D_MODEL), lambda b, i: (b, i, 0)),
            mod_spec,
            mod_spec,
        ] + [w_spec] * len(ws),
        out_specs=[pl.BlockSpec((1, tm, wd), lambda b, i: (b, i, 0)) for wd in widths],
        out_shape=[jax.ShapeDtypeStruct((B, L, wd), F32) for wd in widths],
        compiler_params=_cparams(("arbitrary", "arbitrary")),
        name="inproj",
    )(x, sc, sh, *ws)


def _conv_step(src_rows, buf_ref, cw_ref, cb_ref, nbuf_ref, xp_s, *, c, nc, Lv, Lp, has_state):
    @_when_unless(nc == 1, c == 0)
    def _():
        xp_s[...] = jnp.zeros_like(xp_s)
        if has_state:
            xp_s[SUBLANES - (CONV_WIDTH - 1):SUBLANES, :] = buf_ref[0]

    xp_s[SUBLANES:SUBLANES + Lv, :] = src_rows
    base = SUBLANES - (CONV_WIDTH - 1)
    xc = cb_ref[...]
    for j in range(CONV_WIDTH):
        xc = xc + cw_ref[j:j + 1, :] * xp_s[base + j:base + j + Lp, :]

    @_when_unless(nc == 1, c == nc - 1)
    def _():
        nbuf_ref[0] = xp_s[SUBLANES + Lv - (CONV_WIDTH - 1):SUBLANES + Lv, :]

    if nc > 1:
        xp_s[0:SUBLANES, :] = xp_s[Lp:Lp + SUBLANES, :]
    return xc


def _for_each_sequence(body, refs, bb, keep, squeeze, n_scratch):
    n = len(refs)

    def one(s, j):
        v = list(refs)
        for k in keep:
            v[k] = refs[k].at[pl.ds(s, 1)]
        for k in squeeze:
            v[k] = refs[k].at[s]
        if bb > 1:
            for k in range(n - n_scratch, n):
                v[k] = refs[k].at[j]
        body(*v)

    if bb == 1:
        one(0, 0)
    else:
        def step(i, carry):
            for j in range(SEQ_UNROLL):
                one(i * SEQ_UNROLL + j, j)
            return carry
        lax.fori_loop(0, bb // SEQ_UNROLL, step, 0)


def _when_unless(always, cond):
    def deco(fn):
        if always:
            fn()
        else:
            pl.when(cond)(fn)
        return fn
    return deco


def _seq_scratch(shape, bb):
    return pltpu.VMEM(((SEQ_UNROLL,) + shape) if bb > 1 else shape, F32)


def _ssd_kernel(*refs, bb, **kw):
    n = len(refs)
    _for_each_sequence(functools.partial(_ssd_body, **kw), refs, bb,
                       keep=(0, 1, 2, 4, n - 6, n - 5), squeeze=(3, n - 4), n_scratch=3)


def _ssd_body(z_ref, xbc_ref, dt_ref, h0_ref, buf_ref, cw_ref, cb_ref, dtb_ref, alog_ref, dch_ref,
              nw_ref, e_ref, tri_ref, *rest, Lv, Lp, nc, has_state, precise, native_in, native_out):
    y_ref, nbuf_ref, hn_ref, xp_s, dtp_s, s_s = rest[-6:]
    c = pl.program_id(1)
    hpg = SSD_HEADS // SSD_GROUPS

    @_when_unless(nc == 1, c == 0)
    def _():
        if not has_state:
            s_s[...] = jnp.zeros_like(s_s)
        elif native_in:
            for g in range(SSD_GROUPS):
                s_s[g] = jnp.concatenate([h0_ref[g * hpg + i] for i in range(hpg)], axis=0).T
        else:
            s_s[...] = h0_ref[...]
        if Lv < Lp:
            dtp_s[...] = jnp.zeros_like(dtp_s)

    xc = _conv_step(xbc_ref[0], buf_ref, cw_ref, cb_ref, nbuf_ref, xp_s,
                    c=c, nc=nc, Lv=Lv, Lp=Lp, has_state=has_state)
    xbc = _silu(xc)
    xs = xbc[:, 0:D_SSD]
    Bm = xbc[:, D_SSD:D_SSD + 256]
    Cm = xbc[:, D_SSD + 256:D_SSD + 512]

    if Lv < Lp:
        dtp_s[0:Lv, :] = dt_ref[0]
        dtr = dtp_s[...]
    else:
        dtr = dt_ref[0]
    dt = _softplus(dtr + dtb_ref[...])
    if Lv < Lp:
        rowv = lax.broadcasted_iota(jnp.int32, (Lp, LANES), 0)
        dt = jnp.where(rowv < Lv, dt, 0.0)
    a = dt * (-jnp.exp(alog_ref[...]))
    acum = _dot_exact_lhs(tri_ref[...], a)
    dt_e = _dot_exact_rhs(dt, e_ref[...])
    ac_e = _dot_exact_rhs(acum, e_ref[...])
    acl_e = ac_e[Lp - 1:Lp, :]
    acum_t = acum.T

    vdt = xs * dt_e
    lane = lax.broadcasted_iota(jnp.int32, (Lp, LANES), 1)
    causal = lax.broadcasted_iota(jnp.int32, (Lp, Lp), 0) >= lax.broadcasted_iota(jnp.int32, (Lp, Lp), 1)
    mid_t = F32 if precise else BF16

    y_blocks = []
    for g in range(SSD_GROUPS):
        Cg = Cm[:, g * SSD_STATE:(g + 1) * SSD_STATE]
        Bg = Bm[:, g * SSD_STATE:(g + 1) * SSD_STATE]
        G = _mm_a(Cg, Bg, precise, nt=True)
        for jj in range(4):
            j = 4 * g + jj
            scs = []
            for h in (2 * j, 2 * j + 1):
                col = jnp.sum(jnp.where(lane == h, acum, 0.0), axis=1, keepdims=True)
                rw = acum_t[h:h + 1, :]
                dec = jnp.exp(jnp.where(causal, col - rw, -jnp.inf))
                scs.append((G * dec).astype(mid_t))
            sc = jnp.concatenate(scs, axis=1)
            vb = vdt[:, j * LANES:(j + 1) * LANES]
            v2 = jnp.concatenate([jnp.where(lane < SSD_HEAD_DIM, vb, 0.0),
                                  jnp.where(lane >= SSD_HEAD_DIM, vb, 0.0)], axis=0).astype(mid_t)
            y_blocks.append(_mm_a(sc, v2, precise))
    y = jnp.concatenate(y_blocks, axis=1)

    eac = jnp.exp(ac_e)
    vw = vdt * jnp.exp(acl_e - ac_e)
    cd = jnp.exp(acl_e)
    half = D_SSD // SSD_GROUPS
    ys_parts = []
    for g in range(SSD_GROUPS):
        Sg = s_s[g]
        ys_parts.append(_mm_a(Cm[:, g * SSD_STATE:(g + 1) * SSD_STATE], Sg, precise))
        BgT = Bm[:, g * SSD_STATE:(g + 1) * SSD_STATE].T
        s_s[g] = cd[:, g * half:(g + 1) * half] * Sg + _mm_a(BgT, vw[:, g * half:(g + 1) * half], precise)
    y = y + eac * jnp.concatenate(ys_parts, axis=1) + dch_ref[...] * xs

    yv = y[0:Lv, :] if Lv < Lp else y
    yz = yv * _silu(z_ref[0])
    outs = []
    for g in range(SSD_GROUPS):
        part = yz[:, g * half:(g + 1) * half]
        ms = jnp.mean(part * part, axis=1, keepdims=True)
        outs.append(part * lax.rsqrt(ms + EPS))
    y_ref[0] = jnp.concatenate(outs, axis=1) * nw_ref[...]

    @_when_unless(nc == 1, c == nc - 1)
    def _():
        if native_out:
            for g in range(SSD_GROUPS):
                sgt = s_s[g].T
                for i in range(hpg):
                    hn_ref[g * hpg + i] = sgt[i * SSD_HEAD_DIM:(i + 1) * SSD_HEAD_DIM, :]
        else:
            hn_ref[...] = s_s[...]


def _state_specs(B, bb, shape, in_layer, out_stack):
    zeros = (0,) * len(shape)
    if in_layer is None:
        in_spec = pl.BlockSpec((bb,) + shape, lambda b, c: (b,) + zeros)
    else:
        in_spec = pl.BlockSpec((None, bb) + shape, lambda b, c: (in_layer, b) + zeros)
    if out_stack is None:
        out_spec = pl.BlockSpec((bb,) + shape, lambda b, c: (b,) + zeros)
        out_shape = jax.ShapeDtypeStruct((B,) + shape, F32)
    else:
        l = out_stack[0]
        out_spec = pl.BlockSpec((None, bb) + shape, lambda b, c: (l, b) + zeros)
        out_shape = jax.ShapeDtypeStruct((DEPTH, B) + shape, F32)
    return in_spec, out_spec, out_shape


def _seqs_per_step(B, L):
    return SEQS_PER_STEP if (L < CHUNK and B % SEQS_PER_STEP == 0) else 1


def _ssd(z, xbc, dt, h0, buf, p, consts, has_state, precise, h0_layer=None, native_out=False, out_stack=None):
    B, L, _ = z.shape
    Lv = min(CHUNK, L)
    Lp = _padded_rows(L)
    nc = L // Lv
    native_in = h0_layer is not None
    bb = _seqs_per_step(B, L)
    kern = functools.partial(_ssd_kernel, bb=bb, Lv=Lv, Lp=Lp, nc=nc, has_state=has_state, precise=precise,
                             native_in=native_in, native_out=native_out)
    row = lambda b, c: (b, c, 0)
    per_b3 = lambda b, c: (b, 0, 0)
    fixed = lambda b, c: (0, 0)
    half = D_SSD // SSD_GROUPS
    native = (SSD_HEADS, SSD_HEAD_DIM, SSD_STATE)
    packed = (SSD_GROUPS, SSD_STATE, half)
    h0_spec, _, _ = _state_specs(B, bb, native if native_in else packed, h0_layer, None)
    _, hn_spec, hn_shape = _state_specs(B, bb, native if native_out else packed, None, out_stack)
    prev = out_stack[1] if out_stack is not None else None
    extra_in, extra_specs, aliases = [], [], {}
    if prev is not None:
        extra_in, extra_specs, aliases = [prev], [pl.BlockSpec(memory_space=pl.ANY)], {13: 2}
    return pl.pallas_call(
        kern,
        grid=(B // bb, nc),
        in_specs=[
            pl.BlockSpec((bb, Lv, D_SSD), row),
            pl.BlockSpec((bb, Lv, SSD_CONV_DIM), row),
            pl.BlockSpec((bb, Lv, LANES), row),
            h0_spec,
            pl.BlockSpec((bb, CONV_WIDTH - 1, SSD_CONV_DIM), per_b3),
            pl.BlockSpec((CONV_WIDTH, SSD_CONV_DIM), fixed),
            pl.BlockSpec((1, SSD_CONV_DIM), fixed),
            pl.BlockSpec((1, LANES), fixed),
            pl.BlockSpec((1, LANES), fixed),
            pl.BlockSpec((1, D_SSD), fixed),
            pl.BlockSpec((1, D_SSD), fixed),
            pl.BlockSpec((LANES, D_SSD), fixed),
            pl.BlockSpec((Lp, Lp), fixed),
        ] + extra_specs,
        out_specs=[
            pl.BlockSpec((bb, Lv, D_SSD), row),
            pl.BlockSpec((bb, CONV_WIDTH - 1, SSD_CONV_DIM), per_b3),
            hn_spec,
        ],
        out_shape=[
            jax.ShapeDtypeStruct((B, L, D_SSD), F32),
            jax.ShapeDtypeStruct((B, CONV_WIDTH - 1, SSD_CONV_DIM), F32),
            hn_shape,
        ],
        scratch_shapes=[
            _seq_scratch((SUBLANES + Lp, SSD_CONV_DIM), bb),
            _seq_scratch((Lp, LANES), bb),
            _seq_scratch((SSD_GROUPS, SSD_STATE, half), bb),
        ],
        input_output_aliases=aliases,
        compiler_params=_cparams(("arbitrary", "arbitrary")),
        name="ssd",
    )(z, xbc, dt, h0, buf, p["ssd_conv_w"], p["ssd_conv_b"], p["ssd_dt_bias"], p["ssd_a_log"],
      p["ssd_d_ch"], p["ssd_norm_w"], consts["head_expand"],
      jnp.tril(jnp.ones((Lp, Lp), F32)).astype(BF16), *extra_in)


def _rg_kernel(*refs, bb, **kw):
    n = len(refs)
    _for_each_sequence(functools.partial(_rg_body, **kw), refs, bb,
                       keep=(0, 1, 2, n - 5, n - 4, n - 3), squeeze=(), n_scratch=2)


def _rg_body(rg_ref, h0_ref, buf_ref, cw_ref, cb_ref, wa_ref, ba_ref, wx_ref, bx_ref, lam_ref,
             y_ref, nbuf_ref, hn_ref, xp_s, hc_s, *, Lv, Lp, nc, has_state, precise):
    c = pl.program_id(1)

    @_when_unless(nc == 1, c == 0)
    def _():
        if has_state:
            hc_s[...] = h0_ref[0]
        else:
            hc_s[...] = jnp.zeros_like(hc_s)

    xc = _conv_step(rg_ref[0, :, 0:D_RG], buf_ref, cw_ref, cb_ref, nbuf_ref, xp_s,
                    c=c, nc=nc, Lv=Lv, Lp=Lp, has_state=has_state)
    r = _sigmoid(_mm_w(xc, wa_ref[0], wa_ref[1] if precise else None) + ba_ref[...])
    i = _sigmoid(_mm_w(xc, wx_ref[0], wx_ref[1] if precise else None) + bx_ref[...])
    la = (-RG_C * _softplus(-lam_ref[...])) * r
    u = jnp.sqrt(-jnp.tanh(la) * (jnp.exp(2.0 * la) + 1.0)) * (i * xc)
    row = lax.broadcasted_iota(jnp.int32, (Lp, D_RG), 0)
    if Lv < Lp:
        valid = row < Lv
        la = jnp.where(valid, la, 0.0)
        u = jnp.where(valid, u, 0.0)
    a = jnp.exp(la)
    s = 1
    while s < Lp:
        m = row >= s
        u_sh = jnp.where(m, pltpu.roll(u, s, 0), 0.0)
        a_sh = jnp.where(m, pltpu.roll(a, s, 0), 1.0)
        u = u + a * u_sh
        a = a * a_sh
        s *= 2
    h = u + a * hc_s[...]
    hc_s[...] = h[Lp - 1:Lp, :]
    hv = h[0:Lv, :] if Lv < Lp else h
    y_ref[0] = hv * jax.nn.gelu(rg_ref[0, :, D_RG:2 * D_RG])

    @_when_unless(nc == 1, c == nc - 1)
    def _():
        hn_ref[0] = hc_s[...]


def _rg(rg, h0, buf, p, has_state, precise):
    B, L, _ = rg.shape
    Lv = min(CHUNK, L)
    Lp = _padded_rows(L)
    nc = L // Lv
    bb = _seqs_per_step(B, L)
    kern = functools.partial(_rg_kernel, bb=bb, Lv=Lv, Lp=Lp, nc=nc, has_state=has_state, precise=precise)
    row = lambda b, c: (b, c, 0)
    per_b3 = lambda b, c: (b, 0, 0)
    fixed = lambda b, c: (0, 0)
    fixed3 = lambda b, c: (0, 0, 0)
    return pl.pallas_call(
        kern,
        grid=(B // bb, nc),
        in_specs=[
            pl.BlockSpec((bb, Lv, 2 * D_RG), row),
            pl.BlockSpec((bb, 1, D_RG), per_b3),
            pl.BlockSpec((bb, CONV_WIDTH - 1, D_RG), per_b3),
            pl.BlockSpec((CONV_WIDTH, D_RG), fixed),
            pl.BlockSpec((1, D_RG), fixed),
            pl.BlockSpec((2, D_RG, D_RG), fixed3),
            pl.BlockSpec((1, D_RG), fixed),
            pl.BlockSpec((2, D_RG, D_RG), fixed3),
            pl.BlockSpec((1, D_RG), fixed),
            pl.BlockSpec((1, D_RG), fixed),
        ],
        out_specs=[
            pl.BlockSpec((bb, Lv, D_RG), row),
            pl.BlockSpec((bb, CONV_WIDTH - 1, D_RG), per_b3),
            pl.BlockSpec((bb, 1, D_RG), per_b3),
        ],
        out_shape=[
            jax.ShapeDtypeStruct((B, L, D_RG), F32),
            jax.ShapeDtypeStruct((B, CONV_WIDTH - 1, D_RG), F32),
            jax.ShapeDtypeStruct((B, 1, D_RG), F32),
        ],
        scratch_shapes=[
            _seq_scratch((SUBLANES + Lp, D_RG), bb),
            _seq_scratch((1, D_RG), bb),
        ],
        compiler_params=_cparams(("arbitrary", "arbitrary")),
        name="rglru",
    )(rg, h0, buf, p["rg_conv_w"], p["rg_conv_b"], p["rg_wa_bd"], p["rg_ba"], p["rg_wx_bd"],
      p["rg_bx"], p["rg_lambda"])


def _ret_kernel(*refs, bb, **kw):
    n = len(refs)
    _for_each_sequence(functools.partial(_ret_body, **kw), refs, bb, keep=(0, n - 4), squeeze=(1, n - 3),
                       n_scratch=2)


def _ret_body(ret_ref, h0_ref, cos_ref, sin_ref, dm_ref, ea_ref, te_ref, cd_ref, gn_ref,
              *rest, Lv, Lp, nc, has_state, precise):
    y_ref, hn_ref, pad_s, s_s = rest[-4:]
    c = pl.program_id(1)

    @_when_unless(nc == 1, c == 0)
    def _():
        if has_state:
            s_s[...] = h0_ref[...]
        else:
            s_s[...] = jnp.zeros_like(s_s)
        if Lv < Lp:
            pad_s[...] = jnp.zeros_like(pad_s)

    if Lv < Lp:
        pad_s[0:Lv, :] = ret_ref[0, :, 0:3 * D_RET]
        qkv = pad_s[...]
    else:
        qkv = ret_ref[0, :, 0:3 * D_RET]
    cosf = cos_ref[...]
    sinf = sin_ref[...]
    outs = []
    for h in range(RET_HEADS):
        sl = slice(h * RET_HEAD_DIM, (h + 1) * RET_HEAD_DIM)
        q = qkv[:, sl]
        k = qkv[:, D_RET + h * RET_HEAD_DIM:D_RET + (h + 1) * RET_HEAD_DIM]
        v = qkv[:, 2 * D_RET + h * RET_HEAD_DIM:2 * D_RET + (h + 1) * RET_HEAD_DIM]
        q = q * cosf + pltpu.roll(q, RET_HEAD_DIM // 2, 1) * sinf
        k = (k * cosf + pltpu.roll(k, RET_HEAD_DIM // 2, 1) * sinf) * (RET_HEAD_DIM ** -0.5)
        G = _mm_a(q, k, precise, nt=True)
        S = s_s[h]
        y = _mm_a(G * dm_ref[h], v, precise) + _mm_a(q * ea_ref[h], S, precise)
        s_s[h] = cd_ref[h] * S + _mm_a((k * te_ref[h]).T, v, precise)
        yv = y[0:Lv, :] if Lv < Lp else y
        mu = jnp.mean(yv, axis=1, keepdims=True)
        d = yv - mu
        var = jnp.mean(d * d, axis=1, keepdims=True)
        outs.append(d * lax.rsqrt(var + EPS))
    yn = jnp.concatenate(outs, axis=1) * gn_ref[...]
    y_ref[0] = _silu(ret_ref[0, :, 3 * D_RET:4 * D_RET]) * yn

    @_when_unless(nc == 1, c == nc - 1)
    def _():
        hn_ref[...] = s_s[...]


def _ret(ret, h0, p, rc, has_state, precise, h0_layer=None, out_stack=None):
    B, L, _ = ret.shape
    Lv = min(CHUNK, L)
    Lp = _padded_rows(L)
    nc = L // Lv
    bb = _seqs_per_step(B, L)
    kern = functools.partial(_ret_kernel, bb=bb, Lv=Lv, Lp=Lp, nc=nc, has_state=has_state, precise=precise)
    row = lambda b, c: (b, c, 0)
    fixed2 = lambda b, c: (0, 0)
    fixed3 = lambda b, c: (0, 0, 0)
    hd = RET_HEAD_DIM
    h0_spec, _, _ = _state_specs(B, bb, (RET_HEADS, hd, hd), h0_layer, None)
    _, hn_spec, hn_shape = _state_specs(B, bb, (RET_HEADS, hd, hd), None, out_stack)
    prev = out_stack[1] if out_stack is not None else None
    extra_in, extra_specs, aliases = [], [], {}
    if prev is not None:
        extra_in, extra_specs, aliases = [prev], [pl.BlockSpec(memory_space=pl.ANY)], {9: 1}
    return pl.pallas_call(
        kern,
        grid=(B // bb, nc),
        in_specs=[
            pl.BlockSpec((bb, Lv, 4 * D_RET), row),
            h0_spec,
            pl.BlockSpec((Lp, hd), lambda b, c: (c, 0)),
            pl.BlockSpec((Lp, hd), lambda b, c: (c, 0)),
            pl.BlockSpec((RET_HEADS, Lp, Lp), fixed3),
            pl.BlockSpec((RET_HEADS, Lp, hd), fixed3),
            pl.BlockSpec((RET_HEADS, Lp, hd), fixed3),
            pl.BlockSpec((RET_HEADS, hd, hd), fixed3),
            pl.BlockSpec((1, D_RET), fixed2),
        ] + extra_specs,
        out_specs=[
            pl.BlockSpec((bb, Lv, D_RET), row),
            hn_spec,
        ],
        out_shape=[
            jax.ShapeDtypeStruct((B, L, D_RET), F32),
            hn_shape,
        ],
        scratch_shapes=[
            _seq_scratch((Lp, 3 * D_RET), bb),
            _seq_scratch((RET_HEADS, hd, hd), bb),
        ],
        input_output_aliases=aliases,
        compiler_params=_cparams(("arbitrary", "arbitrary")),
        name="retention",
    )(ret, h0, rc["cos"], rc["sin"], rc["dmat"], rc["eacum"], rc["toend"], rc["cdec"], p["ret_gn_w"], *extra_in)


def _ret_consts(L, pos0):
    Lv = min(CHUNK, L)
    Lp = _padded_rows(L)
    nc = L // Lv
    half = RET_HEAD_DIM // 2
    inv = ROPE_BASE ** (-jnp.arange(half, dtype=F32) / half)
    pos = pos0 + jnp.arange(L, dtype=F32)
    ang = pos[:, None] * inv[None, :]
    cos = jnp.cos(ang)
    sin = jnp.sin(ang)
    cosf = jnp.concatenate([cos, cos], axis=1)
    sinf = jnp.concatenate([-sin, sin], axis=1)
    if Lv < Lp:
        cosf = jnp.pad(cosf, ((0, Lp - Lv), (0, 0)))
        sinf = jnp.pad(sinf, ((0, Lp - Lv), (0, 0)))
    assert cosf.shape[0] == nc * Lp
    log_gamma = jnp.log1p(-jnp.exp2(-5.0 - jnp.arange(RET_HEADS, dtype=F32)))
    steps = jnp.minimum(jnp.arange(Lp) + 1, Lv).astype(F32)
    acum = log_gamma[:, None] * steps[None, :]
    causal = jnp.tril(jnp.ones((Lp, Lp), bool))
    dmat = jnp.exp(jnp.where(causal[None], acum[:, :, None] - acum[:, None, :], -jnp.inf))
    ones = jnp.ones((RET_HEADS, Lp, RET_HEAD_DIM), F32)
    eacum = jnp.exp(acum)[:, :, None] * ones
    toend = jnp.exp(acum[:, -1:] - acum)[:, :, None] * ones
    rowvalid = (jnp.arange(Lp) < Lv).astype(F32)[None, :, None]
    toend = toend * rowvalid
    cdec = jnp.exp(acum[:, -1])[:, None, None] * jnp.ones((RET_HEADS, RET_HEAD_DIM, RET_HEAD_DIM), F32)
    return dict(cos=cosf, sin=sinf, dmat=dmat, eacum=eacum, toend=toend, cdec=cdec)


def _layernorm(v, g, b):
    mu = jnp.mean(v, axis=1, keepdims=True)
    d = v - mu
    var = jnp.mean(d * d, axis=1, keepdims=True)
    return d * lax.rsqrt(var + EPS) * g + b


def _route(logits, bias):
    tm = logits.shape[0]
    lane = lax.broadcasted_iota(jnp.int32, (tm, LANES), 1)
    lanef = lane.astype(F32)
    scores = _sigmoid(logits)
    choice = scores + bias
    neg = -jnp.inf
    best = jnp.full((tm, 1), neg, F32)
    e1 = jnp.zeros((tm, 1), F32)
    e2 = jnp.zeros((tm, 1), F32)
    for g in range(N_EXPERT_GROUPS):
        ing = (lane >= g * EXPERTS_PER_GROUP) & (lane < (g + 1) * EXPERTS_PER_GROUP)
        cg = jnp.where(ing, choice, neg)
        m1 = jnp.max(cg, axis=1, keepdims=True)
        i1 = jnp.min(jnp.where(cg == m1, lanef, float(LANES)), axis=1, keepdims=True)
        cg2 = jnp.where(lanef == i1, neg, cg)
        m2 = jnp.max(cg2, axis=1, keepdims=True)
        i2 = jnp.min(jnp.where(cg2 == m2, lanef, float(LANES)), axis=1, keepdims=True)
        gs = m1 + m2
        better = gs > best
        best = jnp.where(better, gs, best)
        e1 = jnp.where(better, i1, e1)
        e2 = jnp.where(better, i2, e2)
    w1 = jnp.sum(jnp.where(lanef == e1, scores, 0.0), axis=1, keepdims=True)
    w2 = jnp.sum(jnp.where(lanef == e2, scores, 0.0), axis=1, keepdims=True)
    den = w1 + w2
    out = jnp.where(lane == 0, e1, jnp.where(lane == 1, e2, jnp.where(lane == 2, w1 / den, jnp.where(lane == 3, w2 / den, 0.0))))
    return out


def _outproj_kernel(ys_ref, yr_ref, yt_ref, x_ref, g1_ref, sc2_ref, sh2_ref, w_ref, lng_ref, lnb_ref,
                    rwh_ref, rwl_ref, rb_ref, *rest, precise, tile_rows):
    x1_ref, h2_ref, route_ref = rest[-3:]
    def part(y_ref, lo, hi):
        return _mm_w(y_ref[0], w_ref[0, lo:hi, :], w_ref[1, lo:hi, :] if precise else None)

    mix = part(ys_ref, 0, D_SSD) + part(yr_ref, D_SSD, D_SSD + D_RG) + part(yt_ref, D_SSD + D_RG, 2 * D_MODEL)
    x1 = _layernorm(ALPHA * x_ref[0] + (1.0 + g1_ref[0]) * mix, lng_ref[...], lnb_ref[...])
    x1_ref[0] = x1
    h2 = x1 * (1.0 + sc2_ref[0]) + sh2_ref[0]
    hi, lo = _split2(h2)
    if tile_rows:
        tm = h2.shape[0]
        for s in range(ROW_TILES):
            h2_ref[pl.ds(s, tm, stride=ROW_TILES), :] = h2[:, s * LANES:(s + 1) * LANES]
    else:
        h2_ref[0] = h2
    logits = _dot(hi, rwh_ref[...]) + (_dot(lo, rwh_ref[...]) + _dot(hi, rwl_ref[...]))
    route_ref[0] = _route(logits, rb_ref[...])


def _outproj(y_ssd, y_rg, y_ret, x, g1, sc2, sh2, p, consts, per_token_mod, precise, shared=None):
    B, L, _ = x.shape
    tm = _pick_tm(L, LANES if precise else PROJ_TM)
    tile_rows = not precise
    row = lambda b, i: (b, i, 0)
    fixed = lambda b, i: (0, 0)
    if per_token_mod:
        mod_spec = pl.BlockSpec((1, tm, D_MODEL), row)
    else:
        mod_spec = pl.BlockSpec((1, 1, D_MODEL), lambda b, i: (b, 0, 0))
    extra_in, extra_specs, aliases = [], [], {}
    if tile_rows:
        total, off, prev = shared
        assert off % tm == 0
        h2_spec = pl.BlockSpec((ROW_TILES * tm, LANES), lambda b, i: (off // tm + b * (L // tm) + i, 0))
        h2_shape = jax.ShapeDtypeStruct((ROW_TILES * total, LANES), F32)
        if prev is not None:
            extra_in, extra_specs, aliases = [prev], [pl.BlockSpec(memory_space=pl.ANY)], {13: 1}
    else:
        h2_spec = pl.BlockSpec((1, tm, D_MODEL), row)
        h2_shape = jax.ShapeDtypeStruct((B, L, D_MODEL), F32)
    return pl.pallas_call(
        functools.partial(_outproj_kernel, precise=precise, tile_rows=tile_rows),
        grid=(B, L // tm),
        in_specs=[
            pl.BlockSpec((1, tm, D_SSD), row),
            pl.BlockSpec((1, tm, D_RG), row),
            pl.BlockSpec((1, tm, D_RET), row),
            pl.BlockSpec((1, tm, D_MODEL), row),
            mod_spec, mod_spec, mod_spec,
            pl.BlockSpec((2, 2 * D_MODEL, D_MODEL), lambda b, i: (0, 0, 0)),
            pl.BlockSpec((1, D_MODEL), fixed),
            pl.BlockSpec((1, D_MODEL), fixed),
            pl.BlockSpec((D_MODEL, LANES), fixed),
            pl.BlockSpec((D_MODEL, LANES), fixed),
            pl.BlockSpec((1, LANES), fixed),
        ] + extra_specs,
        out_specs=[
            pl.BlockSpec((1, tm, D_MODEL), row),
            h2_spec,
            pl.BlockSpec((1, tm, LANES), row),
        ],
        out_shape=[
            jax.ShapeDtypeStruct((B, L, D_MODEL), F32),
            h2_shape,
            jax.ShapeDtypeStruct((B, L, LANES), F32),
        ],
        input_output_aliases=aliases,
        compiler_params=_cparams(("arbitrary", "arbitrary")),
        name="outproj",
    )(y_ssd, y_rg, y_ret, x, g1, sc2, sh2, p["w_out"], p["ln1_g"], p["ln1_b"],
      consts["rw_hi"], consts["rw_lo"], consts["rbias"], *extra_in)


def _row_gather_copy(src_hbm, first_row, dst_vmem, slot, r, sem):
    return pltpu.make_async_copy(src_hbm.at[pl.ds(pl.multiple_of(first_row, ROW_TILES), ROW_TILES)],
                                 dst_vmem.at[slot, pl.ds(r * ROW_TILES, ROW_TILES)], sem.at[slot])


def _moe_kernel(be_ref, nu_ref, *refs, precise, gather):
    if gather:
        rowc_ref, rown_ref, src_ref, wg_ref, wu_ref, wd_ref, o_ref = refs[:7]
        scr, xbuf, sem = refs[7:-2], refs[-2], refs[-1]
    else:
        x_ref, wg_ref, wu_ref, wd_ref, o_ref = refs[:5]
        scr = refs[5:]
    i = pl.program_id(0)
    n_used = nu_ref[0]
    used = i < n_used
    new_expert = jnp.logical_or(i == 0, be_ref[i] != be_ref[jnp.maximum(i - 1, 0)])

    if gather:
        slot = lax.rem(i, 2)

        def issue(rows_ref, s):
            for r in range(MOE_BM):
                _row_gather_copy(src_ref, rows_ref[0, 0, r], xbuf, s, r, sem).start(priority=r % 2)

        @pl.when(jnp.logical_and(used, i == 0))
        def _():
            issue(rowc_ref, 0)

        @pl.when(i + 1 < n_used)
        def _():
            issue(rown_ref, 1 - slot)

    @pl.when(jnp.logical_and(used, new_expert))
    def _():
        for k, w_ref in enumerate((wg_ref, wu_ref, wd_ref)):
            w = w_ref[...]
            if precise:
                hi, lo = _split2(w)
                scr[k][...] = hi
                scr[3 + k][...] = lo
            else:
                scr[k][...] = w.astype(BF16)

    @pl.when(used)
    def _():
        if gather:
            pltpu.make_async_copy(src_ref.at[pl.ds(0, MOE_BM * ROW_TILES)], xbuf.at[slot], sem.at[slot]).wait()
            xb = xbuf.at[slot]
            x = jnp.concatenate([xb[pl.ds(s, MOE_BM, stride=ROW_TILES), :] for s in range(ROW_TILES)], axis=1)
        else:
            x = x_ref[...]
        lo = (scr[3][...], scr[4][...], scr[5][...]) if precise else (None, None, None)
        g = _mm_w(x, scr[0][...], lo[0])
        u = _mm_w(x, scr[1][...], lo[1])
        o_ref[...] = _mm_w(_silu(g) * u, scr[2][...], lo[2])

    @pl.when(jnp.logical_not(used))
    def _():
        o_ref[...] = jnp.zeros_like(o_ref)


def _moe(xs, block_expert, n_used, weights, layer, precise, slot_rows=None, bm=MOE_BM):
    gather = slot_rows is not None
    P = slot_rows.shape[0] if gather else xs.shape[0]
    bm = MOE_BM if gather else bm
    nblk = P // bm
    in_w = pl.BlockSpec((None, None, D_MODEL, D_EXPERT), lambda i, be, nu: (layer, be[i], 0, 0))
    out_w = pl.BlockSpec((None, None, D_EXPERT, D_MODEL), lambda i, be, nu: (layer, be[i], 0, 0))
    w_scratch = [pltpu.VMEM((D_MODEL, D_EXPERT), BF16), pltpu.VMEM((D_MODEL, D_EXPERT), BF16),
                 pltpu.VMEM((D_EXPERT, D_MODEL), BF16)] * (2 if precise else 1)
    if gather:
        rows3 = slot_rows.reshape(nblk, 1, MOE_BM)
        x_specs = [
            pl.BlockSpec((1, 1, MOE_BM), lambda i, be, nu: (i, 0, 0), memory_space=pltpu.SMEM),
            pl.BlockSpec((1, 1, MOE_BM), lambda i, be, nu: (jnp.minimum(i + 1, nblk - 1), 0, 0),
                         memory_space=pltpu.SMEM),
            pl.BlockSpec(memory_space=pl.ANY),
        ]
        x_args = [rows3, rows3, xs]
        extra_scratch = [pltpu.VMEM((2, MOE_BM * ROW_TILES, LANES), xs.dtype), pltpu.SemaphoreType.DMA((2,))]
    else:
        x_specs = [pl.BlockSpec((bm, D_MODEL), lambda i, be, nu: (i, 0))]
        x_args = [xs]
        extra_scratch = []
    grid_spec = pltpu.PrefetchScalarGridSpec(
        num_scalar_prefetch=2,
        grid=(nblk,),
        in_specs=x_specs + [in_w, in_w, out_w],
        out_specs=pl.BlockSpec((bm, D_MODEL), lambda i, be, nu: (i, 0)),
        scratch_shapes=w_scratch + extra_scratch,
    )
    return pl.pallas_call(
        functools.partial(_moe_kernel, precise=precise, gather=gather),
        grid_spec=grid_spec,
        out_shape=jax.ShapeDtypeStruct((P, D_MODEL), F32),
        compiler_params=_cparams(("arbitrary",)),
        name="moe_gather" if gather else "moe",
    )(block_expert, n_used, *x_args, *weights)


def _combine_kernel(x1_ref, ya_ref, yb_ref, route_ref, g2_ref, lng_ref, lnb_ref, o_ref):
    tm = x1_ref.shape[1]
    lane = lax.broadcasted_iota(jnp.int32, (tm, LANES), 1)
    rt = route_ref[0]
    w1 = jnp.sum(jnp.where(lane == 2, rt, 0.0), axis=1, keepdims=True)
    w2 = jnp.sum(jnp.where(lane == 3, rt, 0.0), axis=1, keepdims=True)
    moe = ya_ref[...] * w1 + yb_ref[...] * w2
    o_ref[0] = _layernorm(ALPHA * x1_ref[0] + (1.0 + g2_ref[0]) * moe, lng_ref[...], lnb_ref[...])


def _combine(x1, ya, yb, row_off, route, g2, p, per_token_mod):
    B, L, _ = x1.shape
    tm = _pick_tm(L, PROJ_TM)
    assert row_off % tm == 0
    row = lambda b, i: (b, i, 0)
    fixed = lambda b, i: (0, 0)
    flat = lambda b, i: (row_off // tm + b * (L // tm) + i, 0)
    if per_token_mod:
        mod_spec = pl.BlockSpec((1, tm, D_MODEL), row)
    else:
        mod_spec = pl.BlockSpec((1, 1, D_MODEL), lambda b, i: (b, 0, 0))
    return pl.pallas_call(
        _combine_kernel,
        grid=(B, L // tm),
        in_specs=[
            pl.BlockSpec((1, tm, D_MODEL), row),
            pl.BlockSpec((tm, D_MODEL), flat),
            pl.BlockSpec((tm, D_MODEL), flat),
            pl.BlockSpec((1, tm, LANES), row),
            mod_spec,
            pl.BlockSpec((1, D_MODEL), fixed),
            pl.BlockSpec((1, D_MODEL), fixed),
        ],
        out_specs=pl.BlockSpec((1, tm, D_MODEL), row),
        out_shape=jax.ShapeDtypeStruct((B, L, D_MODEL), F32),
        compiler_params=_cparams(("arbitrary", "arbitrary")),
        name="combine",
    )(x1, ya, yb, route, g2, p["ln2_g"], p["ln2_b"])


def _dispatch_plan(eidx, bm):
    T = eidx.shape[0]
    A = 2 * T
    flat_e = eidx.reshape(A)
    onehot = (flat_e[:, None] == jnp.arange(N_EXPERTS, dtype=jnp.int32)[None, :]).astype(jnp.int32)
    cum = jnp.cumsum(onehot, axis=0)
    rank = jnp.take_along_axis(cum, flat_e[:, None], axis=1)[:, 0] - 1
    counts = cum[-1]
    padded = (counts + bm - 1) // bm * bm
    pad_end = jnp.cumsum(padded)
    pad_start = pad_end - padded
    dest = pad_start[flat_e] + rank
    nblk = (A + N_EXPERTS * (bm - 1) + bm - 1) // bm
    block_pos = jnp.arange(nblk, dtype=jnp.int32) * bm
    block_expert = jnp.minimum(jnp.sum(pad_end[None, :] <= block_pos[:, None], axis=1), N_EXPERTS - 1).astype(jnp.int32)
    n_used = (pad_end[-1] // bm).astype(jnp.int32).reshape(1)
    return dest, block_expert, n_used, nblk * bm


def _mods(mod_l, n_prompt, dec_seq):
    mp = mod_l[:n_prompt].reshape(n_prompt, 6, D_MODEL)
    ms = mod_l[n_prompt:].reshape(-1, 6, D_MODEL)
    prompt = [mp[:, j][:, None, :] for j in range(6)]
    sample = [jnp.repeat(ms[:, j], dec_seq, axis=0)[None] for j in range(6)]
    return prompt, sample


def kernel(x_prompt, x_sample, c_prompt, c_sample, state_ssd, state_ssd_conv, state_rglru, state_rglru_conv, state_ret, w_ada, b_ada, w_in, ssd_conv_w, ssd_conv_b, ssd_dt_bias, ssd_a_log, ssd_d, ssd_norm_w, rg_conv_w, rg_conv_b, rg_wa, rg_ba, rg_wx, rg_bx, rg_lambda, ret_gn_w, w_out, ln1_g, ln1_b, router_w, router_bias, exp_w_gate, exp_w_up, exp_w_down, ln2_g, ln2_b):
    BP, LP, _ = x_prompt.shape
    BS, LS, _ = x_sample.shape
    TP, TS = BP * LP, BS * LS

    def pad_lanes(v):
        return jnp.pad(v, ((0, 0), (0, LANES - v.shape[-1])))

    eye = jnp.eye(RG_BLOCKS, dtype=F32)

    def block_diag(w):
        return jnp.stack(_hi_lo(jnp.einsum("njk,nm->njmk", w, eye).reshape(D_RG, D_RG)))

    w_in_parts = _hi_lo(jnp.concatenate(
        [w_in[:, :, :2560], w_in[:, :, 2576:5648], w_in[:, :, 2560:2576],
         jnp.zeros((DEPTH, D_MODEL, LANES - SSD_HEADS), F32)], axis=2))

    params = []
    for l in range(DEPTH):
        params.append(dict(
            ssd_conv_w=ssd_conv_w[l], ssd_conv_b=ssd_conv_b[l][None],
            ssd_dt_bias=pad_lanes(ssd_dt_bias[l][None]), ssd_a_log=pad_lanes(ssd_a_log[l][None]),
            ssd_d_ch=jnp.repeat(ssd_d[l], SSD_HEAD_DIM)[None], ssd_norm_w=ssd_norm_w[l][None],
            rg_conv_w=rg_conv_w[l], rg_conv_b=rg_conv_b[l][None],
            rg_wa_bd=block_diag(rg_wa[l]), rg_ba=rg_ba[l][None],
            rg_wx_bd=block_diag(rg_wx[l]), rg_bx=rg_bx[l][None],
            rg_lambda=rg_lambda[l][None], ret_gn_w=ret_gn_w[l][None],
            w_out=jnp.stack(_hi_lo(w_out[l])), ln1_g=ln1_g[l][None], ln1_b=ln1_b[l][None],
            ln2_g=ln2_g[l][None], ln2_b=ln2_b[l][None],
        ))

    rw_hi, rw_lo = _hi_lo(pad_lanes(router_w))
    rbias = pad_lanes(router_bias[None])
    head_of_ch = jnp.arange(D_SSD) // SSD_HEAD_DIM
    consts = dict(
        rw_hi=rw_hi, rw_lo=rw_lo, rbias=rbias.astype(F32),
        head_expand=(jnp.arange(LANES)[:, None] == head_of_ch[None, :]).astype(BF16),
        tri=jnp.tril(jnp.ones((CHUNK, CHUNK), F32)).astype(BF16),
    )
    mod = _ada(jnp.concatenate([c_prompt, c_sample], axis=0), w_ada, b_ada)

    LM = LP - TAIL
    paths = [
        dict(x=x_prompt[:, :LM], B=BP, L=LM, per_token=False, has_state=False, precise=False, rc=_ret_consts(LM, 0.0)),
        dict(x=x_prompt[:, LM:], B=BP, L=TAIL, per_token=False, has_state=True, precise=True, rc=_ret_consts(TAIL, float(LM))),
        dict(x=x_sample.reshape(1, TS, D_MODEL), B=BS, L=LS, per_token=True, has_state=True, precise=False,
             rc=_ret_consts(LS, PAST_LEN)),
    ]
    zero_states = (jnp.zeros((BP, SSD_GROUPS, SSD_STATE, D_SSD // SSD_GROUPS), F32),
                   jnp.zeros((BP, CONV_WIDTH - 1, SSD_CONV_DIM), F32),
                   jnp.zeros((BP, 1, D_RG), F32),
                   jnp.zeros((BP, CONV_WIDTH - 1, D_RG), F32),
                   jnp.zeros((BP, RET_HEADS, RET_HEAD_DIM, RET_HEAD_DIM), F32))

    def mixer_stack(path, p, layer, mods, states, ssd_kw, ret_kw):
        B, L, per_token, precise = path["B"], path["L"], path["per_token"], path["precise"]
        z, xbc, rg, ret, dt = _inproj(path["x"], mods[1], mods[0], w_in_parts, layer, per_token, precise)
        if per_token:
            z, xbc, rg, ret, dt = [a.reshape(B, L, a.shape[-1]) for a in (z, xbc, rg, ret, dt)]
        s_ssd, s_cbuf, s_rg, s_rbuf, s_ret = states
        hs = path["has_state"]
        y_ssd, cbuf_n, ssd_n = _ssd(z, xbc, dt, s_ssd, s_cbuf, p, consts, hs, precise, **ssd_kw)
        y_rg, rbuf_n, rg_n = _rg(rg, s_rg, s_rbuf, p, hs, precise)
        y_ret, ret_n = _ret(ret, s_ret, p, path["rc"], hs, precise, **ret_kw)
        if per_token:
            y_ssd, y_rg, y_ret = [a.reshape(1, B * L, a.shape[-1]) for a in (y_ssd, y_rg, y_ret)]
        return (y_ssd, y_rg, y_ret), (ssd_n, cbuf_n, rg_n, rbuf_n, ret_n)

    expert_w = (exp_w_gate, exp_w_up, exp_w_down)

    def experts(h2_all, route_list, layer, precise):
        route_all = jnp.concatenate([r.reshape(-1, LANES) for r in route_list], axis=0)
        T_all = route_all.shape[0]
        bm = MOE_BM_TAIL if precise else MOE_BM
        dest, block_expert, n_used, P = _dispatch_plan(route_all[:, 0:2].astype(jnp.int32), bm)
        tok = jnp.arange(2 * T_all, dtype=jnp.int32) // 2
        if precise:
            slot_tok = jnp.full((P,), T_all, jnp.int32).at[dest].set(tok)
            xs_sorted = jnp.concatenate([h2_all, jnp.zeros((1, D_MODEL), h2_all.dtype)], axis=0)[slot_tok]
            ys_sorted = _moe(xs_sorted, block_expert, n_used, expert_w, layer, True, bm=bm)
        else:
            slot_row = jnp.zeros((P,), jnp.int32).at[dest].set(tok * ROW_TILES)
            ys_sorted = _moe(h2_all, block_expert, n_used, expert_w, layer, False, slot_rows=slot_row)
        return ys_sorted[dest[0::2]], ys_sorted[dest[1::2]]

    new_tail = []
    state_ssd_t = jnp.swapaxes(state_ssd, -1, -2)
    ssd_s = ret_s = None
    cbuf_s, rg_s, rbuf_s = [], [], []
    for l in range(DEPTH):
        p = params[l]
        mods_p, mods_s = _mods(mod[l], BP, LS)
        mods = [mods_p, mods_p, mods_s]
        st_s = (state_ssd_t, state_ssd_conv[l], state_rglru[l][:, None, :], state_rglru_conv[l], state_ret)

        ys_m, st_m = mixer_stack(paths[0], p, l, mods[0], zero_states, {}, {})
        ys_t, st_t = mixer_stack(paths[1], p, l, mods[1], st_m, dict(native_out=True), {})
        ys_c, st_c = mixer_stack(paths[2], p, l, mods[2], st_s,
                                 dict(h0_layer=l, native_out=True, out_stack=(l, ssd_s)),
                                 dict(h0_layer=l, out_stack=(l, ret_s)))
        new_tail.append(st_t)
        ssd_s, ret_s = st_c[0], st_c[4]
        cbuf_s.append(st_c[1])
        rg_s.append(st_c[2])
        rbuf_s.append(st_c[3])

        offs = (0, 0, BP * LM)
        group_tokens = BP * LM + TS
        post, h2_group = [None] * 3, None
        for k in (0, 2, 1):
            path, ys, m = paths[k], (ys_m, ys_t, ys_c)[k], mods[k]
            shared = None if path["precise"] else (group_tokens, offs[k], h2_group)
            post[k] = _outproj(*ys, path["x"], m[2], m[4], m[3], p, consts, path["per_token"], path["precise"], shared)
            if not path["precise"]:
                h2_group = post[k][1]

        ya_b, yb_b = experts(h2_group, [post[0][2], post[2][2]], l, False)
        ya_t, yb_t = experts(post[1][1].reshape(-1, D_MODEL), [post[1][2]], l, True)

        for path, (x1, _, route), ya, yb, off, m in zip(paths, post, (ya_b, ya_t, ya_b), (yb_b, yb_t, yb_b), offs, mods):
            path["x"] = _combine(x1, ya, yb, off, route, m[5], p, path["per_token"])

    def stack(lst, k):
        return jnp.stack([s[k] for s in lst])

    y_prompt = jnp.concatenate([paths[0]["x"], paths[1]["x"]], axis=1)
    return (y_prompt, paths[2]["x"].reshape(BS, LS, D_MODEL),
            jnp.swapaxes(stack(new_tail, 0), -1, -2), stack(new_tail, 1), stack(new_tail, 2)[:, :, 0],
            stack(new_tail, 3), stack(new_tail, 4),
            jnp.swapaxes(ssd_s, -1, -2), jnp.stack(cbuf_s), jnp.stack(rg_s)[:, :, 0], jnp.stack(rbuf_s), ret_s)
```
